```python
import math
import jax, jax.numpy as jnp
from jax import lax
import numpy as np

D_MODEL = 2048
BATCH = 1
SEQ = 8192
DEPTH = 1

HEAD_DIM = 64
N_Q_HEADS = 16
N_KV_HEADS = 2
GROUP = N_Q_HEADS // N_KV_HEADS
WINDOW = 128
BLOCK = 128
ROT_DIM = HEAD_DIM // 4
ROPE_THETA = 500000.0
Q_W = N_Q_HEADS * HEAD_DIM
KV_W = N_KV_HEADS * HEAD_DIM

SSM_W = D_MODEL // 2
SSM_GC = 16
SSM_G = SSM_W // SSM_GC
SSM_P = 64

D_FF = 4 * D_MODEL

MIX_W = Q_W + SSM_W
IN_W = Q_W + 2 * KV_W + SSM_W + 2 * D_MODEL
EPS = 1e-6

kernel_name = "hybrid_swa_sink_s5_gated_block"


def rms_norm(x, g):
    xf = x.astype(jnp.float32)
    y = xf * lax.rsqrt(jnp.mean(xf * xf, axis=-1, keepdims=True) + EPS)
    return (y * g.astype(jnp.float32)).astype(x.dtype)


def partial_rope(x, pos):
    half = ROT_DIM // 2
    inv = ROPE_THETA ** (-jnp.arange(half, dtype=jnp.float32) * 2.0 / ROT_DIM)
    ang = pos.astype(jnp.float32)[:, None] * inv[None, :]
    cos = jnp.cos(ang)[None, :, None, :]
    sin = jnp.sin(ang)[None, :, None, :]
    xr = x[..., :ROT_DIM].astype(jnp.float32)
    x1, x2 = xr[..., :half], xr[..., half:]
    rot = jnp.concatenate([x1 * cos - x2 * sin, x2 * cos + x1 * sin], axis=-1)
    return jnp.concatenate([rot.astype(x.dtype), x[..., ROT_DIM:]], axis=-1)


def sliding_window_gqa_sinks(q, k, v, sinks):
    B, L = q.shape[0], q.shape[1]
    nb = L // BLOCK
    qb = q.reshape(B, nb, BLOCK, N_KV_HEADS, GROUP, HEAD_DIM)
    pad = jnp.zeros((B, BLOCK, N_KV_HEADS, HEAD_DIM), k.dtype)
    kp = jnp.concatenate([pad, k], axis=1)
    vp = jnp.concatenate([pad, v], axis=1)
    shp = (B, nb, BLOCK, N_KV_HEADS, HEAD_DIM)
    kb = jnp.concatenate([kp[:, :-BLOCK].reshape(shp), kp[:, BLOCK:].reshape(shp)], axis=2)
    vb = jnp.concatenate([vp[:, :-BLOCK].reshape(shp), vp[:, BLOCK:].reshape(shp)], axis=2)
    s = jnp.einsum('bnqhgd,bnkhd->bnhgqk', qb, kb).astype(jnp.float32) / math.sqrt(HEAD_DIM)
    qi = jnp.arange(BLOCK)[:, None]
    kj = jnp.arange(2 * BLOCK)[None, :]
    rel = qi + BLOCK - kj
    kpos = jnp.arange(nb)[:, None, None] * BLOCK - BLOCK + kj[None]
    mask = (rel >= 0)[None] & (rel < WINDOW)[None] & (kpos >= 0)
    s = jnp.where(mask[None, :, None, None], s, jnp.finfo(jnp.float32).min)
    sink = jnp.broadcast_to(
        sinks.astype(jnp.float32).reshape(N_KV_HEADS, GROUP)[None, None, :, :, None, None],
        s.shape[:-1] + (1,))
    p = jax.nn.softmax(jnp.concatenate([s, sink], axis=-1), axis=-1)[..., :-1]
    o = jnp.einsum('bnhgqk,bnkhd->bnqhgd', p.astype(v.dtype), vb)
    return o.reshape(B, L, Q_W)


def s5_mixer(u, lam_re, lam_im, log_dt, b_re, b_im, c_re, c_im, d_skip, w_glu):
    B, L = u.shape[0], u.shape[1]
    f32 = jnp.float32
    ug = u.reshape(B, L, SSM_G, SSM_GC).astype(f32)
    lr, li = lam_re.astype(f32), lam_im.astype(f32)
    dt = jnp.exp(log_dt.astype(f32))[:, None]
    mag = jnp.exp(lr * dt)
    a_re, a_im = mag * jnp.cos(li * dt), mag * jnp.sin(li * dt)
    den = lr * lr + li * li
    nr, ni = a_re - 1.0, a_im
    coef_re = (nr * lr + ni * li) / den
    coef_im = (ni * lr - nr * li) / den
    br, bi = b_re.astype(f32), b_im.astype(f32)
    bb_re = coef_re[..., None] * br - coef_im[..., None] * bi
    bb_im = coef_re[..., None] * bi + coef_im[..., None] * br
    bu_re = jnp.einsum('blgc,gpc->blgp', ug, bb_re)
    bu_im = jnp.einsum('blgc,gpc->blgp', ug, bb_im)
    at_re = jnp.broadcast_to(a_re, bu_re.shape)
    at_im = jnp.broadcast_to(a_im, bu_im.shape)

    def combine(e1, e2):
        ar1, ai1, br1, bi1 = e1
        ar2, ai2, br2, bi2 = e2
        return (ar2 * ar1 - ai2 * ai1,
                ar2 * ai1 + ai2 * ar1,
                ar2 * br1 - ai2 * bi1 + br2,
                ar2 * bi1 + ai2 * br1 + bi2)

    _, _, xs_re, xs_im = lax.associative_scan(combine, (at_re, at_im, bu_re, bu_im), axis=1)
    y = (jnp.einsum('blgp,gcp->blgc', xs_re, c_re.astype(f32))
         - jnp.einsum('blgp,gcp->blgc', xs_im, c_im.astype(f32))
         + d_skip.astype(f32)[None, None] * ug)
    y = jax.nn.gelu(y.reshape(B, L, SSM_W)).astype(u.dtype)
    zg = y @ w_glu
    return zg[..., :SSM_W] * jax.nn.sigmoid(zg[..., SSM_W:])


def setup_inputs(seed: int = 0) -> dict:
    key = jax.random.key(seed)
    ks = jax.random.split(key, 24)
    f32 = jnp.float32
    nrm = lambda k, shp, s: jax.random.normal(k, shp, f32) * s
    x = jax.random.normal(ks[0], (BATCH, SEQ, D_MODEL), f32)
    gain = lambda k: 1.0 + nrm(k, (DEPTH, D_MODEL), 0.02)
    n_idx = jnp.arange(SSM_P, dtype=f32)
    lam_re = -0.5 + nrm(ks[9], (DEPTH, SSM_G, SSM_P), 0.01)
    lam_im = jnp.pi * n_idx[None, None, :] + nrm(ks[10], (DEPTH, SSM_G, SSM_P), 0.01)
    log_dt = jax.random.uniform(ks[11], (DEPTH, SSM_G), f32, math.log(1e-3), math.log(1e-1))
    return {
        "x": x,
        "norm_mix_pre": gain(ks[1]),
        "norm_mix_post": gain(ks[2]),
        "norm_mlp_pre": gain(ks[3]),
        "norm_mlp_post": gain(ks[4]),
        "w_in": nrm(ks[5], (DEPTH, D_MODEL, IN_W), D_MODEL ** -0.5),
        "sinks": nrm(ks[6], (DEPTH, N_Q_HEADS), 0.5),
        "lam_re": lam_re,
        "lam_im": lam_im,
        "log_dt": log_dt,
        "b_re": nrm(ks[12], (DEPTH, SSM_G, SSM_P, SSM_GC), (2 * SSM_GC) ** -0.5),
        "b_im": nrm(ks[13], (DEPTH, SSM_G, SSM_P, SSM_GC), (2 * SSM_GC) ** -0.5),
        "c_re": nrm(ks[14], (DEPTH, SSM_G, SSM_GC, SSM_P), (2 * SSM_P) ** -0.5),
        "c_im": nrm(ks[15], (DEPTH, SSM_G, SSM_GC, SSM_P), (2 * SSM_P) ** -0.5),
        "d_skip": nrm(ks[16], (DEPTH, SSM_G, SSM_GC), 1.0),
        "w_glu": nrm(ks[17], (DEPTH, SSM_W, 2 * SSM_W), SSM_W ** -0.5),
        "w_branch": nrm(ks[18], (DEPTH, MIX_W, D_MODEL), (MIX_W // 2) ** -0.5),
        "w_out": nrm(ks[19], (DEPTH, D_MODEL, D_MODEL), D_MODEL ** -0.5),
        "w_up": nrm(ks[20], (DEPTH, D_MODEL, D_FF), D_MODEL ** -0.5),
        "w_down": nrm(ks[21], (DEPTH, D_FF, D_MODEL), D_FF ** -0.5),
    }


def reference(x, norm_mix_pre, norm_mix_post, norm_mlp_pre, norm_mlp_post, w_in, sinks,
              lam_re, lam_im, log_dt, b_re, b_im, c_re, c_im, d_skip, w_glu,
              w_branch, w_out, w_up, w_down):
    B, L, _ = x.shape
    pos = jnp.arange(L)
    o1 = Q_W
    o2 = o1 + KV_W
    o3 = o2 + KV_W
    o4 = o3 + SSM_W
    o5 = o4 + D_MODEL
    for l in range(DEPTH):
        h = rms_norm(x, norm_mix_pre[l])
        z = h @ w_in[l]
        q = z[..., :o1].reshape(B, L, N_Q_HEADS, HEAD_DIM)
        k = z[..., o1:o2].reshape(B, L, N_KV_HEADS, HEAD_DIM)
        v = z[..., o2:o3].reshape(B, L, N_KV_HEADS, HEAD_DIM)
        u = z[..., o3:o4]
        g_attn = jax.nn.sigmoid(z[..., o4:o5].astype(jnp.float32)).astype(x.dtype)
        g_ssm = jax.nn.sigmoid(z[..., o5:].astype(jnp.float32)).astype(x.dtype)
        q = partial_rope(q, pos)
        k = partial_rope(k, pos)
        o_attn = sliding_window_gqa_sinks(q, k, v, sinks[l])
        o_ssm = s5_mixer(u, lam_re[l], lam_im[l], log_dt[l], b_re[l], b_im[l],
                         c_re[l], c_im[l], d_skip[l], w_glu[l])
        y_attn = o_attn @ w_branch[l][:Q_W]
        y_ssm = o_ssm @ w_branch[l][Q_W:]
        mixed = (g_attn * y_attn + g_ssm * y_ssm) @ w_out[l]
        x = x + rms_norm(mixed, norm_mix_post[l])
        h2 = rms_norm(x, norm_mlp_pre[l])
        a = jax.nn.relu(h2 @ w_up[l])
        x = x + rms_norm((a * a) @ w_down[l], norm_mlp_post[l])
    return x
```

```python
import functools
import math

import jax
import jax.numpy as jnp
from jax import lax
from jax.experimental import pallas as pl
from jax.experimental.pallas import tpu as pltpu

D_MODEL = 2048
SEQ = 8192
HEAD_DIM = 64
N_Q_HEADS = 16
N_KV_HEADS = 2
BLOCK = 128
ROT_DIM = HEAD_DIM // 4
ROPE_THETA = 500000.0
Q_W = N_Q_HEADS * HEAD_DIM
KV_W = N_KV_HEADS * HEAD_DIM
SSM_W = D_MODEL // 2
SSM_GC = 16
SSM_G = SSM_W // SSM_GC
SSM_P = 64
D_FF = 4 * D_MODEL
EPS = 1e-6

LANES = 128
CHUNK = 16
N_CHUNKS = SEQ // CHUNK
SG = SSM_W // LANES
GPS = LANES // SSM_GC
SG_IN = CHUNK * LANES
SG_STATE = 2 * GPS * SSM_P
VMEM_LIMIT = 56 * 1024 * 1024

bf16 = jnp.bfloat16
f32 = jnp.float32


def _dot(a, b):
    return jnp.dot(a, b, preferred_element_type=f32)


def _sigmoid(x):
    return 1.0 / (1.0 + jnp.exp(-x))


def _gelu_tanh(x):
    c = math.sqrt(2.0 / math.pi)
    return x * (0.5 * (1.0 + jnp.tanh(c * (x + 0.044715 * (x * x * x)))))


def _rms_norm(x, g):
    return x * lax.rsqrt(jnp.mean(x * x, axis=-1, keepdims=True) + EPS) * g


def _params(sem):
    return pltpu.CompilerParams(dimension_semantics=sem, vmem_limit_bytes=VMEM_LIMIT)


def _const_spec(shape):
    nd = len(shape)
    return pl.BlockSpec(shape, lambda *_: (0,) * nd, pipeline_mode=pl.Buffered(1))


def _rope_block(z, cos, sin_a, sin_b):
    return (z * cos + pltpu.roll(z, LANES - ROT_DIM // 2, axis=1) * sin_a
            + pltpu.roll(z, ROT_DIM // 2, axis=1) * sin_b)


def _inproj_kernel(x_ref, g_ref, wqkv_ref, wu_ref, cos_ref, sa_ref, sb_ref,
                   q_ref, k_ref, v_ref, u2_ref, h_ref, zu_ref):
    tm = x_ref.shape[0]
    h = _rms_norm(x_ref[...], g_ref[...]).astype(bf16)
    h_ref[...] = h
    cos, sin_a, sin_b = cos_ref[...], sa_ref[...], sb_ref[...]
    for half in range(2):
        zq = _dot(h, wqkv_ref[:, half * 512:(half + 1) * 512])
        for b in range(4):
            blk = zq[:, b * LANES:(b + 1) * LANES]
            col = (half * 4 + b) * LANES
            q_ref[:, col:col + LANES] = _rope_block(blk, cos, sin_a, sin_b).astype(bf16)
    zkv = _dot(h, wqkv_ref[:, Q_W:Q_W + 2 * KV_W])
    k_ref[...] = _rope_block(zkv[:, :KV_W], cos, sin_a, sin_b).astype(bf16)
    v_ref[...] = zkv[:, KV_W:].astype(bf16)
    zu = _dot(h, wu_ref[...])
    for sg in range(SG):
        zu_ref[sg] = zu[:, sg * LANES:(sg + 1) * LANES]
        for s in range(CHUNK):
            col = sg * SG_IN + s * LANES
            u2_ref[:, col:col + LANES] = zu_ref[sg, pl.ds(s, tm // CHUNK, stride=CHUNK), :].astype(bf16)


def _inproj(x, gain, wqkv, wu, cos, sin_a, sin_b, tm=512):
    L = x.shape[0]
    row = lambda w: pl.BlockSpec((tm, w), lambda i: (i, 0))
    return pl.pallas_call(
        _inproj_kernel,
        grid=(L // tm,),
        in_specs=[row(D_MODEL), _const_spec((1, D_MODEL)), _const_spec(wqkv.shape),
                  _const_spec(wu.shape), row(LANES), row(LANES), row(LANES)],
        out_specs=[row(Q_W), row(KV_W), row(KV_W),
                   pl.BlockSpec((tm // CHUNK, SG * SG_IN), lambda i: (i, 0)), row(D_MODEL)],
        out_shape=[jax.ShapeDtypeStruct((L, Q_W), bf16), jax.ShapeDtypeStruct((L, KV_W), bf16),
                   jax.ShapeDtypeStruct((L, KV_W), bf16),
                   jax.ShapeDtypeStruct((L // CHUNK, SG * SG_IN), bf16),
                   jax.ShapeDtypeStruct((L, D_MODEL), bf16)],
        scratch_shapes=[pltpu.VMEM((SG, tm, LANES), f32)],
        compiler_params=_params(("parallel",)),
        name="inproj",
    )(x, gain, wqkv, wu, cos, sin_a, sin_b)


def _gates_kernel(h_ref, w_ref, o_ref):
    o_ref[...] = _sigmoid(_dot(h_ref[...], w_ref[...])).astype(bf16)


def _gates(h, wg, tm=1024, tn=1024):
    L, N = h.shape[0], wg.shape[1]
    return pl.pallas_call(
        _gates_kernel,
        grid=(L // tm, N // tn),
        in_specs=[pl.BlockSpec((tm, D_MODEL), lambda i, j: (i, 0)),
                  pl.BlockSpec((D_MODEL, tn), lambda i, j: (0, j))],
        out_specs=pl.BlockSpec((tm, tn), lambda i, j: (i, j)),
        out_shape=jax.ShapeDtypeStruct((L, N), bf16),
        compiler_params=_params(("parallel", "arbitrary")),
        name="gates",
    )(h, wg)


def _attn_kernel(sinks_ref, q_ref, kp_ref, kc_ref, vp_ref, vc_ref, o_ref):
    n = pl.program_id(0)
    q = q_ref[...] * jnp.asarray(1.0 / math.sqrt(HEAD_DIM), bf16)
    kcat = jnp.concatenate([kp_ref[...], kc_ref[...]], axis=0).astype(f32)
    vcat = jnp.concatenate([vp_ref[...], vc_ref[...]], axis=0).astype(f32)
    k_sw = pltpu.roll(kcat, HEAD_DIM, axis=1)
    v_sw = pltpu.roll(vcat, HEAD_DIM, axis=1)
    lane_kv = lax.broadcasted_iota(jnp.int32, kcat.shape, 1)
    low_kv = lane_kv < HEAD_DIM
    qi = lax.broadcasted_iota(jnp.int32, (BLOCK, 2 * BLOCK), 0)
    kj = lax.broadcasted_iota(jnp.int32, (BLOCK, 2 * BLOCK), 1)
    first_key = jnp.where(n > 0, 0, BLOCK)
    allowed = (kj > qi) & (kj <= qi + BLOCK) & (kj >= first_key)
    low_o = lax.broadcasted_iota(jnp.int32, (BLOCK, LANES), 1) < HEAD_DIM
    neg = jnp.finfo(f32).min
    contract_lanes = (((1,), (1,)), ((), ()))
    pairs_per_kv = (N_Q_HEADS // N_KV_HEADS) // 2
    for hk in range(N_KV_HEADS):
        k_src, k_oth = (kcat, k_sw) if hk == 0 else (k_sw, kcat)
        k_lo = jnp.where(low_kv, k_src, 0.0).astype(bf16)
        k_hi = jnp.where(low_kv, 0.0, k_oth).astype(bf16)
        vv = (jnp.where(low_kv, vcat, v_sw) if hk == 0 else jnp.where(low_kv, v_sw, vcat)).astype(bf16)
        qs = jnp.concatenate(
            [q[:, (pairs_per_kv * hk + b) * LANES:(pairs_per_kv * hk + b + 1) * LANES]
             for b in range(pairs_per_kv)], axis=0)
        s_par = (lax.dot_general(qs, k_lo, contract_lanes, preferred_element_type=f32),
                 lax.dot_general(qs, k_hi, contract_lanes, preferred_element_type=f32))
        for b in range(pairs_per_kv):
            outs = []
            for par in range(2):
                head = 2 * (pairs_per_kv * hk + b) + par
                s = jnp.where(allowed, s_par[par][b * BLOCK:(b + 1) * BLOCK], neg)
                sink = sinks_ref[head]
                m = jnp.maximum(jnp.max(s, axis=1, keepdims=True), sink)
                p = jnp.exp(s - m)
                denom = jnp.sum(p, axis=1, keepdims=True) + jnp.exp(sink - m)
                outs.append(_dot(p.astype(bf16), vv) * (1.0 / denom))
            col = (pairs_per_kv * hk + b) * LANES
            o_ref[:, col:col + LANES] = jnp.where(low_o, outs[0], outs[1]).astype(bf16)


def _attention(q, k, v, sinks):
    L = q.shape[0]
    cur = lambda n: (n, 0)
    prev = lambda n: (jnp.maximum(n - 1, 0), 0)
    return pl.pallas_call(
        _attn_kernel,
        grid=(L // BLOCK,),
        in_specs=[pl.BlockSpec(memory_space=pltpu.SMEM),
                  pl.BlockSpec((BLOCK, Q_W), cur),
                  pl.BlockSpec((BLOCK, KV_W), prev), pl.BlockSpec((BLOCK, KV_W), cur),
                  pl.BlockSpec((BLOCK, KV_W), prev), pl.BlockSpec((BLOCK, KV_W), cur)],
        out_specs=pl.BlockSpec((BLOCK, Q_W), cur),
        out_shape=jax.ShapeDtypeStruct((L, Q_W), bf16),
        compiler_params=_params(("parallel",)),
        name="swa_attention",
    )(sinks, q, k, k, v, v)


def _ssm_state_kernel(u_ref, p_ref, ar_ref, ai_ref, xs_ref, s_ref, xf_ref):
    s_ref[...] = _dot(u_ref[...], p_ref[...])
    ar, ai = ar_ref[...], ai_ref[...]
    half = SG_STATE // 2
    unroll = 8

    def body(i, carry):
        xr, xi = carry
        base = pl.multiple_of(i * unroll, unroll)
        s_tile = s_ref[pl.ds(base, unroll), :]
        rows = []
        for r in range(unroll):
            rows.append(jnp.concatenate([xr, xi], axis=1))
            sr, si = s_tile[r:r + 1, :half], s_tile[r:r + 1, half:]
            xr, xi = ar * xr - ai * xi + sr, ar * xi + ai * xr + si
        xf_ref[pl.ds(base, unroll), :] = jnp.concatenate(rows, axis=0)
        return xr, xi

    zero = jnp.zeros((1, half), f32)
    lax.fori_loop(0, N_CHUNKS // unroll, body, (zero, zero))
    xs_ref[...] = xf_ref[...].astype(bf16)


def _ssm_states(u2, p_op, at_re, at_im):
    return pl.pallas_call(
        _ssm_state_kernel,
        grid=(SG,),
        in_specs=[pl.BlockSpec((N_CHUNKS, SG_IN), lambda g: (0, g)),
                  pl.BlockSpec((None, SG_IN, SG_STATE), lambda g: (g, 0, 0)),
                  pl.BlockSpec((None, 1, SG_STATE // 2), lambda g: (g, 0, 0)),
                  pl.BlockSpec((None, 1, SG_STATE // 2), lambda g: (g, 0, 0))],
        out_specs=pl.BlockSpec((N_CHUNKS, SG_STATE), lambda g: (0, g)),
        out_shape=jax.ShapeDtypeStruct((N_CHUNKS, SG * SG_STATE), bf16),
        scratch_shapes=[pltpu.VMEM((N_CHUNKS, SG_STATE), f32), pltpu.VMEM((N_CHUNKS, SG_STATE), f32)],
        compiler_params=_params(("parallel",)),
        name="ssm_states",
    )(u2, p_op, at_re, at_im)


def _ssm_out_kernel(u_ref, m_ref, xs_ref, q_ref, y_ref):
    y = _dot(u_ref[...], m_ref[...]) + _dot(xs_ref[...], q_ref[...])
    y_ref[...] = _gelu_tanh(y).astype(bf16)


def _ssm_out(u2, m_op, xs, q_op, tn=512):
    nt = SG_IN // tn
    return pl.pallas_call(
        _ssm_out_kernel,
        grid=(SG, nt),
        in_specs=[pl.BlockSpec((N_CHUNKS, SG_IN), lambda g, c: (0, g)),
                  pl.BlockSpec((None, SG_IN, tn), lambda g, c: (g, 0, c)),
                  pl.BlockSpec((N_CHUNKS, SG_STATE), lambda g, c: (0, g)),
                  pl.BlockSpec((None, SG_STATE, tn), lambda g, c: (g, 0, c))],
        out_specs=pl.BlockSpec((N_CHUNKS, tn), lambda g, c: (0, g * nt + c)),
        out_shape=jax.ShapeDtypeStruct((N_CHUNKS, SG * SG_IN), bf16),
        compiler_params=_params(("parallel", "arbitrary")),
        name="ssm_out",
    )(u2, m_op, xs, q_op)


def _ssm_operators(lam_re, lam_im, log_dt, b_re, b_im, c_re, c_im, d_skip):
    hi = lax.Precision.HIGHEST
    lr, li = lam_re.astype(f32), lam_im.astype(f32)
    dt = jnp.exp(log_dt.astype(f32))[:, None]
    mag = jnp.exp(lr * dt)
    a_re, a_im = mag * jnp.cos(li * dt), mag * jnp.sin(li * dt)
    den = lr * lr + li * li
    nr, ni = a_re - 1.0, a_im
    coef_re = (nr * lr + ni * li) / den
    coef_im = (ni * lr - nr * li) / den
    br, bi = b_re.astype(f32), b_im.astype(f32)
    bb_re = coef_re[..., None] * br - coef_im[..., None] * bi
    bb_im = coef_re[..., None] * bi + coef_im[..., None] * br
    cr, ci = c_re.astype(f32), c_im.astype(f32)

    pw_re, pw_im = [jnp.ones_like(a_re)], [jnp.zeros_like(a_im)]
    for _ in range(CHUNK):
        pr, pi = pw_re[-1], pw_im[-1]
        pw_re.append(pr * a_re - pi * a_im)
        pw_im.append(pr * a_im + pi * a_re)
    pw_re, pw_im = jnp.stack(pw_re), jnp.stack(pw_im)

    ab_re = pw_re[..., None] * bb_re[None] - pw_im[..., None] * bb_im[None]
    ab_im = pw_re[..., None] * bb_im[None] + pw_im[..., None] * bb_re[None]
    kern = (jnp.einsum('gop,tgpc->tgco', cr, ab_re[:CHUNK], precision=hi)
            - jnp.einsum('gop,tgpc->tgco', ci, ab_im[:CHUNK], precision=hi))
    kern = kern.at[0].add(d_skip.astype(f32)[:, :, None] * jnp.eye(SSM_GC, dtype=f32)[None])
    eye_g = jnp.eye(GPS, dtype=f32)

    def expand(blocks):
        lead = blocks.shape[:-3]
        a, b = blocks.shape[-2:]
        blk = blocks.reshape(lead + (SG, GPS, a, b))
        out = blk[..., :, :, None, :] * eye_g[:, None, :, None]
        return out.reshape(lead + (SG, GPS * a, GPS * b))

    m_tau = expand(kern)
    s_idx = jnp.arange(CHUNK)[:, None]
    t_idx = jnp.arange(CHUNK)[None, :]
    lag = t_idx - s_idx
    m_full = jnp.where((lag >= 0)[:, :, None, None, None], m_tau[jnp.clip(lag, 0, CHUNK - 1)], 0.0)
    m_op = m_full.transpose(2, 0, 3, 1, 4).reshape(SG, SG_IN, SG_IN).astype(bf16)

    rev = ab_re[CHUNK - 1 - jnp.arange(CHUNK)], ab_im[CHUNK - 1 - jnp.arange(CHUNK)]
    p_parts = [expand(r.transpose(0, 1, 3, 2)) for r in rev]
    p_op = jnp.concatenate(p_parts, axis=-1)
    p_op = p_op.transpose(1, 0, 2, 3).reshape(SG, SG_IN, SG_STATE).astype(bf16)

    ca_re = cr[None] * pw_re[1:, :, None, :] - ci[None] * pw_im[1:, :, None, :]
    ca_im = cr[None] * pw_im[1:, :, None, :] + ci[None] * pw_re[1:, :, None, :]
    q_parts = [expand(r.transpose(0, 1, 3, 2)) for r in (ca_re, -ca_im)]
    q_op = jnp.concatenate(q_parts, axis=-2)
    q_op = q_op.transpose(1, 2, 0, 3).reshape(SG, SG_STATE, SG_IN).astype(bf16)

    at_re = pw_re[CHUNK].reshape(SG, 1, GPS * SSM_P)
    at_im = pw_im[CHUNK].reshape(SG, 1, GPS * SSM_P)
    return m_op, p_op, q_op, at_re, at_im


def _mix_kernel(oa_ref, y2_ref, ga_ref, gs_ref, wglu_ref, wb_ref, m_ref, yb_ref):
    tm = oa_ref.shape[0]
    for sg in range(SG):
        for t in range(CHUNK):
            col = sg * SG_IN + t * LANES
            yb_ref[sg, pl.ds(t, tm // CHUNK, stride=CHUNK), :] = y2_ref[:, col:col + LANES].astype(f32)
    y = jnp.concatenate([yb_ref[sg] for sg in range(SG)], axis=1).astype(bf16)
    zg = _dot(y, wglu_ref[...])
    o_ssm = zg[:, :SSM_W] * _sigmoid(zg[:, SSM_W:])
    y_s = _dot(o_ssm.astype(bf16), wb_ref[Q_W:, :])
    y_a = _dot(oa_ref[...], wb_ref[:Q_W, :])
    m_ref[...] = (ga_ref[...].astype(f32) * y_a + gs_ref[...].astype(f32) * y_s).astype(bf16)


def _mix(o_attn, y2, gates, wglu, wb, tm=512):
    L = o_attn.shape[0]
    return pl.pallas_call(
        _mix_kernel,
        grid=(L // tm,),
        in_specs=[pl.BlockSpec((tm, Q_W), lambda i: (i, 0)),
                  pl.BlockSpec((tm // CHUNK, SG * SG_IN), lambda i: (i, 0)),
                  pl.BlockSpec((tm, D_MODEL), lambda i: (i, 0)),
                  pl.BlockSpec((tm, D_MODEL), lambda i: (i, 1)),
                  _const_spec(wglu.shape), _const_spec(wb.shape)],
        out_specs=pl.BlockSpec((tm, D_MODEL), lambda i: (i, 0)),
        out_shape=jax.ShapeDtypeStruct((L, D_MODEL), bf16),
        scratch_shapes=[pltpu.VMEM((SG, tm, LANES), f32)],
        compiler_params=_params(("parallel",)),
        name="mix",
    )(o_attn, y2, gates, gates, wglu, wb)


def _outproj_kernel(m_ref, w_ref, x_ref, gpost_ref, gpre_ref, x1_ref, h2_ref):
    out = _dot(m_ref[...], w_ref[...])
    x1 = x_ref[...] + _rms_norm(out, gpost_ref[...])
    x1_ref[...] = x1
    h2_ref[...] = _rms_norm(x1, gpre_ref[...]).astype(bf16)


def _outproj(m, w_out, x, g_post, g_pre, tm=512):
    L = x.shape[0]
    row = lambda: pl.BlockSpec((tm, D_MODEL), lambda i: (i, 0))
    return pl.pallas_call(
        _outproj_kernel,
        grid=(L // tm,),
        in_specs=[row(), _const_spec(w_out.shape), row(),
                  _const_spec((1, D_MODEL)), _const_spec((1, D_MODEL))],
        out_specs=[row(), row()],
        out_shape=[jax.ShapeDtypeStruct((L, D_MODEL), f32), jax.ShapeDtypeStruct((L, D_MODEL), bf16)],
        compiler_params=_params(("parallel",)),
        name="outproj",
    )(m, w_out, x, g_post, g_pre)


def _mlp_kernel(h_ref, wu_ref, wd_ref, x_ref, g_ref, o_ref, acc_ref):
    j = pl.program_id(1)
    a = jnp.maximum(_dot(h_ref[...], wu_ref[...]), 0.0)
    part = _dot((a * a).astype(bf16), wd_ref[...])

    @pl.when(j == 0)
    def _():
        acc_ref[...] = part

    @pl.when(j > 0)
    def _():
        acc_ref[...] += part

    @pl.when(j == pl.num_programs(1) - 1)
    def _():
        o_ref[...] = x_ref[...] + _rms_norm(acc_ref[...], g_ref[...])


def _mlp(h2, w_up, w_down, x1, g_post, tm=512, tf=512):
    L = x1.shape[0]
    return pl.pallas_call(
        _mlp_kernel,
        grid=(L // tm, D_FF // tf),
        in_specs=[pl.BlockSpec((tm, D_MODEL), lambda i, j: (i, 0)),
                  pl.BlockSpec((D_MODEL, tf), lambda i, j: (0, j)),
                  pl.BlockSpec((tf, D_MODEL), lambda i, j: (j, 0)),
                  pl.BlockSpec((tm, D_MODEL), lambda i, j: (i, 0)),
                  pl.BlockSpec((1, D_MODEL), lambda i, j: (0, 0))],
        out_specs=pl.BlockSpec((tm, D_MODEL), lambda i, j: (i, 0)),
        out_shape=jax.ShapeDtypeStruct((L, D_MODEL), f32),
        scratch_shapes=[pltpu.VMEM((tm, D_MODEL), f32)],
        compiler_params=_params(("parallel", "arbitrary")),
        name="mlp",
    )(h2, w_up, w_down, x1, g_post)


def _rope_tables(L):
    half = ROT_DIM // 2
    inv = ROPE_THETA ** (-jnp.arange(half, dtype=f32) * 2.0 / ROT_DIM)
    ang = jnp.arange(L).astype(f32)[:, None] * inv[None, :]
    cos, sin = jnp.cos(ang), jnp.sin(ang)
    ones = jnp.ones((L, HEAD_DIM - ROT_DIM), f32)
    zeros = jnp.zeros((L, HEAD_DIM - ROT_DIM), f32)
    zh = jnp.zeros((L, half), f32)
    per_head = lambda parts: jnp.tile(jnp.concatenate(parts, axis=1), (1, LANES // HEAD_DIM))
    return (per_head([cos, cos, ones]), per_head([-sin, zh, zeros]), per_head([zh, sin, zeros]))


def _layer(x, norm_mix_pre, norm_mix_post, norm_mlp_pre, norm_mlp_post, w_in, sinks,
           lam_re, lam_im, log_dt, b_re, b_im, c_re, c_im, d_skip, w_glu, w_branch, w_out,
           w_up, w_down, rope):
    o3 = Q_W + 2 * KV_W
    o4 = o3 + SSM_W
    gain = lambda g: g.astype(f32).reshape(1, D_MODEL)
    q, k, v, u2, h = _inproj(x, gain(norm_mix_pre), w_in[:, :o3].astype(bf16),
                             w_in[:, o3:o4].astype(bf16), *rope)
    gates = _gates(h, w_in[:, o4:].astype(bf16))
    o_attn = _attention(q, k, v, sinks.astype(f32))
    m_op, p_op, q_op, at_re, at_im = _ssm_operators(lam_re, lam_im, log_dt, b_re, b_im, c_re, c_im, d_skip)
    xs = _ssm_states(u2, p_op, at_re, at_im)
    y2 = _ssm_out(u2, m_op, xs, q_op)
    mixed = _mix(o_attn, y2, gates, w_glu.astype(bf16), w_branch.astype(bf16))
    x1, h2 = _outproj(mixed, w_out.astype(bf16), x, gain(norm_mix_post), gain(norm_mlp_pre))
    return _mlp(h2, w_up.astype(bf16), w_down.astype(bf16), x1, gain(norm_mlp_post))


def kernel(x, norm_mix_pre, norm_mix_post, norm_mlp_pre, norm_mlp_post, w_in, sinks, lam_re, lam_im, log_dt, b_re, b_im, c_re, c_im, d_skip, w_glu, w_branch, w_out, w_up, w_down):
    B, L, _ = x.shape
    depth = w_in.shape[0]
    rope = _rope_tables(L)
    outs = []
    for b in range(B):
        xb = x[b]
        for l in range(depth):
            xb = _layer(xb, norm_mix_pre[l], norm_mix_post[l], norm_mlp_pre[l], norm_mlp_post[l],
                        w_in[l], sinks[l], lam_re[l], lam_im[l], log_dt[l], b_re[l], b_im[l],
                        c_re[l], c_im[l], d_skip[l], w_glu[l], w_branch[l], w_out[l],
                        w_up[l], w_down[l], rope)
        outs.append(xb)
    return jnp.stack(outs)
```

```python
import math

import jax
import jax.numpy as jnp
from jax import lax
from jax.experimental import pallas as pl
from jax.experimental.pallas import tpu as pltpu

D_MODEL = 2048
SEQ = 8192
HEAD_DIM = 64
N_Q_HEADS = 16
N_KV_HEADS = 2
BLOCK = 128
ROT_DIM = HEAD_DIM // 4
ROPE_THETA = 500000.0
Q_W = N_Q_HEADS * HEAD_DIM
KV_W = N_KV_HEADS * HEAD_DIM
SSM_W = D_MODEL // 2
SSM_GC = 16
SSM_G = SSM_W // SSM_GC
SSM_P = 64
D_FF = 4 * D_MODEL
EPS = 1e-6

LANES = 128
CHUNK = 16
N_CHUNKS = SEQ // CHUNK
SG = SSM_W // LANES
GPS = LANES // SSM_GC
SG_IN = CHUNK * LANES
SG_STATE = 2 * GPS * SSM_P
VMEM_LIMIT = 56 * 1024 * 1024

bf16 = jnp.bfloat16
f32 = jnp.float32


def _dot(a, b):
    return jnp.dot(a, b, preferred_element_type=f32)


def _sigmoid(x):
    return 1.0 / (1.0 + jnp.exp(-x))


def _gelu_tanh(x):
    c = math.sqrt(2.0 / math.pi)
    return x * (0.5 * (1.0 + jnp.tanh(c * (x + 0.044715 * (x * x * x)))))


def _rms_norm(x, g):
    return x * lax.rsqrt(jnp.mean(x * x, axis=-1, keepdims=True) + EPS) * g


def _params(sem):
    return pltpu.CompilerParams(dimension_semantics=sem, vmem_limit_bytes=VMEM_LIMIT)


def _const_spec(shape):
    nd = len(shape)
    return pl.BlockSpec(shape, lambda *_: (0,) * nd, pipeline_mode=pl.Buffered(1))


def _rope_block(z, cos, sin_a, sin_b):
    return (z * cos + pltpu.roll(z, LANES - ROT_DIM // 2, axis=1) * sin_a
            + pltpu.roll(z, ROT_DIM // 2, axis=1) * sin_b)


def _inproj_kernel(x_ref, g_ref, wqkv_ref, wu_ref, cos_ref, sa_ref, sb_ref,
                   q_ref, k_ref, v_ref, u2_ref, h_ref, zu_ref):
    tm = x_ref.shape[0]
    h = _rms_norm(x_ref[...], g_ref[...]).astype(bf16)
    h_ref[...] = h
    cos, sin_a, sin_b = cos_ref[...], sa_ref[...], sb_ref[...]
    for half in range(2):
        zq = _dot(h, wqkv_ref[:, half * 512:(half + 1) * 512])
        for b in range(4):
            blk = zq[:, b * LANES:(b + 1) * LANES]
            col = (half * 4 + b) * LANES
            q_ref[:, col:col + LANES] = _rope_block(blk, cos, sin_a, sin_b).astype(bf16)
    zkv = _dot(h, wqkv_ref[:, Q_W:Q_W + 2 * KV_W])
    k_ref[...] = _rope_block(zkv[:, :KV_W], cos, sin_a, sin_b).astype(bf16)
    v_ref[...] = zkv[:, KV_W:].astype(bf16)
    zu = _dot(h, wu_ref[...])
    for sg in range(SG):
        zu_ref[sg] = zu[:, sg * LANES:(sg + 1) * LANES]
        for s in range(CHUNK):
            col = sg * SG_IN + s * LANES
            u2_ref[:, col:col + LANES] = zu_ref[sg, pl.ds(s, tm // CHUNK, stride=CHUNK), :].astype(bf16)


def _inproj(x, gain, wqkv, wu, cos, sin_a, sin_b, tm=512):
    L = x.shape[0]
    row = lambda w: pl.BlockSpec((tm, w), lambda i: (i, 0))
    return pl.pallas_call(
        _inproj_kernel,
        grid=(L // tm,),
        in_specs=[row(D_MODEL), _const_spec((1, D_MODEL)), _const_spec(wqkv.shape),
                  _const_spec(wu.shape), row(LANES), row(LANES), row(LANES)],
        out_specs=[row(Q_W), row(KV_W), row(KV_W),
                   pl.BlockSpec((tm // CHUNK, SG * SG_IN), lambda i: (i, 0)), row(D_MODEL)],
        out_shape=[jax.ShapeDtypeStruct((L, Q_W), bf16), jax.ShapeDtypeStruct((L, KV_W), bf16),
                   jax.ShapeDtypeStruct((L, KV_W), bf16),
                   jax.ShapeDtypeStruct((L // CHUNK, SG * SG_IN), bf16),
                   jax.ShapeDtypeStruct((L, D_MODEL), bf16)],
        scratch_shapes=[pltpu.VMEM((SG, tm, LANES), f32)],
        compiler_params=_params(("parallel",)),
        name="inproj",
    )(x, gain, wqkv, wu, cos, sin_a, sin_b)


def _gates_kernel(h_ref, w_ref, o_ref):
    o_ref[...] = _sigmoid(_dot(h_ref[...], w_ref[...])).astype(bf16)


def _gates(h, wg, tm=1024, tn=1024):
    L, N = h.shape[0], wg.shape[1]
    return pl.pallas_call(
        _gates_kernel,
        grid=(L // tm, N // tn),
        in_specs=[pl.BlockSpec((tm, D_MODEL), lambda i, j: (i, 0)),
                  pl.BlockSpec((D_MODEL, tn), lambda i, j: (0, j))],
        out_specs=pl.BlockSpec((tm, tn), lambda i, j: (i, j)),
        out_shape=jax.ShapeDtypeStruct((L, N), bf16),
        compiler_params=_params(("parallel", "arbitrary")),
        name="gates",
    )(h, wg)


def _attn_kernel(sinks_ref, q_ref, kp_ref, kc_ref, vp_ref, vc_ref, o_ref):
    n = pl.program_id(0)
    q = q_ref[...] * jnp.asarray(1.0 / math.sqrt(HEAD_DIM), bf16)
    kcat = jnp.concatenate([kp_ref[...], kc_ref[...]], axis=0).astype(f32)
    vcat = jnp.concatenate([vp_ref[...], vc_ref[...]], axis=0).astype(f32)
    k_sw = pltpu.roll(kcat, HEAD_DIM, axis=1)
    v_sw = pltpu.roll(vcat, HEAD_DIM, axis=1)
    lane_kv = lax.broadcasted_iota(jnp.int32, kcat.shape, 1)
    low_kv = lane_kv < HEAD_DIM
    qi = lax.broadcasted_iota(jnp.int32, (BLOCK, 2 * BLOCK), 0)
    kj = lax.broadcasted_iota(jnp.int32, (BLOCK, 2 * BLOCK), 1)
    first_key = jnp.where(n > 0, 0, BLOCK)
    allowed = (kj > qi) & (kj <= qi + BLOCK) & (kj >= first_key)
    low_o = lax.broadcasted_iota(jnp.int32, (BLOCK, LANES), 1) < HEAD_DIM
    neg = jnp.finfo(f32).min
    contract_lanes = (((1,), (1,)), ((), ()))
    pairs_per_kv = (N_Q_HEADS // N_KV_HEADS) // 2
    for hk in range(N_KV_HEADS):
        k_src, k_oth = (kcat, k_sw) if hk == 0 else (k_sw, kcat)
        k_lo = jnp.where(low_kv, k_src, 0.0).astype(bf16)
        k_hi = jnp.where(low_kv, 0.0, k_oth).astype(bf16)
        vv = (jnp.where(low_kv, vcat, v_sw) if hk == 0 else jnp.where(low_kv, v_sw, vcat)).astype(bf16)
        qs = jnp.concatenate(
            [q[:, (pairs_per_kv * hk + b) * LANES:(pairs_per_kv * hk + b + 1) * LANES]
             for b in range(pairs_per_kv)], axis=0)
        s_par = (lax.dot_general(qs, k_lo, contract_lanes, preferred_element_type=f32),
                 lax.dot_general(qs, k_hi, contract_lanes, preferred_element_type=f32))
        for b in range(pairs_per_kv):
            outs = []
            for par in range(2):
                head = 2 * (pairs_per_kv * hk + b) + par
                s = jnp.where(allowed, s_par[par][b * BLOCK:(b + 1) * BLOCK], neg)
                sink = sinks_ref[head]
                m = jnp.maximum(jnp.max(s, axis=1, keepdims=True), sink)
                p = jnp.exp(s - m)
                denom = jnp.sum(p, axis=1, keepdims=True) + jnp.exp(sink - m)
                outs.append(_dot(p.astype(bf16), vv) * (1.0 / denom))
            col = (pairs_per_kv * hk + b) * LANES
            o_ref[:, col:col + LANES] = jnp.where(low_o, outs[0], outs[1]).astype(bf16)


def _attention(q, k, v, sinks):
    L = q.shape[0]
    cur = lambda n: (n, 0)
    prev = lambda n: (jnp.maximum(n - 1, 0), 0)
    return pl.pallas_call(
        _attn_kernel,
        grid=(L // BLOCK,),
        in_specs=[pl.BlockSpec(memory_space=pltpu.SMEM),
                  pl.BlockSpec((BLOCK, Q_W), cur),
                  pl.BlockSpec((BLOCK, KV_W), prev), pl.BlockSpec((BLOCK, KV_W), cur),
                  pl.BlockSpec((BLOCK, KV_W), prev), pl.BlockSpec((BLOCK, KV_W), cur)],
        out_specs=pl.BlockSpec((BLOCK, Q_W), cur),
        out_shape=jax.ShapeDtypeStruct((L, Q_W), bf16),
        compiler_params=_params(("parallel",)),
        name="swa_attention",
    )(sinks, q, k, k, v, v)


GC_SHIFT = SSM_GC.bit_length() - 1
P_SHIFT = SSM_P.bit_length() - 1
Y_TILE = 512


def _ssm_kernel(u_ref, pre_ref, pim_ref, kt_ref, qc_ref, rep_ref, ar_ref, ai_ref, y_ref,
                p_scr, m_scr, q_scr, s_scr, xf_scr):
    half = SG_STATE // 2
    row = lax.broadcasted_iota(jnp.int32, (SG_IN, LANES), 0)
    lane = lax.broadcasted_iota(jnp.int32, (SG_IN, LANES), 1)
    row_grp = (row >> GC_SHIFT) & (GPS - 1)
    pre, pim = pre_ref[...], pim_ref[...]
    for k in range(half // LANES):
        sel = row_grp == 2 * k + (lane >> P_SHIFT)
        p_scr[:, k * LANES:(k + 1) * LANES] = jnp.where(sel, pre, 0.0).astype(bf16)
        p_scr[:, half + k * LANES:half + (k + 1) * LANES] = jnp.where(sel, pim, 0.0).astype(bf16)

    blk_row = lax.broadcasted_iota(jnp.int32, (LANES, LANES), 0) >> GC_SHIFT
    blk_lane = lax.broadcasted_iota(jnp.int32, (LANES, LANES), 1) >> GC_SHIFT
    zero_blk = jnp.zeros((LANES, LANES), bf16)
    for tau in range(CHUNK):
        blk = jnp.where(blk_row == blk_lane, kt_ref[tau * LANES:(tau + 1) * LANES, :], 0.0).astype(bf16)
        for s in range(CHUNK - tau):
            t = s + tau
            m_scr[s * LANES:(s + 1) * LANES, t * LANES:(t + 1) * LANES] = blk
            if tau > 0:
                m_scr[t * LANES:(t + 1) * LANES, s * LANES:(s + 1) * LANES] = zero_blk

    q_row_grp = (lax.broadcasted_iota(jnp.int32, (SG_STATE, Y_TILE), 0) >> P_SHIFT) & (GPS - 1)
    q_lane_grp = (lax.broadcasted_iota(jnp.int32, (SG_STATE, Y_TILE), 1) >> GC_SHIFT) & (GPS - 1)
    for j in range(SG_IN // Y_TILE):
        full = _dot(qc_ref[...], rep_ref[:, j * Y_TILE:(j + 1) * Y_TILE])
        q_scr[:, j * Y_TILE:(j + 1) * Y_TILE] = jnp.where(q_row_grp == q_lane_grp, full, 0.0).astype(bf16)

    s_scr[...] = _dot(u_ref[...], p_scr[...])
    ar, ai = ar_ref[...], ai_ref[...]
    unroll = 8

    def body(i, carry):
        xr, xi = carry
        base = pl.multiple_of(i * unroll, unroll)
        s_tile = s_scr[pl.ds(base, unroll), :]
        rows = []
        for r in range(unroll):
            rows.append(jnp.concatenate([xr, xi], axis=1))
            sr, si = s_tile[r:r + 1, :half], s_tile[r:r + 1, half:]
            xr, xi = ar * xr - ai * xi + sr, ar * xi + ai * xr + si
        xf_scr[pl.ds(base, unroll), :] = jnp.concatenate(rows, axis=0)
        return xr, xi

    zero = jnp.zeros((1, half), f32)
    lax.fori_loop(0, N_CHUNKS // unroll, body, (zero, zero))

    xs = xf_scr[...].astype(bf16)
    for j in range(SG_IN // Y_TILE):
        kk = (j + 1) * Y_TILE
        cols = slice(j * Y_TILE, (j + 1) * Y_TILE)
        y = _dot(u_ref[:, :kk], m_scr[:kk, cols]) + _dot(xs, q_scr[:, cols])
        y_ref[:, cols] = _gelu_tanh(y).astype(bf16)


def _ssm(u2, pre, pim, kt, qc, rep, at_re, at_im):
    per_sg = lambda r, c: pl.BlockSpec((None, r, c), lambda g: (g, 0, 0))
    return pl.pallas_call(
        _ssm_kernel,
        grid=(SG,),
        in_specs=[pl.BlockSpec((N_CHUNKS, SG_IN), lambda g: (0, g)),
                  per_sg(SG_IN, LANES), per_sg(SG_IN, LANES), per_sg(SG_IN, LANES),
                  per_sg(SG_STATE, CHUNK * SSM_GC), _const_spec(rep.shape),
                  per_sg(1, SG_STATE // 2), per_sg(1, SG_STATE // 2)],
        out_specs=pl.BlockSpec((N_CHUNKS, SG_IN), lambda g: (0, g)),
        out_shape=jax.ShapeDtypeStruct((N_CHUNKS, SG * SG_IN), bf16),
        scratch_shapes=[pltpu.VMEM((SG_IN, SG_STATE), bf16), pltpu.VMEM((SG_IN, SG_IN), bf16),
                        pltpu.VMEM((SG_STATE, SG_IN), bf16),
                        pltpu.VMEM((N_CHUNKS, SG_STATE), f32), pltpu.VMEM((N_CHUNKS, SG_STATE), f32)],
        compiler_params=_params(("arbitrary",)),
        name="ssm",
    )(u2, pre, pim, kt, qc, rep, at_re, at_im)


def _ssm_operators(lam_re, lam_im, log_dt, b_re, b_im, c_re, c_im, d_skip):
    hi = lax.Precision.HIGHEST
    lr, li = lam_re.astype(f32), lam_im.astype(f32)
    dt = jnp.exp(log_dt.astype(f32))[:, None]
    mag = jnp.exp(lr * dt)
    a_re, a_im = mag * jnp.cos(li * dt), mag * jnp.sin(li * dt)
    den = lr * lr + li * li
    nr, ni = a_re - 1.0, a_im
    coef_re = (nr * lr + ni * li) / den
    coef_im = (ni * lr - nr * li) / den
    br, bi = b_re.astype(f32), b_im.astype(f32)
    bb_re = coef_re[..., None] * br - coef_im[..., None] * bi
    bb_im = coef_re[..., None] * bi + coef_im[..., None] * br
    cr, ci = c_re.astype(f32), c_im.astype(f32)

    pw_re, pw_im = [jnp.ones_like(a_re)], [jnp.zeros_like(a_im)]
    for _ in range(CHUNK):
        pr, pi = pw_re[-1], pw_im[-1]
        pw_re.append(pr * a_re - pi * a_im)
        pw_im.append(pr * a_im + pi * a_re)
    pw_re, pw_im = jnp.stack(pw_re), jnp.stack(pw_im)

    ab_re = pw_re[:CHUNK, :, :, None] * bb_re[None] - pw_im[:CHUNK, :, :, None] * bb_im[None]
    ab_im = pw_re[:CHUNK, :, :, None] * bb_im[None] + pw_im[:CHUNK, :, :, None] * bb_re[None]
    kern = (jnp.einsum('gop,tgpc->tgco', cr, ab_re, precision=hi)
            - jnp.einsum('gop,tgpc->tgco', ci, ab_im, precision=hi))
    kern = kern.at[0].add(d_skip.astype(f32)[:, :, None] * jnp.eye(SSM_GC, dtype=f32)[None])
    kt = kern.reshape(CHUNK, SG, GPS, SSM_GC, SSM_GC).transpose(1, 0, 2, 3, 4)
    kt = jnp.tile(kt.reshape(SG, SG_IN, SSM_GC), (1, 1, GPS))

    def p_compact(ab):
        x = ab[::-1].transpose(0, 1, 3, 2).reshape(CHUNK, SG, GPS, SSM_GC, SSM_P)
        x = x.transpose(1, 0, 2, 3, 4).reshape(SG, SG_IN, SSM_P)
        return jnp.concatenate([x, x], axis=-1)

    ca_re = cr[None] * pw_re[1:, :, None, :] - ci[None] * pw_im[1:, :, None, :]
    ca_im = cr[None] * pw_im[1:, :, None, :] + ci[None] * pw_re[1:, :, None, :]
    q_compact = lambda ca: ca.transpose(1, 3, 0, 2).reshape(SG, GPS * SSM_P, CHUNK * SSM_GC)
    qc = jnp.concatenate([q_compact(ca_re), q_compact(-ca_im)], axis=1).astype(bf16)

    at_re = pw_re[CHUNK].reshape(SG, 1, GPS * SSM_P)
    at_im = pw_im[CHUNK].reshape(SG, 1, GPS * SSM_P)
    return p_compact(ab_re), p_compact(ab_im), kt, qc, at_re, at_im


def _replication_matrix():
    src = jnp.arange(CHUNK * SSM_GC)
    dst = jnp.arange(SG_IN)
    same_t = (src[:, None] // SSM_GC) == (dst[None, :] // LANES)
    same_c = (src[:, None] % SSM_GC) == (dst[None, :] % SSM_GC)
    return (same_t & same_c).astype(bf16)


def _mix_kernel(oa_ref, y2_ref, ga_ref, gs_ref, wglu_ref, wb_ref, m_ref, yb_ref):
    tm = oa_ref.shape[0]
    for sg in range(SG):
        for t in range(CHUNK):
            col = sg * SG_IN + t * LANES
            yb_ref[sg, pl.ds(t, tm // CHUNK, stride=CHUNK), :] = y2_ref[:, col:col + LANES].astype(f32)
    y = jnp.concatenate([yb_ref[sg] for sg in range(SG)], axis=1).astype(bf16)
    zg = _dot(y, wglu_ref[...])
    o_ssm = zg[:, :SSM_W] * _sigmoid(zg[:, SSM_W:])
    y_s = _dot(o_ssm.astype(bf16), wb_ref[Q_W:, :])
    y_a = _dot(oa_ref[...], wb_ref[:Q_W, :])
    m_ref[...] = (ga_ref[...].astype(f32) * y_a + gs_ref[...].astype(f32) * y_s).astype(bf16)


def _mix(o_attn, y2, gates, wglu, wb, tm=512):
    L = o_attn.shape[0]
    return pl.pallas_call(
        _mix_kernel,
        grid=(L // tm,),
        in_specs=[pl.BlockSpec((tm, Q_W), lambda i: (i, 0)),
                  pl.BlockSpec((tm // CHUNK, SG * SG_IN), lambda i: (i, 0)),
                  pl.BlockSpec((tm, D_MODEL), lambda i: (i, 0)),
                  pl.BlockSpec((tm, D_MODEL), lambda i: (i, 1)),
                  _const_spec(wglu.shape), _const_spec(wb.shape)],
        out_specs=pl.BlockSpec((tm, D_MODEL), lambda i: (i, 0)),
        out_shape=jax.ShapeDtypeStruct((L, D_MODEL), bf16),
        scratch_shapes=[pltpu.VMEM((SG, tm, LANES), f32)],
        compiler_params=_params(("parallel",)),
        name="mix",
    )(o_attn, y2, gates, gates, wglu, wb)


def _outproj_kernel(m_ref, w_ref, x_ref, gpost_ref, gpre_ref, x1_ref, h2_ref):
    out = _dot(m_ref[...], w_ref[...])
    x1 = x_ref[...] + _rms_norm(out, gpost_ref[...])
    x1_ref[...] = x1
    h2_ref[...] = _rms_norm(x1, gpre_ref[...]).astype(bf16)


def _outproj(m, w_out, x, g_post, g_pre, tm=512):
    L = x.shape[0]
    row = lambda: pl.BlockSpec((tm, D_MODEL), lambda i: (i, 0))
    return pl.pallas_call(
        _outproj_kernel,
        grid=(L // tm,),
        in_specs=[row(), _const_spec(w_out.shape), row(),
                  _const_spec((1, D_MODEL)), _const_spec((1, D_MODEL))],
        out_specs=[row(), row()],
        out_shape=[jax.ShapeDtypeStruct((L, D_MODEL), f32), jax.ShapeDtypeStruct((L, D_MODEL), bf16)],
        compiler_params=_params(("parallel",)),
        name="outproj",
    )(m, w_out, x, g_post, g_pre)


MLP_SUB = 512


def _mlp_kernel(h_ref, wu_ref, wd_ref, x_ref, g_ref, o_ref):
    j = pl.program_id(1)
    tm = h_ref.shape[0]

    @pl.when(j == 0)
    def _():
        o_ref[...] = jnp.zeros_like(o_ref)

    for r in range(tm // MLP_SUB):
        rows = pl.ds(r * MLP_SUB, MLP_SUB)
        a = jnp.maximum(_dot(h_ref[rows, :], wu_ref[...]), 0.0)
        o_ref[rows, :] += _dot((a * a).astype(bf16), wd_ref[...])

    @pl.when(j == pl.num_programs(1) - 1)
    def _():
        for r in range(tm // MLP_SUB):
            rows = pl.ds(r * MLP_SUB, MLP_SUB)
            o_ref[rows, :] = x_ref[rows, :] + _rms_norm(o_ref[rows, :], g_ref[...])


def _mlp(h2, w_up, w_down, x1, g_post, tm=1024, tf=512):
    L = x1.shape[0]
    return pl.pallas_call(
        _mlp_kernel,
        grid=(L // tm, D_FF // tf),
        in_specs=[pl.BlockSpec((tm, D_MODEL), lambda i, j: (i, 0)),
                  pl.BlockSpec((D_MODEL, tf), lambda i, j: (0, j)),
                  pl.BlockSpec((tf, D_MODEL), lambda i, j: (j, 0)),
                  pl.BlockSpec((tm, D_MODEL), lambda i, j: (i, 0), pipeline_mode=pl.Buffered(1)),
                  _const_spec((1, D_MODEL))],
        out_specs=pl.BlockSpec((tm, D_MODEL), lambda i, j: (i, 0)),
        out_shape=jax.ShapeDtypeStruct((L, D_MODEL), f32),
        compiler_params=_params(("parallel", "arbitrary")),
        name="mlp",
    )(h2, w_up, w_down, x1, g_post)


def _rope_tables(L):
    half = ROT_DIM // 2
    inv = ROPE_THETA ** (-jnp.arange(half, dtype=f32) * 2.0 / ROT_DIM)
    ang = jnp.arange(L).astype(f32)[:, None] * inv[None, :]
    cos, sin = jnp.cos(ang), jnp.sin(ang)
    ones = jnp.ones((L, HEAD_DIM - ROT_DIM), f32)
    zeros = jnp.zeros((L, HEAD_DIM - ROT_DIM), f32)
    zh = jnp.zeros((L, half), f32)
    per_head = lambda parts: jnp.tile(jnp.concatenate(parts, axis=1), (1, LANES // HEAD_DIM))
    return (per_head([cos, cos, ones]), per_head([-sin, zh, zeros]), per_head([zh, sin, zeros]))


def _layer(x, norm_mix_pre, norm_mix_post, norm_mlp_pre, norm_mlp_post, w_in, sinks,
           lam_re, lam_im, log_dt, b_re, b_im, c_re, c_im, d_skip, w_glu, w_branch, w_out,
           w_up, w_down, rope):
    o3 = Q_W + 2 * KV_W
    o4 = o3 + SSM_W
    gain = lambda g: g.astype(f32).reshape(1, D_MODEL)
    q, k, v, u2, h = _inproj(x, gain(norm_mix_pre), w_in[:, :o3].astype(bf16),
                             w_in[:, o3:o4].astype(bf16), *rope)
    gates = _gates(h, w_in[:, o4:].astype(bf16))
    o_attn = _attention(q, k, v, sinks.astype(f32))
    pre, pim, kt, qc, at_re, at_im = _ssm_operators(lam_re, lam_im, log_dt, b_re, b_im, c_re, c_im, d_skip)
    y2 = _ssm(u2, pre, pim, kt, qc, _replication_matrix(), at_re, at_im)
    mixed = _mix(o_attn, y2, gates, w_glu.astype(bf16), w_branch.astype(bf16))
    x1, h2 = _outproj(mixed, w_out.astype(bf16), x, gain(norm_mix_post), gain(norm_mlp_pre))
    return _mlp(h2, w_up.astype(bf16), w_down.astype(bf16), x1, gain(norm_mlp_post))


def kernel(x, norm_mix_pre, norm_mix_post, norm_mlp_pre, norm_mlp_post, w_in, sinks, lam_re, lam_im, log_dt, b_re, b_im, c_re, c_im, d_skip, w_glu, w_branch, w_out, w_up, w_down):
    B, L, _ = x.shape
    depth = w_in.shape[0]
    rope = _rope_tables(L)
    outs = []
    for b in range(B):
        xb = x[b]
        for l in range(depth):
            xb = _layer(xb, norm_mix_pre[l], norm_mix_post[l], norm_mlp_pre[l], norm_mlp_post[l],
                        w_in[l], sinks[l], lam_re[l], lam_im[l], log_dt[l], b_re[l], b_im[l],
                        c_re[l], c_im[l], d_skip[l], w_glu[l], w_branch[l], w_out[l],
                        w_up[l], w_down[l], rope)
        outs.append(xb)
    return jnp.stack(outs)
```

```python
import math

import jax
import jax.numpy as jnp
from jax import lax
from jax.experimental import pallas as pl
from jax.experimental.pallas import tpu as pltpu

D_MODEL = 2048
SEQ = 8192
HEAD_DIM = 64
N_Q_HEADS = 16
N_KV_HEADS = 2
BLOCK = 128
ROT_DIM = HEAD_DIM // 4
ROPE_THETA = 500000.0
Q_W = N_Q_HEADS * HEAD_DIM
KV_W = N_KV_HEADS * HEAD_DIM
SSM_W = D_MODEL // 2
SSM_GC = 16
SSM_G = SSM_W // SSM_GC
SSM_P = 64
D_FF = 4 * D_MODEL
EPS = 1e-6

LANES = 128
CHUNK = 16
N_CHUNKS = SEQ // CHUNK
SG = SSM_W // LANES
GPS = LANES // SSM_GC
SG_IN = CHUNK * LANES
SG_STATE = 2 * GPS * SSM_P
VMEM_LIMIT = 56 * 1024 * 1024
ROW_SUB = 256
OUT_SUB = 128

bf16 = jnp.bfloat16
f32 = jnp.float32


def _dot(a, b):
    return jnp.dot(a, b, preferred_element_type=f32)


def _sigmoid(x):
    return 1.0 / (1.0 + jnp.exp(-x))


def _gelu_tanh(x):
    c = math.sqrt(2.0 / math.pi)
    return x * (0.5 * (1.0 + jnp.tanh(c * (x + 0.044715 * (x * x * x)))))


def _rms_norm(x, g):
    return x * lax.rsqrt(jnp.mean(x * x, axis=-1, keepdims=True) + EPS) * g


def _params(sem):
    return pltpu.CompilerParams(dimension_semantics=sem, vmem_limit_bytes=VMEM_LIMIT)


def _const_spec(shape):
    nd = len(shape)
    return pl.BlockSpec(shape, lambda *_: (0,) * nd, pipeline_mode=pl.Buffered(1))


def _rope_block(z, cos, sin_a, sin_b):
    return (z * cos + pltpu.roll(z, LANES - ROT_DIM // 2, axis=1) * sin_a
            + pltpu.roll(z, ROT_DIM // 2, axis=1) * sin_b)


def _inproj_kernel(x_ref, g_ref, wqkv_ref, wu_ref, cos_ref, sa_ref, sb_ref,
                   q_ref, k_ref, v_ref, u2_ref, h_ref, zu_ref):
    for r in range(x_ref.shape[0] // ROW_SUB):
        rows = pl.ds(r * ROW_SUB, ROW_SUB)
        h = _rms_norm(x_ref[rows, :], g_ref[...]).astype(bf16)
        h_ref[rows, :] = h
        cos, sin_a, sin_b = cos_ref[rows, :], sa_ref[rows, :], sb_ref[rows, :]
        for half in range(2):
            zq = _dot(h, wqkv_ref[:, half * 512:(half + 1) * 512])
            for b in range(4):
                blk = zq[:, b * LANES:(b + 1) * LANES]
                col = (half * 4 + b) * LANES
                q_ref[rows, col:col + LANES] = _rope_block(blk, cos, sin_a, sin_b).astype(bf16)
        zkv = _dot(h, wqkv_ref[:, Q_W:Q_W + 2 * KV_W])
        k_ref[rows, :] = _rope_block(zkv[:, :KV_W], cos, sin_a, sin_b).astype(bf16)
        v_ref[rows, :] = zkv[:, KV_W:].astype(bf16)
        zu = _dot(h, wu_ref[...])
        chunk_rows = pl.ds(r * (ROW_SUB // CHUNK), ROW_SUB // CHUNK)
        for sg in range(SG):
            zu_ref[r, sg] = zu[:, sg * LANES:(sg + 1) * LANES]
            for s in range(CHUNK):
                col = sg * SG_IN + s * LANES
                u2_ref[chunk_rows, col:col + LANES] = (
                    zu_ref[r, sg, pl.ds(s, ROW_SUB // CHUNK, stride=CHUNK), :].astype(bf16))


def _inproj(x, gain, wqkv, wu, cos, sin_a, sin_b, tm=512):
    L = x.shape[0]
    row = lambda w: pl.BlockSpec((tm, w), lambda i: (i, 0))
    return pl.pallas_call(
        _inproj_kernel,
        grid=(L // tm,),
        in_specs=[row(D_MODEL), _const_spec((1, D_MODEL)), _const_spec(wqkv.shape),
                  _const_spec(wu.shape), row(LANES), row(LANES), row(LANES)],
        out_specs=[row(Q_W), row(KV_W), row(KV_W),
                   pl.BlockSpec((tm // CHUNK, SG * SG_IN), lambda i: (i, 0)), row(D_MODEL)],
        out_shape=[jax.ShapeDtypeStruct((L, Q_W), bf16), jax.ShapeDtypeStruct((L, KV_W), bf16),
                   jax.ShapeDtypeStruct((L, KV_W), bf16),
                   jax.ShapeDtypeStruct((L // CHUNK, SG * SG_IN), bf16),
                   jax.ShapeDtypeStruct((L, D_MODEL), bf16)],
        scratch_shapes=[pltpu.VMEM((tm // ROW_SUB, SG, ROW_SUB, LANES), f32)],
        compiler_params=_params(("parallel",)),
        name="inproj",
    )(x, gain, wqkv, wu, cos, sin_a, sin_b)


def _gates_kernel(h_ref, w_ref, o_ref):
    o_ref[...] = _sigmoid(_dot(h_ref[...], w_ref[...])).astype(bf16)


def _gates(h, wg, tm=1024, tn=1024):
    L, N = h.shape[0], wg.shape[1]
    return pl.pallas_call(
        _gates_kernel,
        grid=(L // tm, N // tn),
        in_specs=[pl.BlockSpec((tm, D_MODEL), lambda i, j: (i, 0)),
                  pl.BlockSpec((D_MODEL, tn), lambda i, j: (0, j))],
        out_specs=pl.BlockSpec((tm, tn), lambda i, j: (i, j)),
        out_shape=jax.ShapeDtypeStruct((L, N), bf16),
        compiler_params=_params(("parallel", "arbitrary")),
        name="gates",
    )(h, wg)


def _attn_kernel(sinks_ref, q_ref, kp_ref, kc_ref, vp_ref, vc_ref, o_ref):
    n = pl.program_id(0)
    q = q_ref[...] * jnp.asarray(1.0 / math.sqrt(HEAD_DIM), bf16)
    kcat = jnp.concatenate([kp_ref[...], kc_ref[...]], axis=0).astype(f32)
    vcat = jnp.concatenate([vp_ref[...], vc_ref[...]], axis=0).astype(f32)
    k_sw = pltpu.roll(kcat, HEAD_DIM, axis=1)
    v_sw = pltpu.roll(vcat, HEAD_DIM, axis=1)
    lane_kv = lax.broadcasted_iota(jnp.int32, kcat.shape, 1)
    low_kv = lane_kv < HEAD_DIM
    qi = lax.broadcasted_iota(jnp.int32, (BLOCK, 2 * BLOCK), 0)
    kj = lax.broadcasted_iota(jnp.int32, (BLOCK, 2 * BLOCK), 1)
    first_key = jnp.where(n > 0, 0, BLOCK)
    allowed = (kj > qi) & (kj <= qi + BLOCK) & (kj >= first_key)
    low_o = lax.broadcasted_iota(jnp.int32, (BLOCK, LANES), 1) < HEAD_DIM
    neg = jnp.finfo(f32).min
    contract_lanes = (((1,), (1,)), ((), ()))
    pairs_per_kv = (N_Q_HEADS // N_KV_HEADS) // 2
    for hk in range(N_KV_HEADS):
        k_src, k_oth = (kcat, k_sw) if hk == 0 else (k_sw, kcat)
        k_lo = jnp.where(low_kv, k_src, 0.0).astype(bf16)
        k_hi = jnp.where(low_kv, 0.0, k_oth).astype(bf16)
        vv = (jnp.where(low_kv, vcat, v_sw) if hk == 0 else jnp.where(low_kv, v_sw, vcat)).astype(bf16)
        qs = jnp.concatenate(
            [q[:, (pairs_per_kv * hk + b) * LANES:(pairs_per_kv * hk + b + 1) * LANES]
             for b in range(pairs_per_kv)], axis=0)
        s_par = (lax.dot_general(qs, k_lo, contract_lanes, preferred_element_type=f32),
                 lax.dot_general(qs, k_hi, contract_lanes, preferred_element_type=f32))
        for b in range(pairs_per_kv):
            outs = []
            for par in range(2):
                head = 2 * (pairs_per_kv * hk + b) + par
                s = jnp.where(allowed, s_par[par][b * BLOCK:(b + 1) * BLOCK], neg)
                sink = sinks_ref[head]
                m = jnp.maximum(jnp.max(s, axis=1, keepdims=True), sink)
                p = jnp.exp(s - m)
                denom = jnp.sum(p, axis=1, keepdims=True) + jnp.exp(sink - m)
                outs.append(_dot(p.astype(bf16), vv) * (1.0 / denom))
            col = (pairs_per_kv * hk + b) * LANES
            o_ref[:, col:col + LANES] = jnp.where(low_o, outs[0], outs[1]).astype(bf16)


def _attention(q, k, v, sinks):
    L = q.shape[0]
    cur = lambda n: (n, 0)
    prev = lambda n: (jnp.maximum(n - 1, 0), 0)
    return pl.pallas_call(
        _attn_kernel,
        grid=(L // BLOCK,),
        in_specs=[pl.BlockSpec(memory_space=pltpu.SMEM),
                  pl.BlockSpec((BLOCK, Q_W), cur),
                  pl.BlockSpec((BLOCK, KV_W), prev), pl.BlockSpec((BLOCK, KV_W), cur),
                  pl.BlockSpec((BLOCK, KV_W), prev), pl.BlockSpec((BLOCK, KV_W), cur)],
        out_specs=pl.BlockSpec((BLOCK, Q_W), cur),
        out_shape=jax.ShapeDtypeStruct((L, Q_W), bf16),
        compiler_params=_params(("parallel",)),
        name="swa_attention",
    )(sinks, q, k, k, v, v)


GC_SHIFT = SSM_GC.bit_length() - 1
P_SHIFT = SSM_P.bit_length() - 1
Y_TILE = 512
NT = SG_STATE // LANES
NSEG = 8
SEG = N_CHUNKS // NSEG


def _ssm_kernel(u_ref, pre_ref, pim_ref, kt_ref, qc_ref, rep_ref, ar_ref, ai_ref, y_ref,
                p_scr, m_scr, q_scr, s_scr, xl_scr, pw_scr, xs_scr):
    half = SG_STATE // 2
    row = lax.broadcasted_iota(jnp.int32, (SG_IN, LANES), 0)
    lane = lax.broadcasted_iota(jnp.int32, (SG_IN, LANES), 1)
    row_grp = (row >> GC_SHIFT) & (GPS - 1)
    pre, pim = pre_ref[...], pim_ref[...]
    for k in range(half // LANES):
        sel = row_grp == 2 * k + (lane >> P_SHIFT)
        p_scr[:, k * LANES:(k + 1) * LANES] = jnp.where(sel, pre, 0.0).astype(bf16)
        p_scr[:, half + k * LANES:half + (k + 1) * LANES] = jnp.where(sel, pim, 0.0).astype(bf16)

    blk_row = lax.broadcasted_iota(jnp.int32, (LANES, LANES), 0) >> GC_SHIFT
    blk_lane = lax.broadcasted_iota(jnp.int32, (LANES, LANES), 1) >> GC_SHIFT
    zero_blk = jnp.zeros((LANES, LANES), bf16)
    for tau in range(CHUNK):
        blk = jnp.where(blk_row == blk_lane, kt_ref[tau * LANES:(tau + 1) * LANES, :], 0.0).astype(bf16)
        for s in range(CHUNK - tau):
            t = s + tau
            m_scr[s * LANES:(s + 1) * LANES, t * LANES:(t + 1) * LANES] = blk
            if tau > 0:
                m_scr[t * LANES:(t + 1) * LANES, s * LANES:(s + 1) * LANES] = zero_blk

    q_row_grp = (lax.broadcasted_iota(jnp.int32, (SG_STATE, Y_TILE), 0) >> P_SHIFT) & (GPS - 1)
    q_lane_grp = (lax.broadcasted_iota(jnp.int32, (SG_STATE, Y_TILE), 1) >> GC_SHIFT) & (GPS - 1)
    for j in range(SG_IN // Y_TILE):
        full = _dot(qc_ref[...], rep_ref[:, j * Y_TILE:(j + 1) * Y_TILE])
        q_scr[:, j * Y_TILE:(j + 1) * Y_TILE] = jnp.where(q_row_grp == q_lane_grp, full, 0.0).astype(bf16)

    s = _dot(u_ref[...], p_scr[...])
    for j in range(NT):
        s_scr[j] = s[:, j * LANES:(j + 1) * LANES]
    ht = NT // 2
    cmul = lambda xr, xi, yr, yi: (xr * yr - xi * yi, xr * yi + xi * yr)
    ar = [ar_ref[:, j * LANES:(j + 1) * LANES] for j in range(ht)]
    ai = [ai_ref[:, j * LANES:(j + 1) * LANES] for j in range(ht)]
    one, nil = jnp.ones((1, LANES), f32), jnp.zeros((1, LANES), f32)
    a_seg = []
    for j in range(ht):
        rr, ri = [one], [nil]
        for _ in range(7):
            nr, ni = cmul(rr[-1], ri[-1], ar[j], ai[j])
            rr.append(nr)
            ri.append(ni)
        a8r, a8i = cmul(rr[-1], ri[-1], ar[j], ai[j])
        blk_r, blk_i = jnp.concatenate(rr, axis=0), jnp.concatenate(ri, axis=0)
        for b in range(SEG // 8):
            pw_scr[j, b * 8:(b + 1) * 8, :] = blk_r
            pw_scr[ht + j, b * 8:(b + 1) * 8, :] = blk_i
            blk_r, blk_i = cmul(blk_r, blk_i, a8r, a8i)
        a_seg.append((blk_r[0:1], blk_i[0:1]))

    ar_b = [jnp.broadcast_to(a, (NSEG, LANES)) for a in ar]
    ai_b = [jnp.broadcast_to(a, (NSEG, LANES)) for a in ai]

    def body(i, carry):
        new_r, new_i = [], []
        for j in range(ht):
            xr, xi = carry[j], carry[ht + j]
            seg_rows = pl.ds(i, NSEG, stride=SEG)
            xl_scr[j, seg_rows, :] = xr
            xl_scr[ht + j, seg_rows, :] = xi
            pr, pi = cmul(xr, xi, ar_b[j], ai_b[j])
            new_r.append(pr + s_scr[j, seg_rows, :])
            new_i.append(pi + s_scr[ht + j, seg_rows, :])
        return tuple(new_r + new_i)

    ends = lax.fori_loop(0, SEG, body, tuple(jnp.zeros((NSEG, LANES), f32) for _ in range(NT)))

    for j in range(ht):
        zr, zi = nil, nil
        for k in range(NSEG):
            rows = slice(k * SEG, (k + 1) * SEG)
            fr, fi = cmul(pw_scr[j], pw_scr[ht + j], zr, zi)
            xs_scr[rows, j * LANES:(j + 1) * LANES] = (xl_scr[j, rows, :] + fr).astype(bf16)
            xs_scr[rows, half + j * LANES:half + (j + 1) * LANES] = (xl_scr[ht + j, rows, :] + fi).astype(bf16)
            pr, pi = cmul(zr, zi, *a_seg[j])
            zr, zi = pr + ends[j][k:k + 1], pi + ends[ht + j][k:k + 1]

    xs = xs_scr[...]
    for j in range(SG_IN // Y_TILE):
        kk = (j + 1) * Y_TILE
        cols = slice(j * Y_TILE, (j + 1) * Y_TILE)
        y = _dot(u_ref[:, :kk], m_scr[:kk, cols]) + _dot(xs, q_scr[:, cols])
        y_ref[:, cols] = _gelu_tanh(y).astype(bf16)


def _ssm(u2, pre, pim, kt, qc, rep, at_re, at_im):
    per_sg = lambda r, c: pl.BlockSpec((None, r, c), lambda g: (g, 0, 0))
    return pl.pallas_call(
        _ssm_kernel,
        grid=(SG,),
        in_specs=[pl.BlockSpec((N_CHUNKS, SG_IN), lambda g: (0, g)),
                  per_sg(SG_IN, LANES), per_sg(SG_IN, LANES), per_sg(SG_IN, LANES),
                  per_sg(SG_STATE, CHUNK * SSM_GC), _const_spec(rep.shape),
                  per_sg(1, SG_STATE // 2), per_sg(1, SG_STATE // 2)],
        out_specs=pl.BlockSpec((N_CHUNKS, SG_IN), lambda g: (0, g)),
        out_shape=jax.ShapeDtypeStruct((N_CHUNKS, SG * SG_IN), bf16),
        scratch_shapes=[pltpu.VMEM((SG_IN, SG_STATE), bf16), pltpu.VMEM((SG_IN, SG_IN), bf16),
                        pltpu.VMEM((SG_STATE, SG_IN), bf16),
                        pltpu.VMEM((NT, N_CHUNKS, LANES), f32), pltpu.VMEM((NT, N_CHUNKS, LANES), f32),
                        pltpu.VMEM((NT, SEG, LANES), f32), pltpu.VMEM((N_CHUNKS, SG_STATE), bf16)],
        compiler_params=_params(("arbitrary",)),
        name="ssm",
    )(u2, pre, pim, kt, qc, rep, at_re, at_im)


def _ssm_operators(lam_re, lam_im, log_dt, b_re, b_im, c_re, c_im, d_skip):
    hi = lax.Precision.HIGHEST
    lr, li = lam_re.astype(f32), lam_im.astype(f32)
    dt = jnp.exp(log_dt.astype(f32))[:, None]
    mag = jnp.exp(lr * dt)
    a_re, a_im = mag * jnp.cos(li * dt), mag * jnp.sin(li * dt)
    den = lr * lr + li * li
    nr, ni = a_re - 1.0, a_im
    coef_re = (nr * lr + ni * li) / den
    coef_im = (ni * lr - nr * li) / den
    brt, bit = b_re.astype(f32).swapaxes(1, 2), b_im.astype(f32).swapaxes(1, 2)
    bb_re = coef_re[:, None, :] * brt - coef_im[:, None, :] * bit
    bb_im = coef_re[:, None, :] * bit + coef_im[:, None, :] * brt
    cr, ci = c_re.astype(f32), c_im.astype(f32)

    pw_re, pw_im = [jnp.ones_like(a_re)], [jnp.zeros_like(a_im)]
    for _ in range(CHUNK):
        pr, pi = pw_re[-1], pw_im[-1]
        pw_re.append(pr * a_re - pi * a_im)
        pw_im.append(pr * a_im + pi * a_re)
    pw_re, pw_im = jnp.stack(pw_re, axis=1), jnp.stack(pw_im, axis=1)

    pr, pi = pw_re[:, :CHUNK, None, :], pw_im[:, :CHUNK, None, :]
    ab_re = pr * bb_re[:, None] - pi * bb_im[:, None]
    ab_im = pr * bb_im[:, None] + pi * bb_re[:, None]
    kern = (jnp.einsum('gtcp,gop->gtco', ab_re, cr, precision=hi)
            - jnp.einsum('gtcp,gop->gtco', ab_im, ci, precision=hi))
    kern = kern.at[:, 0].add(d_skip.astype(f32)[:, :, None] * jnp.eye(SSM_GC, dtype=f32)[None])
    kt = kern.reshape(SG, GPS, CHUNK, SSM_GC, SSM_GC).transpose(0, 2, 1, 3, 4)
    kt = jnp.tile(kt.reshape(SG, SG_IN, SSM_GC), (1, 1, GPS))

    def p_compact(ab):
        x = ab[:, ::-1].reshape(SG, GPS, CHUNK, SSM_GC, SSM_P).transpose(0, 2, 1, 3, 4)
        x = x.reshape(SG, SG_IN, SSM_P)
        return jnp.concatenate([x, x], axis=-1)

    crt, cit = cr.swapaxes(1, 2)[:, :, None, :], ci.swapaxes(1, 2)[:, :, None, :]
    qr, qi = pw_re[:, 1:].swapaxes(1, 2)[..., None], pw_im[:, 1:].swapaxes(1, 2)[..., None]
    q_compact = lambda ca: ca.reshape(SG, GPS * SSM_P, CHUNK * SSM_GC)
    qc = jnp.concatenate([q_compact(crt * qr - cit * qi), q_compact(-(crt * qi + cit * qr))],
                         axis=1).astype(bf16)

    at_re = pw_re[:, CHUNK].reshape(SG, 1, GPS * SSM_P)
    at_im = pw_im[:, CHUNK].reshape(SG, 1, GPS * SSM_P)
    return p_compact(ab_re), p_compact(ab_im), kt, qc, at_re, at_im


def _replication_matrix():
    src = jnp.arange(CHUNK * SSM_GC)
    dst = jnp.arange(SG_IN)
    same_t = (src[:, None] // SSM_GC) == (dst[None, :] // LANES)
    same_c = (src[:, None] % SSM_GC) == (dst[None, :] % SSM_GC)
    return (same_t & same_c).astype(bf16)


def _mix_kernel(oa_ref, y2_ref, ga_ref, gs_ref, wglu_ref, wb_ref, m_ref, yb_ref):
    tm = oa_ref.shape[0]
    for sg in range(SG):
        for t in range(CHUNK):
            col = sg * SG_IN + t * LANES
            yb_ref[sg, pl.ds(t, tm // CHUNK, stride=CHUNK), :] = y2_ref[:, col:col + LANES].astype(f32)
    y = jnp.concatenate([yb_ref[sg] for sg in range(SG)], axis=1).astype(bf16)
    zg = _dot(y, wglu_ref[...])
    o_ssm = zg[:, :SSM_W] * _sigmoid(zg[:, SSM_W:])
    y_s = _dot(o_ssm.astype(bf16), wb_ref[Q_W:, :])
    y_a = _dot(oa_ref[...], wb_ref[:Q_W, :])
    m_ref[...] = (ga_ref[...].astype(f32) * y_a + gs_ref[...].astype(f32) * y_s).astype(bf16)


def _mix(o_attn, y2, gates, wglu, wb, tm=512):
    L = o_attn.shape[0]
    return pl.pallas_call(
        _mix_kernel,
        grid=(L // tm,),
        in_specs=[pl.BlockSpec((tm, Q_W), lambda i: (i, 0)),
                  pl.BlockSpec((tm // CHUNK, SG * SG_IN), lambda i: (i, 0)),
                  pl.BlockSpec((tm, D_MODEL), lambda i: (i, 0)),
                  pl.BlockSpec((tm, D_MODEL), lambda i: (i, 1)),
                  _const_spec(wglu.shape), _const_spec(wb.shape)],
        out_specs=pl.BlockSpec((tm, D_MODEL), lambda i: (i, 0)),
        out_shape=jax.ShapeDtypeStruct((L, D_MODEL), bf16),
        scratch_shapes=[pltpu.VMEM((SG, tm, LANES), f32)],
        compiler_params=_params(("parallel",)),
        name="mix",
    )(o_attn, y2, gates, gates, wglu, wb)


def _outproj_kernel(m_ref, w_ref, x_ref, gpost_ref, gpre_ref, x1_ref, h2_ref):
    for r in range(m_ref.shape[0] // OUT_SUB):
        rows = pl.ds(r * OUT_SUB, OUT_SUB)
        out = _dot(m_ref[rows, :], w_ref[...])
        x1 = x_ref[rows, :] + _rms_norm(out, gpost_ref[...])
        x1_ref[rows, :] = x1
        h2_ref[rows, :] = _rms_norm(x1, gpre_ref[...]).astype(bf16)


def _outproj(m, w_out, x, g_post, g_pre, tm=512):
    L = x.shape[0]
    row = lambda: pl.BlockSpec((tm, D_MODEL), lambda i: (i, 0))
    return pl.pallas_call(
        _outproj_kernel,
        grid=(L // tm,),
        in_specs=[row(), _const_spec(w_out.shape), row(),
                  _const_spec((1, D_MODEL)), _const_spec((1, D_MODEL))],
        out_specs=[row(), row()],
        out_shape=[jax.ShapeDtypeStruct((L, D_MODEL), f32), jax.ShapeDtypeStruct((L, D_MODEL), bf16)],
        compiler_params=_params(("parallel",)),
        name="outproj",
    )(m, w_out, x, g_post, g_pre)


MLP_SUB = 512
MLP_COLS = 512


def _mlp_kernel(h_ref, wu_ref, wd_ref, x_ref, g_ref, o_ref):
    j = pl.program_id(1)
    tm = h_ref.shape[0]

    @pl.when(j == 0)
    def _():
        o_ref[...] = jnp.zeros_like(o_ref)

    for r in range(tm // MLP_SUB):
        rows = pl.ds(r * MLP_SUB, MLP_SUB)
        a = jnp.maximum(_dot(h_ref[rows, :], wu_ref[...]), 0.0)
        a = (a * a).astype(bf16)
        for c in range(D_MODEL // MLP_COLS):
            cols = slice(c * MLP_COLS, (c + 1) * MLP_COLS)
            o_ref[rows, cols] += _dot(a, wd_ref[:, cols])

    @pl.when(j == pl.num_programs(1) - 1)
    def _():
        for r in range(tm // MLP_SUB):
            rows = pl.ds(r * MLP_SUB, MLP_SUB)
            o_ref[rows, :] = x_ref[rows, :] + _rms_norm(o_ref[rows, :], g_ref[...])


def _mlp(h2, w_up, w_down, x1, g_post, tm=1024, tf=1024):
    L = x1.shape[0]
    return pl.pallas_call(
        _mlp_kernel,
        grid=(L // tm, D_FF // tf),
        in_specs=[pl.BlockSpec((tm, D_MODEL), lambda i, j: (i, 0)),
                  pl.BlockSpec((D_MODEL, tf), lambda i, j: (0, j)),
                  pl.BlockSpec((tf, D_MODEL), lambda i, j: (j, 0)),
                  pl.BlockSpec((tm, D_MODEL), lambda i, j: (i, 0), pipeline_mode=pl.Buffered(1)),
                  _const_spec((1, D_MODEL))],
        out_specs=pl.BlockSpec((tm, D_MODEL), lambda i, j: (i, 0)),
        out_shape=jax.ShapeDtypeStruct((L, D_MODEL), f32),
        compiler_params=_params(("parallel", "arbitrary")),
        name="mlp",
    )(h2, w_up, w_down, x1, g_post)


def _rope_tables(L):
    half = ROT_DIM // 2
    inv = ROPE_THETA ** (-jnp.arange(half, dtype=f32) * 2.0 / ROT_DIM)
    ang = jnp.arange(L).astype(f32)[:, None] * inv[None, :]
    cos, sin = jnp.cos(ang), jnp.sin(ang)
    ones = jnp.ones((L, HEAD_DIM - ROT_DIM), f32)
    zeros = jnp.zeros((L, HEAD_DIM - ROT_DIM), f32)
    zh = jnp.zeros((L, half), f32)
    per_head = lambda parts: jnp.tile(jnp.concatenate(parts, axis=1), (1, LANES // HEAD_DIM))
    return (per_head([cos, cos, ones]), per_head([-sin, zh, zeros]), per_head([zh, sin, zeros]))


def _layer(x, norm_mix_pre, norm_mix_post, norm_mlp_pre, norm_mlp_post, w_in, sinks,
           lam_re, lam_im, log_dt, b_re, b_im, c_re, c_im, d_skip, w_glu, w_branch, w_out,
           w_up, w_down, rope):
    o3 = Q_W + 2 * KV_W
    o4 = o3 + SSM_W
    gain = lambda g: g.astype(f32).reshape(1, D_MODEL)
    q, k, v, u2, h = _inproj(x, gain(norm_mix_pre), w_in[:, :o3].astype(bf16),
                             w_in[:, o3:o4].astype(bf16), *rope)
    gates = _gates(h, w_in[:, o4:].astype(bf16))
    o_attn = _attention(q, k, v, sinks.astype(f32))
    pre, pim, kt, qc, at_re, at_im = _ssm_operators(lam_re, lam_im, log_dt, b_re, b_im, c_re, c_im, d_skip)
    y2 = _ssm(u2, pre, pim, kt, qc, _replication_matrix(), at_re, at_im)
    mixed = _mix(o_attn, y2, gates, w_glu.astype(bf16), w_branch.astype(bf16))
    x1, h2 = _outproj(mixed, w_out.astype(bf16), x, gain(norm_mix_post), gain(norm_mlp_pre))
    return _mlp(h2, w_up.astype(bf16), w_down.astype(bf16), x1, gain(norm_mlp_post))


def kernel(x, norm_mix_pre, norm_mix_post, norm_mlp_pre, norm_mlp_post, w_in, sinks, lam_re, lam_im, log_dt, b_re, b_im, c_re, c_im, d_skip, w_glu, w_branch, w_out, w_up, w_down):
    B, L, _ = x.shape
    depth = w_in.shape[0]
    rope = _rope_tables(L)
    outs = []
    for b in range(B):
        xb = x[b]
        for l in range(depth):
            xb = _layer(xb, norm_mix_pre[l], norm_mix_post[l], norm_mlp_pre[l], norm_mlp_post[l],
                        w_in[l], sinks[l], lam_re[l], lam_im[l], log_dt[l], b_re[l], b_im[l],
                        c_re[l], c_im[l], d_skip[l], w_glu[l], w_branch[l], w_out[l],
                        w_up[l], w_down[l], rope)
        outs.append(xb)
    return jnp.stack(outs)
```

```python
import math

import jax
import jax.numpy as jnp
from jax import lax
from jax.experimental import pallas as pl
from jax.experimental.pallas import tpu as pltpu

D_MODEL = 2048
SEQ = 8192
HEAD_DIM = 64
N_Q_HEADS = 16
N_KV_HEADS = 2
BLOCK = 128
ROT_DIM = HEAD_DIM // 4
ROPE_THETA = 500000.0
Q_W = N_Q_HEADS * HEAD_DIM
KV_W = N_KV_HEADS * HEAD_DIM
SSM_W = D_MODEL // 2
QKV_W = Q_W + 2 * KV_W
IN_MIX_W = QKV_W + SSM_W
SSM_GC = 16
SSM_G = SSM_W // SSM_GC
SSM_P = 64
D_FF = 4 * D_MODEL
EPS = 1e-6

LANES = 128
CHUNK = 16
N_CHUNKS = SEQ // CHUNK
SG = SSM_W // LANES
GPS = LANES // SSM_GC
SG_IN = CHUNK * LANES
SG_STATE = 2 * GPS * SSM_P
VMEM_LIMIT = 56 * 1024 * 1024
ROW_SUB = 256
OUT_SUB = 128

bf16 = jnp.bfloat16
f32 = jnp.float32


def _dot(a, b):
    return jnp.dot(a, b, preferred_element_type=f32)


def _sigmoid(x):
    return 1.0 / (1.0 + jnp.exp(-x))


def _gelu_tanh(x):
    c = math.sqrt(2.0 / math.pi)
    return x * (0.5 * (1.0 + jnp.tanh(c * (x + 0.044715 * (x * x * x)))))


def _rms_norm(x, g):
    return x * lax.rsqrt(jnp.mean(x * x, axis=-1, keepdims=True) + EPS) * g


def _params(sem):
    return pltpu.CompilerParams(dimension_semantics=sem, vmem_limit_bytes=VMEM_LIMIT)


def _const_spec(shape):
    nd = len(shape)
    return pl.BlockSpec(shape, lambda *_: (0,) * nd, pipeline_mode=pl.Buffered(1))


def _rope_block(z, cos, sin_a, sin_b):
    return (z * cos + pltpu.roll(z, LANES - ROT_DIM // 2, axis=1) * sin_a
            + pltpu.roll(z, ROT_DIM // 2, axis=1) * sin_b)


def _inproj_kernel(x_ref, g_ref, w_ref, wcast_ref, cos_ref, sa_ref, sb_ref,
                   q_ref, k_ref, v_ref, u2_ref, h_ref, wg_ref, wqkv_ref, wu_ref, zu_ref):
    @pl.when(pl.program_id(0) == 0)
    def _():
        wqkv_ref[...] = w_ref[:, :QKV_W].astype(bf16)
        wu_ref[...] = w_ref[:, QKV_W:].astype(bf16)

    wg_ref[...] = wcast_ref[...].astype(bf16)
    for r in range(x_ref.shape[0] // ROW_SUB):
        rows = pl.ds(r * ROW_SUB, ROW_SUB)
        h = _rms_norm(x_ref[rows, :], g_ref[...]).astype(bf16)
        h_ref[rows, :] = h
        cos, sin_a, sin_b = cos_ref[rows, :], sa_ref[rows, :], sb_ref[rows, :]
        for half in range(2):
            zq = _dot(h, wqkv_ref[:, half * 512:(half + 1) * 512])
            for b in range(4):
                blk = zq[:, b * LANES:(b + 1) * LANES]
                col = (half * 4 + b) * LANES
                q_ref[rows, col:col + LANES] = _rope_block(blk, cos, sin_a, sin_b).astype(bf16)
        zkv = _dot(h, wqkv_ref[:, Q_W:Q_W + 2 * KV_W])
        k_ref[rows, :] = _rope_block(zkv[:, :KV_W], cos, sin_a, sin_b).astype(bf16)
        v_ref[rows, :] = zkv[:, KV_W:].astype(bf16)
        zu = _dot(h, wu_ref[...])
        chunk_rows = pl.ds(r * (ROW_SUB // CHUNK), ROW_SUB // CHUNK)
        for sg in range(SG):
            zu_ref[r, sg] = zu[:, sg * LANES:(sg + 1) * LANES]
            for s in range(CHUNK):
                col = sg * SG_IN + s * LANES
                u2_ref[chunk_rows, col:col + LANES] = (
                    zu_ref[r, sg, pl.ds(s, ROW_SUB // CHUNK, stride=CHUNK), :].astype(bf16))


def _inproj(x, gain, w_in, cos, sin_a, sin_b, tm=512):
    L = x.shape[0]
    steps = L // tm
    gate_w = w_in.shape[1] - IN_MIX_W
    slab = gate_w // steps
    assert IN_MIX_W % slab == 0 and slab % LANES == 0
    row = lambda w: pl.BlockSpec((tm, w), lambda i: (i, 0))
    return pl.pallas_call(
        _inproj_kernel,
        grid=(steps,),
        in_specs=[row(D_MODEL), _const_spec((1, D_MODEL)),
                  pl.BlockSpec((D_MODEL, IN_MIX_W), lambda i: (0, 0), pipeline_mode=pl.Buffered(1)),
                  pl.BlockSpec((D_MODEL, slab), lambda i: (0, IN_MIX_W // slab + i)),
                  row(LANES), row(LANES), row(LANES)],
        out_specs=[row(Q_W), row(KV_W), row(KV_W),
                   pl.BlockSpec((tm // CHUNK, SG * SG_IN), lambda i: (i, 0)), row(D_MODEL),
                   pl.BlockSpec((D_MODEL, slab), lambda i: (0, i))],
        out_shape=[jax.ShapeDtypeStruct((L, Q_W), bf16), jax.ShapeDtypeStruct((L, KV_W), bf16),
                   jax.ShapeDtypeStruct((L, KV_W), bf16),
                   jax.ShapeDtypeStruct((L // CHUNK, SG * SG_IN), bf16),
                   jax.ShapeDtypeStruct((L, D_MODEL), bf16),
                   jax.ShapeDtypeStruct((D_MODEL, gate_w), bf16)],
        scratch_shapes=[pltpu.VMEM((D_MODEL, QKV_W), bf16), pltpu.VMEM((D_MODEL, SSM_W), bf16),
                        pltpu.VMEM((tm // ROW_SUB, SG, ROW_SUB, LANES), f32)],
        compiler_params=_params(("arbitrary",)),
        name="inproj",
    )(x, gain, w_in, w_in, cos, sin_a, sin_b)


def _gates_kernel(h_ref, w_ref, wup_ref, wdn_ref, o_ref, wup_o_ref, wdn_o_ref):
    o_ref[...] = _sigmoid(_dot(h_ref[...], w_ref[...])).astype(bf16)
    wup_o_ref[...] = wup_ref[...].astype(bf16)
    wdn_o_ref[...] = wdn_ref[...].astype(bf16)


def _gates(h, wg, w_up, w_down, tm=1024, tn=1024):
    L, N = h.shape[0], wg.shape[1]
    nj = N // tn
    steps = (L // tm) * nj
    up_slab, dn_slab = w_up.shape[1] // steps, w_down.shape[0] // steps
    step = lambda i, j: i * nj + j
    return pl.pallas_call(
        _gates_kernel,
        grid=(L // tm, nj),
        in_specs=[pl.BlockSpec((tm, D_MODEL), lambda i, j: (i, 0)),
                  pl.BlockSpec((D_MODEL, tn), lambda i, j: (0, j)),
                  pl.BlockSpec((w_up.shape[0], up_slab), lambda i, j: (0, step(i, j))),
                  pl.BlockSpec((dn_slab, w_down.shape[1]), lambda i, j: (step(i, j), 0))],
        out_specs=[pl.BlockSpec((tm, tn), lambda i, j: (i, j)),
                   pl.BlockSpec((w_up.shape[0], up_slab), lambda i, j: (0, step(i, j))),
                   pl.BlockSpec((dn_slab, w_down.shape[1]), lambda i, j: (step(i, j), 0))],
        out_shape=[jax.ShapeDtypeStruct((L, N), bf16), jax.ShapeDtypeStruct(w_up.shape, bf16),
                   jax.ShapeDtypeStruct(w_down.shape, bf16)],
        compiler_params=_params(("parallel", "arbitrary")),
        name="gates",
    )(h, wg, w_up, w_down)


def _attn_kernel(sinks_ref, q_ref, kp_ref, kc_ref, vp_ref, vc_ref, *rest):
    n_cast = (len(rest) - 1) // 2
    o_ref = rest[n_cast]
    for src, dst in zip(rest[:n_cast], rest[n_cast + 1:]):
        dst[...] = src[...].astype(bf16)
    n = pl.program_id(0)
    q = q_ref[...] * jnp.asarray(1.0 / math.sqrt(HEAD_DIM), bf16)
    kcat = jnp.concatenate([kp_ref[...], kc_ref[...]], axis=0).astype(f32)
    vcat = jnp.concatenate([vp_ref[...], vc_ref[...]], axis=0).astype(f32)
    k_sw = pltpu.roll(kcat, HEAD_DIM, axis=1)
    v_sw = pltpu.roll(vcat, HEAD_DIM, axis=1)
    lane_kv = lax.broadcasted_iota(jnp.int32, kcat.shape, 1)
    low_kv = lane_kv < HEAD_DIM
    qi = lax.broadcasted_iota(jnp.int32, (BLOCK, 2 * BLOCK), 0)
    kj = lax.broadcasted_iota(jnp.int32, (BLOCK, 2 * BLOCK), 1)
    first_key = jnp.where(n > 0, 0, BLOCK)
    allowed = (kj > qi) & (kj <= qi + BLOCK) & (kj >= first_key)
    low_o = lax.broadcasted_iota(jnp.int32, (BLOCK, LANES), 1) < HEAD_DIM
    neg = jnp.finfo(f32).min
    contract_lanes = (((1,), (1,)), ((), ()))
    pairs_per_kv = (N_Q_HEADS // N_KV_HEADS) // 2
    for hk in range(N_KV_HEADS):
        k_src, k_oth = (kcat, k_sw) if hk == 0 else (k_sw, kcat)
        k_lo = jnp.where(low_kv, k_src, 0.0).astype(bf16)
        k_hi = jnp.where(low_kv, 0.0, k_oth).astype(bf16)
        vv = (jnp.where(low_kv, vcat, v_sw) if hk == 0 else jnp.where(low_kv, v_sw, vcat)).astype(bf16)
        qs = jnp.concatenate(
            [q[:, (pairs_per_kv * hk + b) * LANES:(pairs_per_kv * hk + b + 1) * LANES]
             for b in range(pairs_per_kv)], axis=0)
        s_par = (lax.dot_general(qs, k_lo, contract_lanes, preferred_element_type=f32),
                 lax.dot_general(qs, k_hi, contract_lanes, preferred_element_type=f32))
        for b in range(pairs_per_kv):
            outs = []
            for par in range(2):
                head = 2 * (pairs_per_kv * hk + b) + par
                s = jnp.where(allowed, s_par[par][b * BLOCK:(b + 1) * BLOCK], neg)
                sink = sinks_ref[head]
                m = jnp.maximum(jnp.max(s, axis=1, keepdims=True), sink)
                p = jnp.exp(s - m)
                denom = jnp.sum(p, axis=1, keepdims=True) + jnp.exp(sink - m)
                outs.append(_dot(p.astype(bf16), vv) * (1.0 / denom))
            col = (pairs_per_kv * hk + b) * LANES
            o_ref[:, col:col + LANES] = jnp.where(low_o, outs[0], outs[1]).astype(bf16)


def _attention(q, k, v, sinks, cast_weights):
    L = q.shape[0]
    steps = L // BLOCK
    cur = lambda n: (n, 0)
    prev = lambda n: (jnp.maximum(n - 1, 0), 0)
    slabs = [pl.BlockSpec((w.shape[0] // steps, w.shape[1]), cur) for w in cast_weights]
    outs = pl.pallas_call(
        _attn_kernel,
        grid=(steps,),
        in_specs=[pl.BlockSpec(memory_space=pltpu.SMEM),
                  pl.BlockSpec((BLOCK, Q_W), cur),
                  pl.BlockSpec((BLOCK, KV_W), prev), pl.BlockSpec((BLOCK, KV_W), cur),
                  pl.BlockSpec((BLOCK, KV_W), prev), pl.BlockSpec((BLOCK, KV_W), cur)] + slabs,
        out_specs=[pl.BlockSpec((BLOCK, Q_W), cur)] + slabs,
        out_shape=[jax.ShapeDtypeStruct((L, Q_W), bf16)]
                  + [jax.ShapeDtypeStruct(w.shape, bf16) for w in cast_weights],
        compiler_params=_params(("parallel",)),
        name="swa_attention",
    )(sinks, q, k, k, v, v, *cast_weights)
    return outs[0], outs[1:]


GC_SHIFT = SSM_GC.bit_length() - 1
P_SHIFT = SSM_P.bit_length() - 1
Y_TILE = 512
NT = SG_STATE // LANES
NSEG = 8
SEG = N_CHUNKS // NSEG
SEG_PITCH = SEG + 8


def _ssm_kernel(u_ref, pre_ref, pim_ref, kt_ref, qc_ref, rep_ref, ar_ref, ai_ref, y_ref,
                p_scr, m_scr, q_scr, s_scr, xl_scr, pw_scr, xs_scr):
    half = SG_STATE // 2
    row = lax.broadcasted_iota(jnp.int32, (SG_IN, LANES), 0)
    lane = lax.broadcasted_iota(jnp.int32, (SG_IN, LANES), 1)
    row_grp = (row >> GC_SHIFT) & (GPS - 1)
    pre, pim = pre_ref[...], pim_ref[...]
    for k in range(half // LANES):
        sel = row_grp == 2 * k + (lane >> P_SHIFT)
        p_scr[:, k * LANES:(k + 1) * LANES] = jnp.where(sel, pre, 0.0).astype(bf16)
        p_scr[:, half + k * LANES:half + (k + 1) * LANES] = jnp.where(sel, pim, 0.0).astype(bf16)

    blk_row = lax.broadcasted_iota(jnp.int32, (LANES, LANES), 0) >> GC_SHIFT
    blk_lane = lax.broadcasted_iota(jnp.int32, (LANES, LANES), 1) >> GC_SHIFT
    zero_blk = jnp.zeros((LANES, LANES), bf16)
    for tau in range(CHUNK):
        blk = jnp.where(blk_row == blk_lane, kt_ref[tau * LANES:(tau + 1) * LANES, :], 0.0).astype(bf16)
        for s in range(CHUNK - tau):
            t = s + tau
            m_scr[s * LANES:(s + 1) * LANES, t * LANES:(t + 1) * LANES] = blk
            if tau > 0:
                m_scr[t * LANES:(t + 1) * LANES, s * LANES:(s + 1) * LANES] = zero_blk

    q_row_grp = (lax.broadcasted_iota(jnp.int32, (SG_STATE, Y_TILE), 0) >> P_SHIFT) & (GPS - 1)
    q_lane_grp = (lax.broadcasted_iota(jnp.int32, (SG_STATE, Y_TILE), 1) >> GC_SHIFT) & (GPS - 1)
    for j in range(SG_IN // Y_TILE):
        full = _dot(qc_ref[...], rep_ref[:, j * Y_TILE:(j + 1) * Y_TILE])
        q_scr[:, j * Y_TILE:(j + 1) * Y_TILE] = jnp.where(q_row_grp == q_lane_grp, full, 0.0).astype(bf16)

    s = _dot(u_ref[...], p_scr[...])
    for j in range(NT):
        for k in range(NSEG):
            s_scr[j, k * SEG_PITCH:k * SEG_PITCH + SEG, :] = s[k * SEG:(k + 1) * SEG, j * LANES:(j + 1) * LANES]
    ht = NT // 2
    cmul = lambda xr, xi, yr, yi: (xr * yr - xi * yi, xr * yi + xi * yr)
    ar = [ar_ref[:, j * LANES:(j + 1) * LANES] for j in range(ht)]
    ai = [ai_ref[:, j * LANES:(j + 1) * LANES] for j in range(ht)]
    one, nil = jnp.ones((1, LANES), f32), jnp.zeros((1, LANES), f32)
    a_seg = []
    for j in range(ht):
        rr, ri = [one], [nil]
        for _ in range(7):
            nr, ni = cmul(rr[-1], ri[-1], ar[j], ai[j])
            rr.append(nr)
            ri.append(ni)
        a8r, a8i = cmul(rr[-1], ri[-1], ar[j], ai[j])
        blk_r, blk_i = jnp.concatenate(rr, axis=0), jnp.concatenate(ri, axis=0)
        for b in range(SEG // 8):
            pw_scr[j, b * 8:(b + 1) * 8, :] = blk_r
            pw_scr[ht + j, b * 8:(b + 1) * 8, :] = blk_i
            blk_r, blk_i = cmul(blk_r, blk_i, a8r, a8i)
        a_seg.append((blk_r[0:1], blk_i[0:1]))

    ar_b = [jnp.broadcast_to(a, (NSEG, LANES)) for a in ar]
    ai_b = [jnp.broadcast_to(a, (NSEG, LANES)) for a in ai]

    def body(i, carry):
        new_r, new_i = [], []
        for j in range(ht):
            xr, xi = carry[j], carry[ht + j]
            seg_rows = pl.ds(i, NSEG, stride=SEG_PITCH)
            xl_scr[j, seg_rows, :] = xr
            xl_scr[ht + j, seg_rows, :] = xi
            pr, pi = cmul(xr, xi, ar_b[j], ai_b[j])
            new_r.append(pr + s_scr[j, seg_rows, :])
            new_i.append(pi + s_scr[ht + j, seg_rows, :])
        return tuple(new_r + new_i)

    ends = lax.fori_loop(0, SEG, body, tuple(jnp.zeros((NSEG, LANES), f32) for _ in range(NT)))

    for j in range(ht):
        zr, zi = nil, nil
        for k in range(NSEG):
            rows = slice(k * SEG, (k + 1) * SEG)
            loc = slice(k * SEG_PITCH, k * SEG_PITCH + SEG)
            fr, fi = cmul(pw_scr[j], pw_scr[ht + j], zr, zi)
            xs_scr[rows, j * LANES:(j + 1) * LANES] = (xl_scr[j, loc, :] + fr).astype(bf16)
            xs_scr[rows, half + j * LANES:half + (j + 1) * LANES] = (xl_scr[ht + j, loc, :] + fi).astype(bf16)
            pr, pi = cmul(zr, zi, *a_seg[j])
            zr, zi = pr + ends[j][k:k + 1], pi + ends[ht + j][k:k + 1]

    xs = xs_scr[...]
    for j in range(SG_IN // Y_TILE):
        kk = (j + 1) * Y_TILE
        cols = slice(j * Y_TILE, (j + 1) * Y_TILE)
        y = _dot(u_ref[:, :kk], m_scr[:kk, cols]) + _dot(xs, q_scr[:, cols])
        y_ref[:, cols] = _gelu_tanh(y).astype(bf16)


def _ssm(u2, pre, pim, kt, qc, rep, at_re, at_im):
    per_sg = lambda r, c: pl.BlockSpec((None, r, c), lambda g: (g, 0, 0))
    return pl.pallas_call(
        _ssm_kernel,
        grid=(SG,),
        in_specs=[pl.BlockSpec((N_CHUNKS, SG_IN), lambda g: (0, g)),
                  per_sg(SG_IN, LANES), per_sg(SG_IN, LANES), per_sg(SG_IN, LANES),
                  per_sg(SG_STATE, CHUNK * SSM_GC), _const_spec(rep.shape),
                  per_sg(1, SG_STATE // 2), per_sg(1, SG_STATE // 2)],
        out_specs=pl.BlockSpec((N_CHUNKS, SG_IN), lambda g: (0, g)),
        out_shape=jax.ShapeDtypeStruct((N_CHUNKS, SG * SG_IN), bf16),
        scratch_shapes=[pltpu.VMEM((SG_IN, SG_STATE), bf16), pltpu.VMEM((SG_IN, SG_IN), bf16),
                        pltpu.VMEM((SG_STATE, SG_IN), bf16),
                        pltpu.VMEM((NT, NSEG * SEG_PITCH, LANES), f32),
                        pltpu.VMEM((NT, NSEG * SEG_PITCH, LANES), f32),
                        pltpu.VMEM((NT, SEG, LANES), f32), pltpu.VMEM((N_CHUNKS, SG_STATE), bf16)],
        compiler_params=_params(("arbitrary",)),
        name="ssm",
    )(u2, pre, pim, kt, qc, rep, at_re, at_im)


def _ssm_operators(lam_re, lam_im, log_dt, b_re, b_im, c_re, c_im, d_skip):
    hi = lax.Precision.HIGHEST
    lr, li = lam_re.astype(f32), lam_im.astype(f32)
    dt = jnp.exp(log_dt.astype(f32))[:, None]
    mag = jnp.exp(lr * dt)
    a_re, a_im = mag * jnp.cos(li * dt), mag * jnp.sin(li * dt)
    den = lr * lr + li * li
    nr, ni = a_re - 1.0, a_im
    coef_re = (nr * lr + ni * li) / den
    coef_im = (ni * lr - nr * li) / den
    brt, bit = b_re.astype(f32).swapaxes(1, 2), b_im.astype(f32).swapaxes(1, 2)
    bb_re = coef_re[:, None, :] * brt - coef_im[:, None, :] * bit
    bb_im = coef_re[:, None, :] * bit + coef_im[:, None, :] * brt
    cr, ci = c_re.astype(f32), c_im.astype(f32)

    pw_re, pw_im = [jnp.ones_like(a_re)], [jnp.zeros_like(a_im)]
    for _ in range(CHUNK):
        pr, pi = pw_re[-1], pw_im[-1]
        pw_re.append(pr * a_re - pi * a_im)
        pw_im.append(pr * a_im + pi * a_re)
    pw_re, pw_im = jnp.stack(pw_re, axis=1), jnp.stack(pw_im, axis=1)

    pr, pi = pw_re[:, :CHUNK, None, :], pw_im[:, :CHUNK, None, :]
    ab_re = pr * bb_re[:, None] - pi * bb_im[:, None]
    ab_im = pr * bb_im[:, None] + pi * bb_re[:, None]
    kern = (jnp.einsum('gtcp,gop->gtco', ab_re, cr, precision=hi)
            - jnp.einsum('gtcp,gop->gtco', ab_im, ci, precision=hi))
    kern = kern.at[:, 0].add(d_skip.astype(f32)[:, :, None] * jnp.eye(SSM_GC, dtype=f32)[None])
    kt = kern.reshape(SG, GPS, CHUNK, SSM_GC, SSM_GC).transpose(0, 2, 1, 3, 4)
    kt = jnp.tile(kt.reshape(SG, SG_IN, SSM_GC), (1, 1, GPS))

    def p_compact(ab):
        x = ab[:, ::-1].reshape(SG, GPS, CHUNK, SSM_GC, SSM_P).transpose(0, 2, 1, 3, 4)
        x = x.reshape(SG, SG_IN, SSM_P)
        return jnp.concatenate([x, x], axis=-1)

    crt, cit = cr.swapaxes(1, 2)[:, :, None, :], ci.swapaxes(1, 2)[:, :, None, :]
    qr, qi = pw_re[:, 1:].swapaxes(1, 2)[..., None], pw_im[:, 1:].swapaxes(1, 2)[..., None]
    q_compact = lambda ca: ca.reshape(SG, GPS * SSM_P, CHUNK * SSM_GC)
    qc = jnp.concatenate([q_compact(crt * qr - cit * qi), q_compact(-(crt * qi + cit * qr))],
                         axis=1).astype(bf16)

    at_re = pw_re[:, CHUNK].reshape(SG, 1, GPS * SSM_P)
    at_im = pw_im[:, CHUNK].reshape(SG, 1, GPS * SSM_P)
    return p_compact(ab_re), p_compact(ab_im), kt, qc, at_re, at_im


def _replication_matrix():
    src = jnp.arange(CHUNK * SSM_GC)
    dst = jnp.arange(SG_IN)
    same_t = (src[:, None] // SSM_GC) == (dst[None, :] // LANES)
    same_c = (src[:, None] % SSM_GC) == (dst[None, :] % SSM_GC)
    return (same_t & same_c).astype(bf16)


def _mix_kernel(oa_ref, y2_ref, ga_ref, gs_ref, wglu_ref, wb_ref, m_ref, yb_ref):
    tm = oa_ref.shape[0]
    for sg in range(SG):
        for t in range(CHUNK):
            col = sg * SG_IN + t * LANES
            yb_ref[sg, pl.ds(t, tm // CHUNK, stride=CHUNK), :] = y2_ref[:, col:col + LANES].astype(f32)
    y = jnp.concatenate([yb_ref[sg] for sg in range(SG)], axis=1).astype(bf16)
    zg = _dot(y, wglu_ref[...])
    o_ssm = zg[:, :SSM_W] * _sigmoid(zg[:, SSM_W:])
    y_s = _dot(o_ssm.astype(bf16), wb_ref[Q_W:, :])
    y_a = _dot(oa_ref[...], wb_ref[:Q_W, :])
    m_ref[...] = (ga_ref[...].astype(f32) * y_a + gs_ref[...].astype(f32) * y_s).astype(bf16)


def _mix(o_attn, y2, gates, wglu, wb, tm=512):
    L = o_attn.shape[0]
    return pl.pallas_call(
        _mix_kernel,
        grid=(L // tm,),
        in_specs=[pl.BlockSpec((tm, Q_W), lambda i: (i, 0)),
                  pl.BlockSpec((tm // CHUNK, SG * SG_IN), lambda i: (i, 0)),
                  pl.BlockSpec((tm, D_MODEL), lambda i: (i, 0)),
                  pl.BlockSpec((tm, D_MODEL), lambda i: (i, 1)),
                  _const_spec(wglu.shape), _const_spec(wb.shape)],
        out_specs=pl.BlockSpec((tm, D_MODEL), lambda i: (i, 0)),
        out_shape=jax.ShapeDtypeStruct((L, D_MODEL), bf16),
        scratch_shapes=[pltpu.VMEM((SG, tm, LANES), f32)],
        compiler_params=_params(("parallel",)),
        name="mix",
    )(o_attn, y2, gates, gates, wglu, wb)


def _outproj_kernel(m_ref, w_ref, x_ref, gpost_ref, gpre_ref, x1_ref, h2_ref):
    for r in range(m_ref.shape[0] // OUT_SUB):
        rows = pl.ds(r * OUT_SUB, OUT_SUB)
        out = _dot(m_ref[rows, :], w_ref[...])
        x1 = x_ref[rows, :] + _rms_norm(out, gpost_ref[...])
        x1_ref[rows, :] = x1
        h2_ref[rows, :] = _rms_norm(x1, gpre_ref[...]).astype(bf16)


def _outproj(m, w_out, x, g_post, g_pre, tm=512):
    L = x.shape[0]
    row = lambda: pl.BlockSpec((tm, D_MODEL), lambda i: (i, 0))
    return pl.pallas_call(
        _outproj_kernel,
        grid=(L // tm,),
        in_specs=[row(), _const_spec(w_out.shape), row(),
                  _const_spec((1, D_MODEL)), _const_spec((1, D_MODEL))],
        out_specs=[row(), row()],
        out_shape=[jax.ShapeDtypeStruct((L, D_MODEL), f32), jax.ShapeDtypeStruct((L, D_MODEL), bf16)],
        compiler_params=_params(("parallel",)),
        name="outproj",
    )(m, w_out, x, g_post, g_pre)


MLP_SUB = 512
MLP_COLS = 512


def _mlp_kernel(h_ref, wu_ref, wd_ref, x_ref, g_ref, o_ref):
    j = pl.program_id(1)
    tm = h_ref.shape[0]

    @pl.when(j == 0)
    def _():
        o_ref[...] = jnp.zeros_like(o_ref)

    for r in range(tm // MLP_SUB):
        rows = pl.ds(r * MLP_SUB, MLP_SUB)
        a = jnp.maximum(_dot(h_ref[rows, :], wu_ref[...]), 0.0)
        a = (a * a).astype(bf16)
        for c in range(D_MODEL // MLP_COLS):
            cols = slice(c * MLP_COLS, (c + 1) * MLP_COLS)
            o_ref[rows, cols] += _dot(a, wd_ref[:, cols])

    @pl.when(j == pl.num_programs(1) - 1)
    def _():
        for r in range(tm // MLP_SUB):
            rows = pl.ds(r * MLP_SUB, MLP_SUB)
            o_ref[rows, :] = x_ref[rows, :] + _rms_norm(o_ref[rows, :], g_ref[...])


def _mlp(h2, w_up, w_down, x1, g_post, tm=1024, tf=1024):
    L = x1.shape[0]
    return pl.pallas_call(
        _mlp_kernel,
        grid=(L // tm, D_FF // tf),
        in_specs=[pl.BlockSpec((tm, D_MODEL), lambda i, j: (i, 0)),
                  pl.BlockSpec((D_MODEL, tf), lambda i, j: (0, j)),
                  pl.BlockSpec((tf, D_MODEL), lambda i, j: (j, 0)),
                  pl.BlockSpec((tm, D_MODEL), lambda i, j: (i, 0), pipeline_mode=pl.Buffered(1)),
                  _const_spec((1, D_MODEL))],
        out_specs=pl.BlockSpec((tm, D_MODEL), lambda i, j: (i, 0)),
        out_shape=jax.ShapeDtypeStruct((L, D_MODEL), f32),
        compiler_params=_params(("parallel", "arbitrary")),
        name="mlp",
    )(h2, w_up, w_down, x1, g_post)


def _rope_tables(L):
    half = ROT_DIM // 2
    inv = ROPE_THETA ** (-jnp.arange(half, dtype=f32) * 2.0 / ROT_DIM)
    ang = jnp.arange(L).astype(f32)[:, None] * inv[None, :]
    cos, sin = jnp.cos(ang), jnp.sin(ang)
    ones = jnp.ones((L, HEAD_DIM - ROT_DIM), f32)
    zeros = jnp.zeros((L, HEAD_DIM - ROT_DIM), f32)
    zh = jnp.zeros((L, half), f32)
    per_head = lambda parts: jnp.tile(jnp.concatenate(parts, axis=1), (1, LANES // HEAD_DIM))
    return (per_head([cos, cos, ones]), per_head([-sin, zh, zeros]), per_head([zh, sin, zeros]))


def _layer(x, norm_mix_pre, norm_mix_post, norm_mlp_pre, norm_mlp_post, w_in, sinks,
           lam_re, lam_im, log_dt, b_re, b_im, c_re, c_im, d_skip, w_glu, w_branch, w_out,
           w_up, w_down, rope):
    gain = lambda g: g.astype(f32).reshape(1, D_MODEL)
    q, k, v, u2, h, wg = _inproj(x, gain(norm_mix_pre), w_in.astype(f32), *rope)
    gates, wup, wdn = _gates(h, wg, w_up.astype(f32), w_down.astype(f32))
    o_attn, (wglu, wb, wo) = _attention(q, k, v, sinks.astype(f32),
                                        [w_glu.astype(f32), w_branch.astype(f32), w_out.astype(f32)])
    pre, pim, kt, qc, at_re, at_im = _ssm_operators(lam_re, lam_im, log_dt, b_re, b_im, c_re, c_im, d_skip)
    y2 = _ssm(u2, pre, pim, kt, qc, _replication_matrix(), at_re, at_im)
    mixed = _mix(o_attn, y2, gates, wglu, wb)
    x1, h2 = _outproj(mixed, wo, x, gain(norm_mix_post), gain(norm_mlp_pre))
    return _mlp(h2, wup, wdn, x1, gain(norm_mlp_post))


def kernel(x, norm_mix_pre, norm_mix_post, norm_mlp_pre, norm_mlp_post, w_in, sinks, lam_re, lam_im, log_dt, b_re, b_im, c_re, c_im, d_skip, w_glu, w_branch, w_out, w_up, w_down):
    B, L, _ = x.shape
    depth = w_in.shape[0]
    rope = _rope_tables(L)
    outs = []
    for b in range(B):
        xb = x[b]
        for l in range(depth):
            xb = _layer(xb, norm_mix_pre[l], norm_mix_post[l], norm_mlp_pre[l], norm_mlp_post[l],
                        w_in[l], sinks[l], lam_re[l], lam_im[l], log_dt[l], b_re[l], b_im[l],
                        c_re[l], c_im[l], d_skip[l], w_glu[l], w_branch[l], w_out[l],
                        w_up[l], w_down[l], rope)
        outs.append(xb)
    return jnp.stack(outs)
```

```python
import math

import jax
import jax.numpy as jnp
from jax import lax
from jax.experimental import pallas as pl
from jax.experimental.pallas import tpu as pltpu

D_MODEL = 2048
SEQ = 8192
HEAD_DIM = 64
N_Q_HEADS = 16
N_KV_HEADS = 2
BLOCK = 128
ROT_DIM = HEAD_DIM // 4
ROPE_THETA = 500000.0
Q_W = N_Q_HEADS * HEAD_DIM
KV_W = N_KV_HEADS * HEAD_DIM
SSM_W = D_MODEL // 2
QKV_W = Q_W + 2 * KV_W
IN_MIX_W = QKV_W + SSM_W
SSM_GC = 16
SSM_G = SSM_W // SSM_GC
SSM_P = 64
D_FF = 4 * D_MODEL
EPS = 1e-6

LANES = 128
CHUNK = 16
N_CHUNKS = SEQ // CHUNK
SG = SSM_W // LANES
GPS = LANES // SSM_GC
SG_IN = CHUNK * LANES
SG_STATE = 2 * GPS * SSM_P
VMEM_LIMIT = 56 * 1024 * 1024
ROW_SUB = 256
OUT_SUB = 128

bf16 = jnp.bfloat16
f32 = jnp.float32


def _dot(a, b):
    return jnp.dot(a, b, preferred_element_type=f32)


def _sigmoid(x):
    return 1.0 / (1.0 + jnp.exp(-x))


def _gelu_tanh(x):
    c = math.sqrt(2.0 / math.pi)
    return x * (0.5 * (1.0 + jnp.tanh(c * (x + 0.044715 * (x * x * x)))))


def _rms_norm(x, g):
    return x * lax.rsqrt(jnp.mean(x * x, axis=-1, keepdims=True) + EPS) * g


def _params(sem):
    return pltpu.CompilerParams(dimension_semantics=sem, vmem_limit_bytes=VMEM_LIMIT)


def _const_spec(shape):
    nd = len(shape)
    return pl.BlockSpec(shape, lambda *_: (0,) * nd, pipeline_mode=pl.Buffered(1))


def _rope_block(z, cos, sin_a, sin_b):
    return (z * cos + pltpu.roll(z, LANES - ROT_DIM // 2, axis=1) * sin_a
            + pltpu.roll(z, ROT_DIM // 2, axis=1) * sin_b)


def _inproj_kernel(x_ref, g_ref, w_ref, wcast_ref, cos_ref, sa_ref, sb_ref,
                   q_ref, k_ref, v_ref, u2_ref, h_ref, wg_ref, wqkv_ref, wu_ref, zu_ref):
    @pl.when(pl.program_id(0) == 0)
    def _():
        wqkv_ref[...] = w_ref[:, :QKV_W].astype(bf16)
        wu_ref[...] = w_ref[:, QKV_W:].astype(bf16)

    wg_ref[...] = wcast_ref[...].astype(bf16)
    for r in range(x_ref.shape[0] // ROW_SUB):
        rows = pl.ds(r * ROW_SUB, ROW_SUB)
        h = _rms_norm(x_ref[rows, :], g_ref[...]).astype(bf16)
        h_ref[rows, :] = h
        cos, sin_a, sin_b = cos_ref[rows, :], sa_ref[rows, :], sb_ref[rows, :]
        for half in range(2):
            zq = _dot(h, wqkv_ref[:, half * 512:(half + 1) * 512])
            for b in range(4):
                blk = zq[:, b * LANES:(b + 1) * LANES]
                col = (half * 4 + b) * LANES
                q_ref[rows, col:col + LANES] = _rope_block(blk, cos, sin_a, sin_b).astype(bf16)
        zkv = _dot(h, wqkv_ref[:, Q_W:Q_W + 2 * KV_W])
        k_ref[rows, :] = _rope_block(zkv[:, :KV_W], cos, sin_a, sin_b).astype(bf16)
        v_ref[rows, :] = zkv[:, KV_W:].astype(bf16)
        zu = _dot(h, wu_ref[...])
        chunk_rows = pl.ds(r * (ROW_SUB // CHUNK), ROW_SUB // CHUNK)
        for sg in range(SG):
            zu_ref[r, sg] = zu[:, sg * LANES:(sg + 1) * LANES]
            for s in range(CHUNK):
                col = sg * SG_IN + s * LANES
                u2_ref[chunk_rows, col:col + LANES] = (
                    zu_ref[r, sg, pl.ds(s, ROW_SUB // CHUNK, stride=CHUNK), :].astype(bf16))


def _inproj(x, gain, w_in, cos, sin_a, sin_b, tm=512):
    L = x.shape[0]
    steps = L // tm
    gate_w = w_in.shape[1] - IN_MIX_W
    slab = gate_w // steps
    assert IN_MIX_W % slab == 0 and slab % LANES == 0
    row = lambda w: pl.BlockSpec((tm, w), lambda i: (i, 0))
    return pl.pallas_call(
        _inproj_kernel,
        grid=(steps,),
        in_specs=[row(D_MODEL), _const_spec((1, D_MODEL)),
                  pl.BlockSpec((D_MODEL, IN_MIX_W), lambda i: (0, 0), pipeline_mode=pl.Buffered(1)),
                  pl.BlockSpec((D_MODEL, slab), lambda i: (0, IN_MIX_W // slab + i)),
                  row(LANES), row(LANES), row(LANES)],
        out_specs=[row(Q_W), row(KV_W), row(KV_W),
                   pl.BlockSpec((tm // CHUNK, SG * SG_IN), lambda i: (i, 0)), row(D_MODEL),
                   pl.BlockSpec((D_MODEL, slab), lambda i: (0, i))],
        out_shape=[jax.ShapeDtypeStruct((L, Q_W), bf16), jax.ShapeDtypeStruct((L, KV_W), bf16),
                   jax.ShapeDtypeStruct((L, KV_W), bf16),
                   jax.ShapeDtypeStruct((L // CHUNK, SG * SG_IN), bf16),
                   jax.ShapeDtypeStruct((L, D_MODEL), bf16),
                   jax.ShapeDtypeStruct((D_MODEL, gate_w), bf16)],
        scratch_shapes=[pltpu.VMEM((D_MODEL, QKV_W), bf16), pltpu.VMEM((D_MODEL, SSM_W), bf16),
                        pltpu.VMEM((tm // ROW_SUB, SG, ROW_SUB, LANES), f32)],
        compiler_params=_params(("arbitrary",)),
        name="inproj",
    )(x, gain, w_in, w_in, cos, sin_a, sin_b)


def _gates_kernel(h_ref, w_ref, wup_ref, wdn_ref, o_ref, wup_o_ref, wdn_o_ref):
    o_ref[...] = _sigmoid(_dot(h_ref[...], w_ref[...])).astype(bf16)
    wup_o_ref[...] = wup_ref[...].astype(bf16)
    wdn_o_ref[...] = wdn_ref[...].astype(bf16)


def _gates(h, wg, w_up, w_down, tm=1024, tn=1024):
    L, N = h.shape[0], wg.shape[1]
    nj = N // tn
    steps = (L // tm) * nj
    up_slab, dn_slab = w_up.shape[1] // steps, w_down.shape[0] // steps
    step = lambda i, j: i * nj + j
    return pl.pallas_call(
        _gates_kernel,
        grid=(L // tm, nj),
        in_specs=[pl.BlockSpec((tm, D_MODEL), lambda i, j: (i, 0)),
                  pl.BlockSpec((D_MODEL, tn), lambda i, j: (0, j)),
                  pl.BlockSpec((w_up.shape[0], up_slab), lambda i, j: (0, step(i, j))),
                  pl.BlockSpec((dn_slab, w_down.shape[1]), lambda i, j: (step(i, j), 0))],
        out_specs=[pl.BlockSpec((tm, tn), lambda i, j: (i, j)),
                   pl.BlockSpec((w_up.shape[0], up_slab), lambda i, j: (0, step(i, j))),
                   pl.BlockSpec((dn_slab, w_down.shape[1]), lambda i, j: (step(i, j), 0))],
        out_shape=[jax.ShapeDtypeStruct((L, N), bf16), jax.ShapeDtypeStruct(w_up.shape, bf16),
                   jax.ShapeDtypeStruct(w_down.shape, bf16)],
        compiler_params=_params(("parallel", "arbitrary")),
        name="gates",
    )(h, wg, w_up, w_down)


def _attn_kernel(sinks_ref, q_ref, kp_ref, kc_ref, vp_ref, vc_ref, *rest):
    n_cast = (len(rest) - 1) // 2
    o_ref = rest[n_cast]
    for src, dst in zip(rest[:n_cast], rest[n_cast + 1:]):
        dst[...] = src[...].astype(bf16)
    n = pl.program_id(0)
    q = q_ref[...] * jnp.asarray(1.0 / math.sqrt(HEAD_DIM), bf16)
    kcat = jnp.concatenate([kp_ref[...], kc_ref[...]], axis=0).astype(f32)
    vcat = jnp.concatenate([vp_ref[...], vc_ref[...]], axis=0).astype(f32)
    k_sw = pltpu.roll(kcat, HEAD_DIM, axis=1)
    v_sw = pltpu.roll(vcat, HEAD_DIM, axis=1)
    lane_kv = lax.broadcasted_iota(jnp.int32, kcat.shape, 1)
    low_kv = lane_kv < HEAD_DIM
    qi = lax.broadcasted_iota(jnp.int32, (BLOCK, 2 * BLOCK), 0)
    kj = lax.broadcasted_iota(jnp.int32, (BLOCK, 2 * BLOCK), 1)
    first_key = jnp.where(n > 0, 0, BLOCK)
    allowed = (kj > qi) & (kj <= qi + BLOCK) & (kj >= first_key)
    low_o = lax.broadcasted_iota(jnp.int32, (BLOCK, LANES), 1) < HEAD_DIM
    neg = jnp.finfo(f32).min
    contract_lanes = (((1,), (1,)), ((), ()))
    pairs_per_kv = (N_Q_HEADS // N_KV_HEADS) // 2
    for hk in range(N_KV_HEADS):
        k_src, k_oth = (kcat, k_sw) if hk == 0 else (k_sw, kcat)
        k_lo = jnp.where(low_kv, k_src, 0.0).astype(bf16)
        k_hi = jnp.where(low_kv, 0.0, k_oth).astype(bf16)
        vv = (jnp.where(low_kv, vcat, v_sw) if hk == 0 else jnp.where(low_kv, v_sw, vcat)).astype(bf16)
        qs = jnp.concatenate(
            [q[:, (pairs_per_kv * hk + b) * LANES:(pairs_per_kv * hk + b + 1) * LANES]
             for b in range(pairs_per_kv)], axis=0)
        s_par = (lax.dot_general(qs, k_lo, contract_lanes, preferred_element_type=f32),
                 lax.dot_general(qs, k_hi, contract_lanes, preferred_element_type=f32))
        for b in range(pairs_per_kv):
            outs = []
            for par in range(2):
                head = 2 * (pairs_per_kv * hk + b) + par
                s = jnp.where(allowed, s_par[par][b * BLOCK:(b + 1) * BLOCK], neg)
                sink = sinks_ref[head]
                m = jnp.maximum(jnp.max(s, axis=1, keepdims=True), sink)
                p = jnp.exp(s - m)
                denom = jnp.sum(p, axis=1, keepdims=True) + jnp.exp(sink - m)
                outs.append(_dot(p.astype(bf16), vv) * (1.0 / denom))
            col = (pairs_per_kv * hk + b) * LANES
            o_ref[:, col:col + LANES] = jnp.where(low_o, outs[0], outs[1]).astype(bf16)


def _attention(q, k, v, sinks, cast_weights):
    L = q.shape[0]
    steps = L // BLOCK
    cur = lambda n: (n, 0)
    prev = lambda n: (jnp.maximum(n - 1, 0), 0)
    slabs = [pl.BlockSpec((w.shape[0] // steps, w.shape[1]), cur) for w in cast_weights]
    outs = pl.pallas_call(
        _attn_kernel,
        grid=(steps,),
        in_specs=[pl.BlockSpec(memory_space=pltpu.SMEM),
                  pl.BlockSpec((BLOCK, Q_W), cur),
                  pl.BlockSpec((BLOCK, KV_W), prev), pl.BlockSpec((BLOCK, KV_W), cur),
                  pl.BlockSpec((BLOCK, KV_W), prev), pl.BlockSpec((BLOCK, KV_W), cur)] + slabs,
        out_specs=[pl.BlockSpec((BLOCK, Q_W), cur)] + slabs,
        out_shape=[jax.ShapeDtypeStruct((L, Q_W), bf16)]
                  + [jax.ShapeDtypeStruct(w.shape, bf16) for w in cast_weights],
        compiler_params=_params(("parallel",)),
        name="swa_attention",
    )(sinks, q, k, k, v, v, *cast_weights)
    return outs[0], outs[1:]


GC_SHIFT = SSM_GC.bit_length() - 1
P_SHIFT = SSM_P.bit_length() - 1
Y_TILE = 512
NT = SG_STATE // LANES
NSEG = 8
SEG = N_CHUNKS // NSEG
SEG_PITCH = SEG + 8


def _ssm_kernel(u_ref, pre_ref, pim_ref, kt_ref, qc_ref, rep_ref, ar_ref, ai_ref, y_ref,
                p_scr, m_scr, q_scr, s_scr, xl_scr, pw_scr, xs_scr):
    half = SG_STATE // 2
    row = lax.broadcasted_iota(jnp.int32, (SG_IN, LANES), 0)
    lane = lax.broadcasted_iota(jnp.int32, (SG_IN, LANES), 1)
    row_grp = (row >> GC_SHIFT) & (GPS - 1)
    pre, pim = pre_ref[...], pim_ref[...]
    for k in range(half // LANES):
        sel = row_grp == 2 * k + (lane >> P_SHIFT)
        p_scr[:, k * LANES:(k + 1) * LANES] = jnp.where(sel, pre, 0.0).astype(bf16)
        p_scr[:, half + k * LANES:half + (k + 1) * LANES] = jnp.where(sel, pim, 0.0).astype(bf16)

    blk_row = lax.broadcasted_iota(jnp.int32, (LANES, LANES), 0) >> GC_SHIFT
    blk_lane = lax.broadcasted_iota(jnp.int32, (LANES, LANES), 1) >> GC_SHIFT
    zero_blk = jnp.zeros((LANES, LANES), bf16)
    for tau in range(CHUNK):
        blk = jnp.where(blk_row == blk_lane, kt_ref[tau * LANES:(tau + 1) * LANES, :], 0.0).astype(bf16)
        for s in range(CHUNK - tau):
            t = s + tau
            m_scr[s * LANES:(s + 1) * LANES, t * LANES:(t + 1) * LANES] = blk
            if tau > 0:
                m_scr[t * LANES:(t + 1) * LANES, s * LANES:(s + 1) * LANES] = zero_blk

    q_row_grp = (lax.broadcasted_iota(jnp.int32, (SG_STATE, Y_TILE), 0) >> P_SHIFT) & (GPS - 1)
    q_lane_grp = (lax.broadcasted_iota(jnp.int32, (SG_STATE, Y_TILE), 1) >> GC_SHIFT) & (GPS - 1)
    for j in range(SG_IN // Y_TILE):
        full = _dot(qc_ref[...], rep_ref[:, j * Y_TILE:(j + 1) * Y_TILE])
        q_scr[:, j * Y_TILE:(j + 1) * Y_TILE] = jnp.where(q_row_grp == q_lane_grp, full, 0.0).astype(bf16)

    s = _dot(u_ref[...], p_scr[...])
    for j in range(NT):
        for k in range(NSEG):
            s_scr[j, k * SEG_PITCH:k * SEG_PITCH + SEG, :] = s[k * SEG:(k + 1) * SEG, j * LANES:(j + 1) * LANES]
    ht = NT // 2
    cmul = lambda xr, xi, yr, yi: (xr * yr - xi * yi, xr * yi + xi * yr)
    ar = [ar_ref[:, j * LANES:(j + 1) * LANES] for j in range(ht)]
    ai = [ai_ref[:, j * LANES:(j + 1) * LANES] for j in range(ht)]
    one, nil = jnp.ones((1, LANES), f32), jnp.zeros((1, LANES), f32)
    a_seg = []
    for j in range(ht):
        rr, ri = [one], [nil]
        for _ in range(7):
            nr, ni = cmul(rr[-1], ri[-1], ar[j], ai[j])
            rr.append(nr)
            ri.append(ni)
        a8r, a8i = cmul(rr[-1], ri[-1], ar[j], ai[j])
        blk_r, blk_i = jnp.concatenate(rr, axis=0), jnp.concatenate(ri, axis=0)
        for b in range(SEG // 8):
            pw_scr[j, b * 8:(b + 1) * 8, :] = blk_r
            pw_scr[ht + j, b * 8:(b + 1) * 8, :] = blk_i
            blk_r, blk_i = cmul(blk_r, blk_i, a8r, a8i)
        a_seg.append((blk_r[0:1], blk_i[0:1]))

    ar_b = [jnp.broadcast_to(a, (NSEG, LANES)) for a in ar]
    ai_b = [jnp.broadcast_to(a, (NSEG, LANES)) for a in ai]

    def body(i, carry):
        new_r, new_i = [], []
        for j in range(ht):
            xr, xi = carry[j], carry[ht + j]
            seg_rows = pl.ds(i, NSEG, stride=SEG_PITCH)
            xl_scr[j, seg_rows, :] = xr
            xl_scr[ht + j, seg_rows, :] = xi
            pr, pi = cmul(xr, xi, ar_b[j], ai_b[j])
            new_r.append(pr + s_scr[j, seg_rows, :])
            new_i.append(pi + s_scr[ht + j, seg_rows, :])
        return tuple(new_r + new_i)

    ends = lax.fori_loop(0, SEG, body, tuple(jnp.zeros((NSEG, LANES), f32) for _ in range(NT)))

    for j in range(ht):
        zr, zi = nil, nil
        for k in range(NSEG):
            rows = slice(k * SEG, (k + 1) * SEG)
            loc = slice(k * SEG_PITCH, k * SEG_PITCH + SEG)
            fr, fi = cmul(pw_scr[j], pw_scr[ht + j], zr, zi)
            xs_scr[rows, j * LANES:(j + 1) * LANES] = (xl_scr[j, loc, :] + fr).astype(bf16)
            xs_scr[rows, half + j * LANES:half + (j + 1) * LANES] = (xl_scr[ht + j, loc, :] + fi).astype(bf16)
            pr, pi = cmul(zr, zi, *a_seg[j])
            zr, zi = pr + ends[j][k:k + 1], pi + ends[ht + j][k:k + 1]

    xs = xs_scr[...]
    for j in range(SG_IN // Y_TILE):
        kk = (j + 1) * Y_TILE
        cols = slice(j * Y_TILE, (j + 1) * Y_TILE)
        y = _dot(u_ref[:, :kk], m_scr[:kk, cols]) + _dot(xs, q_scr[:, cols])
        y_ref[:, cols] = _gelu_tanh(y).astype(bf16)


def _ssm(u2, pre, pim, kt, qc, rep, at_re, at_im):
    per_sg = lambda r, c: pl.BlockSpec((None, r, c), lambda g: (g, 0, 0))
    return pl.pallas_call(
        _ssm_kernel,
        grid=(SG,),
        in_specs=[pl.BlockSpec((N_CHUNKS, SG_IN), lambda g: (0, g)),
                  per_sg(SG_IN, LANES), per_sg(SG_IN, LANES), per_sg(SG_IN, LANES),
                  per_sg(SG_STATE, CHUNK * SSM_GC), _const_spec(rep.shape),
                  per_sg(1, SG_STATE // 2), per_sg(1, SG_STATE // 2)],
        out_specs=pl.BlockSpec((N_CHUNKS, SG_IN), lambda g: (0, g)),
        out_shape=jax.ShapeDtypeStruct((N_CHUNKS, SG * SG_IN), bf16),
        scratch_shapes=[pltpu.VMEM((SG_IN, SG_STATE), bf16), pltpu.VMEM((SG_IN, SG_IN), bf16),
                        pltpu.VMEM((SG_STATE, SG_IN), bf16),
                        pltpu.VMEM((NT, NSEG * SEG_PITCH, LANES), f32),
                        pltpu.VMEM((NT, NSEG * SEG_PITCH, LANES), f32),
                        pltpu.VMEM((NT, SEG, LANES), f32), pltpu.VMEM((N_CHUNKS, SG_STATE), bf16)],
        compiler_params=_params(("arbitrary",)),
        name="ssm",
    )(u2, pre, pim, kt, qc, rep, at_re, at_im)


def _ssm_prep_kernel(lam_re_ref, lam_im_ref, ldt_ref, bt_re_ref, bt_im_ref, yc_ref, d_ref, eye_ref,
                     pre_ref, pim_ref, kt_ref, qc_ref, at_re_ref, at_im_ref):
    lr, li = lam_re_ref[...], lam_im_ref[...]
    dt = jnp.exp(ldt_ref[...])
    mag = jnp.exp(lr * dt)
    a_re, a_im = mag * jnp.cos(li * dt), mag * jnp.sin(li * dt)
    den = lr * lr + li * li
    nr, ni = a_re - 1.0, a_im
    coef_re = (nr * lr + ni * li) / den
    coef_im = (ni * lr - nr * li) / den
    pw_re, pw_im = [jnp.ones_like(a_re)], [jnp.zeros_like(a_im)]
    for _ in range(CHUNK):
        pr, pi = pw_re[-1], pw_im[-1]
        pw_re.append(pr * a_re - pi * a_im)
        pw_im.append(pr * a_im + pi * a_re)

    low = lax.broadcasted_iota(jnp.int32, (SSM_GC, LANES), 1) < SSM_P
    diag = ((lax.broadcasted_iota(jnp.int32, (SSM_GC, LANES), 1) & (SSM_GC - 1))
            == lax.broadcasted_iota(jnp.int32, (SSM_GC, LANES), 0))
    contract_lanes = (((1,), (1,)), ((), ()))
    for g in range(GPS):
        grp = slice(g * SSM_GC, (g + 1) * SSM_GC)
        one = lambda x: x[g:g + 1]
        bb_re = one(coef_re) * bt_re_ref[grp, :] - one(coef_im) * bt_im_ref[grp, :]
        bb_im = one(coef_re) * bt_im_ref[grp, :] + one(coef_im) * bt_re_ref[grp, :]
        yc = yc_ref[grp, :]
        yc_sw = pltpu.roll(yc, SSM_P, axis=1)
        y2 = jnp.where(low, yc_sw, -yc_sw)
        x_rows, ca_rows = [], []
        for tau in range(CHUNK):
            pr, pi = one(pw_re[tau]), one(pw_im[tau])
            ab_re = bb_re * pr - bb_im * pi
            ab_im = bb_re * pi + bb_im * pr
            s = CHUNK - 1 - tau
            rows = slice(s * LANES + g * SSM_GC, s * LANES + (g + 1) * SSM_GC)
            pre_ref[rows, :] = ab_re
            pim_ref[rows, :] = ab_im
            x_rows.append(jnp.where(low, ab_re, ab_im))
            ca_rows.append(yc * one(pw_re[tau + 1]) + y2 * one(pw_im[tau + 1]))
        kern = lax.dot_general(jnp.concatenate(x_rows, axis=0), jnp.concatenate([yc] * GPS, axis=0),
                               contract_lanes, precision=lax.Precision.HIGHEST,
                               preferred_element_type=f32)
        for tau in range(CHUNK):
            blk = kern[tau * SSM_GC:(tau + 1) * SSM_GC]
            if tau == 0:
                blk = blk + jnp.where(diag, d_ref[...], 0.0)
            kt_ref[tau * LANES + g * SSM_GC:tau * LANES + (g + 1) * SSM_GC, :] = blk
        qt = lax.dot_general(eye_ref[...], jnp.concatenate(ca_rows, axis=0).astype(bf16),
                             contract_lanes, preferred_element_type=f32)
        qc_ref[g * SSM_P:(g + 1) * SSM_P, :] = qt[:SSM_P].astype(bf16)
        qc_ref[GPS * SSM_P + g * SSM_P:GPS * SSM_P + (g + 1) * SSM_P, :] = qt[SSM_P:].astype(bf16)
    lo_lane = lax.broadcasted_iota(jnp.int32, (1, LANES), 1) < SSM_P
    for k in range(GPS // 2):
        sel = lambda pw: jnp.where(lo_lane, pw[2 * k:2 * k + 1], pw[2 * k + 1:2 * k + 2])
        at_re_ref[:, k * LANES:(k + 1) * LANES] = sel(pw_re[CHUNK])
        at_im_ref[:, k * LANES:(k + 1) * LANES] = sel(pw_im[CHUNK])


def _ssm_operators(lam_re, lam_im, log_dt, b_re, b_im, c_re, c_im, d_skip):
    dup = lambda x: jnp.concatenate([x, x], axis=-1)
    per_sg = lambda x: x.astype(f32).reshape((SG, x.size // (SG * x.shape[-1]), x.shape[-1]))
    bt = lambda b: per_sg(dup(b.astype(f32).swapaxes(1, 2)))
    yc = per_sg(jnp.concatenate([c_re.astype(f32), -c_im.astype(f32)], axis=-1))
    args = (per_sg(dup(lam_re)), per_sg(dup(lam_im)), log_dt.astype(f32).reshape(SG, GPS, 1),
            bt(b_re), bt(b_im), yc, d_skip.astype(f32).reshape(SG, 1, LANES), jnp.eye(LANES, dtype=bf16))
    blk = lambda r, c: pl.BlockSpec((None, r, c), lambda g: (g, 0, 0))
    return pl.pallas_call(
        _ssm_prep_kernel,
        grid=(SG,),
        in_specs=[blk(GPS, LANES), blk(GPS, LANES), blk(GPS, 1), blk(LANES, LANES), blk(LANES, LANES),
                  blk(LANES, LANES), blk(1, LANES), _const_spec((LANES, LANES))],
        out_specs=[blk(SG_IN, LANES), blk(SG_IN, LANES), blk(SG_IN, LANES),
                   blk(SG_STATE, CHUNK * SSM_GC), blk(1, SG_STATE // 2), blk(1, SG_STATE // 2)],
        out_shape=[jax.ShapeDtypeStruct((SG, SG_IN, LANES), f32)] * 3
                  + [jax.ShapeDtypeStruct((SG, SG_STATE, CHUNK * SSM_GC), bf16)]
                  + [jax.ShapeDtypeStruct((SG, 1, SG_STATE // 2), f32)] * 2,
        compiler_params=_params(("parallel",)),
        name="ssm_prep",
    )(*args)


def _replication_matrix():
    src = jnp.arange(CHUNK * SSM_GC)
    dst = jnp.arange(SG_IN)
    same_t = (src[:, None] // SSM_GC) == (dst[None, :] // LANES)
    same_c = (src[:, None] % SSM_GC) == (dst[None, :] % SSM_GC)
    return (same_t & same_c).astype(bf16)


def _mix_kernel(oa_ref, y2_ref, ga_ref, gs_ref, wglu_ref, wb_ref, m_ref, yb_ref):
    tm = oa_ref.shape[0]
    for sg in range(SG):
        for t in range(CHUNK):
            col = sg * SG_IN + t * LANES
            yb_ref[sg, pl.ds(t, tm // CHUNK, stride=CHUNK), :] = y2_ref[:, col:col + LANES].astype(f32)
    y = jnp.concatenate([yb_ref[sg] for sg in range(SG)], axis=1).astype(bf16)
    zg = _dot(y, wglu_ref[...])
    o_ssm = zg[:, :SSM_W] * _sigmoid(zg[:, SSM_W:])
    y_s = _dot(o_ssm.astype(bf16), wb_ref[Q_W:, :])
    y_a = _dot(oa_ref[...], wb_ref[:Q_W, :])
    m_ref[...] = (ga_ref[...].astype(f32) * y_a + gs_ref[...].astype(f32) * y_s).astype(bf16)


def _mix(o_attn, y2, gates, wglu, wb, tm=512):
    L = o_attn.shape[0]
    return pl.pallas_call(
        _mix_kernel,
        grid=(L // tm,),
        in_specs=[pl.BlockSpec((tm, Q_W), lambda i: (i, 0)),
                  pl.BlockSpec((tm // CHUNK, SG * SG_IN), lambda i: (i, 0)),
                  pl.BlockSpec((tm, D_MODEL), lambda i: (i, 0)),
                  pl.BlockSpec((tm, D_MODEL), lambda i: (i, 1)),
                  _const_spec(wglu.shape), _const_spec(wb.shape)],
        out_specs=pl.BlockSpec((tm, D_MODEL), lambda i: (i, 0)),
        out_shape=jax.ShapeDtypeStruct((L, D_MODEL), bf16),
        scratch_shapes=[pltpu.VMEM((SG, tm, LANES), f32)],
        compiler_params=_params(("parallel",)),
        name="mix",
    )(o_attn, y2, gates, gates, wglu, wb)


def _outproj_kernel(m_ref, w_ref, x_ref, gpost_ref, gpre_ref, x1_ref, h2_ref):
    for r in range(m_ref.shape[0] // OUT_SUB):
        rows = pl.ds(r * OUT_SUB, OUT_SUB)
        out = _dot(m_ref[rows, :], w_ref[...])
        x1 = x_ref[rows, :] + _rms_norm(out, gpost_ref[...])
        x1_ref[rows, :] = x1
        h2_ref[rows, :] = _rms_norm(x1, gpre_ref[...]).astype(bf16)


def _outproj(m, w_out, x, g_post, g_pre, tm=512):
    L = x.shape[0]
    row = lambda: pl.BlockSpec((tm, D_MODEL), lambda i: (i, 0))
    return pl.pallas_call(
        _outproj_kernel,
        grid=(L // tm,),
        in_specs=[row(), _const_spec(w_out.shape), row(),
                  _const_spec((1, D_MODEL)), _const_spec((1, D_MODEL))],
        out_specs=[row(), row()],
        out_shape=[jax.ShapeDtypeStruct((L, D_MODEL), f32), jax.ShapeDtypeStruct((L, D_MODEL), bf16)],
        compiler_params=_params(("parallel",)),
        name="outproj",
    )(m, w_out, x, g_post, g_pre)


MLP_SUB = 512
MLP_COLS = 512


def _mlp_kernel(h_ref, wu_ref, wd_ref, x_ref, g_ref, o_ref):
    j = pl.program_id(1)
    tm = h_ref.shape[0]

    @pl.when(j == 0)
    def _():
        o_ref[...] = jnp.zeros_like(o_ref)

    for r in range(tm // MLP_SUB):
        rows = pl.ds(r * MLP_SUB, MLP_SUB)
        a = jnp.maximum(_dot(h_ref[rows, :], wu_ref[...]), 0.0)
        a = (a * a).astype(bf16)
        for c in range(D_MODEL // MLP_COLS):
            cols = slice(c * MLP_COLS, (c + 1) * MLP_COLS)
            o_ref[rows, cols] += _dot(a, wd_ref[:, cols])

    @pl.when(j == pl.num_programs(1) - 1)
    def _():
        for r in range(tm // MLP_SUB):
            rows = pl.ds(r * MLP_SUB, MLP_SUB)
            o_ref[rows, :] = x_ref[rows, :] + _rms_norm(o_ref[rows, :], g_ref[...])


def _mlp(h2, w_up, w_down, x1, g_post, tm=1024, tf=1024):
    L = x1.shape[0]
    return pl.pallas_call(
        _mlp_kernel,
        grid=(L // tm, D_FF // tf),
        in_specs=[pl.BlockSpec((tm, D_MODEL), lambda i, j: (i, 0)),
                  pl.BlockSpec((D_MODEL, tf), lambda i, j: (0, j)),
                  pl.BlockSpec((tf, D_MODEL), lambda i, j: (j, 0)),
                  pl.BlockSpec((tm, D_MODEL), lambda i, j: (i, 0), pipeline_mode=pl.Buffered(1)),
                  _const_spec((1, D_MODEL))],
        out_specs=pl.BlockSpec((tm, D_MODEL), lambda i, j: (i, 0)),
        out_shape=jax.ShapeDtypeStruct((L, D_MODEL), f32),
        compiler_params=_params(("parallel", "arbitrary")),
        name="mlp",
    )(h2, w_up, w_down, x1, g_post)


def _rope_tables(L):
    half = ROT_DIM // 2
    inv = ROPE_THETA ** (-jnp.arange(half, dtype=f32) * 2.0 / ROT_DIM)
    ang = jnp.arange(L).astype(f32)[:, None] * inv[None, :]
    cos, sin = jnp.cos(ang), jnp.sin(ang)
    ones = jnp.ones((L, HEAD_DIM - ROT_DIM), f32)
    zeros = jnp.zeros((L, HEAD_DIM - ROT_DIM), f32)
    zh = jnp.zeros((L, half), f32)
    per_head = lambda parts: jnp.tile(jnp.concatenate(parts, axis=1), (1, LANES // HEAD_DIM))
    return (per_head([cos, cos, ones]), per_head([-sin, zh, zeros]), per_head([zh, sin, zeros]))


def _layer(x, norm_mix_pre, norm_mix_post, norm_mlp_pre, norm_mlp_post, w_in, sinks,
           lam_re, lam_im, log_dt, b_re, b_im, c_re, c_im, d_skip, w_glu, w_branch, w_out,
           w_up, w_down, rope):
    gain = lambda g: g.astype(f32).reshape(1, D_MODEL)
    q, k, v, u2, h, wg = _inproj(x, gain(norm_mix_pre), w_in.astype(f32), *rope)
    gates, wup, wdn = _gates(h, wg, w_up.astype(f32), w_down.astype(f32))
    o_attn, (wglu, wb, wo) = _attention(q, k, v, sinks.astype(f32),
                                        [w_glu.astype(f32), w_branch.astype(f32), w_out.astype(f32)])
    pre, pim, kt, qc, at_re, at_im = _ssm_operators(lam_re, lam_im, log_dt, b_re, b_im, c_re, c_im, d_skip)
    y2 = _ssm(u2, pre, pim, kt, qc, _replication_matrix(), at_re, at_im)
    mixed = _mix(o_attn, y2, gates, wglu, wb)
    x1, h2 = _outproj(mixed, wo, x, gain(norm_mix_post), gain(norm_mlp_pre))
    return _mlp(h2, wup, wdn, x1, gain(norm_mlp_post))


def kernel(x, norm_mix_pre, norm_mix_post, norm_mlp_pre, norm_mlp_post, w_in, sinks, lam_re, lam_im, log_dt, b_re, b_im, c_re, c_im, d_skip, w_glu, w_branch, w_out, w_up, w_down):
    B, L, _ = x.shape
    depth = w_in.shape[0]
    rope = _rope_tables(L)
    outs = []
    for b in range(B):
        xb = x[b]
        for l in range(depth):
            xb = _layer(xb, norm_mix_pre[l], norm_mix_post[l], norm_mlp_pre[l], norm_mlp_post[l],
                        w_in[l], sinks[l], lam_re[l], lam_im[l], log_dt[l], b_re[l], b_im[l],
                        c_re[l], c_im[l], d_skip[l], w_glu[l], w_branch[l], w_out[l],
                        w_up[l], w_down[l], rope)
        outs.append(xb)
    return jnp.stack(outs)
```

```python
import math

import jax
import jax.numpy as jnp
from jax import lax
from jax.experimental import pallas as pl
from jax.experimental.pallas import tpu as pltpu

D_MODEL = 2048
SEQ = 8192
HEAD_DIM = 64
N_Q_HEADS = 16
N_KV_HEADS = 2
BLOCK = 128
ROT_DIM = HEAD_DIM // 4
ROPE_THETA = 500000.0
Q_W = N_Q_HEADS * HEAD_DIM
KV_W = N_KV_HEADS * HEAD_DIM
SSM_W = D_MODEL // 2
QKV_W = Q_W + 2 * KV_W
IN_MIX_W = QKV_W + SSM_W
SSM_GC = 16
SSM_G = SSM_W // SSM_GC
SSM_P = 64
D_FF = 4 * D_MODEL
EPS = 1e-6

LANES = 128
CHUNK = 16
N_CHUNKS = SEQ // CHUNK
SG = SSM_W // LANES
GPS = LANES // SSM_GC
SG_IN = CHUNK * LANES
SG_STATE = 2 * GPS * SSM_P
VMEM_LIMIT = 56 * 1024 * 1024
ROW_SUB = 256
OUT_SUB = 128
ATT_BLOCKS = 2
MASKED = -1e30

bf16 = jnp.bfloat16
f32 = jnp.float32


def _dot(a, b):
    return jnp.dot(a, b, preferred_element_type=f32)


def _sigmoid(x):
    return 1.0 / (1.0 + jnp.exp(-x))


def _gelu_tanh(x):
    c = math.sqrt(2.0 / math.pi)
    return x * (0.5 * (1.0 + jnp.tanh(c * (x + 0.044715 * (x * x * x)))))


def _rms_norm(x, g):
    return x * lax.rsqrt(jnp.mean(x * x, axis=-1, keepdims=True) + EPS) * g


def _params(sem):
    return pltpu.CompilerParams(dimension_semantics=sem, vmem_limit_bytes=VMEM_LIMIT)


def _const_spec(shape):
    nd = len(shape)
    return pl.BlockSpec(shape, lambda *_: (0,) * nd, pipeline_mode=pl.Buffered(1))


def _rope_block(z, cos, sin_a, sin_b):
    return (z * cos + pltpu.roll(z, LANES - ROT_DIM // 2, axis=1) * sin_a
            + pltpu.roll(z, ROT_DIM // 2, axis=1) * sin_b)


def _inproj_kernel(x_ref, g_ref, w_ref, wcast_ref, cos_ref, sa_ref, sb_ref,
                   q_ref, k_ref, v_ref, u2_ref, h_ref, wg_ref, wqkv_ref, wu_ref, zu_ref):
    @pl.when(pl.program_id(0) == 0)
    def _():
        wqkv_ref[...] = w_ref[:, :QKV_W].astype(bf16)
        wu_ref[...] = w_ref[:, QKV_W:].astype(bf16)

    wg_ref[...] = wcast_ref[...].astype(bf16)
    for r in range(x_ref.shape[0] // ROW_SUB):
        rows = pl.ds(r * ROW_SUB, ROW_SUB)
        h = _rms_norm(x_ref[rows, :], g_ref[...]).astype(bf16)
        h_ref[rows, :] = h
        cos, sin_a, sin_b = cos_ref[rows, :], sa_ref[rows, :], sb_ref[rows, :]
        for half in range(2):
            zq = _dot(h, wqkv_ref[:, half * 512:(half + 1) * 512])
            for b in range(4):
                blk = zq[:, b * LANES:(b + 1) * LANES]
                col = (half * 4 + b) * LANES
                q_ref[rows, col:col + LANES] = _rope_block(blk, cos, sin_a, sin_b).astype(bf16)
        zkv = _dot(h, wqkv_ref[:, Q_W:Q_W + 2 * KV_W])
        k_ref[rows, :] = _rope_block(zkv[:, :KV_W], cos, sin_a, sin_b).astype(bf16)
        v_ref[rows, :] = zkv[:, KV_W:].astype(bf16)
        zu = _dot(h, wu_ref[...])
        chunk_rows = pl.ds(r * (ROW_SUB // CHUNK), ROW_SUB // CHUNK)
        for sg in range(SG):
            zu_ref[r, sg] = zu[:, sg * LANES:(sg + 1) * LANES]
            for s in range(CHUNK):
                col = sg * SG_IN + s * LANES
                u2_ref[chunk_rows, col:col + LANES] = (
                    zu_ref[r, sg, pl.ds(s, ROW_SUB // CHUNK, stride=CHUNK), :].astype(bf16))


def _inproj(x, gain, w_in, cos, sin_a, sin_b, tm=512):
    L = x.shape[0]
    steps = L // tm
    gate_w = w_in.shape[1] - IN_MIX_W
    slab = gate_w // steps
    assert IN_MIX_W % slab == 0 and slab % LANES == 0
    row = lambda w: pl.BlockSpec((tm, w), lambda i: (i, 0))
    return pl.pallas_call(
        _inproj_kernel,
        grid=(steps,),
        in_specs=[row(D_MODEL), _const_spec((1, D_MODEL)),
                  pl.BlockSpec((D_MODEL, IN_MIX_W), lambda i: (0, 0), pipeline_mode=pl.Buffered(1)),
                  pl.BlockSpec((D_MODEL, slab), lambda i: (0, IN_MIX_W // slab + i)),
                  row(LANES), row(LANES), row(LANES)],
        out_specs=[row(Q_W), row(KV_W), row(KV_W),
                   pl.BlockSpec((tm // CHUNK, SG * SG_IN), lambda i: (i, 0)), row(D_MODEL),
                   pl.BlockSpec((D_MODEL, slab), lambda i: (0, i))],
        out_shape=[jax.ShapeDtypeStruct((L, Q_W), bf16), jax.ShapeDtypeStruct((L, KV_W), bf16),
                   jax.ShapeDtypeStruct((L, KV_W), bf16),
                   jax.ShapeDtypeStruct((L // CHUNK, SG * SG_IN), bf16),
                   jax.ShapeDtypeStruct((L, D_MODEL), bf16),
                   jax.ShapeDtypeStruct((D_MODEL, gate_w), bf16)],
        scratch_shapes=[pltpu.VMEM((D_MODEL, QKV_W), bf16), pltpu.VMEM((D_MODEL, SSM_W), bf16),
                        pltpu.VMEM((tm // ROW_SUB, SG, ROW_SUB, LANES), f32)],
        compiler_params=_params(("arbitrary",)),
        name="inproj",
    )(x, gain, w_in, w_in, cos, sin_a, sin_b)


def _gates_kernel(h_ref, w_ref, wup_ref, wdn_ref, o_ref, wup_o_ref, wdn_o_ref):
    for r in range(h_ref.shape[0] // ROW_SUB):
        rows = pl.ds(r * ROW_SUB, ROW_SUB)
        o_ref[rows, :] = _sigmoid(_dot(h_ref[rows, :], w_ref[...])).astype(bf16)
    wup_o_ref[...] = wup_ref[...].astype(bf16)
    wdn_o_ref[...] = wdn_ref[...].astype(bf16)


def _gates(h, wg, w_up, w_down, tm=1024, tn=1024):
    L, N = h.shape[0], wg.shape[1]
    nj = N // tn
    steps = (L // tm) * nj
    up_slab, dn_slab = w_up.shape[1] // steps, w_down.shape[0] // steps
    step = lambda i, j: i * nj + j
    return pl.pallas_call(
        _gates_kernel,
        grid=(L // tm, nj),
        in_specs=[pl.BlockSpec((tm, D_MODEL), lambda i, j: (i, 0)),
                  pl.BlockSpec((D_MODEL, tn), lambda i, j: (0, j)),
                  pl.BlockSpec((w_up.shape[0], up_slab), lambda i, j: (0, step(i, j))),
                  pl.BlockSpec((dn_slab, w_down.shape[1]), lambda i, j: (step(i, j), 0))],
        out_specs=[pl.BlockSpec((tm, tn), lambda i, j: (i, j)),
                   pl.BlockSpec((w_up.shape[0], up_slab), lambda i, j: (0, step(i, j))),
                   pl.BlockSpec((dn_slab, w_down.shape[1]), lambda i, j: (step(i, j), 0))],
        out_shape=[jax.ShapeDtypeStruct((L, N), bf16), jax.ShapeDtypeStruct(w_up.shape, bf16),
                   jax.ShapeDtypeStruct(w_down.shape, bf16)],
        compiler_params=_params(("parallel", "arbitrary")),
        name="gates",
    )(h, wg, w_up, w_down)


def _attn_kernel(sinks_ref, q_ref, kp_ref, kc_ref, vp_ref, vc_ref, *rest):
    n_cast = (len(rest) - 1) // 2
    o_ref = rest[n_cast]
    for src, dst in zip(rest[:n_cast], rest[n_cast + 1:]):
        dst[...] = src[...].astype(bf16)
    step = pl.program_id(0)
    q = q_ref[...] * jnp.asarray(1.0 / math.sqrt(HEAD_DIM), bf16)
    kcat = jnp.concatenate([kp_ref[...], kc_ref[...]], axis=0).astype(f32)
    vcat = jnp.concatenate([vp_ref[...], vc_ref[...]], axis=0).astype(f32)
    k_sw = pltpu.roll(kcat, HEAD_DIM, axis=1)
    v_sw = pltpu.roll(vcat, HEAD_DIM, axis=1)
    low_kv = lax.broadcasted_iota(jnp.int32, kcat.shape, 1) < HEAD_DIM
    from_prev = (lax.broadcasted_iota(jnp.int32, (BLOCK, BLOCK), 1)
                 > lax.broadcasted_iota(jnp.int32, (BLOCK, BLOCK), 0))
    no_prev = jnp.where(step > 0, 0.0, MASKED)
    low_o = lax.broadcasted_iota(jnp.int32, (BLOCK, LANES), 1) < HEAD_DIM
    contract_lanes = (((1,), (1,)), ((), ()))
    pairs_per_kv = (N_Q_HEADS // N_KV_HEADS) // 2
    for hk in range(N_KV_HEADS):
        k_src, k_oth = (kcat, k_sw) if hk == 0 else (k_sw, kcat)
        k_lo = jnp.where(low_kv, k_src, 0.0).astype(bf16)
        k_hi = jnp.where(low_kv, 0.0, k_oth).astype(bf16)
        vv = (jnp.where(low_kv, vcat, v_sw) if hk == 0 else jnp.where(low_kv, v_sw, vcat)).astype(bf16)
        for blk in range(ATT_BLOCKS):
            q_rows = slice(blk * BLOCK, (blk + 1) * BLOCK)
            kv_rows = slice(blk * BLOCK, (blk + 2) * BLOCK)
            qs = jnp.concatenate(
                [q[q_rows, (pairs_per_kv * hk + b) * LANES:(pairs_per_kv * hk + b + 1) * LANES]
                 for b in range(pairs_per_kv)], axis=0)
            s_par = (lax.dot_general(qs, k_lo[kv_rows], contract_lanes, preferred_element_type=f32),
                     lax.dot_general(qs, k_hi[kv_rows], contract_lanes, preferred_element_type=f32))
            for b in range(pairs_per_kv):
                outs = []
                for par in range(2):
                    head = 2 * (pairs_per_kv * hk + b) + par
                    s_all = s_par[par][b * BLOCK:(b + 1) * BLOCK]
                    s_prev = s_all[:, :BLOCK] + no_prev if blk == 0 else s_all[:, :BLOCK]
                    s = jnp.where(from_prev, s_prev, s_all[:, BLOCK:])
                    sink = sinks_ref[head]
                    m = jnp.maximum(jnp.max(s, axis=1, keepdims=True), sink)
                    p = jnp.exp(s - m)
                    denom = jnp.sum(p, axis=1, keepdims=True) + jnp.exp(sink - m)
                    p_kv = jnp.concatenate([jnp.where(from_prev, p, 0.0), jnp.where(from_prev, 0.0, p)],
                                           axis=1).astype(bf16)
                    outs.append(_dot(p_kv, vv[kv_rows]) * (1.0 / denom))
                col = (pairs_per_kv * hk + b) * LANES
                o_ref[q_rows, col:col + LANES] = jnp.where(low_o, outs[0], outs[1]).astype(bf16)


def _attention(q, k, v, sinks, cast_weights):
    L = q.shape[0]
    tq = ATT_BLOCKS * BLOCK
    steps = L // tq
    cur = lambda n: (n, 0)
    prev = lambda n: (jnp.maximum(ATT_BLOCKS * n - 1, 0), 0)
    slabs = [pl.BlockSpec((w.shape[0] // steps, w.shape[1]), cur) for w in cast_weights]
    outs = pl.pallas_call(
        _attn_kernel,
        grid=(steps,),
        in_specs=[pl.BlockSpec(memory_space=pltpu.SMEM),
                  pl.BlockSpec((tq, Q_W), cur),
                  pl.BlockSpec((BLOCK, KV_W), prev), pl.BlockSpec((tq, KV_W), cur),
                  pl.BlockSpec((BLOCK, KV_W), prev), pl.BlockSpec((tq, KV_W), cur)] + slabs,
        out_specs=[pl.BlockSpec((tq, Q_W), cur)] + slabs,
        out_shape=[jax.ShapeDtypeStruct((L, Q_W), bf16)]
                  + [jax.ShapeDtypeStruct(w.shape, bf16) for w in cast_weights],
        compiler_params=_params(("parallel",)),
        name="swa_attention",
    )(sinks, q, k, k, v, v, *cast_weights)
    return outs[0], outs[1:]


GC_SHIFT = SSM_GC.bit_length() - 1
P_SHIFT = SSM_P.bit_length() - 1
Y_TILE = 512
NT = SG_STATE // LANES
NSEG = 8
SEG = N_CHUNKS // NSEG
SEG_PITCH = SEG + 8


def _ssm_kernel(u_ref, pre_ref, pim_ref, kt_ref, qc_ref, rep_ref, ar_ref, ai_ref, y_ref,
                p_scr, m_scr, q_scr, s_scr, xl_scr, pw_scr, xs_scr):
    half = SG_STATE // 2
    row = lax.broadcasted_iota(jnp.int32, (SG_IN, LANES), 0)
    lane = lax.broadcasted_iota(jnp.int32, (SG_IN, LANES), 1)
    row_grp = (row >> GC_SHIFT) & (GPS - 1)
    pre, pim = pre_ref[...], pim_ref[...]
    for k in range(half // LANES):
        sel = row_grp == 2 * k + (lane >> P_SHIFT)
        p_scr[:, k * LANES:(k + 1) * LANES] = jnp.where(sel, pre, 0.0).astype(bf16)
        p_scr[:, half + k * LANES:half + (k + 1) * LANES] = jnp.where(sel, pim, 0.0).astype(bf16)

    blk_row = lax.broadcasted_iota(jnp.int32, (LANES, LANES), 0) >> GC_SHIFT
    blk_lane = lax.broadcasted_iota(jnp.int32, (LANES, LANES), 1) >> GC_SHIFT
    zero_blk = jnp.zeros((LANES, LANES), bf16)
    for tau in range(CHUNK):
        blk = jnp.where(blk_row == blk_lane, kt_ref[tau * LANES:(tau + 1) * LANES, :], 0.0).astype(bf16)
        for s in range(CHUNK - tau):
            t = s + tau
            m_scr[s * LANES:(s + 1) * LANES, t * LANES:(t + 1) * LANES] = blk
            if tau > 0:
                m_scr[t * LANES:(t + 1) * LANES, s * LANES:(s + 1) * LANES] = zero_blk

    q_row_grp = (lax.broadcasted_iota(jnp.int32, (SG_STATE, Y_TILE), 0) >> P_SHIFT) & (GPS - 1)
    q_lane_grp = (lax.broadcasted_iota(jnp.int32, (SG_STATE, Y_TILE), 1) >> GC_SHIFT) & (GPS - 1)
    for j in range(SG_IN // Y_TILE):
        full = _dot(qc_ref[...], rep_ref[:, j * Y_TILE:(j + 1) * Y_TILE])
        q_scr[:, j * Y_TILE:(j + 1) * Y_TILE] = jnp.where(q_row_grp == q_lane_grp, full, 0.0).astype(bf16)

    s = _dot(u_ref[...], p_scr[...])
    for j in range(NT):
        for k in range(NSEG):
            s_scr[j, k * SEG_PITCH:k * SEG_PITCH + SEG, :] = s[k * SEG:(k + 1) * SEG, j * LANES:(j + 1) * LANES]
    ht = NT // 2
    cmul = lambda xr, xi, yr, yi: (xr * yr - xi * yi, xr * yi + xi * yr)
    ar = [ar_ref[:, j * LANES:(j + 1) * LANES] for j in range(ht)]
    ai = [ai_ref[:, j * LANES:(j + 1) * LANES] for j in range(ht)]
    one, nil = jnp.ones((1, LANES), f32), jnp.zeros((1, LANES), f32)
    a_seg = []
    for j in range(ht):
        rr, ri = [one], [nil]
        for _ in range(7):
            nr, ni = cmul(rr[-1], ri[-1], ar[j], ai[j])
            rr.append(nr)
            ri.append(ni)
        a8r, a8i = cmul(rr[-1], ri[-1], ar[j], ai[j])
        blk_r, blk_i = jnp.concatenate(rr, axis=0), jnp.concatenate(ri, axis=0)
        for b in range(SEG // 8):
            pw_scr[j, b * 8:(b + 1) * 8, :] = blk_r
            pw_scr[ht + j, b * 8:(b + 1) * 8, :] = blk_i
            blk_r, blk_i = cmul(blk_r, blk_i, a8r, a8i)
        a_seg.append((blk_r[0:1], blk_i[0:1]))

    ar_b = [jnp.broadcast_to(a, (NSEG, LANES)) for a in ar]
    ai_b = [jnp.broadcast_to(a, (NSEG, LANES)) for a in ai]

    def body(i, carry):
        new_r, new_i = [], []
        for j in range(ht):
            xr, xi = carry[j], carry[ht + j]
            seg_rows = pl.ds(i, NSEG, stride=SEG_PITCH)
            xl_scr[j, seg_rows, :] = xr
            xl_scr[ht + j, seg_rows, :] = xi
            pr, pi = cmul(xr, xi, ar_b[j], ai_b[j])
            new_r.append(pr + s_scr[j, seg_rows, :])
            new_i.append(pi + s_scr[ht + j, seg_rows, :])
        return tuple(new_r + new_i)

    ends = lax.fori_loop(0, SEG, body, tuple(jnp.zeros((NSEG, LANES), f32) for _ in range(NT)))

    for j in range(ht):
        zr, zi = nil, nil
        for k in range(NSEG):
            rows = slice(k * SEG, (k + 1) * SEG)
            loc = slice(k * SEG_PITCH, k * SEG_PITCH + SEG)
            fr, fi = cmul(pw_scr[j], pw_scr[ht + j], zr, zi)
            xs_scr[rows, j * LANES:(j + 1) * LANES] = (xl_scr[j, loc, :] + fr).astype(bf16)
            xs_scr[rows, half + j * LANES:half + (j + 1) * LANES] = (xl_scr[ht + j, loc, :] + fi).astype(bf16)
            pr, pi = cmul(zr, zi, *a_seg[j])
            zr, zi = pr + ends[j][k:k + 1], pi + ends[ht + j][k:k + 1]

    xs = xs_scr[...]
    for j in range(SG_IN // Y_TILE):
        kk = (j + 1) * Y_TILE
        cols = slice(j * Y_TILE, (j + 1) * Y_TILE)
        y = _dot(u_ref[:, :kk], m_scr[:kk, cols]) + _dot(xs, q_scr[:, cols])
        y_ref[:, cols] = _gelu_tanh(y).astype(bf16)


def _ssm(u2, pre, pim, kt, qc, rep, at_re, at_im):
    per_sg = lambda r, c: pl.BlockSpec((None, r, c), lambda g: (g, 0, 0))
    return pl.pallas_call(
        _ssm_kernel,
        grid=(SG,),
        in_specs=[pl.BlockSpec((N_CHUNKS, SG_IN), lambda g: (0, g)),
                  per_sg(SG_IN, LANES), per_sg(SG_IN, LANES), per_sg(SG_IN, LANES),
                  per_sg(SG_STATE, CHUNK * SSM_GC), _const_spec(rep.shape),
                  per_sg(1, SG_STATE // 2), per_sg(1, SG_STATE // 2)],
        out_specs=pl.BlockSpec((N_CHUNKS, SG_IN), lambda g: (0, g)),
        out_shape=jax.ShapeDtypeStruct((N_CHUNKS, SG * SG_IN), bf16),
        scratch_shapes=[pltpu.VMEM((SG_IN, SG_STATE), bf16), pltpu.VMEM((SG_IN, SG_IN), bf16),
                        pltpu.VMEM((SG_STATE, SG_IN), bf16),
                        pltpu.VMEM((NT, NSEG * SEG_PITCH, LANES), f32),
                        pltpu.VMEM((NT, NSEG * SEG_PITCH, LANES), f32),
                        pltpu.VMEM((NT, SEG, LANES), f32), pltpu.VMEM((N_CHUNKS, SG_STATE), bf16)],
        compiler_params=_params(("arbitrary",)),
        name="ssm",
    )(u2, pre, pim, kt, qc, rep, at_re, at_im)


def _ssm_prep_kernel(lam_re_ref, lam_im_ref, ldt_ref, bt_re_ref, bt_im_ref, yc_ref, d_ref, eye_ref,
                     pre_ref, pim_ref, kt_ref, qc_ref, at_re_ref, at_im_ref):
    lr, li = lam_re_ref[...], lam_im_ref[...]
    dt = jnp.exp(ldt_ref[...])
    mag = jnp.exp(lr * dt)
    a_re, a_im = mag * jnp.cos(li * dt), mag * jnp.sin(li * dt)
    den = lr * lr + li * li
    nr, ni = a_re - 1.0, a_im
    coef_re = (nr * lr + ni * li) / den
    coef_im = (ni * lr - nr * li) / den
    pw_re, pw_im = [jnp.ones_like(a_re)], [jnp.zeros_like(a_im)]
    for _ in range(CHUNK):
        pr, pi = pw_re[-1], pw_im[-1]
        pw_re.append(pr * a_re - pi * a_im)
        pw_im.append(pr * a_im + pi * a_re)

    low = lax.broadcasted_iota(jnp.int32, (SSM_GC, LANES), 1) < SSM_P
    diag = ((lax.broadcasted_iota(jnp.int32, (SSM_GC, LANES), 1) & (SSM_GC - 1))
            == lax.broadcasted_iota(jnp.int32, (SSM_GC, LANES), 0))
    contract_lanes = (((1,), (1,)), ((), ()))
    for g in range(GPS):
        grp = slice(g * SSM_GC, (g + 1) * SSM_GC)
        one = lambda x: x[g:g + 1]
        bb_re = one(coef_re) * bt_re_ref[grp, :] - one(coef_im) * bt_im_ref[grp, :]
        bb_im = one(coef_re) * bt_im_ref[grp, :] + one(coef_im) * bt_re_ref[grp, :]
        yc = yc_ref[grp, :]
        yc_sw = pltpu.roll(yc, SSM_P, axis=1)
        y2 = jnp.where(low, yc_sw, -yc_sw)
        x_rows, ca_rows = [], []
        for tau in range(CHUNK):
            pr, pi = one(pw_re[tau]), one(pw_im[tau])
            ab_re = bb_re * pr - bb_im * pi
            ab_im = bb_re * pi + bb_im * pr
            s = CHUNK - 1 - tau
            rows = slice(s * LANES + g * SSM_GC, s * LANES + (g + 1) * SSM_GC)
            pre_ref[rows, :] = ab_re
            pim_ref[rows, :] = ab_im
            x_rows.append(jnp.where(low, ab_re, ab_im))
            ca_rows.append(yc * one(pw_re[tau + 1]) + y2 * one(pw_im[tau + 1]))
        kern = lax.dot_general(jnp.concatenate(x_rows, axis=0), jnp.concatenate([yc] * GPS, axis=0),
                               contract_lanes, precision=lax.Precision.HIGHEST,
                               preferred_element_type=f32)
        for tau in range(CHUNK):
            blk = kern[tau * SSM_GC:(tau + 1) * SSM_GC]
            if tau == 0:
                blk = blk + jnp.where(diag, d_ref[...], 0.0)
            kt_ref[tau * LANES + g * SSM_GC:tau * LANES + (g + 1) * SSM_GC, :] = blk
        qt = lax.dot_general(eye_ref[...], jnp.concatenate(ca_rows, axis=0).astype(bf16),
                             contract_lanes, preferred_element_type=f32)
        qc_ref[g * SSM_P:(g + 1) * SSM_P, :] = qt[:SSM_P].astype(bf16)
        qc_ref[GPS * SSM_P + g * SSM_P:GPS * SSM_P + (g + 1) * SSM_P, :] = qt[SSM_P:].astype(bf16)
    lo_lane = lax.broadcasted_iota(jnp.int32, (1, LANES), 1) < SSM_P
    for k in range(GPS // 2):
        sel = lambda pw: jnp.where(lo_lane, pw[2 * k:2 * k + 1], pw[2 * k + 1:2 * k + 2])
        at_re_ref[:, k * LANES:(k + 1) * LANES] = sel(pw_re[CHUNK])
        at_im_ref[:, k * LANES:(k + 1) * LANES] = sel(pw_im[CHUNK])


def _ssm_operators(lam_re, lam_im, log_dt, b_re, b_im, c_re, c_im, d_skip):
    dup = lambda x: jnp.concatenate([x, x], axis=-1)
    per_sg = lambda x: x.astype(f32).reshape((SG, x.size // (SG * x.shape[-1]), x.shape[-1]))
    bt = lambda b: per_sg(dup(b.astype(f32).swapaxes(1, 2)))
    yc = per_sg(jnp.concatenate([c_re.astype(f32), -c_im.astype(f32)], axis=-1))
    args = (per_sg(dup(lam_re)), per_sg(dup(lam_im)), log_dt.astype(f32).reshape(SG, GPS, 1),
            bt(b_re), bt(b_im), yc, d_skip.astype(f32).reshape(SG, 1, LANES), jnp.eye(LANES, dtype=bf16))
    blk = lambda r, c: pl.BlockSpec((None, r, c), lambda g: (g, 0, 0))
    return pl.pallas_call(
        _ssm_prep_kernel,
        grid=(SG,),
        in_specs=[blk(GPS, LANES), blk(GPS, LANES), blk(GPS, 1), blk(LANES, LANES), blk(LANES, LANES),
                  blk(LANES, LANES), blk(1, LANES), _const_spec((LANES, LANES))],
        out_specs=[blk(SG_IN, LANES), blk(SG_IN, LANES), blk(SG_IN, LANES),
                   blk(SG_STATE, CHUNK * SSM_GC), blk(1, SG_STATE // 2), blk(1, SG_STATE // 2)],
        out_shape=[jax.ShapeDtypeStruct((SG, SG_IN, LANES), f32)] * 3
                  + [jax.ShapeDtypeStruct((SG, SG_STATE, CHUNK * SSM_GC), bf16)]
                  + [jax.ShapeDtypeStruct((SG, 1, SG_STATE // 2), f32)] * 2,
        compiler_params=_params(("parallel",)),
        name="ssm_prep",
    )(*args)


def _replication_matrix():
    src = jnp.arange(CHUNK * SSM_GC)
    dst = jnp.arange(SG_IN)
    same_t = (src[:, None] // SSM_GC) == (dst[None, :] // LANES)
    same_c = (src[:, None] % SSM_GC) == (dst[None, :] % SSM_GC)
    return (same_t & same_c).astype(bf16)


def _mix_kernel(oa_ref, y2_ref, ga_ref, gs_ref, wglu_ref, wb_ref, m_ref, yb_ref):
    for r in range(oa_ref.shape[0] // ROW_SUB):
        rows = pl.ds(r * ROW_SUB, ROW_SUB)
        chunk_rows = pl.ds(r * (ROW_SUB // CHUNK), ROW_SUB // CHUNK)
        for sg in range(SG):
            for t in range(CHUNK):
                col = sg * SG_IN + t * LANES
                yb_ref[r, sg, pl.ds(t, ROW_SUB // CHUNK, stride=CHUNK), :] = (
                    y2_ref[chunk_rows, col:col + LANES].astype(f32))
        y = jnp.concatenate([yb_ref[r, sg] for sg in range(SG)], axis=1).astype(bf16)
        zg = _dot(y, wglu_ref[...])
        o_ssm = zg[:, :SSM_W] * _sigmoid(zg[:, SSM_W:])
        y_s = _dot(o_ssm.astype(bf16), wb_ref[Q_W:, :])
        y_a = _dot(oa_ref[rows, :], wb_ref[:Q_W, :])
        m_ref[rows, :] = (ga_ref[rows, :].astype(f32) * y_a + gs_ref[rows, :].astype(f32) * y_s).astype(bf16)


def _mix(o_attn, y2, gates, wglu, wb, tm=512):
    L = o_attn.shape[0]
    return pl.pallas_call(
        _mix_kernel,
        grid=(L // tm,),
        in_specs=[pl.BlockSpec((tm, Q_W), lambda i: (i, 0)),
                  pl.BlockSpec((tm // CHUNK, SG * SG_IN), lambda i: (i, 0)),
                  pl.BlockSpec((tm, D_MODEL), lambda i: (i, 0)),
                  pl.BlockSpec((tm, D_MODEL), lambda i: (i, 1)),
                  _const_spec(wglu.shape), _const_spec(wb.shape)],
        out_specs=pl.BlockSpec((tm, D_MODEL), lambda i: (i, 0)),
        out_shape=jax.ShapeDtypeStruct((L, D_MODEL), bf16),
        scratch_shapes=[pltpu.VMEM((tm // ROW_SUB, SG, ROW_SUB, LANES), f32)],
        compiler_params=_params(("parallel",)),
        name="mix",
    )(o_attn, y2, gates, gates, wglu, wb)


def _outproj_kernel(m_ref, w_ref, x_ref, gpost_ref, gpre_ref, x1_ref, h2_ref):
    for r in range(m_ref.shape[0] // OUT_SUB):
        rows = pl.ds(r * OUT_SUB, OUT_SUB)
        out = _dot(m_ref[rows, :], w_ref[...])
        x1 = x_ref[rows, :] + _rms_norm(out, gpost_ref[...])
        x1_ref[rows, :] = x1
        h2_ref[rows, :] = _rms_norm(x1, gpre_ref[...]).astype(bf16)


def _outproj(m, w_out, x, g_post, g_pre, tm=512):
    L = x.shape[0]
    row = lambda: pl.BlockSpec((tm, D_MODEL), lambda i: (i, 0))
    return pl.pallas_call(
        _outproj_kernel,
        grid=(L // tm,),
        in_specs=[row(), _const_spec(w_out.shape), row(),
                  _const_spec((1, D_MODEL)), _const_spec((1, D_MODEL))],
        out_specs=[row(), row()],
        out_shape=[jax.ShapeDtypeStruct((L, D_MODEL), f32), jax.ShapeDtypeStruct((L, D_MODEL), bf16)],
        compiler_params=_params(("parallel",)),
        name="outproj",
    )(m, w_out, x, g_post, g_pre)


MLP_SUB = 512
MLP_COLS = 512


def _mlp_kernel(h_ref, wu_ref, wd_ref, x_ref, g_ref, o_ref):
    j = pl.program_id(1)
    tm = h_ref.shape[0]

    @pl.when(j == 0)
    def _():
        o_ref[...] = jnp.zeros_like(o_ref)

    for r in range(tm // MLP_SUB):
        rows = pl.ds(r * MLP_SUB, MLP_SUB)
        a = jnp.maximum(_dot(h_ref[rows, :], wu_ref[...]), 0.0)
        a = (a * a).astype(bf16)
        for c in range(D_MODEL // MLP_COLS):
            cols = slice(c * MLP_COLS, (c + 1) * MLP_COLS)
            o_ref[rows, cols] += _dot(a, wd_ref[:, cols])

    @pl.when(j == pl.num_programs(1) - 1)
    def _():
        for r in range(tm // MLP_SUB):
            rows = pl.ds(r * MLP_SUB, MLP_SUB)
            o_ref[rows, :] = x_ref[rows, :] + _rms_norm(o_ref[rows, :], g_ref[...])


def _mlp(h2, w_up, w_down, x1, g_post, tm=1024, tf=1024):
    L = x1.shape[0]
    return pl.pallas_call(
        _mlp_kernel,
        grid=(L // tm, D_FF // tf),
        in_specs=[pl.BlockSpec((tm, D_MODEL), lambda i, j: (i, 0)),
                  pl.BlockSpec((D_MODEL, tf), lambda i, j: (0, j)),
                  pl.BlockSpec((tf, D_MODEL), lambda i, j: (j, 0)),
                  pl.BlockSpec((tm, D_MODEL), lambda i, j: (i, 0), pipeline_mode=pl.Buffered(1)),
                  _const_spec((1, D_MODEL))],
        out_specs=pl.BlockSpec((tm, D_MODEL), lambda i, j: (i, 0)),
        out_shape=jax.ShapeDtypeStruct((L, D_MODEL), f32),
        compiler_params=_params(("parallel", "arbitrary")),
        name="mlp",
    )(h2, w_up, w_down, x1, g_post)


def _rope_tables(L):
    half = ROT_DIM // 2
    inv = ROPE_THETA ** (-jnp.arange(half, dtype=f32) * 2.0 / ROT_DIM)
    ang = jnp.arange(L).astype(f32)[:, None] * inv[None, :]
    cos, sin = jnp.cos(ang), jnp.sin(ang)
    ones = jnp.ones((L, HEAD_DIM - ROT_DIM), f32)
    zeros = jnp.zeros((L, HEAD_DIM - ROT_DIM), f32)
    zh = jnp.zeros((L, half), f32)
    per_head = lambda parts: jnp.tile(jnp.concatenate(parts, axis=1), (1, LANES // HEAD_DIM))
    return (per_head([cos, cos, ones]), per_head([-sin, zh, zeros]), per_head([zh, sin, zeros]))


def _layer(x, norm_mix_pre, norm_mix_post, norm_mlp_pre, norm_mlp_post, w_in, sinks,
           lam_re, lam_im, log_dt, b_re, b_im, c_re, c_im, d_skip, w_glu, w_branch, w_out,
           w_up, w_down, rope):
    gain = lambda g: g.astype(f32).reshape(1, D_MODEL)
    q, k, v, u2, h, wg = _inproj(x, gain(norm_mix_pre), w_in.astype(f32), *rope)
    gates, wup, wdn = _gates(h, wg, w_up.astype(f32), w_down.astype(f32))
    o_attn, (wglu, wb, wo) = _attention(q, k, v, sinks.astype(f32),
                                        [w_glu.astype(f32), w_branch.astype(f32), w_out.astype(f32)])
    pre, pim, kt, qc, at_re, at_im = _ssm_operators(lam_re, lam_im, log_dt, b_re, b_im, c_re, c_im, d_skip)
    y2 = _ssm(u2, pre, pim, kt, qc, _replication_matrix(), at_re, at_im)
    mixed = _mix(o_attn, y2, gates, wglu, wb)
    x1, h2 = _outproj(mixed, wo, x, gain(norm_mix_post), gain(norm_mlp_pre))
    return _mlp(h2, wup, wdn, x1, gain(norm_mlp_post))


def kernel(x, norm_mix_pre, norm_mix_post, norm_mlp_pre, norm_mlp_post, w_in, sinks, lam_re, lam_im, log_dt, b_re, b_im, c_re, c_im, d_skip, w_glu, w_branch, w_out, w_up, w_down):
    B, L, _ = x.shape
    depth = w_in.shape[0]
    rope = _rope_tables(L)
    outs = []
    for b in range(B):
        xb = x[b]
        for l in range(depth):
            xb = _layer(xb, norm_mix_pre[l], norm_mix_post[l], norm_mlp_pre[l], norm_mlp_post[l],
                        w_in[l], sinks[l], lam_re[l], lam_im[l], log_dt[l], b_re[l], b_im[l],
                        c_re[l], c_im[l], d_skip[l], w_glu[l], w_branch[l], w_out[l],
                        w_up[l], w_down[l], rope)
        outs.append(xb)
    return jnp.stack(outs)
```

```python
import math

import jax
import jax.numpy as jnp
from jax import lax
from jax.experimental import pallas as pl
from jax.experimental.pallas import tpu as pltpu

D_MODEL = 2048
SEQ = 8192
HEAD_DIM = 64
N_Q_HEADS = 16
N_KV_HEADS = 2
BLOCK = 128
ROT_DIM = HEAD_DIM // 4
ROPE_THETA = 500000.0
Q_W = N_Q_HEADS * HEAD_DIM
KV_W = N_KV_HEADS * HEAD_DIM
SSM_W = D_MODEL // 2
QKV_W = Q_W + 2 * KV_W
IN_MIX_W = QKV_W + SSM_W
SSM_GC = 16
SSM_G = SSM_W // SSM_GC
SSM_P = 64
D_FF = 4 * D_MODEL
EPS = 1e-6

LANES = 128
CHUNK = 16
N_CHUNKS = SEQ // CHUNK
SG = SSM_W // LANES
GPS = LANES // SSM_GC
SG_IN = CHUNK * LANES
SG_STATE = 2 * GPS * SSM_P
VMEM_LIMIT = 56 * 1024 * 1024
ROW_SUB = 256
OUT_SUB = 128
ATT_BLOCKS = 2
MASKED = -1e30

bf16 = jnp.bfloat16
f32 = jnp.float32


def _dot(a, b):
    return jnp.dot(a, b, preferred_element_type=f32)


def _sigmoid(x):
    return 1.0 / (1.0 + jnp.exp(-x))


def _gelu_tanh(x):
    c = math.sqrt(2.0 / math.pi)
    return x * (0.5 * (1.0 + jnp.tanh(c * (x + 0.044715 * (x * x * x)))))


def _rms_norm(x, g):
    return x * lax.rsqrt(jnp.mean(x * x, axis=-1, keepdims=True) + EPS) * g


def _params(sem):
    return pltpu.CompilerParams(dimension_semantics=sem, vmem_limit_bytes=VMEM_LIMIT)


def _const_spec(shape):
    nd = len(shape)
    return pl.BlockSpec(shape, lambda *_: (0,) * nd, pipeline_mode=pl.Buffered(1))


def _rope_block(z, cos, sin_a, sin_b):
    return (z * cos + pltpu.roll(z, LANES - ROT_DIM // 2, axis=1) * sin_a
            + pltpu.roll(z, ROT_DIM // 2, axis=1) * sin_b)


def _inproj_kernel(x_ref, g_ref, w_ref, wcast_ref, cos_ref, sa_ref, sb_ref,
                   q_ref, k_ref, v_ref, u2_ref, h_ref, wg_ref, wqkv_ref, wu_ref, zu_ref):
    @pl.when(pl.program_id(0) == 0)
    def _():
        wqkv_ref[...] = w_ref[:, :QKV_W].astype(bf16)
        wu_ref[...] = w_ref[:, QKV_W:].astype(bf16)

    wg_ref[...] = wcast_ref[...].astype(bf16)
    for r in range(x_ref.shape[0] // ROW_SUB):
        rows = pl.ds(r * ROW_SUB, ROW_SUB)
        h = _rms_norm(x_ref[rows, :], g_ref[...]).astype(bf16)
        h_ref[rows, :] = h
        cos, sin_a, sin_b = cos_ref[rows, :], sa_ref[rows, :], sb_ref[rows, :]
        for half in range(2):
            zq = _dot(h, wqkv_ref[:, half * 512:(half + 1) * 512])
            for b in range(4):
                blk = zq[:, b * LANES:(b + 1) * LANES]
                col = (half * 4 + b) * LANES
                q_ref[rows, col:col + LANES] = _rope_block(blk, cos, sin_a, sin_b).astype(bf16)
        zkv = _dot(h, wqkv_ref[:, Q_W:Q_W + 2 * KV_W])
        k_ref[rows, :] = _rope_block(zkv[:, :KV_W], cos, sin_a, sin_b).astype(bf16)
        v_ref[rows, :] = zkv[:, KV_W:].astype(bf16)
        zu = _dot(h, wu_ref[...])
        chunk_rows = pl.ds(r * (ROW_SUB // CHUNK), ROW_SUB // CHUNK)
        for sg in range(SG):
            zu_ref[r, sg] = zu[:, sg * LANES:(sg + 1) * LANES]
            for s in range(CHUNK):
                col = sg * SG_IN + s * LANES
                u2_ref[chunk_rows, col:col + LANES] = (
                    zu_ref[r, sg, pl.ds(s, ROW_SUB // CHUNK, stride=CHUNK), :].astype(bf16))


def _inproj(x, gain, w_in, cos, sin_a, sin_b, tm=512):
    L = x.shape[0]
    steps = L // tm
    gate_w = w_in.shape[1] - IN_MIX_W
    slab = gate_w // steps
    assert IN_MIX_W % slab == 0 and slab % LANES == 0
    row = lambda w: pl.BlockSpec((tm, w), lambda i: (i, 0))
    return pl.pallas_call(
        _inproj_kernel,
        grid=(steps,),
        in_specs=[row(D_MODEL), _const_spec((1, D_MODEL)),
                  pl.BlockSpec((D_MODEL, IN_MIX_W), lambda i: (0, 0), pipeline_mode=pl.Buffered(1)),
                  pl.BlockSpec((D_MODEL, slab), lambda i: (0, IN_MIX_W // slab + i)),
                  row(LANES), row(LANES), row(LANES)],
        out_specs=[row(Q_W), row(KV_W), row(KV_W),
                   pl.BlockSpec((tm // CHUNK, SG * SG_IN), lambda i: (i, 0)), row(D_MODEL),
                   pl.BlockSpec((D_MODEL, slab), lambda i: (0, i))],
        out_shape=[jax.ShapeDtypeStruct((L, Q_W), bf16), jax.ShapeDtypeStruct((L, KV_W), bf16),
                   jax.ShapeDtypeStruct((L, KV_W), bf16),
                   jax.ShapeDtypeStruct((L // CHUNK, SG * SG_IN), bf16),
                   jax.ShapeDtypeStruct((L, D_MODEL), bf16),
                   jax.ShapeDtypeStruct((D_MODEL, gate_w), bf16)],
        scratch_shapes=[pltpu.VMEM((D_MODEL, QKV_W), bf16), pltpu.VMEM((D_MODEL, SSM_W), bf16),
                        pltpu.VMEM((tm // ROW_SUB, SG, ROW_SUB, LANES), f32)],
        compiler_params=_params(("arbitrary",)),
        name="inproj",
    )(x, gain, w_in, w_in, cos, sin_a, sin_b)


def _gates_kernel(h_ref, w_ref, wup_ref, wdn_ref, o_ref, wup_o_ref, wdn_o_ref):
    for r in range(h_ref.shape[0] // ROW_SUB):
        rows = pl.ds(r * ROW_SUB, ROW_SUB)
        o_ref[rows, :] = _sigmoid(_dot(h_ref[rows, :], w_ref[...])).astype(bf16)
    wup_o_ref[...] = wup_ref[...].astype(bf16)
    wdn_o_ref[...] = wdn_ref[...].astype(bf16)


def _gates(h, wg, w_up, w_down, tm=1024, tn=1024):
    L, N = h.shape[0], wg.shape[1]
    nj = N // tn
    steps = (L // tm) * nj
    up_slab, dn_slab = w_up.shape[1] // steps, w_down.shape[0] // steps
    step = lambda i, j: i * nj + j
    return pl.pallas_call(
        _gates_kernel,
        grid=(L // tm, nj),
        in_specs=[pl.BlockSpec((tm, D_MODEL), lambda i, j: (i, 0)),
                  pl.BlockSpec((D_MODEL, tn), lambda i, j: (0, j)),
                  pl.BlockSpec((w_up.shape[0], up_slab), lambda i, j: (0, step(i, j))),
                  pl.BlockSpec((dn_slab, w_down.shape[1]), lambda i, j: (step(i, j), 0))],
        out_specs=[pl.BlockSpec((tm, tn), lambda i, j: (i, j)),
                   pl.BlockSpec((w_up.shape[0], up_slab), lambda i, j: (0, step(i, j))),
                   pl.BlockSpec((dn_slab, w_down.shape[1]), lambda i, j: (step(i, j), 0))],
        out_shape=[jax.ShapeDtypeStruct((L, N), bf16), jax.ShapeDtypeStruct(w_up.shape, bf16),
                   jax.ShapeDtypeStruct(w_down.shape, bf16)],
        compiler_params=_params(("parallel", "arbitrary")),
        name="gates",
    )(h, wg, w_up, w_down)


def _attn_kernel(sinks_ref, q_ref, kp_ref, kc_ref, vp_ref, vc_ref, *rest):
    n_cast = (len(rest) - 1) // 2
    o_ref = rest[n_cast]
    for src, dst in zip(rest[:n_cast], rest[n_cast + 1:]):
        dst[...] = src[...].astype(bf16)
    step = pl.program_id(0)
    q = q_ref[...] * jnp.asarray(1.0 / math.sqrt(HEAD_DIM), bf16)
    kcat = jnp.concatenate([kp_ref[...], kc_ref[...]], axis=0).astype(f32)
    vcat = jnp.concatenate([vp_ref[...], vc_ref[...]], axis=0).astype(f32)
    k_sw = pltpu.roll(kcat, HEAD_DIM, axis=1)
    v_sw = pltpu.roll(vcat, HEAD_DIM, axis=1)
    low_kv = lax.broadcasted_iota(jnp.int32, kcat.shape, 1) < HEAD_DIM
    from_prev = (lax.broadcasted_iota(jnp.int32, (BLOCK, BLOCK), 1)
                 > lax.broadcasted_iota(jnp.int32, (BLOCK, BLOCK), 0))
    no_prev = jnp.where(step > 0, 0.0, MASKED)
    low_o = lax.broadcasted_iota(jnp.int32, (BLOCK, LANES), 1) < HEAD_DIM
    contract_lanes = (((1,), (1,)), ((), ()))
    pairs_per_kv = (N_Q_HEADS // N_KV_HEADS) // 2
    for hk in range(N_KV_HEADS):
        k_src, k_oth = (kcat, k_sw) if hk == 0 else (k_sw, kcat)
        k_lo = jnp.where(low_kv, k_src, 0.0).astype(bf16)
        k_hi = jnp.where(low_kv, 0.0, k_oth).astype(bf16)
        vv = (jnp.where(low_kv, vcat, v_sw) if hk == 0 else jnp.where(low_kv, v_sw, vcat)).astype(bf16)
        for blk in range(ATT_BLOCKS):
            q_rows = slice(blk * BLOCK, (blk + 1) * BLOCK)
            kv_rows = slice(blk * BLOCK, (blk + 2) * BLOCK)
            qs = jnp.concatenate(
                [q[q_rows, (pairs_per_kv * hk + b) * LANES:(pairs_per_kv * hk + b + 1) * LANES]
                 for b in range(pairs_per_kv)], axis=0)
            s_par = (lax.dot_general(qs, k_lo[kv_rows], contract_lanes, preferred_element_type=f32),
                     lax.dot_general(qs, k_hi[kv_rows], contract_lanes, preferred_element_type=f32))
            for b in range(pairs_per_kv):
                outs = []
                for par in range(2):
                    head = 2 * (pairs_per_kv * hk + b) + par
                    s_all = s_par[par][b * BLOCK:(b + 1) * BLOCK]
                    s_prev = s_all[:, :BLOCK] + no_prev if blk == 0 else s_all[:, :BLOCK]
                    s = jnp.where(from_prev, s_prev, s_all[:, BLOCK:])
                    sink = sinks_ref[head]
                    m = jnp.maximum(jnp.max(s, axis=1, keepdims=True), sink)
                    p = jnp.exp(s - m)
                    denom = jnp.sum(p, axis=1, keepdims=True) + jnp.exp(sink - m)
                    p_kv = jnp.concatenate([jnp.where(from_prev, p, 0.0), jnp.where(from_prev, 0.0, p)],
                                           axis=1).astype(bf16)
                    outs.append(_dot(p_kv, vv[kv_rows]) * (1.0 / denom))
                col = (pairs_per_kv * hk + b) * LANES
                o_ref[q_rows, col:col + LANES] = jnp.where(low_o, outs[0], outs[1]).astype(bf16)


def _attention(q, k, v, sinks, cast_weights):
    L = q.shape[0]
    tq = ATT_BLOCKS * BLOCK
    steps = L // tq
    cur = lambda n: (n, 0)
    prev = lambda n: (jnp.maximum(ATT_BLOCKS * n - 1, 0), 0)
    slabs = [pl.BlockSpec((w.shape[0] // steps, w.shape[1]), cur) for w in cast_weights]
    outs = pl.pallas_call(
        _attn_kernel,
        grid=(steps,),
        in_specs=[pl.BlockSpec(memory_space=pltpu.SMEM),
                  pl.BlockSpec((tq, Q_W), cur),
                  pl.BlockSpec((BLOCK, KV_W), prev), pl.BlockSpec((tq, KV_W), cur),
                  pl.BlockSpec((BLOCK, KV_W), prev), pl.BlockSpec((tq, KV_W), cur)] + slabs,
        out_specs=[pl.BlockSpec((tq, Q_W), cur)] + slabs,
        out_shape=[jax.ShapeDtypeStruct((L, Q_W), bf16)]
                  + [jax.ShapeDtypeStruct(w.shape, bf16) for w in cast_weights],
        compiler_params=_params(("parallel",)),
        name="swa_attention",
    )(sinks, q, k, k, v, v, *cast_weights)
    return outs[0], outs[1:]


GC_SHIFT = SSM_GC.bit_length() - 1
P_SHIFT = SSM_P.bit_length() - 1
Y_TILE = 512
NT = SG_STATE // LANES
NSEG = 8
SEG = N_CHUNKS // NSEG
SEG_PITCH = SEG + 8


def _ssm_kernel(u_ref, pre_ref, pim_ref, kt_ref, qc_ref, rep_ref, ar_ref, ai_ref, y_ref,
                p_scr, m_scr, q_scr, s_scr, xl_scr, pw_scr, xs_scr):
    half = SG_STATE // 2
    row = lax.broadcasted_iota(jnp.int32, (SG_IN, LANES), 0)
    lane = lax.broadcasted_iota(jnp.int32, (SG_IN, LANES), 1)
    row_grp = (row >> GC_SHIFT) & (GPS - 1)
    pre, pim = pre_ref[...], pim_ref[...]
    for k in range(half // LANES):
        sel = row_grp == 2 * k + (lane >> P_SHIFT)
        p_scr[:, k * LANES:(k + 1) * LANES] = jnp.where(sel, pre, 0.0).astype(bf16)
        p_scr[:, half + k * LANES:half + (k + 1) * LANES] = jnp.where(sel, pim, 0.0).astype(bf16)

    blk_row = lax.broadcasted_iota(jnp.int32, (LANES, LANES), 0) >> GC_SHIFT
    blk_lane = lax.broadcasted_iota(jnp.int32, (LANES, LANES), 1) >> GC_SHIFT
    zero_blk = jnp.zeros((LANES, LANES), bf16)
    for tau in range(CHUNK):
        blk = jnp.where(blk_row == blk_lane, kt_ref[tau * LANES:(tau + 1) * LANES, :], 0.0).astype(bf16)
        for s in range(CHUNK - tau):
            t = s + tau
            m_scr[s * LANES:(s + 1) * LANES, t * LANES:(t + 1) * LANES] = blk
            if tau > 0:
                m_scr[t * LANES:(t + 1) * LANES, s * LANES:(s + 1) * LANES] = zero_blk

    q_row_grp = (lax.broadcasted_iota(jnp.int32, (SG_STATE, Y_TILE), 0) >> P_SHIFT) & (GPS - 1)
    q_lane_grp = (lax.broadcasted_iota(jnp.int32, (SG_STATE, Y_TILE), 1) >> GC_SHIFT) & (GPS - 1)
    for j in range(SG_IN // Y_TILE):
        full = _dot(qc_ref[...], rep_ref[:, j * Y_TILE:(j + 1) * Y_TILE])
        q_scr[:, j * Y_TILE:(j + 1) * Y_TILE] = jnp.where(q_row_grp == q_lane_grp, full, 0.0).astype(bf16)

    s = _dot(u_ref[...], p_scr[...])
    for j in range(NT):
        for k in range(NSEG):
            s_scr[j, k * SEG_PITCH:k * SEG_PITCH + SEG, :] = s[k * SEG:(k + 1) * SEG, j * LANES:(j + 1) * LANES]
    ht = NT // 2
    cmul = lambda xr, xi, yr, yi: (xr * yr - xi * yi, xr * yi + xi * yr)
    ar = [ar_ref[:, j * LANES:(j + 1) * LANES] for j in range(ht)]
    ai = [ai_ref[:, j * LANES:(j + 1) * LANES] for j in range(ht)]
    one, nil = jnp.ones((1, LANES), f32), jnp.zeros((1, LANES), f32)
    a_seg = []
    for j in range(ht):
        rr, ri = [one], [nil]
        for _ in range(7):
            nr, ni = cmul(rr[-1], ri[-1], ar[j], ai[j])
            rr.append(nr)
            ri.append(ni)
        a8r, a8i = cmul(rr[-1], ri[-1], ar[j], ai[j])
        blk_r, blk_i = jnp.concatenate(rr, axis=0), jnp.concatenate(ri, axis=0)
        for b in range(SEG // 8):
            pw_scr[j, b * 8:(b + 1) * 8, :] = blk_r
            pw_scr[ht + j, b * 8:(b + 1) * 8, :] = blk_i
            blk_r, blk_i = cmul(blk_r, blk_i, a8r, a8i)
        a_seg.append((blk_r[0:1], blk_i[0:1]))

    ar_b = [jnp.broadcast_to(a, (NSEG, LANES)) for a in ar]
    ai_b = [jnp.broadcast_to(a, (NSEG, LANES)) for a in ai]

    def body(i, carry):
        new_r, new_i = [], []
        for j in range(ht):
            xr, xi = carry[j], carry[ht + j]
            seg_rows = pl.ds(i, NSEG, stride=SEG_PITCH)
            xl_scr[j, seg_rows, :] = xr
            xl_scr[ht + j, seg_rows, :] = xi
            pr, pi = cmul(xr, xi, ar_b[j], ai_b[j])
            new_r.append(pr + s_scr[j, seg_rows, :])
            new_i.append(pi + s_scr[ht + j, seg_rows, :])
        return tuple(new_r + new_i)

    ends = lax.fori_loop(0, SEG, body, tuple(jnp.zeros((NSEG, LANES), f32) for _ in range(NT)))

    for j in range(ht):
        zr, zi = nil, nil
        for k in range(NSEG):
            rows = slice(k * SEG, (k + 1) * SEG)
            loc = slice(k * SEG_PITCH, k * SEG_PITCH + SEG)
            fr, fi = cmul(pw_scr[j], pw_scr[ht + j], zr, zi)
            xs_scr[rows, j * LANES:(j + 1) * LANES] = (xl_scr[j, loc, :] + fr).astype(bf16)
            xs_scr[rows, half + j * LANES:half + (j + 1) * LANES] = (xl_scr[ht + j, loc, :] + fi).astype(bf16)
            pr, pi = cmul(zr, zi, *a_seg[j])
            zr, zi = pr + ends[j][k:k + 1], pi + ends[ht + j][k:k + 1]

    xs = xs_scr[...]
    for j in range(SG_IN // Y_TILE):
        kk = (j + 1) * Y_TILE
        cols = slice(j * Y_TILE, (j + 1) * Y_TILE)
        y = _dot(u_ref[:, :kk], m_scr[:kk, cols]) + _dot(xs, q_scr[:, cols])
        y_ref[:, cols] = _gelu_tanh(y).astype(bf16)


def _ssm(u2, pre, pim, kt, qc, rep, at_re, at_im):
    per_sg = lambda r, c: pl.BlockSpec((None, r, c), lambda g: (g, 0, 0))
    return pl.pallas_call(
        _ssm_kernel,
        grid=(SG,),
        in_specs=[pl.BlockSpec((N_CHUNKS, SG_IN), lambda g: (0, g)),
                  per_sg(SG_IN, LANES), per_sg(SG_IN, LANES), per_sg(SG_IN, LANES),
                  per_sg(SG_STATE, CHUNK * SSM_GC), _const_spec(rep.shape),
                  per_sg(1, SG_STATE // 2), per_sg(1, SG_STATE // 2)],
        out_specs=pl.BlockSpec((N_CHUNKS, SG_IN), lambda g: (0, g)),
        out_shape=jax.ShapeDtypeStruct((N_CHUNKS, SG * SG_IN), bf16),
        scratch_shapes=[pltpu.VMEM((SG_IN, SG_STATE), bf16), pltpu.VMEM((SG_IN, SG_IN), bf16),
                        pltpu.VMEM((SG_STATE, SG_IN), bf16),
                        pltpu.VMEM((NT, NSEG * SEG_PITCH, LANES), f32),
                        pltpu.VMEM((NT, NSEG * SEG_PITCH, LANES), f32),
                        pltpu.VMEM((NT, SEG, LANES), f32), pltpu.VMEM((N_CHUNKS, SG_STATE), bf16)],
        compiler_params=_params(("arbitrary",)),
        name="ssm",
    )(u2, pre, pim, kt, qc, rep, at_re, at_im)


def _ssm_prep_kernel(lam_re_ref, lam_im_ref, ldt_ref, bt_re_ref, bt_im_ref, yc_ref, d_ref, eye_ref,
                     pre_ref, pim_ref, kt_ref, qc_ref, at_re_ref, at_im_ref):
    lr, li = lam_re_ref[...], lam_im_ref[...]
    dt = jnp.exp(ldt_ref[...])
    mag = jnp.exp(lr * dt)
    a_re, a_im = mag * jnp.cos(li * dt), mag * jnp.sin(li * dt)
    den = lr * lr + li * li
    nr, ni = a_re - 1.0, a_im
    coef_re = (nr * lr + ni * li) / den
    coef_im = (ni * lr - nr * li) / den
    pw_re, pw_im = [jnp.ones_like(a_re)], [jnp.zeros_like(a_im)]
    for _ in range(CHUNK):
        pr, pi = pw_re[-1], pw_im[-1]
        pw_re.append(pr * a_re - pi * a_im)
        pw_im.append(pr * a_im + pi * a_re)

    low = lax.broadcasted_iota(jnp.int32, (SSM_GC, LANES), 1) < SSM_P
    diag = ((lax.broadcasted_iota(jnp.int32, (SSM_GC, LANES), 1) & (SSM_GC - 1))
            == lax.broadcasted_iota(jnp.int32, (SSM_GC, LANES), 0))
    contract_lanes = (((1,), (1,)), ((), ()))
    for g in range(GPS):
        grp = slice(g * SSM_GC, (g + 1) * SSM_GC)
        one = lambda x: x[g:g + 1]
        bb_re = one(coef_re) * bt_re_ref[grp, :] - one(coef_im) * bt_im_ref[grp, :]
        bb_im = one(coef_re) * bt_im_ref[grp, :] + one(coef_im) * bt_re_ref[grp, :]
        yc = yc_ref[grp, :]
        yc_sw = pltpu.roll(yc, SSM_P, axis=1)
        y2 = jnp.where(low, yc_sw, -yc_sw)
        x_rows, ca_rows = [], []
        for tau in range(CHUNK):
            pr, pi = one(pw_re[tau]), one(pw_im[tau])
            ab_re = bb_re * pr - bb_im * pi
            ab_im = bb_re * pi + bb_im * pr
            s = CHUNK - 1 - tau
            rows = slice(s * LANES + g * SSM_GC, s * LANES + (g + 1) * SSM_GC)
            pre_ref[rows, :] = ab_re
            pim_ref[rows, :] = ab_im
            x_rows.append(jnp.where(low, ab_re, ab_im))
            ca_rows.append(yc * one(pw_re[tau + 1]) + y2 * one(pw_im[tau + 1]))
        kern = lax.dot_general(jnp.concatenate(x_rows, axis=0), jnp.concatenate([yc] * GPS, axis=0),
                               contract_lanes, precision=lax.Precision.HIGHEST,
                               preferred_element_type=f32)
        for tau in range(CHUNK):
            blk = kern[tau * SSM_GC:(tau + 1) * SSM_GC]
            if tau == 0:
                blk = blk + jnp.where(diag, d_ref[...], 0.0)
            kt_ref[tau * LANES + g * SSM_GC:tau * LANES + (g + 1) * SSM_GC, :] = blk
        qt = lax.dot_general(eye_ref[...], jnp.concatenate(ca_rows, axis=0).astype(bf16),
                             contract_lanes, preferred_element_type=f32)
        qc_ref[g * SSM_P:(g + 1) * SSM_P, :] = qt[:SSM_P].astype(bf16)
        qc_ref[GPS * SSM_P + g * SSM_P:GPS * SSM_P + (g + 1) * SSM_P, :] = qt[SSM_P:].astype(bf16)
    lo_lane = lax.broadcasted_iota(jnp.int32, (1, LANES), 1) < SSM_P
    for k in range(GPS // 2):
        sel = lambda pw: jnp.where(lo_lane, pw[2 * k:2 * k + 1], pw[2 * k + 1:2 * k + 2])
        at_re_ref[:, k * LANES:(k + 1) * LANES] = sel(pw_re[CHUNK])
        at_im_ref[:, k * LANES:(k + 1) * LANES] = sel(pw_im[CHUNK])


def _ssm_operators(lam_re, lam_im, log_dt, b_re, b_im, c_re, c_im, d_skip):
    dup = lambda x: jnp.concatenate([x, x], axis=-1)
    per_sg = lambda x: x.astype(f32).reshape((SG, x.size // (SG * x.shape[-1]), x.shape[-1]))
    bt = lambda b: per_sg(dup(b.astype(f32).swapaxes(1, 2)))
    yc = per_sg(jnp.concatenate([c_re.astype(f32), -c_im.astype(f32)], axis=-1))
    args = (per_sg(dup(lam_re)), per_sg(dup(lam_im)), log_dt.astype(f32).reshape(SG, GPS, 1),
            bt(b_re), bt(b_im), yc, d_skip.astype(f32).reshape(SG, 1, LANES), jnp.eye(LANES, dtype=bf16))
    blk = lambda r, c: pl.BlockSpec((None, r, c), lambda g: (g, 0, 0))
    return pl.pallas_call(
        _ssm_prep_kernel,
        grid=(SG,),
        in_specs=[blk(GPS, LANES), blk(GPS, LANES), blk(GPS, 1), blk(LANES, LANES), blk(LANES, LANES),
                  blk(LANES, LANES), blk(1, LANES), _const_spec((LANES, LANES))],
        out_specs=[blk(SG_IN, LANES), blk(SG_IN, LANES), blk(SG_IN, LANES),
                   blk(SG_STATE, CHUNK * SSM_GC), blk(1, SG_STATE // 2), blk(1, SG_STATE // 2)],
        out_shape=[jax.ShapeDtypeStruct((SG, SG_IN, LANES), f32)] * 3
                  + [jax.ShapeDtypeStruct((SG, SG_STATE, CHUNK * SSM_GC), bf16)]
                  + [jax.ShapeDtypeStruct((SG, 1, SG_STATE // 2), f32)] * 2,
        compiler_params=_params(("parallel",)),
        name="ssm_prep",
    )(*args)


def _replication_matrix():
    src = jnp.arange(CHUNK * SSM_GC)
    dst = jnp.arange(SG_IN)
    same_t = (src[:, None] // SSM_GC) == (dst[None, :] // LANES)
    same_c = (src[:, None] % SSM_GC) == (dst[None, :] % SSM_GC)
    return (same_t & same_c).astype(bf16)


def _mix_kernel(oa_ref, y2_ref, ga_ref, gs_ref, wglu_ref, wb_ref, m_ref, yb_ref):
    for r in range(oa_ref.shape[0] // ROW_SUB):
        rows = pl.ds(r * ROW_SUB, ROW_SUB)
        chunk_rows = pl.ds(r * (ROW_SUB // CHUNK), ROW_SUB // CHUNK)
        for sg in range(SG):
            for t in range(CHUNK):
                col = sg * SG_IN + t * LANES
                yb_ref[r, sg, pl.ds(t, ROW_SUB // CHUNK, stride=CHUNK), :] = (
                    y2_ref[chunk_rows, col:col + LANES].astype(f32))
        y = jnp.concatenate([yb_ref[r, sg] for sg in range(SG)], axis=1).astype(bf16)
        zg = _dot(y, wglu_ref[...])
        o_ssm = zg[:, :SSM_W] * _sigmoid(zg[:, SSM_W:])
        y_s = _dot(o_ssm.astype(bf16), wb_ref[Q_W:, :])
        y_a = _dot(oa_ref[rows, :], wb_ref[:Q_W, :])
        m_ref[rows, :] = (ga_ref[rows, :].astype(f32) * y_a + gs_ref[rows, :].astype(f32) * y_s).astype(bf16)


def _mix(o_attn, y2, gates, wglu, wb, tm=512):
    L = o_attn.shape[0]
    return pl.pallas_call(
        _mix_kernel,
        grid=(L // tm,),
        in_specs=[pl.BlockSpec((tm, Q_W), lambda i: (i, 0)),
                  pl.BlockSpec((tm // CHUNK, SG * SG_IN), lambda i: (i, 0)),
                  pl.BlockSpec((tm, D_MODEL), lambda i: (i, 0)),
                  pl.BlockSpec((tm, D_MODEL), lambda i: (i, 1)),
                  _const_spec(wglu.shape), _const_spec(wb.shape)],
        out_specs=pl.BlockSpec((tm, D_MODEL), lambda i: (i, 0)),
        out_shape=jax.ShapeDtypeStruct((L, D_MODEL), bf16),
        scratch_shapes=[pltpu.VMEM((tm // ROW_SUB, SG, ROW_SUB, LANES), f32)],
        compiler_params=_params(("parallel",)),
        name="mix",
    )(o_attn, y2, gates, gates, wglu, wb)


def _outproj_kernel(m_ref, w_ref, x_ref, gpost_ref, gpre_ref, x1_ref, h2_ref):
    for r in range(m_ref.shape[0] // OUT_SUB):
        rows = pl.ds(r * OUT_SUB, OUT_SUB)
        out = _dot(m_ref[rows, :], w_ref[...])
        x1 = x_ref[rows, :] + _rms_norm(out, gpost_ref[...])
        x1_ref[rows, :] = x1
        h2_ref[rows, :] = _rms_norm(x1, gpre_ref[...]).astype(bf16)


def _outproj(m, w_out, x, g_post, g_pre, tm=512):
    L = x.shape[0]
    row = lambda: pl.BlockSpec((tm, D_MODEL), lambda i: (i, 0))
    return pl.pallas_call(
        _outproj_kernel,
        grid=(L // tm,),
        in_specs=[row(), _const_spec(w_out.shape), row(),
                  _const_spec((1, D_MODEL)), _const_spec((1, D_MODEL))],
        out_specs=[row(), row()],
        out_shape=[jax.ShapeDtypeStruct((L, D_MODEL), f32), jax.ShapeDtypeStruct((L, D_MODEL), bf16)],
        compiler_params=_params(("parallel",)),
        name="outproj",
    )(m, w_out, x, g_post, g_pre)


MLP_SUB = 512


def _mlp_kernel(h_ref, wu_ref, wd_ref, x_ref, g_ref, o_ref, acc_ref):
    i, j = pl.program_id(0), pl.program_id(1)
    n_tiles = pl.num_programs(0) - 1
    tm = h_ref.shape[0]
    slab = o_ref.shape[0]
    cur, prev = acc_ref.at[i % 2], acc_ref.at[(i + 1) % 2]
    slab_rows = pl.ds(pl.multiple_of(j * slab, slab), slab)

    def finish_prev_slab():
        o_ref[...] = x_ref[...] + _rms_norm(prev[slab_rows, :], g_ref[...])
        prev[slab_rows, :] = jnp.zeros((slab, D_MODEL), f32)

    @pl.when((i == 0) & (j == 0))
    def _():
        acc_ref[...] = jnp.zeros_like(acc_ref)

    @pl.when(i < n_tiles)
    def _():
        for r in range(tm // MLP_SUB):
            rows = pl.ds(r * MLP_SUB, MLP_SUB)
            a = jnp.maximum(_dot(h_ref[rows, :], wu_ref[...]), 0.0)
            cur[rows, :] += _dot((a * a).astype(bf16), wd_ref[...])
        finish_prev_slab()

    @pl.when(i == n_tiles)
    def _():
        finish_prev_slab()


def _mlp(h2, w_up, w_down, x1, g_post, tm=1024, tf=1024):
    L = x1.shape[0]
    n_tiles, n_ff = L // tm, D_FF // tf
    slab = tm // n_ff
    ff = lambda i, j: jnp.where(i < n_tiles, j, n_ff - 1)
    prev_slab = lambda i, j: (jnp.maximum(i - 1, 0) * n_ff + jnp.where(i > 0, j, 0), 0)
    return pl.pallas_call(
        _mlp_kernel,
        grid=(n_tiles + 1, n_ff),
        in_specs=[pl.BlockSpec((tm, D_MODEL), lambda i, j: (jnp.minimum(i, n_tiles - 1), 0)),
                  pl.BlockSpec((D_MODEL, tf), lambda i, j: (0, ff(i, j))),
                  pl.BlockSpec((tf, D_MODEL), lambda i, j: (ff(i, j), 0)),
                  pl.BlockSpec((slab, D_MODEL), prev_slab),
                  _const_spec((1, D_MODEL))],
        out_specs=pl.BlockSpec((slab, D_MODEL), prev_slab),
        out_shape=jax.ShapeDtypeStruct((L, D_MODEL), f32),
        scratch_shapes=[pltpu.VMEM((2, tm, D_MODEL), f32)],
        compiler_params=_params(("arbitrary", "arbitrary")),
        name="mlp",
    )(h2, w_up, w_down, x1, g_post)


def _rope_tables(L):
    half = ROT_DIM // 2
    inv = ROPE_THETA ** (-jnp.arange(half, dtype=f32) * 2.0 / ROT_DIM)
    ang = jnp.arange(L).astype(f32)[:, None] * inv[None, :]
    cos, sin = jnp.cos(ang), jnp.sin(ang)
    ones = jnp.ones((L, HEAD_DIM - ROT_DIM), f32)
    zeros = jnp.zeros((L, HEAD_DIM - ROT_DIM), f32)
    zh = jnp.zeros((L, half), f32)
    per_head = lambda parts: jnp.tile(jnp.concatenate(parts, axis=1), (1, LANES // HEAD_DIM))
    return (per_head([cos, cos, ones]), per_head([-sin, zh, zeros]), per_head([zh, sin, zeros]))


def _layer(x, norm_mix_pre, norm_mix_post, norm_mlp_pre, norm_mlp_post, w_in, sinks,
           lam_re, lam_im, log_dt, b_re, b_im, c_re, c_im, d_skip, w_glu, w_branch, w_out,
           w_up, w_down, rope):
    gain = lambda g: g.astype(f32).reshape(1, D_MODEL)
    q, k, v, u2, h, wg = _inproj(x, gain(norm_mix_pre), w_in.astype(f32), *rope)
    gates, wup, wdn = _gates(h, wg, w_up.astype(f32), w_down.astype(f32))
    o_attn, (wglu, wb, wo) = _attention(q, k, v, sinks.astype(f32),
                                        [w_glu.astype(f32), w_branch.astype(f32), w_out.astype(f32)])
    pre, pim, kt, qc, at_re, at_im = _ssm_operators(lam_re, lam_im, log_dt, b_re, b_im, c_re, c_im, d_skip)
    y2 = _ssm(u2, pre, pim, kt, qc, _replication_matrix(), at_re, at_im)
    mixed = _mix(o_attn, y2, gates, wglu, wb)
    x1, h2 = _outproj(mixed, wo, x, gain(norm_mix_post), gain(norm_mlp_pre))
    return _mlp(h2, wup, wdn, x1, gain(norm_mlp_post))


def kernel(x, norm_mix_pre, norm_mix_post, norm_mlp_pre, norm_mlp_post, w_in, sinks, lam_re, lam_im, log_dt, b_re, b_im, c_re, c_im, d_skip, w_glu, w_branch, w_out, w_up, w_down):
    B, L, _ = x.shape
    depth = w_in.shape[0]
    rope = _rope_tables(L)
    outs = []
    for b in range(B):
        xb = x[b]
        for l in range(depth):
            xb = _layer(xb, norm_mix_pre[l], norm_mix_post[l], norm_mlp_pre[l], norm_mlp_post[l],
                        w_in[l], sinks[l], lam_re[l], lam_im[l], log_dt[l], b_re[l], b_im[l],
                        c_re[l], c_im[l], d_skip[l], w_glu[l], w_branch[l], w_out[l],
                        w_up[l], w_down[l], rope)
        outs.append(xb)
    return jnp.stack(outs)
```

```python
import math

import jax
import jax.numpy as jnp
from jax import lax
from jax.experimental import pallas as pl
from jax.experimental.pallas import tpu as pltpu

D_MODEL = 2048
SEQ = 8192
HEAD_DIM = 64
N_Q_HEADS = 16
N_KV_HEADS = 2
BLOCK = 128
ROT_DIM = HEAD_DIM // 4
ROPE_THETA = 500000.0
Q_W = N_Q_HEADS * HEAD_DIM
KV_W = N_KV_HEADS * HEAD_DIM
SSM_W = D_MODEL // 2
QKV_W = Q_W + 2 * KV_W
IN_MIX_W = QKV_W + SSM_W
SSM_GC = 16
SSM_G = SSM_W // SSM_GC
SSM_P = 64
D_FF = 4 * D_MODEL
EPS = 1e-6

LANES = 128
CHUNK = 16
N_CHUNKS = SEQ // CHUNK
SG = SSM_W // LANES
GPS = LANES // SSM_GC
SG_IN = CHUNK * LANES
SG_STATE = 2 * GPS * SSM_P
VMEM_LIMIT = 56 * 1024 * 1024
ROW_SUB = 256
OUT_SUB = 128
ATT_BLOCKS = 2
MASKED = -1e30

bf16 = jnp.bfloat16
f32 = jnp.float32


def _dot(a, b):
    return jnp.dot(a, b, preferred_element_type=f32)


def _sigmoid(x):
    return 1.0 / (1.0 + jnp.exp(-x))


def _gelu_tanh(x):
    c = math.sqrt(2.0 / math.pi)
    return x * (0.5 * (1.0 + jnp.tanh(c * (x + 0.044715 * (x * x * x)))))


def _rms_norm(x, g):
    return x * lax.rsqrt(jnp.mean(x * x, axis=-1, keepdims=True) + EPS) * g


def _params(sem):
    return pltpu.CompilerParams(dimension_semantics=sem, vmem_limit_bytes=VMEM_LIMIT)


def _const_spec(shape):
    nd = len(shape)
    return pl.BlockSpec(shape, lambda *_: (0,) * nd, pipeline_mode=pl.Buffered(1))


def _rope_block(z, cos, sin_a, sin_b):
    return (z * cos + pltpu.roll(z, LANES - ROT_DIM // 2, axis=1) * sin_a
            + pltpu.roll(z, ROT_DIM // 2, axis=1) * sin_b)


def _inproj_kernel(x_ref, g_ref, w_ref, wcast_ref, cos_ref, sa_ref, sb_ref,
                   q_ref, k_ref, v_ref, u2_ref, h_ref, wg_ref, wqkv_ref, wu_ref, zu_ref):
    @pl.when(pl.program_id(0) == 0)
    def _():
        wqkv_ref[...] = w_ref[:, :QKV_W].astype(bf16)
        wu_ref[...] = w_ref[:, QKV_W:].astype(bf16)

    wg_ref[...] = wcast_ref[...].astype(bf16)
    for r in range(x_ref.shape[0] // ROW_SUB):
        rows = pl.ds(r * ROW_SUB, ROW_SUB)
        h = _rms_norm(x_ref[rows, :], g_ref[...]).astype(bf16)
        h_ref[rows, :] = h
        cos, sin_a, sin_b = cos_ref[rows, :], sa_ref[rows, :], sb_ref[rows, :]
        for half in range(2):
            zq = _dot(h, wqkv_ref[:, half * 512:(half + 1) * 512])
            for b in range(4):
                blk = zq[:, b * LANES:(b + 1) * LANES]
                col = (half * 4 + b) * LANES
                q_ref[rows, col:col + LANES] = _rope_block(blk, cos, sin_a, sin_b).astype(bf16)
        zkv = _dot(h, wqkv_ref[:, Q_W:Q_W + 2 * KV_W])
        k_ref[rows, :] = _rope_block(zkv[:, :KV_W], cos, sin_a, sin_b).astype(bf16)
        v_ref[rows, :] = zkv[:, KV_W:].astype(bf16)
        zu = _dot(h, wu_ref[...])
        chunk_rows = pl.ds(r * (ROW_SUB // CHUNK), ROW_SUB // CHUNK)
        for sg in range(SG):
            zu_ref[r, sg] = zu[:, sg * LANES:(sg + 1) * LANES]
            for s in range(CHUNK):
                col = sg * SG_IN + s * LANES
                u2_ref[chunk_rows, col:col + LANES] = (
                    zu_ref[r, sg, pl.ds(s, ROW_SUB // CHUNK, stride=CHUNK), :].astype(bf16))


def _inproj(x, gain, w_in, cos, sin_a, sin_b, tm=512):
    L = x.shape[0]
    steps = L // tm
    gate_w = w_in.shape[1] - IN_MIX_W
    slab = gate_w // steps
    assert IN_MIX_W % slab == 0 and slab % LANES == 0
    row = lambda w: pl.BlockSpec((tm, w), lambda i: (i, 0))
    return pl.pallas_call(
        _inproj_kernel,
        grid=(steps,),
        in_specs=[row(D_MODEL), _const_spec((1, D_MODEL)),
                  pl.BlockSpec((D_MODEL, IN_MIX_W), lambda i: (0, 0), pipeline_mode=pl.Buffered(1)),
                  pl.BlockSpec((D_MODEL, slab), lambda i: (0, IN_MIX_W // slab + i)),
                  row(LANES), row(LANES), row(LANES)],
        out_specs=[row(Q_W), row(KV_W), row(KV_W),
                   pl.BlockSpec((tm // CHUNK, SG * SG_IN), lambda i: (i, 0)), row(D_MODEL),
                   pl.BlockSpec((D_MODEL, slab), lambda i: (0, i))],
        out_shape=[jax.ShapeDtypeStruct((L, Q_W), bf16), jax.ShapeDtypeStruct((L, KV_W), bf16),
                   jax.ShapeDtypeStruct((L, KV_W), bf16),
                   jax.ShapeDtypeStruct((L // CHUNK, SG * SG_IN), bf16),
                   jax.ShapeDtypeStruct((L, D_MODEL), bf16),
                   jax.ShapeDtypeStruct((D_MODEL, gate_w), bf16)],
        scratch_shapes=[pltpu.VMEM((D_MODEL, QKV_W), bf16), pltpu.VMEM((D_MODEL, SSM_W), bf16),
                        pltpu.VMEM((tm // ROW_SUB, SG, ROW_SUB, LANES), f32)],
        compiler_params=_params(("arbitrary",)),
        name="inproj",
    )(x, gain, w_in, w_in, cos, sin_a, sin_b)


def _gates_kernel(h_ref, w_ref, wup_ref, o_ref, wup_o_ref):
    for r in range(h_ref.shape[0] // ROW_SUB):
        rows = pl.ds(r * ROW_SUB, ROW_SUB)
        o_ref[rows, :] = _sigmoid(_dot(h_ref[rows, :], w_ref[...])).astype(bf16)
    wup_o_ref[...] = wup_ref[...].astype(bf16)


def _gates(h, wg, w_up, tm=1024, tn=2048):
    L, N = h.shape[0], wg.shape[1]
    ni = L // tm
    up_slab = w_up.shape[1] // (ni * (N // tn))
    slab = lambda j, i: (0, j * ni + i)
    return pl.pallas_call(
        _gates_kernel,
        grid=(N // tn, ni),
        in_specs=[pl.BlockSpec((tm, D_MODEL), lambda j, i: (i, 0)),
                  pl.BlockSpec((D_MODEL, tn), lambda j, i: (0, j)),
                  pl.BlockSpec((w_up.shape[0], up_slab), slab)],
        out_specs=[pl.BlockSpec((tm, tn), lambda j, i: (i, j)),
                   pl.BlockSpec((w_up.shape[0], up_slab), slab)],
        out_shape=[jax.ShapeDtypeStruct((L, N), bf16), jax.ShapeDtypeStruct(w_up.shape, bf16)],
        compiler_params=_params(("parallel", "arbitrary")),
        name="gates",
    )(h, wg, w_up)


def _attn_kernel(sinks_ref, q_ref, kp_ref, kc_ref, vp_ref, vc_ref, *rest):
    n_cast = (len(rest) - 1) // 2
    o_ref = rest[n_cast]
    for src, dst in zip(rest[:n_cast], rest[n_cast + 1:]):
        dst[...] = src[...].astype(bf16)
    step = pl.program_id(0)
    q = q_ref[...] * jnp.asarray(1.0 / math.sqrt(HEAD_DIM), bf16)
    kcat = jnp.concatenate([kp_ref[...], kc_ref[...]], axis=0).astype(f32)
    vcat = jnp.concatenate([vp_ref[...], vc_ref[...]], axis=0).astype(f32)
    k_sw = pltpu.roll(kcat, HEAD_DIM, axis=1)
    v_sw = pltpu.roll(vcat, HEAD_DIM, axis=1)
    low_kv = lax.broadcasted_iota(jnp.int32, kcat.shape, 1) < HEAD_DIM
    from_prev = (lax.broadcasted_iota(jnp.int32, (BLOCK, BLOCK), 1)
                 > lax.broadcasted_iota(jnp.int32, (BLOCK, BLOCK), 0))
    no_prev = jnp.where(step > 0, 0.0, MASKED)
    low_o = lax.broadcasted_iota(jnp.int32, (BLOCK, LANES), 1) < HEAD_DIM
    contract_lanes = (((1,), (1,)), ((), ()))
    pairs_per_kv = (N_Q_HEADS // N_KV_HEADS) // 2
    for hk in range(N_KV_HEADS):
        k_src, k_oth = (kcat, k_sw) if hk == 0 else (k_sw, kcat)
        k_lo = jnp.where(low_kv, k_src, 0.0).astype(bf16)
        k_hi = jnp.where(low_kv, 0.0, k_oth).astype(bf16)
        vv = (jnp.where(low_kv, vcat, v_sw) if hk == 0 else jnp.where(low_kv, v_sw, vcat)).astype(bf16)
        for blk in range(ATT_BLOCKS):
            q_rows = slice(blk * BLOCK, (blk + 1) * BLOCK)
            kv_rows = slice(blk * BLOCK, (blk + 2) * BLOCK)
            qs = jnp.concatenate(
                [q[q_rows, (pairs_per_kv * hk + b) * LANES:(pairs_per_kv * hk + b + 1) * LANES]
                 for b in range(pairs_per_kv)], axis=0)
            s_par = (lax.dot_general(qs, k_lo[kv_rows], contract_lanes, preferred_element_type=f32),
                     lax.dot_general(qs, k_hi[kv_rows], contract_lanes, preferred_element_type=f32))
            for b in range(pairs_per_kv):
                outs = []
                for par in range(2):
                    head = 2 * (pairs_per_kv * hk + b) + par
                    s_all = s_par[par][b * BLOCK:(b + 1) * BLOCK]
                    s_prev = s_all[:, :BLOCK] + no_prev if blk == 0 else s_all[:, :BLOCK]
                    s = jnp.where(from_prev, s_prev, s_all[:, BLOCK:])
                    sink = sinks_ref[head]
                    m = jnp.maximum(jnp.max(s, axis=1, keepdims=True), sink)
                    p = jnp.exp(s - m)
                    denom = jnp.sum(p, axis=1, keepdims=True) + jnp.exp(sink - m)
                    p_kv = jnp.concatenate([jnp.where(from_prev, p, 0.0), jnp.where(from_prev, 0.0, p)],
                                           axis=1).astype(bf16)
                    outs.append(_dot(p_kv, vv[kv_rows]) * (1.0 / denom))
                col = (pairs_per_kv * hk + b) * LANES
                o_ref[q_rows, col:col + LANES] = jnp.where(low_o, outs[0], outs[1]).astype(bf16)


def _attention(q, k, v, sinks, cast_weights):
    L = q.shape[0]
    tq = ATT_BLOCKS * BLOCK
    steps = L // tq
    cur = lambda n: (n, 0)
    prev = lambda n: (jnp.maximum(ATT_BLOCKS * n - 1, 0), 0)
    slabs = [pl.BlockSpec((w.shape[0] // steps, w.shape[1]), cur) for w in cast_weights]
    outs = pl.pallas_call(
        _attn_kernel,
        grid=(steps,),
        in_specs=[pl.BlockSpec(memory_space=pltpu.SMEM),
                  pl.BlockSpec((tq, Q_W), cur),
                  pl.BlockSpec((BLOCK, KV_W), prev), pl.BlockSpec((tq, KV_W), cur),
                  pl.BlockSpec((BLOCK, KV_W), prev), pl.BlockSpec((tq, KV_W), cur)] + slabs,
        out_specs=[pl.BlockSpec((tq, Q_W), cur)] + slabs,
        out_shape=[jax.ShapeDtypeStruct((L, Q_W), bf16)]
                  + [jax.ShapeDtypeStruct(w.shape, bf16) for w in cast_weights],
        compiler_params=_params(("parallel",)),
        name="swa_attention",
    )(sinks, q, k, k, v, v, *cast_weights)
    return outs[0], outs[1:]


GC_SHIFT = SSM_GC.bit_length() - 1
P_SHIFT = SSM_P.bit_length() - 1
Y_TILE = 512
NT = SG_STATE // LANES
NSEG = 8
SEG = N_CHUNKS // NSEG
SEG_PITCH = SEG + 8


def _ssm_kernel(u_ref, pre_ref, pim_ref, kt_ref, qc_ref, rep_ref, ar_ref, ai_ref, y_ref,
                p_scr, m_scr, q_scr, s_scr, xl_scr, pw_scr, xs_scr):
    half = SG_STATE // 2
    row = lax.broadcasted_iota(jnp.int32, (SG_IN, LANES), 0)
    lane = lax.broadcasted_iota(jnp.int32, (SG_IN, LANES), 1)
    row_grp = (row >> GC_SHIFT) & (GPS - 1)
    pre, pim = pre_ref[...], pim_ref[...]
    for k in range(half // LANES):
        sel = row_grp == 2 * k + (lane >> P_SHIFT)
        p_scr[:, k * LANES:(k + 1) * LANES] = jnp.where(sel, pre, 0.0).astype(bf16)
        p_scr[:, half + k * LANES:half + (k + 1) * LANES] = jnp.where(sel, pim, 0.0).astype(bf16)

    blk_row = lax.broadcasted_iota(jnp.int32, (LANES, LANES), 0) >> GC_SHIFT
    blk_lane = lax.broadcasted_iota(jnp.int32, (LANES, LANES), 1) >> GC_SHIFT
    zero_blk = jnp.zeros((LANES, LANES), bf16)
    for tau in range(CHUNK):
        blk = jnp.where(blk_row == blk_lane, kt_ref[tau * LANES:(tau + 1) * LANES, :], 0.0).astype(bf16)
        for s in range(CHUNK - tau):
            t = s + tau
            m_scr[s * LANES:(s + 1) * LANES, t * LANES:(t + 1) * LANES] = blk
            if tau > 0:
                m_scr[t * LANES:(t + 1) * LANES, s * LANES:(s + 1) * LANES] = zero_blk

    q_row_grp = (lax.broadcasted_iota(jnp.int32, (SG_STATE, Y_TILE), 0) >> P_SHIFT) & (GPS - 1)
    q_lane_grp = (lax.broadcasted_iota(jnp.int32, (SG_STATE, Y_TILE), 1) >> GC_SHIFT) & (GPS - 1)
    for j in range(SG_IN // Y_TILE):
        full = _dot(qc_ref[...], rep_ref[:, j * Y_TILE:(j + 1) * Y_TILE])
        q_scr[:, j * Y_TILE:(j + 1) * Y_TILE] = jnp.where(q_row_grp == q_lane_grp, full, 0.0).astype(bf16)

    s = _dot(u_ref[...], p_scr[...])
    for j in range(NT):
        for k in range(NSEG):
            s_scr[j, k * SEG_PITCH:k * SEG_PITCH + SEG, :] = s[k * SEG:(k + 1) * SEG, j * LANES:(j + 1) * LANES]
    ht = NT // 2
    cmul = lambda xr, xi, yr, yi: (xr * yr - xi * yi, xr * yi + xi * yr)
    ar = [ar_ref[:, j * LANES:(j + 1) * LANES] for j in range(ht)]
    ai = [ai_ref[:, j * LANES:(j + 1) * LANES] for j in range(ht)]
    one, nil = jnp.ones((1, LANES), f32), jnp.zeros((1, LANES), f32)
    a_seg = []
    for j in range(ht):
        rr, ri = [one], [nil]
        for _ in range(7):
            nr, ni = cmul(rr[-1], ri[-1], ar[j], ai[j])
            rr.append(nr)
            ri.append(ni)
        a8r, a8i = cmul(rr[-1], ri[-1], ar[j], ai[j])
        blk_r, blk_i = jnp.concatenate(rr, axis=0), jnp.concatenate(ri, axis=0)
        for b in range(SEG // 8):
            pw_scr[j, b * 8:(b + 1) * 8, :] = blk_r
            pw_scr[ht + j, b * 8:(b + 1) * 8, :] = blk_i
            blk_r, blk_i = cmul(blk_r, blk_i, a8r, a8i)
        a_seg.append((blk_r[0:1], blk_i[0:1]))

    ar_b = [jnp.broadcast_to(a, (NSEG, LANES)) for a in ar]
    ai_b = [jnp.broadcast_to(a, (NSEG, LANES)) for a in ai]

    def body(i, carry):
        new_r, new_i = [], []
        for j in range(ht):
            xr, xi = carry[j], carry[ht + j]
            seg_rows = pl.ds(i, NSEG, stride=SEG_PITCH)
            xl_scr[j, seg_rows, :] = xr
            xl_scr[ht + j, seg_rows, :] = xi
            pr, pi = cmul(xr, xi, ar_b[j], ai_b[j])
            new_r.append(pr + s_scr[j, seg_rows, :])
            new_i.append(pi + s_scr[ht + j, seg_rows, :])
        return tuple(new_r + new_i)

    ends = lax.fori_loop(0, SEG, body, tuple(jnp.zeros((NSEG, LANES), f32) for _ in range(NT)))

    for j in range(ht):
        zr, zi = nil, nil
        for k in range(NSEG):
            rows = slice(k * SEG, (k + 1) * SEG)
            loc = slice(k * SEG_PITCH, k * SEG_PITCH + SEG)
            fr, fi = cmul(pw_scr[j], pw_scr[ht + j], zr, zi)
            xs_scr[rows, j * LANES:(j + 1) * LANES] = (xl_scr[j, loc, :] + fr).astype(bf16)
            xs_scr[rows, half + j * LANES:half + (j + 1) * LANES] = (xl_scr[ht + j, loc, :] + fi).astype(bf16)
            pr, pi = cmul(zr, zi, *a_seg[j])
            zr, zi = pr + ends[j][k:k + 1], pi + ends[ht + j][k:k + 1]

    xs = xs_scr[...]
    for j in range(SG_IN // Y_TILE):
        kk = (j + 1) * Y_TILE
        cols = slice(j * Y_TILE, (j + 1) * Y_TILE)
        y = _dot(u_ref[:, :kk], m_scr[:kk, cols]) + _dot(xs, q_scr[:, cols])
        y_ref[:, cols] = _gelu_tanh(y).astype(bf16)


def _ssm(u2, pre, pim, kt, qc, rep, at_re, at_im):
    per_sg = lambda r, c: pl.BlockSpec((None, r, c), lambda g: (g, 0, 0))
    return pl.pallas_call(
        _ssm_kernel,
        grid=(SG,),
        in_specs=[pl.BlockSpec((N_CHUNKS, SG_IN), lambda g: (0, g)),
                  per_sg(SG_IN, LANES), per_sg(SG_IN, LANES), per_sg(SG_IN, LANES),
                  per_sg(SG_STATE, CHUNK * SSM_GC), _const_spec(rep.shape),
                  per_sg(1, SG_STATE // 2), per_sg(1, SG_STATE // 2)],
        out_specs=pl.BlockSpec((N_CHUNKS, SG_IN), lambda g: (0, g)),
        out_shape=jax.ShapeDtypeStruct((N_CHUNKS, SG * SG_IN), bf16),
        scratch_shapes=[pltpu.VMEM((SG_IN, SG_STATE), bf16), pltpu.VMEM((SG_IN, SG_IN), bf16),
                        pltpu.VMEM((SG_STATE, SG_IN), bf16),
                        pltpu.VMEM((NT, NSEG * SEG_PITCH, LANES), f32),
                        pltpu.VMEM((NT, NSEG * SEG_PITCH, LANES), f32),
                        pltpu.VMEM((NT, SEG, LANES), f32), pltpu.VMEM((N_CHUNKS, SG_STATE), bf16)],
        compiler_params=_params(("arbitrary",)),
        name="ssm",
    )(u2, pre, pim, kt, qc, rep, at_re, at_im)


def _ssm_prep_kernel(lam_re_ref, lam_im_ref, ldt_ref, bt_re_ref, bt_im_ref, yc_ref, d_ref, eye_ref,
                     pre_ref, pim_ref, kt_ref, qc_ref, at_re_ref, at_im_ref):
    lr, li = lam_re_ref[...], lam_im_ref[...]
    dt = jnp.exp(ldt_ref[...])
    mag = jnp.exp(lr * dt)
    a_re, a_im = mag * jnp.cos(li * dt), mag * jnp.sin(li * dt)
    den = lr * lr + li * li
    nr, ni = a_re - 1.0, a_im
    coef_re = (nr * lr + ni * li) / den
    coef_im = (ni * lr - nr * li) / den
    pw_re, pw_im = [jnp.ones_like(a_re)], [jnp.zeros_like(a_im)]
    for _ in range(CHUNK):
        pr, pi = pw_re[-1], pw_im[-1]
        pw_re.append(pr * a_re - pi * a_im)
        pw_im.append(pr * a_im + pi * a_re)

    low = lax.broadcasted_iota(jnp.int32, (SSM_GC, LANES), 1) < SSM_P
    diag = ((lax.broadcasted_iota(jnp.int32, (SSM_GC, LANES), 1) & (SSM_GC - 1))
            == lax.broadcasted_iota(jnp.int32, (SSM_GC, LANES), 0))
    contract_lanes = (((1,), (1,)), ((), ()))
    for g in range(GPS):
        grp = slice(g * SSM_GC, (g + 1) * SSM_GC)
        one = lambda x: x[g:g + 1]
        bb_re = one(coef_re) * bt_re_ref[grp, :] - one(coef_im) * bt_im_ref[grp, :]
        bb_im = one(coef_re) * bt_im_ref[grp, :] + one(coef_im) * bt_re_ref[grp, :]
        yc = yc_ref[grp, :]
        yc_sw = pltpu.roll(yc, SSM_P, axis=1)
        y2 = jnp.where(low, yc_sw, -yc_sw)
        x_rows, ca_rows = [], []
        for tau in range(CHUNK):
            pr, pi = one(pw_re[tau]), one(pw_im[tau])
            ab_re = bb_re * pr - bb_im * pi
            ab_im = bb_re * pi + bb_im * pr
            s = CHUNK - 1 - tau
            rows = slice(s * LANES + g * SSM_GC, s * LANES + (g + 1) * SSM_GC)
            pre_ref[rows, :] = ab_re
            pim_ref[rows, :] = ab_im
            x_rows.append(jnp.where(low, ab_re, ab_im))
            ca_rows.append(yc * one(pw_re[tau + 1]) + y2 * one(pw_im[tau + 1]))
        kern = lax.dot_general(jnp.concatenate(x_rows, axis=0), jnp.concatenate([yc] * GPS, axis=0),
                               contract_lanes, precision=lax.Precision.HIGHEST,
                               preferred_element_type=f32)
        for tau in range(CHUNK):
            blk = kern[tau * SSM_GC:(tau + 1) * SSM_GC]
            if tau == 0:
                blk = blk + jnp.where(diag, d_ref[...], 0.0)
            kt_ref[tau * LANES + g * SSM_GC:tau * LANES + (g + 1) * SSM_GC, :] = blk
        qt = lax.dot_general(eye_ref[...], jnp.concatenate(ca_rows, axis=0).astype(bf16),
                             contract_lanes, preferred_element_type=f32)
        qc_ref[g * SSM_P:(g + 1) * SSM_P, :] = qt[:SSM_P].astype(bf16)
        qc_ref[GPS * SSM_P + g * SSM_P:GPS * SSM_P + (g + 1) * SSM_P, :] = qt[SSM_P:].astype(bf16)
    lo_lane = lax.broadcasted_iota(jnp.int32, (1, LANES), 1) < SSM_P
    for k in range(GPS // 2):
        sel = lambda pw: jnp.where(lo_lane, pw[2 * k:2 * k + 1], pw[2 * k + 1:2 * k + 2])
        at_re_ref[:, k * LANES:(k + 1) * LANES] = sel(pw_re[CHUNK])
        at_im_ref[:, k * LANES:(k + 1) * LANES] = sel(pw_im[CHUNK])


def _ssm_operators(lam_re, lam_im, log_dt, b_re, b_im, c_re, c_im, d_skip):
    dup = lambda x: jnp.concatenate([x, x], axis=-1)
    per_sg = lambda x: x.astype(f32).reshape((SG, x.size // (SG * x.shape[-1]), x.shape[-1]))
    bt = lambda b: per_sg(dup(b.astype(f32).swapaxes(1, 2)))
    yc = per_sg(jnp.concatenate([c_re.astype(f32), -c_im.astype(f32)], axis=-1))
    args = (per_sg(dup(lam_re)), per_sg(dup(lam_im)), log_dt.astype(f32).reshape(SG, GPS, 1),
            bt(b_re), bt(b_im), yc, d_skip.astype(f32).reshape(SG, 1, LANES), jnp.eye(LANES, dtype=bf16))
    blk = lambda r, c: pl.BlockSpec((None, r, c), lambda g: (g, 0, 0))
    return pl.pallas_call(
        _ssm_prep_kernel,
        grid=(SG,),
        in_specs=[blk(GPS, LANES), blk(GPS, LANES), blk(GPS, 1), blk(LANES, LANES), blk(LANES, LANES),
                  blk(LANES, LANES), blk(1, LANES), _const_spec((LANES, LANES))],
        out_specs=[blk(SG_IN, LANES), blk(SG_IN, LANES), blk(SG_IN, LANES),
                   blk(SG_STATE, CHUNK * SSM_GC), blk(1, SG_STATE // 2), blk(1, SG_STATE // 2)],
        out_shape=[jax.ShapeDtypeStruct((SG, SG_IN, LANES), f32)] * 3
                  + [jax.ShapeDtypeStruct((SG, SG_STATE, CHUNK * SSM_GC), bf16)]
                  + [jax.ShapeDtypeStruct((SG, 1, SG_STATE // 2), f32)] * 2,
        compiler_params=_params(("parallel",)),
        name="ssm_prep",
    )(*args)


def _replication_matrix():
    src = jnp.arange(CHUNK * SSM_GC)
    dst = jnp.arange(SG_IN)
    same_t = (src[:, None] // SSM_GC) == (dst[None, :] // LANES)
    same_c = (src[:, None] % SSM_GC) == (dst[None, :] % SSM_GC)
    return (same_t & same_c).astype(bf16)


def _mix_kernel(oa_ref, y2_ref, ga_ref, gs_ref, wglu_ref, wb_ref, wdn_ref, m_ref, wdn_o_ref, yb_ref):
    wdn_o_ref[...] = wdn_ref[...].astype(bf16)
    for r in range(oa_ref.shape[0] // ROW_SUB):
        rows = pl.ds(r * ROW_SUB, ROW_SUB)
        chunk_rows = pl.ds(r * (ROW_SUB // CHUNK), ROW_SUB // CHUNK)
        for sg in range(SG):
            for t in range(CHUNK):
                col = sg * SG_IN + t * LANES
                yb_ref[r, sg, pl.ds(t, ROW_SUB // CHUNK, stride=CHUNK), :] = (
                    y2_ref[chunk_rows, col:col + LANES].astype(f32))
        y = jnp.concatenate([yb_ref[r, sg] for sg in range(SG)], axis=1).astype(bf16)
        zg = _dot(y, wglu_ref[...])
        o_ssm = zg[:, :SSM_W] * _sigmoid(zg[:, SSM_W:])
        y_s = _dot(o_ssm.astype(bf16), wb_ref[Q_W:, :])
        y_a = _dot(oa_ref[rows, :], wb_ref[:Q_W, :])
        m_ref[rows, :] = (ga_ref[rows, :].astype(f32) * y_a + gs_ref[rows, :].astype(f32) * y_s).astype(bf16)


def _mix(o_attn, y2, gates, wglu, wb, w_down, tm=512):
    L = o_attn.shape[0]
    dn_slab = pl.BlockSpec((w_down.shape[0] // (L // tm), w_down.shape[1]), lambda i: (i, 0))
    return pl.pallas_call(
        _mix_kernel,
        grid=(L // tm,),
        in_specs=[pl.BlockSpec((tm, Q_W), lambda i: (i, 0)),
                  pl.BlockSpec((tm // CHUNK, SG * SG_IN), lambda i: (i, 0)),
                  pl.BlockSpec((tm, D_MODEL), lambda i: (i, 0)),
                  pl.BlockSpec((tm, D_MODEL), lambda i: (i, 1)),
                  _const_spec(wglu.shape), _const_spec(wb.shape), dn_slab],
        out_specs=[pl.BlockSpec((tm, D_MODEL), lambda i: (i, 0)), dn_slab],
        out_shape=[jax.ShapeDtypeStruct((L, D_MODEL), bf16), jax.ShapeDtypeStruct(w_down.shape, bf16)],
        scratch_shapes=[pltpu.VMEM((tm // ROW_SUB, SG, ROW_SUB, LANES), f32)],
        compiler_params=_params(("parallel",)),
        name="mix",
    )(o_attn, y2, gates, gates, wglu, wb, w_down)


def _outproj_kernel(m_ref, w_ref, x_ref, gpost_ref, gpre_ref, x1_ref, h2_ref):
    for r in range(m_ref.shape[0] // OUT_SUB):
        rows = pl.ds(r * OUT_SUB, OUT_SUB)
        out = _dot(m_ref[rows, :], w_ref[...])
        x1 = x_ref[rows, :] + _rms_norm(out, gpost_ref[...])
        x1_ref[rows, :] = x1
        h2_ref[rows, :] = _rms_norm(x1, gpre_ref[...]).astype(bf16)


def _outproj(m, w_out, x, g_post, g_pre, tm=512):
    L = x.shape[0]
    row = lambda: pl.BlockSpec((tm, D_MODEL), lambda i: (i, 0))
    return pl.pallas_call(
        _outproj_kernel,
        grid=(L // tm,),
        in_specs=[row(), _const_spec(w_out.shape), row(),
                  _const_spec((1, D_MODEL)), _const_spec((1, D_MODEL))],
        out_specs=[row(), row()],
        out_shape=[jax.ShapeDtypeStruct((L, D_MODEL), f32), jax.ShapeDtypeStruct((L, D_MODEL), bf16)],
        compiler_params=_params(("parallel",)),
        name="outproj",
    )(m, w_out, x, g_post, g_pre)


MLP_SUB = 512


def _mlp_kernel(h_ref, wu_ref, wd_ref, x_ref, g_ref, o_ref, acc_ref):
    i, j = pl.program_id(0), pl.program_id(1)
    n_tiles = pl.num_programs(0) - 1
    tm = h_ref.shape[0]
    slab = o_ref.shape[0]
    cur, prev = acc_ref.at[i % 2], acc_ref.at[(i + 1) % 2]
    slab_rows = pl.ds(pl.multiple_of(j * slab, slab), slab)

    def finish_prev_slab():
        o_ref[...] = x_ref[...] + _rms_norm(prev[slab_rows, :], g_ref[...])
        prev[slab_rows, :] = jnp.zeros((slab, D_MODEL), f32)

    @pl.when((i == 0) & (j == 0))
    def _():
        acc_ref[...] = jnp.zeros_like(acc_ref)

    @pl.when(i < n_tiles)
    def _():
        for r in range(tm // MLP_SUB):
            rows = pl.ds(r * MLP_SUB, MLP_SUB)
            a = jnp.maximum(_dot(h_ref[rows, :], wu_ref[...]), 0.0)
            cur[rows, :] += _dot((a * a).astype(bf16), wd_ref[...])
        finish_prev_slab()

    @pl.when(i == n_tiles)
    def _():
        finish_prev_slab()


def _mlp(h2, w_up, w_down, x1, g_post, tm=1024, tf=1024):
    L = x1.shape[0]
    n_tiles, n_ff = L // tm, D_FF // tf
    slab = tm // n_ff
    ff = lambda i, j: jnp.where(i < n_tiles, j, n_ff - 1)
    prev_slab = lambda i, j: (jnp.maximum(i - 1, 0) * n_ff + jnp.where(i > 0, j, 0), 0)
    return pl.pallas_call(
        _mlp_kernel,
        grid=(n_tiles + 1, n_ff),
        in_specs=[pl.BlockSpec((tm, D_MODEL), lambda i, j: (jnp.minimum(i, n_tiles - 1), 0)),
                  pl.BlockSpec((D_MODEL, tf), lambda i, j: (0, ff(i, j))),
                  pl.BlockSpec((tf, D_MODEL), lambda i, j: (ff(i, j), 0)),
                  pl.BlockSpec((slab, D_MODEL), prev_slab),
                  _const_spec((1, D_MODEL))],
        out_specs=pl.BlockSpec((slab, D_MODEL), prev_slab),
        out_shape=jax.ShapeDtypeStruct((L, D_MODEL), f32),
        scratch_shapes=[pltpu.VMEM((2, tm, D_MODEL), f32)],
        compiler_params=_params(("arbitrary", "arbitrary")),
        name="mlp",
    )(h2, w_up, w_down, x1, g_post)


def _rope_tables(L):
    half = ROT_DIM // 2
    inv = ROPE_THETA ** (-jnp.arange(half, dtype=f32) * 2.0 / ROT_DIM)
    ang = jnp.arange(L).astype(f32)[:, None] * inv[None, :]
    cos, sin = jnp.cos(ang), jnp.sin(ang)
    ones = jnp.ones((L, HEAD_DIM - ROT_DIM), f32)
    zeros = jnp.zeros((L, HEAD_DIM - ROT_DIM), f32)
    zh = jnp.zeros((L, half), f32)
    per_head = lambda parts: jnp.tile(jnp.concatenate(parts, axis=1), (1, LANES // HEAD_DIM))
    return (per_head([cos, cos, ones]), per_head([-sin, zh, zeros]), per_head([zh, sin, zeros]))


def _layer(x, norm_mix_pre, norm_mix_post, norm_mlp_pre, norm_mlp_post, w_in, sinks,
           lam_re, lam_im, log_dt, b_re, b_im, c_re, c_im, d_skip, w_glu, w_branch, w_out,
           w_up, w_down, rope):
    gain = lambda g: g.astype(f32).reshape(1, D_MODEL)
    q, k, v, u2, h, wg = _inproj(x, gain(norm_mix_pre), w_in.astype(f32), *rope)
    gates, wup = _gates(h, wg, w_up.astype(f32))
    o_attn, (wglu, wb, wo) = _attention(q, k, v, sinks.astype(f32),
                                        [w_glu.astype(f32), w_branch.astype(f32), w_out.astype(f32)])
    pre, pim, kt, qc, at_re, at_im = _ssm_operators(lam_re, lam_im, log_dt, b_re, b_im, c_re, c_im, d_skip)
    y2 = _ssm(u2, pre, pim, kt, qc, _replication_matrix(), at_re, at_im)
    mixed, wdn = _mix(o_attn, y2, gates, wglu, wb, w_down.astype(f32))
    x1, h2 = _outproj(mixed, wo, x, gain(norm_mix_post), gain(norm_mlp_pre))
    return _mlp(h2, wup, wdn, x1, gain(norm_mlp_post))


def kernel(x, norm_mix_pre, norm_mix_post, norm_mlp_pre, norm_mlp_post, w_in, sinks, lam_re, lam_im, log_dt, b_re, b_im, c_re, c_im, d_skip, w_glu, w_branch, w_out, w_up, w_down):
    B, L, _ = x.shape
    depth = w_in.shape[0]
    rope = _rope_tables(L)
    outs = []
    for b in range(B):
        xb = x[b]
        for l in range(depth):
            xb = _layer(xb, norm_mix_pre[l], norm_mix_post[l], norm_mlp_pre[l], norm_mlp_post[l],
                        w_in[l], sinks[l], lam_re[l], lam_im[l], log_dt[l], b_re[l], b_im[l],
                        c_re[l], c_im[l], d_skip[l], w_glu[l], w_branch[l], w_out[l],
                        w_up[l], w_down[l], rope)
        outs.append(xb)
    return jnp.stack(outs)
```

```python
import math

import jax
import jax.numpy as jnp
from jax import lax
from jax.experimental import pallas as pl
from jax.experimental.pallas import tpu as pltpu

D_MODEL = 2048
SEQ = 8192
HEAD_DIM = 64
N_Q_HEADS = 16
N_KV_HEADS = 2
BLOCK = 128
ROT_DIM = HEAD_DIM // 4
ROPE_THETA = 500000.0
Q_W = N_Q_HEADS * HEAD_DIM
KV_W = N_KV_HEADS * HEAD_DIM
SSM_W = D_MODEL // 2
QKV_W = Q_W + 2 * KV_W
IN_MIX_W = QKV_W + SSM_W
SSM_GC = 16
SSM_G = SSM_W // SSM_GC
SSM_P = 64
D_FF = 4 * D_MODEL
EPS = 1e-6

LANES = 128
CHUNK = 16
N_CHUNKS = SEQ // CHUNK
SG = SSM_W // LANES
GPS = LANES // SSM_GC
SG_IN = CHUNK * LANES
SG_STATE = 2 * GPS * SSM_P
VMEM_LIMIT = 56 * 1024 * 1024
ROW_SUB = 256
OUT_SUB = 128
MASKED = -1e30

bf16 = jnp.bfloat16
f32 = jnp.float32


def _dot(a, b):
    return jnp.dot(a, b, preferred_element_type=f32)


def _sigmoid(x):
    return 1.0 / (1.0 + jnp.exp(-x))


def _gelu_tanh(x):
    c = math.sqrt(2.0 / math.pi)
    return x * (0.5 * (1.0 + jnp.tanh(c * (x + 0.044715 * (x * x * x)))))


def _rms_norm(x, g):
    return x * lax.rsqrt(jnp.mean(x * x, axis=-1, keepdims=True) + EPS) * g


def _params(sem):
    return pltpu.CompilerParams(dimension_semantics=sem, vmem_limit_bytes=VMEM_LIMIT)


def _const_spec(shape):
    nd = len(shape)
    return pl.BlockSpec(shape, lambda *_: (0,) * nd, pipeline_mode=pl.Buffered(1))


def _rope_block(z, cos, sin_a, sin_b):
    return (z * cos + pltpu.roll(z, LANES - ROT_DIM // 2, axis=1) * sin_a
            + pltpu.roll(z, ROT_DIM // 2, axis=1) * sin_b)


def _inproj_kernel(x_ref, g_ref, w_ref, wcast_ref, cos_ref, sa_ref, sb_ref,
                   q_ref, k_ref, v_ref, u2_ref, h_ref, wg_ref, wqkv_ref, wu_ref, zu_ref):
    @pl.when(pl.program_id(0) == 0)
    def _():
        wqkv_ref[...] = w_ref[:, :QKV_W].astype(bf16)
        wu_ref[...] = w_ref[:, QKV_W:].astype(bf16)

    wg_ref[...] = wcast_ref[...].astype(bf16)
    for r in range(x_ref.shape[0] // ROW_SUB):
        rows = pl.ds(r * ROW_SUB, ROW_SUB)
        h = _rms_norm(x_ref[rows, :], g_ref[...]).astype(bf16)
        h_ref[rows, :] = h
        cos, sin_a, sin_b = cos_ref[rows, :], sa_ref[rows, :], sb_ref[rows, :]
        for half in range(2):
            zq = _dot(h, wqkv_ref[:, half * 512:(half + 1) * 512])
            for b in range(4):
                blk = zq[:, b * LANES:(b + 1) * LANES]
                col = (half * 4 + b) * LANES
                q_ref[rows, col:col + LANES] = _rope_block(blk, cos, sin_a, sin_b).astype(bf16)
        zkv = _dot(h, wqkv_ref[:, Q_W:Q_W + 2 * KV_W])
        k_ref[rows, :] = _rope_block(zkv[:, :KV_W], cos, sin_a, sin_b).astype(bf16)
        v_ref[rows, :] = zkv[:, KV_W:].astype(bf16)
        zu = _dot(h, wu_ref[...])
        chunk_rows = pl.ds(r * (ROW_SUB // CHUNK), ROW_SUB // CHUNK)
        for sg in range(SG):
            zu_ref[r, sg] = zu[:, sg * LANES:(sg + 1) * LANES]
            for s in range(CHUNK):
                col = sg * SG_IN + s * LANES
                u2_ref[chunk_rows, col:col + LANES] = (
                    zu_ref[r, sg, pl.ds(s, ROW_SUB // CHUNK, stride=CHUNK), :].astype(bf16))


def _inproj(x, gain, w_in, cos, sin_a, sin_b, tm=512):
    L = x.shape[0]
    steps = L // tm
    gate_w = w_in.shape[1] - IN_MIX_W
    slab = gate_w // steps
    assert IN_MIX_W % slab == 0 and slab % LANES == 0
    row = lambda w: pl.BlockSpec((tm, w), lambda i: (i, 0))
    return pl.pallas_call(
        _inproj_kernel,
        grid=(steps,),
        in_specs=[row(D_MODEL), _const_spec((1, D_MODEL)),
                  pl.BlockSpec((D_MODEL, IN_MIX_W), lambda i: (0, 0), pipeline_mode=pl.Buffered(1)),
                  pl.BlockSpec((D_MODEL, slab), lambda i: (0, IN_MIX_W // slab + i)),
                  row(LANES), row(LANES), row(LANES)],
        out_specs=[row(Q_W), row(KV_W), row(KV_W),
                   pl.BlockSpec((tm // CHUNK, SG * SG_IN), lambda i: (i, 0)), row(D_MODEL),
                   pl.BlockSpec((D_MODEL, slab), lambda i: (0, i))],
        out_shape=[jax.ShapeDtypeStruct((L, Q_W), bf16), jax.ShapeDtypeStruct((L, KV_W), bf16),
                   jax.ShapeDtypeStruct((L, KV_W), bf16),
                   jax.ShapeDtypeStruct((L // CHUNK, SG * SG_IN), bf16),
                   jax.ShapeDtypeStruct((L, D_MODEL), bf16),
                   jax.ShapeDtypeStruct((D_MODEL, gate_w), bf16)],
        scratch_shapes=[pltpu.VMEM((D_MODEL, QKV_W), bf16), pltpu.VMEM((D_MODEL, SSM_W), bf16),
                        pltpu.VMEM((tm // ROW_SUB, SG, ROW_SUB, LANES), f32)],
        compiler_params=_params(("arbitrary",)),
        name="inproj",
    )(x, gain, w_in, w_in, cos, sin_a, sin_b)


def _gates_kernel(h_ref, w_ref, *rest):
    n_cast = (len(rest) - 1) // 2
    o_ref = rest[n_cast]
    for r in range(h_ref.shape[0] // ROW_SUB):
        rows = pl.ds(r * ROW_SUB, ROW_SUB)
        o_ref[rows, :] = _sigmoid(_dot(h_ref[rows, :], w_ref[...])).astype(bf16)
    for src, dst in zip(rest[:n_cast], rest[n_cast + 1:]):
        dst[...] = src[...].astype(bf16)


def _gates(h, wg, w_up, row_cast_weights, tm=1024, tn=2048):
    L, N = h.shape[0], wg.shape[1]
    ni = L // tm
    steps = ni * (N // tn)
    step = lambda j, i: j * ni + i
    slabs = [pl.BlockSpec((w_up.shape[0], w_up.shape[1] // steps), lambda j, i: (0, step(j, i)))]
    slabs += [pl.BlockSpec((w.shape[0] // steps, w.shape[1]), lambda j, i: (step(j, i), 0))
              for w in row_cast_weights]
    cast_weights = [w_up] + list(row_cast_weights)
    outs = pl.pallas_call(
        _gates_kernel,
        grid=(N // tn, ni),
        in_specs=[pl.BlockSpec((tm, D_MODEL), lambda j, i: (i, 0)),
                  pl.BlockSpec((D_MODEL, tn), lambda j, i: (0, j))] + slabs,
        out_specs=[pl.BlockSpec((tm, tn), lambda j, i: (i, j))] + slabs,
        out_shape=[jax.ShapeDtypeStruct((L, N), bf16)]
                  + [jax.ShapeDtypeStruct(w.shape, bf16) for w in cast_weights],
        compiler_params=_params(("parallel", "arbitrary")),
        name="gates",
    )(h, wg, *cast_weights)
    return outs[0], outs[1], outs[2:]


GC_SHIFT = SSM_GC.bit_length() - 1
P_SHIFT = SSM_P.bit_length() - 1
Y_TILE = 512
NT = SG_STATE // LANES
NSEG = 8
SEG = N_CHUNKS // NSEG
SEG_PITCH = SEG + 8


def _ssm_kernel(u_ref, pre_ref, pim_ref, kt_ref, qc_ref, rep_ref, ar_ref, ai_ref, y_ref,
                p_scr, m_scr, q_scr, s_scr, xl_scr, pw_scr, xs_scr):
    half = SG_STATE // 2
    row = lax.broadcasted_iota(jnp.int32, (SG_IN, LANES), 0)
    lane = lax.broadcasted_iota(jnp.int32, (SG_IN, LANES), 1)
    row_grp = (row >> GC_SHIFT) & (GPS - 1)
    pre, pim = pre_ref[...], pim_ref[...]
    for k in range(half // LANES):
        sel = row_grp == 2 * k + (lane >> P_SHIFT)
        p_scr[:, k * LANES:(k + 1) * LANES] = jnp.where(sel, pre, 0.0).astype(bf16)
        p_scr[:, half + k * LANES:half + (k + 1) * LANES] = jnp.where(sel, pim, 0.0).astype(bf16)

    blk_row = lax.broadcasted_iota(jnp.int32, (LANES, LANES), 0) >> GC_SHIFT
    blk_lane = lax.broadcasted_iota(jnp.int32, (LANES, LANES), 1) >> GC_SHIFT
    zero_blk = jnp.zeros((LANES, LANES), bf16)
    for tau in range(CHUNK):
        blk = jnp.where(blk_row == blk_lane, kt_ref[tau * LANES:(tau + 1) * LANES, :], 0.0).astype(bf16)
        for s in range(CHUNK - tau):
            t = s + tau
            m_scr[s * LANES:(s + 1) * LANES, t * LANES:(t + 1) * LANES] = blk
            if tau > 0:
                m_scr[t * LANES:(t + 1) * LANES, s * LANES:(s + 1) * LANES] = zero_blk

    q_row_grp = (lax.broadcasted_iota(jnp.int32, (SG_STATE, Y_TILE), 0) >> P_SHIFT) & (GPS - 1)
    q_lane_grp = (lax.broadcasted_iota(jnp.int32, (SG_STATE, Y_TILE), 1) >> GC_SHIFT) & (GPS - 1)
    for j in range(SG_IN // Y_TILE):
        full = _dot(qc_ref[...], rep_ref[:, j * Y_TILE:(j + 1) * Y_TILE])
        q_scr[:, j * Y_TILE:(j + 1) * Y_TILE] = jnp.where(q_row_grp == q_lane_grp, full, 0.0).astype(bf16)

    s = _dot(u_ref[...], p_scr[...])
    for j in range(NT):
        for k in range(NSEG):
            s_scr[j, k * SEG_PITCH:k * SEG_PITCH + SEG, :] = s[k * SEG:(k + 1) * SEG, j * LANES:(j + 1) * LANES]
    ht = NT // 2
    cmul = lambda xr, xi, yr, yi: (xr * yr - xi * yi, xr * yi + xi * yr)
    ar = [ar_ref[:, j * LANES:(j + 1) * LANES] for j in range(ht)]
    ai = [ai_ref[:, j * LANES:(j + 1) * LANES] for j in range(ht)]
    one, nil = jnp.ones((1, LANES), f32), jnp.zeros((1, LANES), f32)
    a_seg = []
    for j in range(ht):
        rr, ri = [one], [nil]
        for _ in range(7):
            nr, ni = cmul(rr[-1], ri[-1], ar[j], ai[j])
            rr.append(nr)
            ri.append(ni)
        a8r, a8i = cmul(rr[-1], ri[-1], ar[j], ai[j])
        blk_r, blk_i = jnp.concatenate(rr, axis=0), jnp.concatenate(ri, axis=0)
        for b in range(SEG // 8):
            pw_scr[j, b * 8:(b + 1) * 8, :] = blk_r
            pw_scr[ht + j, b * 8:(b + 1) * 8, :] = blk_i
            blk_r, blk_i = cmul(blk_r, blk_i, a8r, a8i)
        a_seg.append((blk_r[0:1], blk_i[0:1]))

    ar_b = [jnp.broadcast_to(a, (NSEG, LANES)) for a in ar]
    ai_b = [jnp.broadcast_to(a, (NSEG, LANES)) for a in ai]

    def body(i, carry):
        new_r, new_i = [], []
        for j in range(ht):
            xr, xi = carry[j], carry[ht + j]
            seg_rows = pl.ds(i, NSEG, stride=SEG_PITCH)
            xl_scr[j, seg_rows, :] = xr
            xl_scr[ht + j, seg_rows, :] = xi
            pr, pi = cmul(xr, xi, ar_b[j], ai_b[j])
            new_r.append(pr + s_scr[j, seg_rows, :])
            new_i.append(pi + s_scr[ht + j, seg_rows, :])
        return tuple(new_r + new_i)

    ends = tuple(jnp.zeros((NSEG, LANES), f32) for _ in range(NT))
    for i in range(SEG):
        ends = body(i, ends)

    for j in range(ht):
        zr, zi = nil, nil
        for k in range(NSEG):
            rows = slice(k * SEG, (k + 1) * SEG)
            loc = slice(k * SEG_PITCH, k * SEG_PITCH + SEG)
            fr, fi = cmul(pw_scr[j], pw_scr[ht + j], zr, zi)
            xs_scr[rows, j * LANES:(j + 1) * LANES] = (xl_scr[j, loc, :] + fr).astype(bf16)
            xs_scr[rows, half + j * LANES:half + (j + 1) * LANES] = (xl_scr[ht + j, loc, :] + fi).astype(bf16)
            pr, pi = cmul(zr, zi, *a_seg[j])
            zr, zi = pr + ends[j][k:k + 1], pi + ends[ht + j][k:k + 1]

    xs = xs_scr[...]
    for j in range(SG_IN // Y_TILE):
        kk = (j + 1) * Y_TILE
        cols = slice(j * Y_TILE, (j + 1) * Y_TILE)
        y = _dot(u_ref[:, :kk], m_scr[:kk, cols]) + _dot(xs, q_scr[:, cols])
        y_ref[:, cols] = _gelu_tanh(y).astype(bf16)


def _ssm(u2, pre, pim, kt, qc, rep, at_re, at_im):
    per_sg = lambda r, c: pl.BlockSpec((None, r, c), lambda g: (g, 0, 0))
    return pl.pallas_call(
        _ssm_kernel,
        grid=(SG,),
        in_specs=[pl.BlockSpec((N_CHUNKS, SG_IN), lambda g: (0, g)),
                  per_sg(SG_IN, LANES), per_sg(SG_IN, LANES), per_sg(SG_IN, LANES),
                  per_sg(SG_STATE, CHUNK * SSM_GC), _const_spec(rep.shape),
                  per_sg(1, SG_STATE // 2), per_sg(1, SG_STATE // 2)],
        out_specs=pl.BlockSpec((N_CHUNKS, SG_IN), lambda g: (0, g)),
        out_shape=jax.ShapeDtypeStruct((N_CHUNKS, SG * SG_IN), bf16),
        scratch_shapes=[pltpu.VMEM((SG_IN, SG_STATE), bf16), pltpu.VMEM((SG_IN, SG_IN), bf16),
                        pltpu.VMEM((SG_STATE, SG_IN), bf16),
                        pltpu.VMEM((NT, NSEG * SEG_PITCH, LANES), f32),
                        pltpu.VMEM((NT, NSEG * SEG_PITCH, LANES), f32),
                        pltpu.VMEM((NT, SEG, LANES), f32), pltpu.VMEM((N_CHUNKS, SG_STATE), bf16)],
        compiler_params=_params(("arbitrary",)),
        name="ssm",
    )(u2, pre, pim, kt, qc, rep, at_re, at_im)


def _ssm_prep_kernel(lam_re_ref, lam_im_ref, ldt_ref, bt_re_ref, bt_im_ref, yc_ref, d_ref, eye_ref,
                     pre_ref, pim_ref, kt_ref, qc_ref, at_re_ref, at_im_ref):
    lr, li = lam_re_ref[...], lam_im_ref[...]
    dt = jnp.exp(ldt_ref[...])
    mag = jnp.exp(lr * dt)
    a_re, a_im = mag * jnp.cos(li * dt), mag * jnp.sin(li * dt)
    den = lr * lr + li * li
    nr, ni = a_re - 1.0, a_im
    coef_re = (nr * lr + ni * li) / den
    coef_im = (ni * lr - nr * li) / den
    pw_re, pw_im = [jnp.ones_like(a_re)], [jnp.zeros_like(a_im)]
    for _ in range(CHUNK):
        pr, pi = pw_re[-1], pw_im[-1]
        pw_re.append(pr * a_re - pi * a_im)
        pw_im.append(pr * a_im + pi * a_re)

    low = lax.broadcasted_iota(jnp.int32, (SSM_GC, LANES), 1) < SSM_P
    diag = ((lax.broadcasted_iota(jnp.int32, (SSM_GC, LANES), 1) & (SSM_GC - 1))
            == lax.broadcasted_iota(jnp.int32, (SSM_GC, LANES), 0))
    contract_lanes = (((1,), (1,)), ((), ()))
    for g in range(GPS):
        grp = slice(g * SSM_GC, (g + 1) * SSM_GC)
        one = lambda x: x[g:g + 1]
        bb_re = one(coef_re) * bt_re_ref[grp, :] - one(coef_im) * bt_im_ref[grp, :]
        bb_im = one(coef_re) * bt_im_ref[grp, :] + one(coef_im) * bt_re_ref[grp, :]
        yc = yc_ref[grp, :]
        yc_sw = pltpu.roll(yc, SSM_P, axis=1)
        y2 = jnp.where(low, yc_sw, -yc_sw)
        x_rows, ca_rows = [], []
        for tau in range(CHUNK):
            pr, pi = one(pw_re[tau]), one(pw_im[tau])
            ab_re = bb_re * pr - bb_im * pi
            ab_im = bb_re * pi + bb_im * pr
            s = CHUNK - 1 - tau
            rows = slice(s * LANES + g * SSM_GC, s * LANES + (g + 1) * SSM_GC)
            pre_ref[rows, :] = ab_re
            pim_ref[rows, :] = ab_im
            x_rows.append(jnp.where(low, ab_re, ab_im))
            ca_rows.append(yc * one(pw_re[tau + 1]) + y2 * one(pw_im[tau + 1]))
        kern = lax.dot_general(jnp.concatenate(x_rows, axis=0), jnp.concatenate([yc] * GPS, axis=0),
                               contract_lanes, precision=lax.Precision.HIGHEST,
                               preferred_element_type=f32)
        for tau in range(CHUNK):
            blk = kern[tau * SSM_GC:(tau + 1) * SSM_GC]
            if tau == 0:
                blk = blk + jnp.where(diag, d_ref[...], 0.0)
            kt_ref[tau * LANES + g * SSM_GC:tau * LANES + (g + 1) * SSM_GC, :] = blk
        qt = lax.dot_general(eye_ref[...], jnp.concatenate(ca_rows, axis=0).astype(bf16),
                             contract_lanes, preferred_element_type=f32)
        qc_ref[g * SSM_P:(g + 1) * SSM_P, :] = qt[:SSM_P].astype(bf16)
        qc_ref[GPS * SSM_P + g * SSM_P:GPS * SSM_P + (g + 1) * SSM_P, :] = qt[SSM_P:].astype(bf16)
    lo_lane = lax.broadcasted_iota(jnp.int32, (1, LANES), 1) < SSM_P
    for k in range(GPS // 2):
        sel = lambda pw: jnp.where(lo_lane, pw[2 * k:2 * k + 1], pw[2 * k + 1:2 * k + 2])
        at_re_ref[:, k * LANES:(k + 1) * LANES] = sel(pw_re[CHUNK])
        at_im_ref[:, k * LANES:(k + 1) * LANES] = sel(pw_im[CHUNK])


def _ssm_operators(lam_re, lam_im, log_dt, b_re, b_im, c_re, c_im, d_skip):
    dup = lambda x: jnp.concatenate([x, x], axis=-1)
    per_sg = lambda x: x.astype(f32).reshape((SG, x.size // (SG * x.shape[-1]), x.shape[-1]))
    bt = lambda b: per_sg(dup(b.astype(f32).swapaxes(1, 2)))
    yc = per_sg(jnp.concatenate([c_re.astype(f32), -c_im.astype(f32)], axis=-1))
    args = (per_sg(dup(lam_re)), per_sg(dup(lam_im)), log_dt.astype(f32).reshape(SG, GPS, 1),
            bt(b_re), bt(b_im), yc, d_skip.astype(f32).reshape(SG, 1, LANES), jnp.eye(LANES, dtype=bf16))
    blk = lambda r, c: pl.BlockSpec((None, r, c), lambda g: (g, 0, 0))
    return pl.pallas_call(
        _ssm_prep_kernel,
        grid=(SG,),
        in_specs=[blk(GPS, LANES), blk(GPS, LANES), blk(GPS, 1), blk(LANES, LANES), blk(LANES, LANES),
                  blk(LANES, LANES), blk(1, LANES), _const_spec((LANES, LANES))],
        out_specs=[blk(SG_IN, LANES), blk(SG_IN, LANES), blk(SG_IN, LANES),
                   blk(SG_STATE, CHUNK * SSM_GC), blk(1, SG_STATE // 2), blk(1, SG_STATE // 2)],
        out_shape=[jax.ShapeDtypeStruct((SG, SG_IN, LANES), f32)] * 3
                  + [jax.ShapeDtypeStruct((SG, SG_STATE, CHUNK * SSM_GC), bf16)]
                  + [jax.ShapeDtypeStruct((SG, 1, SG_STATE // 2), f32)] * 2,
        compiler_params=_params(("parallel",)),
        name="ssm_prep",
    )(*args)


def _replication_matrix():
    src = jnp.arange(CHUNK * SSM_GC)
    dst = jnp.arange(SG_IN)
    same_t = (src[:, None] // SSM_GC) == (dst[None, :] // LANES)
    same_c = (src[:, None] % SSM_GC) == (dst[None, :] % SSM_GC)
    return (same_t & same_c).astype(bf16)


def _kv_operands(kp_ref, kc_ref, vp_ref, vc_ref):
    kcat = jnp.concatenate([kp_ref[...], kc_ref[...]], axis=0).astype(f32)
    vcat = jnp.concatenate([vp_ref[...], vc_ref[...]], axis=0).astype(f32)
    k_sw = pltpu.roll(kcat, HEAD_DIM, axis=1)
    v_sw = pltpu.roll(vcat, HEAD_DIM, axis=1)
    low_kv = lax.broadcasted_iota(jnp.int32, kcat.shape, 1) < HEAD_DIM
    operands = []
    for hk in range(N_KV_HEADS):
        k_src, k_oth = (kcat, k_sw) if hk == 0 else (k_sw, kcat)
        k_lo = jnp.where(low_kv, k_src, 0.0).astype(bf16)
        k_hi = jnp.where(low_kv, 0.0, k_oth).astype(bf16)
        vv = (jnp.where(low_kv, vcat, v_sw) if hk == 0 else jnp.where(low_kv, v_sw, vcat)).astype(bf16)
        operands.append((k_lo, k_hi, vv))
    return operands


def _attention_block(q, kv_operands, sinks_ref, blk, no_prev, oa_ref):
    q_rows = slice(blk * BLOCK, (blk + 1) * BLOCK)
    kv_rows = slice(blk * BLOCK, (blk + 2) * BLOCK)
    from_prev = (lax.broadcasted_iota(jnp.int32, (BLOCK, BLOCK), 1)
                 > lax.broadcasted_iota(jnp.int32, (BLOCK, BLOCK), 0))
    low_o = lax.broadcasted_iota(jnp.int32, (BLOCK, LANES), 1) < HEAD_DIM
    contract_lanes = (((1,), (1,)), ((), ()))
    pairs_per_kv = (N_Q_HEADS // N_KV_HEADS) // 2
    for hk, (k_lo, k_hi, vv) in enumerate(kv_operands):
        qs = jnp.concatenate(
            [q[q_rows, (pairs_per_kv * hk + b) * LANES:(pairs_per_kv * hk + b + 1) * LANES]
             for b in range(pairs_per_kv)], axis=0)
        s_par = (lax.dot_general(qs, k_lo[kv_rows], contract_lanes, preferred_element_type=f32),
                 lax.dot_general(qs, k_hi[kv_rows], contract_lanes, preferred_element_type=f32))
        for b in range(pairs_per_kv):
            outs = []
            for par in range(2):
                head = 2 * (pairs_per_kv * hk + b) + par
                s_all = s_par[par][b * BLOCK:(b + 1) * BLOCK]
                s_prev = s_all[:, :BLOCK] + no_prev if blk == 0 else s_all[:, :BLOCK]
                s = jnp.where(from_prev, s_prev, s_all[:, BLOCK:])
                sink = sinks_ref[head]
                m = jnp.maximum(jnp.max(s, axis=1, keepdims=True), sink)
                p = jnp.exp(s - m)
                denom = jnp.sum(p, axis=1, keepdims=True) + jnp.exp(sink - m)
                p_kv = jnp.concatenate([jnp.where(from_prev, p, 0.0), jnp.where(from_prev, 0.0, p)],
                                       axis=1).astype(bf16)
                outs.append(_dot(p_kv, vv[kv_rows]) * (1.0 / denom))
            col = (pairs_per_kv * hk + b) * LANES
            oa_ref[q_rows, col:col + LANES] = jnp.where(low_o, outs[0], outs[1]).astype(bf16)


def _attn_mix_kernel(sinks_ref, q_ref, kp_ref, kc_ref, vp_ref, vc_ref, y2_ref, ga_ref, gs_ref,
                     wglu_ref, wb_ref, wdn_ref, m_ref, wdn_o_ref, oa_ref, yb_ref):
    wdn_o_ref[...] = wdn_ref[...].astype(bf16)
    no_prev = jnp.where(pl.program_id(0) > 0, 0.0, MASKED)
    q = q_ref[...] * jnp.asarray(1.0 / math.sqrt(HEAD_DIM), bf16)
    kv_operands = _kv_operands(kp_ref, kc_ref, vp_ref, vc_ref)
    for r in range(q_ref.shape[0] // ROW_SUB):
        for blk in range(r * (ROW_SUB // BLOCK), (r + 1) * (ROW_SUB // BLOCK)):
            _attention_block(q, kv_operands, sinks_ref, blk, no_prev, oa_ref)
        rows = pl.ds(r * ROW_SUB, ROW_SUB)
        chunk_rows = pl.ds(r * (ROW_SUB // CHUNK), ROW_SUB // CHUNK)
        for sg in range(SG):
            for t in range(CHUNK):
                col = sg * SG_IN + t * LANES
                yb_ref[r, sg, pl.ds(t, ROW_SUB // CHUNK, stride=CHUNK), :] = (
                    y2_ref[chunk_rows, col:col + LANES].astype(f32))
        y = jnp.concatenate([yb_ref[r, sg] for sg in range(SG)], axis=1).astype(bf16)
        zg = _dot(y, wglu_ref[...])
        o_ssm = zg[:, :SSM_W] * _sigmoid(zg[:, SSM_W:])
        y_s = _dot(o_ssm.astype(bf16), wb_ref[Q_W:, :])
        y_a = _dot(oa_ref[rows, :], wb_ref[:Q_W, :])
        m_ref[rows, :] = (ga_ref[rows, :].astype(f32) * y_a + gs_ref[rows, :].astype(f32) * y_s).astype(bf16)


def _attn_mix(q, k, v, sinks, y2, gates, wglu, wb, w_down, tm=512):
    L = q.shape[0]
    cur = lambda i: (i, 0)
    prev = lambda i: (jnp.maximum(i * (tm // BLOCK) - 1, 0), 0)
    dn_slab = pl.BlockSpec((w_down.shape[0] // (L // tm), w_down.shape[1]), cur)
    return pl.pallas_call(
        _attn_mix_kernel,
        grid=(L // tm,),
        in_specs=[pl.BlockSpec(memory_space=pltpu.SMEM),
                  pl.BlockSpec((tm, Q_W), cur),
                  pl.BlockSpec((BLOCK, KV_W), prev), pl.BlockSpec((tm, KV_W), cur),
                  pl.BlockSpec((BLOCK, KV_W), prev), pl.BlockSpec((tm, KV_W), cur),
                  pl.BlockSpec((tm // CHUNK, SG * SG_IN), cur),
                  pl.BlockSpec((tm, D_MODEL), cur),
                  pl.BlockSpec((tm, D_MODEL), lambda i: (i, 1)),
                  _const_spec(wglu.shape), _const_spec(wb.shape), dn_slab],
        out_specs=[pl.BlockSpec((tm, D_MODEL), cur), dn_slab],
        out_shape=[jax.ShapeDtypeStruct((L, D_MODEL), bf16), jax.ShapeDtypeStruct(w_down.shape, bf16)],
        scratch_shapes=[pltpu.VMEM((tm, Q_W), bf16),
                        pltpu.VMEM((tm // ROW_SUB, SG, ROW_SUB, LANES), f32)],
        compiler_params=_params(("parallel",)),
        name="attn_mix",
    )(sinks, q, k, k, v, v, y2, gates, gates, wglu, wb, w_down)


def _outproj_kernel(m_ref, w_ref, x_ref, gpost_ref, gpre_ref, x1_ref, h2_ref):
    for r in range(m_ref.shape[0] // OUT_SUB):
        rows = pl.ds(r * OUT_SUB, OUT_SUB)
        out = _dot(m_ref[rows, :], w_ref[...])
        x1 = x_ref[rows, :] + _rms_norm(out, gpost_ref[...])
        x1_ref[rows, :] = x1
        h2_ref[rows, :] = _rms_norm(x1, gpre_ref[...]).astype(bf16)


def _outproj(m, w_out, x, g_post, g_pre, tm=512):
    L = x.shape[0]
    row = lambda: pl.BlockSpec((tm, D_MODEL), lambda i: (i, 0))
    return pl.pallas_call(
        _outproj_kernel,
        grid=(L // tm,),
        in_specs=[row(), _const_spec(w_out.shape), row(),
                  _const_spec((1, D_MODEL)), _const_spec((1, D_MODEL))],
        out_specs=[row(), row()],
        out_shape=[jax.ShapeDtypeStruct((L, D_MODEL), f32), jax.ShapeDtypeStruct((L, D_MODEL), bf16)],
        compiler_params=_params(("parallel",)),
        name="outproj",
    )(m, w_out, x, g_post, g_pre)


MLP_SUB = 512


def _mlp_kernel(h_ref, wu_ref, wd_ref, x_ref, g_ref, o_ref, acc_ref):
    i, j = pl.program_id(0), pl.program_id(1)
    n_tiles = pl.num_programs(0) - 1
    tm = h_ref.shape[0]
    slab = o_ref.shape[0]
    cur, prev = acc_ref.at[i % 2], acc_ref.at[(i + 1) % 2]
    slab_rows = pl.ds(pl.multiple_of(j * slab, slab), slab)

    def finish_prev_slab():
        o_ref[...] = x_ref[...] + _rms_norm(prev[slab_rows, :], g_ref[...])
        prev[slab_rows, :] = jnp.zeros((slab, D_MODEL), f32)

    @pl.when((i == 0) & (j == 0))
    def _():
        acc_ref[...] = jnp.zeros_like(acc_ref)

    @pl.when(i < n_tiles)
    def _():
        for r in range(tm // MLP_SUB):
            rows = pl.ds(r * MLP_SUB, MLP_SUB)
            a = jnp.maximum(_dot(h_ref[rows, :], wu_ref[...]), 0.0)
            cur[rows, :] += _dot((a * a).astype(bf16), wd_ref[...])
        finish_prev_slab()

    @pl.when(i == n_tiles)
    def _():
        finish_prev_slab()


def _mlp(h2, w_up, w_down, x1, g_post, tm=1024, tf=1024):
    L = x1.shape[0]
    n_tiles, n_ff = L // tm, D_FF // tf
    slab = tm // n_ff
    ff = lambda i, j: jnp.where(i < n_tiles, j, n_ff - 1)
    prev_slab = lambda i, j: (jnp.maximum(i - 1, 0) * n_ff + jnp.where(i > 0, j, 0), 0)
    return pl.pallas_call(
        _mlp_kernel,
        grid=(n_tiles + 1, n_ff),
        in_specs=[pl.BlockSpec((tm, D_MODEL), lambda i, j: (jnp.minimum(i, n_tiles - 1), 0)),
                  pl.BlockSpec((D_MODEL, tf), lambda i, j: (0, ff(i, j))),
                  pl.BlockSpec((tf, D_MODEL), lambda i, j: (ff(i, j), 0)),
                  pl.BlockSpec((slab, D_MODEL), prev_slab),
                  _const_spec((1, D_MODEL))],
        out_specs=pl.BlockSpec((slab, D_MODEL), prev_slab),
        out_shape=jax.ShapeDtypeStruct((L, D_MODEL), f32),
        scratch_shapes=[pltpu.VMEM((2, tm, D_MODEL), f32)],
        compiler_params=_params(("arbitrary", "arbitrary")),
        name="mlp",
    )(h2, w_up, w_down, x1, g_post)


def _rope_tables(L):
    half = ROT_DIM // 2
    inv = ROPE_THETA ** (-jnp.arange(half, dtype=f32) * 2.0 / ROT_DIM)
    ang = jnp.arange(L).astype(f32)[:, None] * inv[None, :]
    cos, sin = jnp.cos(ang), jnp.sin(ang)
    ones = jnp.ones((L, HEAD_DIM - ROT_DIM), f32)
    zeros = jnp.zeros((L, HEAD_DIM - ROT_DIM), f32)
    zh = jnp.zeros((L, half), f32)
    per_head = lambda parts: jnp.tile(jnp.concatenate(parts, axis=1), (1, LANES // HEAD_DIM))
    return (per_head([cos, cos, ones]), per_head([-sin, zh, zeros]), per_head([zh, sin, zeros]))


def _layer(x, norm_mix_pre, norm_mix_post, norm_mlp_pre, norm_mlp_post, w_in, sinks,
           lam_re, lam_im, log_dt, b_re, b_im, c_re, c_im, d_skip, w_glu, w_branch, w_out,
           w_up, w_down, rope):
    gain = lambda g: g.astype(f32).reshape(1, D_MODEL)
    q, k, v, u2, h, wg = _inproj(x, gain(norm_mix_pre), w_in.astype(f32), *rope)
    gates, wup, (wglu, wb, wo) = _gates(h, wg, w_up.astype(f32),
                                        [w_glu.astype(f32), w_branch.astype(f32), w_out.astype(f32)])
    pre, pim, kt, qc, at_re, at_im = _ssm_operators(lam_re, lam_im, log_dt, b_re, b_im, c_re, c_im, d_skip)
    y2 = _ssm(u2, pre, pim, kt, qc, _replication_matrix(), at_re, at_im)
    mixed, wdn = _attn_mix(q, k, v, sinks.astype(f32), y2, gates, wglu, wb, w_down.astype(f32))
    x1, h2 = _outproj(mixed, wo, x, gain(norm_mix_post), gain(norm_mlp_pre))
    return _mlp(h2, wup, wdn, x1, gain(norm_mlp_post))


def kernel(x, norm_mix_pre, norm_mix_post, norm_mlp_pre, norm_mlp_post, w_in, sinks, lam_re, lam_im, log_dt, b_re, b_im, c_re, c_im, d_skip, w_glu, w_branch, w_out, w_up, w_down):
    B, L, _ = x.shape
    depth = w_in.shape[0]
    rope = _rope_tables(L)
    outs = []
    for b in range(B):
        xb = x[b]
        for l in range(depth):
            xb = _layer(xb, norm_mix_pre[l], norm_mix_post[l], norm_mlp_pre[l], norm_mlp_post[l],
                        w_in[l], sinks[l], lam_re[l], lam_im[l], log_dt[l], b_re[l], b_im[l],
                        c_re[l], c_im[l], d_skip[l], w_glu[l], w_branch[l], w_out[l],
                        w_up[l], w_down[l], rope)
        outs.append(xb)
    return jnp.stack(outs)
```

```python
import math

import jax
import jax.numpy as jnp
import numpy as np
from jax import lax
from jax.experimental import pallas as pl
from jax.experimental.pallas import tpu as pltpu

D_MODEL = 2048
SEQ = 8192
HEAD_DIM = 64
N_Q_HEADS = 16
N_KV_HEADS = 2
BLOCK = 128
ROT_DIM = HEAD_DIM // 4
ROPE_THETA = 500000.0
Q_W = N_Q_HEADS * HEAD_DIM
KV_W = N_KV_HEADS * HEAD_DIM
SSM_W = D_MODEL // 2
QKV_W = Q_W + 2 * KV_W
IN_MIX_W = QKV_W + SSM_W
SSM_GC = 16
SSM_G = SSM_W // SSM_GC
SSM_P = 64
D_FF = 4 * D_MODEL
EPS = 1e-6

LANES = 128
CHUNK = 16
N_CHUNKS = SEQ // CHUNK
SG = SSM_W // LANES
GPS = LANES // SSM_GC
SG_IN = CHUNK * LANES
SG_STATE = 2 * GPS * SSM_P
VMEM_LIMIT = 56 * 1024 * 1024
ROW_SUB = 256
OUT_SUB = 128
MASKED = -1e30

bf16 = jnp.bfloat16
f32 = jnp.float32


def _dot(a, b):
    return jnp.dot(a, b, preferred_element_type=f32)


def _sigmoid(x):
    return 1.0 / (1.0 + jnp.exp(-x))


def _gelu_tanh(x):
    c = math.sqrt(2.0 / math.pi)
    return x * (0.5 * (1.0 + jnp.tanh(c * (x + 0.044715 * (x * x * x)))))


def _rms_norm(x, g):
    return x * lax.rsqrt(jnp.mean(x * x, axis=-1, keepdims=True) + EPS) * g


def _params(sem):
    return pltpu.CompilerParams(dimension_semantics=sem, vmem_limit_bytes=VMEM_LIMIT)


def _const_spec(shape):
    nd = len(shape)
    return pl.BlockSpec(shape, lambda *_: (0,) * nd, pipeline_mode=pl.Buffered(1))


def _rope_block(z, cos, sin_a, sin_b):
    return (z * cos + pltpu.roll(z, LANES - ROT_DIM // 2, axis=1) * sin_a
            + pltpu.roll(z, ROT_DIM // 2, axis=1) * sin_b)


def _inproj_kernel(x_ref, g_ref, w_ref, cos_ref, sa_ref, sb_ref,
                   q_ref, k_ref, v_ref, u2_ref, h_ref, zu_ref):
    for r in range(x_ref.shape[0] // ROW_SUB):
        rows = pl.ds(r * ROW_SUB, ROW_SUB)
        h = _rms_norm(x_ref[rows, :], g_ref[...]).astype(bf16)
        h_ref[rows, :] = h
        cos, sin_a, sin_b = cos_ref[rows, :], sa_ref[rows, :], sb_ref[rows, :]
        for half in range(2):
            zq = _dot(h, w_ref[:, half * 512:(half + 1) * 512])
            for b in range(4):
                blk = zq[:, b * LANES:(b + 1) * LANES]
                col = (half * 4 + b) * LANES
                q_ref[rows, col:col + LANES] = _rope_block(blk, cos, sin_a, sin_b).astype(bf16)
        zkv = _dot(h, w_ref[:, Q_W:QKV_W])
        k_ref[rows, :] = _rope_block(zkv[:, :KV_W], cos, sin_a, sin_b).astype(bf16)
        v_ref[rows, :] = zkv[:, KV_W:].astype(bf16)
        zu = _dot(h, w_ref[:, QKV_W:])
        chunk_rows = pl.ds(r * (ROW_SUB // CHUNK), ROW_SUB // CHUNK)
        for sg in range(SG):
            zu_ref[r, sg] = zu[:, sg * LANES:(sg + 1) * LANES]
            for s in range(CHUNK):
                col = sg * SG_IN + s * LANES
                u2_ref[chunk_rows, col:col + LANES] = (
                    zu_ref[r, sg, pl.ds(s, ROW_SUB // CHUNK, stride=CHUNK), :].astype(bf16))


def _inproj(x, gain, wmix, cos, sin_a, sin_b, tm=1024):
    L = x.shape[0]
    row = lambda w: pl.BlockSpec((tm, w), lambda i: (i, 0))
    return pl.pallas_call(
        _inproj_kernel,
        grid=(L // tm,),
        in_specs=[row(D_MODEL), _const_spec((1, D_MODEL)), _const_spec(wmix.shape),
                  row(LANES), row(LANES), row(LANES)],
        out_specs=[row(Q_W), row(KV_W), row(KV_W),
                   pl.BlockSpec((tm // CHUNK, SG * SG_IN), lambda i: (i, 0)), row(D_MODEL)],
        out_shape=[jax.ShapeDtypeStruct((L, Q_W), bf16), jax.ShapeDtypeStruct((L, KV_W), bf16),
                   jax.ShapeDtypeStruct((L, KV_W), bf16),
                   jax.ShapeDtypeStruct((L // CHUNK, SG * SG_IN), bf16),
                   jax.ShapeDtypeStruct((L, D_MODEL), bf16)],
        scratch_shapes=[pltpu.VMEM((tm // ROW_SUB, SG, ROW_SUB, LANES), f32)],
        compiler_params=_params(("parallel",)),
        name="inproj",
    )(x, gain, wmix, cos, sin_a, sin_b)


def _gates_kernel(h_ref, w_ref, *rest):
    n_cast = (len(rest) - 1) // 2
    o_ref = rest[n_cast]
    for r in range(h_ref.shape[0] // ROW_SUB):
        rows = pl.ds(r * ROW_SUB, ROW_SUB)
        o_ref[rows, :] = _sigmoid(_dot(h_ref[rows, :], w_ref[...])).astype(bf16)
    for src, dst in zip(rest[:n_cast], rest[n_cast + 1:]):
        dst[...] = src[...].astype(bf16)


def _gates(h, wg, w_up, row_cast_weights, tm=1024, tn=2048):
    L, N = h.shape[0], wg.shape[1]
    ni = L // tm
    steps = ni * (N // tn)
    step = lambda j, i: j * ni + i
    slabs = [pl.BlockSpec((w_up.shape[0], w_up.shape[1] // steps), lambda j, i: (0, step(j, i)))]
    slabs += [pl.BlockSpec((w.shape[0] // steps, w.shape[1]), lambda j, i: (step(j, i), 0))
              for w in row_cast_weights]
    cast_weights = [w_up] + list(row_cast_weights)
    outs = pl.pallas_call(
        _gates_kernel,
        grid=(N // tn, ni),
        in_specs=[pl.BlockSpec((tm, D_MODEL), lambda j, i: (i, 0)),
                  pl.BlockSpec((D_MODEL, tn), lambda j, i: (0, j))] + slabs,
        out_specs=[pl.BlockSpec((tm, tn), lambda j, i: (i, j))] + slabs,
        out_shape=[jax.ShapeDtypeStruct((L, N), bf16)]
                  + [jax.ShapeDtypeStruct(w.shape, bf16) for w in cast_weights],
        compiler_params=_params(("parallel", "arbitrary")),
        name="gates",
    )(h, wg, *cast_weights)
    return outs[0], outs[1], outs[2:]


GC_SHIFT = SSM_GC.bit_length() - 1
P_SHIFT = SSM_P.bit_length() - 1
Y_TILE = 512
NT = SG_STATE // LANES
NSEG = 8
SEG = N_CHUNKS // NSEG
SEG_PITCH = SEG + 8


def _ssm_kernel(u_ref, pre_ref, pim_ref, kt_ref, qc_ref, rep_ref, ar_ref, ai_ref, y_ref,
                p_scr, m_scr, q_scr, s_scr, xl_scr, pw_scr, xs_scr):
    half = SG_STATE // 2
    row = lax.broadcasted_iota(jnp.int32, (SG_IN, LANES), 0)
    lane = lax.broadcasted_iota(jnp.int32, (SG_IN, LANES), 1)
    row_grp = (row >> GC_SHIFT) & (GPS - 1)
    pre, pim = pre_ref[...], pim_ref[...]
    for k in range(half // LANES):
        sel = row_grp == 2 * k + (lane >> P_SHIFT)
        p_scr[:, k * LANES:(k + 1) * LANES] = jnp.where(sel, pre, 0.0).astype(bf16)
        p_scr[:, half + k * LANES:half + (k + 1) * LANES] = jnp.where(sel, pim, 0.0).astype(bf16)

    blk_row = lax.broadcasted_iota(jnp.int32, (LANES, LANES), 0) >> GC_SHIFT
    blk_lane = lax.broadcasted_iota(jnp.int32, (LANES, LANES), 1) >> GC_SHIFT
    zero_blk = jnp.zeros((LANES, LANES), bf16)
    for tau in range(CHUNK):
        blk = jnp.where(blk_row == blk_lane, kt_ref[tau * LANES:(tau + 1) * LANES, :], 0.0).astype(bf16)
        for s in range(CHUNK - tau):
            t = s + tau
            m_scr[s * LANES:(s + 1) * LANES, t * LANES:(t + 1) * LANES] = blk
            if tau > 0:
                m_scr[t * LANES:(t + 1) * LANES, s * LANES:(s + 1) * LANES] = zero_blk

    q_row_grp = (lax.broadcasted_iota(jnp.int32, (SG_STATE, Y_TILE), 0) >> P_SHIFT) & (GPS - 1)
    q_lane_grp = (lax.broadcasted_iota(jnp.int32, (SG_STATE, Y_TILE), 1) >> GC_SHIFT) & (GPS - 1)
    for j in range(SG_IN // Y_TILE):
        full = _dot(qc_ref[...], rep_ref[:, j * Y_TILE:(j + 1) * Y_TILE])
        q_scr[:, j * Y_TILE:(j + 1) * Y_TILE] = jnp.where(q_row_grp == q_lane_grp, full, 0.0).astype(bf16)

    s = _dot(u_ref[...], p_scr[...])
    for j in range(NT):
        for k in range(NSEG):
            s_scr[j, k * SEG_PITCH:k * SEG_PITCH + SEG, :] = s[k * SEG:(k + 1) * SEG, j * LANES:(j + 1) * LANES]
    ht = NT // 2
    cmul = lambda xr, xi, yr, yi: (xr * yr - xi * yi, xr * yi + xi * yr)
    ar = [ar_ref[:, j * LANES:(j + 1) * LANES] for j in range(ht)]
    ai = [ai_ref[:, j * LANES:(j + 1) * LANES] for j in range(ht)]
    one, nil = jnp.ones((1, LANES), f32), jnp.zeros((1, LANES), f32)
    a_seg = []
    for j in range(ht):
        rr, ri = [one], [nil]
        for _ in range(7):
            nr, ni = cmul(rr[-1], ri[-1], ar[j], ai[j])
            rr.append(nr)
            ri.append(ni)
        a8r, a8i = cmul(rr[-1], ri[-1], ar[j], ai[j])
        blk_r, blk_i = jnp.concatenate(rr, axis=0), jnp.concatenate(ri, axis=0)
        for b in range(SEG // 8):
            pw_scr[j, b * 8:(b + 1) * 8, :] = blk_r
            pw_scr[ht + j, b * 8:(b + 1) * 8, :] = blk_i
            blk_r, blk_i = cmul(blk_r, blk_i, a8r, a8i)
        a_seg.append((blk_r[0:1], blk_i[0:1]))

    ar_b = [jnp.broadcast_to(a, (NSEG, LANES)) for a in ar]
    ai_b = [jnp.broadcast_to(a, (NSEG, LANES)) for a in ai]

    def body(i, carry):
        new_r, new_i = [], []
        for j in range(ht):
            xr, xi = carry[j], carry[ht + j]
            seg_rows = pl.ds(i, NSEG, stride=SEG_PITCH)
            xl_scr[j, seg_rows, :] = xr
            xl_scr[ht + j, seg_rows, :] = xi
            pr, pi = cmul(xr, xi, ar_b[j], ai_b[j])
            new_r.append(pr + s_scr[j, seg_rows, :])
            new_i.append(pi + s_scr[ht + j, seg_rows, :])
        return tuple(new_r + new_i)

    ends = tuple(jnp.zeros((NSEG, LANES), f32) for _ in range(NT))
    for i in range(SEG):
        ends = body(i, ends)

    for j in range(ht):
        zr, zi = nil, nil
        for k in range(NSEG):
            rows = slice(k * SEG, (k + 1) * SEG)
            loc = slice(k * SEG_PITCH, k * SEG_PITCH + SEG)
            fr, fi = cmul(pw_scr[j], pw_scr[ht + j], zr, zi)
            xs_scr[rows, j * LANES:(j + 1) * LANES] = (xl_scr[j, loc, :] + fr).astype(bf16)
            xs_scr[rows, half + j * LANES:half + (j + 1) * LANES] = (xl_scr[ht + j, loc, :] + fi).astype(bf16)
            pr, pi = cmul(zr, zi, *a_seg[j])
            zr, zi = pr + ends[j][k:k + 1], pi + ends[ht + j][k:k + 1]

    xs = xs_scr[...]
    for j in range(SG_IN // Y_TILE):
        kk = (j + 1) * Y_TILE
        cols = slice(j * Y_TILE, (j + 1) * Y_TILE)
        y = _dot(u_ref[:, :kk], m_scr[:kk, cols]) + _dot(xs, q_scr[:, cols])
        y_ref[:, cols] = _gelu_tanh(y).astype(bf16)


def _ssm(u2, pre, pim, kt, qc, rep, at_re, at_im):
    per_sg = lambda r, c: pl.BlockSpec((None, r, c), lambda g: (g, 0, 0))
    return pl.pallas_call(
        _ssm_kernel,
        grid=(SG,),
        in_specs=[pl.BlockSpec((N_CHUNKS, SG_IN), lambda g: (0, g)),
                  per_sg(SG_IN, LANES), per_sg(SG_IN, LANES), per_sg(SG_IN, LANES),
                  per_sg(SG_STATE, CHUNK * SSM_GC), _const_spec(rep.shape),
                  per_sg(1, SG_STATE // 2), per_sg(1, SG_STATE // 2)],
        out_specs=pl.BlockSpec((N_CHUNKS, SG_IN), lambda g: (0, g)),
        out_shape=jax.ShapeDtypeStruct((N_CHUNKS, SG * SG_IN), bf16),
        scratch_shapes=[pltpu.VMEM((SG_IN, SG_STATE), bf16), pltpu.VMEM((SG_IN, SG_IN), bf16),
                        pltpu.VMEM((SG_STATE, SG_IN), bf16),
                        pltpu.VMEM((NT, NSEG * SEG_PITCH, LANES), f32),
                        pltpu.VMEM((NT, NSEG * SEG_PITCH, LANES), f32),
                        pltpu.VMEM((NT, SEG, LANES), f32), pltpu.VMEM((N_CHUNKS, SG_STATE), bf16)],
        compiler_params=_params(("arbitrary",)),
        name="ssm",
    )(u2, pre, pim, kt, qc, rep, at_re, at_im)


def _ssm_prep_kernel(lam_re_ref, lam_im_ref, ldt_ref, bt_re_ref, bt_im_ref, yc_ref, d_ref, eye_ref, win_ref,
                     pre_ref, pim_ref, kt_ref, qc_ref, at_re_ref, at_im_ref, wmix_ref, wg_ref):
    wmix_ref[...] = win_ref[:, :IN_MIX_W].astype(bf16)
    wg_ref[...] = win_ref[:, IN_MIX_W:].astype(bf16)
    lr, li = lam_re_ref[...], lam_im_ref[...]
    dt = jnp.exp(ldt_ref[...])
    mag = jnp.exp(lr * dt)
    a_re, a_im = mag * jnp.cos(li * dt), mag * jnp.sin(li * dt)
    den = lr * lr + li * li
    nr, ni = a_re - 1.0, a_im
    coef_re = (nr * lr + ni * li) / den
    coef_im = (ni * lr - nr * li) / den
    pw_re, pw_im = [jnp.ones_like(a_re)], [jnp.zeros_like(a_im)]
    for _ in range(CHUNK):
        pr, pi = pw_re[-1], pw_im[-1]
        pw_re.append(pr * a_re - pi * a_im)
        pw_im.append(pr * a_im + pi * a_re)

    low = lax.broadcasted_iota(jnp.int32, (SSM_GC, LANES), 1) < SSM_P
    diag = ((lax.broadcasted_iota(jnp.int32, (SSM_GC, LANES), 1) & (SSM_GC - 1))
            == lax.broadcasted_iota(jnp.int32, (SSM_GC, LANES), 0))
    contract_lanes = (((1,), (1,)), ((), ()))
    for g in range(GPS):
        grp = slice(g * SSM_GC, (g + 1) * SSM_GC)
        one = lambda x: x[g:g + 1]
        bb_re = one(coef_re) * bt_re_ref[grp, :] - one(coef_im) * bt_im_ref[grp, :]
        bb_im = one(coef_re) * bt_im_ref[grp, :] + one(coef_im) * bt_re_ref[grp, :]
        yc = yc_ref[grp, :]
        yc_sw = pltpu.roll(yc, SSM_P, axis=1)
        y2 = jnp.where(low, yc_sw, -yc_sw)
        x_rows, ca_rows = [], []
        for tau in range(CHUNK):
            pr, pi = one(pw_re[tau]), one(pw_im[tau])
            ab_re = bb_re * pr - bb_im * pi
            ab_im = bb_re * pi + bb_im * pr
            s = CHUNK - 1 - tau
            rows = slice(s * LANES + g * SSM_GC, s * LANES + (g + 1) * SSM_GC)
            pre_ref[rows, :] = ab_re
            pim_ref[rows, :] = ab_im
            x_rows.append(jnp.where(low, ab_re, ab_im))
            ca_rows.append(yc * one(pw_re[tau + 1]) + y2 * one(pw_im[tau + 1]))
        kern = lax.dot_general(jnp.concatenate(x_rows, axis=0), jnp.concatenate([yc] * GPS, axis=0),
                               contract_lanes, precision=lax.Precision.HIGHEST,
                               preferred_element_type=f32)
        for tau in range(CHUNK):
            blk = kern[tau * SSM_GC:(tau + 1) * SSM_GC]
            if tau == 0:
                blk = blk + jnp.where(diag, d_ref[...], 0.0)
            kt_ref[tau * LANES + g * SSM_GC:tau * LANES + (g + 1) * SSM_GC, :] = blk
        qt = lax.dot_general(eye_ref[...], jnp.concatenate(ca_rows, axis=0).astype(bf16),
                             contract_lanes, preferred_element_type=f32)
        qc_ref[g * SSM_P:(g + 1) * SSM_P, :] = qt[:SSM_P].astype(bf16)
        qc_ref[GPS * SSM_P + g * SSM_P:GPS * SSM_P + (g + 1) * SSM_P, :] = qt[SSM_P:].astype(bf16)
    lo_lane = lax.broadcasted_iota(jnp.int32, (1, LANES), 1) < SSM_P
    for k in range(GPS // 2):
        sel = lambda pw: jnp.where(lo_lane, pw[2 * k:2 * k + 1], pw[2 * k + 1:2 * k + 2])
        at_re_ref[:, k * LANES:(k + 1) * LANES] = sel(pw_re[CHUNK])
        at_im_ref[:, k * LANES:(k + 1) * LANES] = sel(pw_im[CHUNK])


def _ssm_operators(lam_re, lam_im, log_dt, b_re, b_im, c_re, c_im, d_skip, w_in):
    dup = lambda x: jnp.concatenate([x, x], axis=-1)
    per_sg = lambda x: x.astype(f32).reshape((SG, x.size // (SG * x.shape[-1]), x.shape[-1]))
    bt = lambda b: per_sg(dup(b.astype(f32).swapaxes(1, 2)))
    yc = per_sg(jnp.concatenate([c_re.astype(f32), -c_im.astype(f32)], axis=-1))
    args = (per_sg(dup(lam_re)), per_sg(dup(lam_im)), log_dt.astype(f32).reshape(SG, GPS, 1),
            bt(b_re), bt(b_im), yc, d_skip.astype(f32).reshape(SG, 1, LANES), jnp.eye(LANES, dtype=bf16),
            w_in)
    blk = lambda r, c: pl.BlockSpec((None, r, c), lambda g: (g, 0, 0))
    slab = lambda c: pl.BlockSpec((w_in.shape[0] // SG, c), lambda g: (g, 0))
    gate_w = w_in.shape[1] - IN_MIX_W
    return pl.pallas_call(
        _ssm_prep_kernel,
        grid=(SG,),
        in_specs=[blk(GPS, LANES), blk(GPS, LANES), blk(GPS, 1), blk(LANES, LANES), blk(LANES, LANES),
                  blk(LANES, LANES), blk(1, LANES), _const_spec((LANES, LANES)), slab(w_in.shape[1])],
        out_specs=[blk(SG_IN, LANES), blk(SG_IN, LANES), blk(SG_IN, LANES),
                   blk(SG_STATE, CHUNK * SSM_GC), blk(1, SG_STATE // 2), blk(1, SG_STATE // 2),
                   slab(IN_MIX_W), slab(gate_w)],
        out_shape=[jax.ShapeDtypeStruct((SG, SG_IN, LANES), f32)] * 3
                  + [jax.ShapeDtypeStruct((SG, SG_STATE, CHUNK * SSM_GC), bf16)]
                  + [jax.ShapeDtypeStruct((SG, 1, SG_STATE // 2), f32)] * 2
                  + [jax.ShapeDtypeStruct((w_in.shape[0], IN_MIX_W), bf16),
                     jax.ShapeDtypeStruct((w_in.shape[0], gate_w), bf16)],
        compiler_params=_params(("parallel",)),
        name="ssm_prep",
    )(*args)


def _replication_matrix():
    src = jnp.arange(CHUNK * SSM_GC)
    dst = jnp.arange(SG_IN)
    same_t = (src[:, None] // SSM_GC) == (dst[None, :] // LANES)
    same_c = (src[:, None] % SSM_GC) == (dst[None, :] % SSM_GC)
    return (same_t & same_c).astype(bf16)


def _kv_operands(kp_ref, kc_ref, vp_ref, vc_ref):
    kcat = jnp.concatenate([kp_ref[...], kc_ref[...]], axis=0).astype(f32)
    vcat = jnp.concatenate([vp_ref[...], vc_ref[...]], axis=0).astype(f32)
    k_sw = pltpu.roll(kcat, HEAD_DIM, axis=1)
    v_sw = pltpu.roll(vcat, HEAD_DIM, axis=1)
    low_kv = lax.broadcasted_iota(jnp.int32, kcat.shape, 1) < HEAD_DIM
    operands = []
    for hk in range(N_KV_HEADS):
        k_src, k_oth = (kcat, k_sw) if hk == 0 else (k_sw, kcat)
        k_lo = jnp.where(low_kv, k_src, 0.0).astype(bf16)
        k_hi = jnp.where(low_kv, 0.0, k_oth).astype(bf16)
        vv = (jnp.where(low_kv, vcat, v_sw) if hk == 0 else jnp.where(low_kv, v_sw, vcat)).astype(bf16)
        operands.append((k_lo, k_hi, vv))
    return operands


def _attention_block(q, kv_operands, sinks_ref, blk, no_prev, oa_ref):
    q_rows = slice(blk * BLOCK, (blk + 1) * BLOCK)
    kv_rows = slice(blk * BLOCK, (blk + 2) * BLOCK)
    from_prev = (lax.broadcasted_iota(jnp.int32, (BLOCK, BLOCK), 1)
                 > lax.broadcasted_iota(jnp.int32, (BLOCK, BLOCK), 0))
    low_o = lax.broadcasted_iota(jnp.int32, (BLOCK, LANES), 1) < HEAD_DIM
    contract_lanes = (((1,), (1,)), ((), ()))
    pairs_per_kv = (N_Q_HEADS // N_KV_HEADS) // 2
    for hk, (k_lo, k_hi, vv) in enumerate(kv_operands):
        qs = jnp.concatenate(
            [q[q_rows, (pairs_per_kv * hk + b) * LANES:(pairs_per_kv * hk + b + 1) * LANES]
             for b in range(pairs_per_kv)], axis=0)
        s_par = (lax.dot_general(qs, k_lo[kv_rows], contract_lanes, preferred_element_type=f32),
                 lax.dot_general(qs, k_hi[kv_rows], contract_lanes, preferred_element_type=f32))
        for b in range(pairs_per_kv):
            outs = []
            for par in range(2):
                head = 2 * (pairs_per_kv * hk + b) + par
                s_all = s_par[par][b * BLOCK:(b + 1) * BLOCK]
                s_prev = s_all[:, :BLOCK] + no_prev if blk == 0 else s_all[:, :BLOCK]
                s = jnp.where(from_prev, s_prev, s_all[:, BLOCK:])
                sink = sinks_ref[head]
                m = jnp.maximum(jnp.max(s, axis=1, keepdims=True), sink)
                p = jnp.exp(s - m)
                denom = jnp.sum(p, axis=1, keepdims=True) + jnp.exp(sink - m)
                p_kv = jnp.concatenate([jnp.where(from_prev, p, 0.0), jnp.where(from_prev, 0.0, p)],
                                       axis=1).astype(bf16)
                outs.append(_dot(p_kv, vv[kv_rows]) * (1.0 / denom))
            col = (pairs_per_kv * hk + b) * LANES
            oa_ref[q_rows, col:col + LANES] = jnp.where(low_o, outs[0], outs[1]).astype(bf16)


def _attn_mix_kernel(sinks_ref, q_ref, kp_ref, kc_ref, vp_ref, vc_ref, y2_ref, ga_ref, gs_ref,
                     wglu_ref, wb_ref, wdn_ref, m_ref, wdn_o_ref, oa_ref, yb_ref):
    wdn_o_ref[...] = wdn_ref[...].astype(bf16)
    no_prev = jnp.where(pl.program_id(0) > 0, 0.0, MASKED)
    q = q_ref[...] * jnp.asarray(1.0 / math.sqrt(HEAD_DIM), bf16)
    kv_operands = _kv_operands(kp_ref, kc_ref, vp_ref, vc_ref)
    for r in range(q_ref.shape[0] // ROW_SUB):
        for blk in range(r * (ROW_SUB // BLOCK), (r + 1) * (ROW_SUB // BLOCK)):
            _attention_block(q, kv_operands, sinks_ref, blk, no_prev, oa_ref)
        rows = pl.ds(r * ROW_SUB, ROW_SUB)
        chunk_rows = pl.ds(r * (ROW_SUB // CHUNK), ROW_SUB // CHUNK)
        for sg in range(SG):
            for t in range(CHUNK):
                col = sg * SG_IN + t * LANES
                yb_ref[r, sg, pl.ds(t, ROW_SUB // CHUNK, stride=CHUNK), :] = (
                    y2_ref[chunk_rows, col:col + LANES].astype(f32))
        y = jnp.concatenate([yb_ref[r, sg] for sg in range(SG)], axis=1).astype(bf16)
        zg = _dot(y, wglu_ref[...])
        o_ssm = zg[:, :SSM_W] * _sigmoid(zg[:, SSM_W:])
        y_s = _dot(o_ssm.astype(bf16), wb_ref[Q_W:, :])
        y_a = _dot(oa_ref[rows, :], wb_ref[:Q_W, :])
        m_ref[rows, :] = (ga_ref[rows, :].astype(f32) * y_a + gs_ref[rows, :].astype(f32) * y_s).astype(bf16)


def _attn_mix(q, k, v, sinks, y2, gates, wglu, wb, w_down, tm=512):
    L = q.shape[0]
    cur = lambda i: (i, 0)
    prev = lambda i: (jnp.maximum(i * (tm // BLOCK) - 1, 0), 0)
    dn_slab = pl.BlockSpec((w_down.shape[0] // (L // tm), w_down.shape[1]), cur)
    return pl.pallas_call(
        _attn_mix_kernel,
        grid=(L // tm,),
        in_specs=[pl.BlockSpec(memory_space=pltpu.SMEM),
                  pl.BlockSpec((tm, Q_W), cur),
                  pl.BlockSpec((BLOCK, KV_W), prev), pl.BlockSpec((tm, KV_W), cur),
                  pl.BlockSpec((BLOCK, KV_W), prev), pl.BlockSpec((tm, KV_W), cur),
                  pl.BlockSpec((tm // CHUNK, SG * SG_IN), cur),
                  pl.BlockSpec((tm, D_MODEL), cur),
                  pl.BlockSpec((tm, D_MODEL), lambda i: (i, 1)),
                  _const_spec(wglu.shape), _const_spec(wb.shape), dn_slab],
        out_specs=[pl.BlockSpec((tm, D_MODEL), cur), dn_slab],
        out_shape=[jax.ShapeDtypeStruct((L, D_MODEL), bf16), jax.ShapeDtypeStruct(w_down.shape, bf16)],
        scratch_shapes=[pltpu.VMEM((tm, Q_W), bf16),
                        pltpu.VMEM((tm // ROW_SUB, SG, ROW_SUB, LANES), f32)],
        compiler_params=_params(("parallel",)),
        name="attn_mix",
    )(sinks, q, k, k, v, v, y2, gates, gates, wglu, wb, w_down)


def _outproj_kernel(m_ref, w_ref, x_ref, gpost_ref, gpre_ref, x1_ref, h2_ref):
    for r in range(m_ref.shape[0] // OUT_SUB):
        rows = pl.ds(r * OUT_SUB, OUT_SUB)
        out = _dot(m_ref[rows, :], w_ref[...])
        x1 = x_ref[rows, :] + _rms_norm(out, gpost_ref[...])
        x1_ref[rows, :] = x1
        h2_ref[rows, :] = _rms_norm(x1, gpre_ref[...]).astype(bf16)


def _outproj(m, w_out, x, g_post, g_pre, tm=512):
    L = x.shape[0]
    row = lambda: pl.BlockSpec((tm, D_MODEL), lambda i: (i, 0))
    return pl.pallas_call(
        _outproj_kernel,
        grid=(L // tm,),
        in_specs=[row(), _const_spec(w_out.shape), row(),
                  _const_spec((1, D_MODEL)), _const_spec((1, D_MODEL))],
        out_specs=[row(), row()],
        out_shape=[jax.ShapeDtypeStruct((L, D_MODEL), f32), jax.ShapeDtypeStruct((L, D_MODEL), bf16)],
        compiler_params=_params(("parallel",)),
        name="outproj",
    )(m, w_out, x, g_post, g_pre)


MLP_SUB = 512


def _mlp_kernel(h_ref, wu_ref, wd_ref, x_ref, g_ref, o_ref, acc_ref):
    i, j = pl.program_id(0), pl.program_id(1)
    n_tiles = pl.num_programs(0) - 1
    tm = h_ref.shape[0]
    slab = o_ref.shape[0]
    cur, prev = acc_ref.at[i % 2], acc_ref.at[(i + 1) % 2]
    slab_rows = pl.ds(pl.multiple_of(j * slab, slab), slab)

    def finish_prev_slab():
        o_ref[...] = x_ref[...] + _rms_norm(prev[slab_rows, :], g_ref[...])
        prev[slab_rows, :] = jnp.zeros((slab, D_MODEL), f32)

    @pl.when((i == 0) & (j == 0))
    def _():
        acc_ref[...] = jnp.zeros_like(acc_ref)

    @pl.when(i < n_tiles)
    def _():
        for r in range(tm // MLP_SUB):
            rows = pl.ds(r * MLP_SUB, MLP_SUB)
            a = jnp.maximum(_dot(h_ref[rows, :], wu_ref[...]), 0.0)
            cur[rows, :] += _dot((a * a).astype(bf16), wd_ref[...])
        finish_prev_slab()

    @pl.when(i == n_tiles)
    def _():
        finish_prev_slab()


def _mlp(h2, w_up, w_down, x1, g_post, tm=1024, tf=1024):
    L = x1.shape[0]
    n_tiles, n_ff = L // tm, D_FF // tf
    slab = tm // n_ff
    ff = lambda i, j: jnp.where(i < n_tiles, j, n_ff - 1)
    prev_slab = lambda i, j: (jnp.maximum(i - 1, 0) * n_ff + jnp.where(i > 0, j, 0), 0)
    return pl.pallas_call(
        _mlp_kernel,
        grid=(n_tiles + 1, n_ff),
        in_specs=[pl.BlockSpec((tm, D_MODEL), lambda i, j: (jnp.minimum(i, n_tiles - 1), 0)),
                  pl.BlockSpec((D_MODEL, tf), lambda i, j: (0, ff(i, j))),
                  pl.BlockSpec((tf, D_MODEL), lambda i, j: (ff(i, j), 0)),
                  pl.BlockSpec((slab, D_MODEL), prev_slab),
                  _const_spec((1, D_MODEL))],
        out_specs=pl.BlockSpec((slab, D_MODEL), prev_slab),
        out_shape=jax.ShapeDtypeStruct((L, D_MODEL), f32),
        scratch_shapes=[pltpu.VMEM((2, tm, D_MODEL), f32)],
        compiler_params=_params(("arbitrary", "arbitrary")),
        name="mlp",
    )(h2, w_up, w_down, x1, g_post)


def _rope_tables(L):
    half = ROT_DIM // 2
    inv = ROPE_THETA ** (-np.arange(half, dtype=np.float64) * 2.0 / ROT_DIM)
    ang = np.arange(L, dtype=np.float64)[:, None] * inv[None, :]
    cos, sin = np.cos(ang).astype(np.float32), np.sin(ang).astype(np.float32)
    ones = np.ones((L, HEAD_DIM - ROT_DIM), np.float32)
    zeros = np.zeros((L, HEAD_DIM - ROT_DIM), np.float32)
    zh = np.zeros((L, half), np.float32)
    per_head = lambda parts: jnp.asarray(np.tile(np.concatenate(parts, axis=1), (1, LANES // HEAD_DIM)))
    return (per_head([cos, cos, ones]), per_head([-sin, zh, zeros]), per_head([zh, sin, zeros]))


def _layer(x, norm_mix_pre, norm_mix_post, norm_mlp_pre, norm_mlp_post, w_in, sinks,
           lam_re, lam_im, log_dt, b_re, b_im, c_re, c_im, d_skip, w_glu, w_branch, w_out,
           w_up, w_down, rope):
    gain = lambda g: g.astype(f32).reshape(1, D_MODEL)
    pre, pim, kt, qc, at_re, at_im, wmix, wg = _ssm_operators(
        lam_re, lam_im, log_dt, b_re, b_im, c_re, c_im, d_skip, w_in.astype(f32))
    q, k, v, u2, h = _inproj(x, gain(norm_mix_pre), wmix, *rope)
    gates, wup, (wglu, wb, wo) = _gates(h, wg, w_up.astype(f32),
                                        [w_glu.astype(f32), w_branch.astype(f32), w_out.astype(f32)])
    y2 = _ssm(u2, pre, pim, kt, qc, _replication_matrix(), at_re, at_im)
    mixed, wdn = _attn_mix(q, k, v, sinks.astype(f32), y2, gates, wglu, wb, w_down.astype(f32))
    x1, h2 = _outproj(mixed, wo, x, gain(norm_mix_post), gain(norm_mlp_pre))
    return _mlp(h2, wup, wdn, x1, gain(norm_mlp_post))


def kernel(x, norm_mix_pre, norm_mix_post, norm_mlp_pre, norm_mlp_post, w_in, sinks, lam_re, lam_im, log_dt, b_re, b_im, c_re, c_im, d_skip, w_glu, w_branch, w_out, w_up, w_down):
    B, L, _ = x.shape
    depth = w_in.shape[0]
    rope = _rope_tables(L)
    outs = []
    for b in range(B):
        xb = x[b]
        for l in range(depth):
            xb = _layer(xb, norm_mix_pre[l], norm_mix_post[l], norm_mlp_pre[l], norm_mlp_post[l],
                        w_in[l], sinks[l], lam_re[l], lam_im[l], log_dt[l], b_re[l], b_im[l],
                        c_re[l], c_im[l], d_skip[l], w_glu[l], w_branch[l], w_out[l],
                        w_up[l], w_down[l], rope)
        outs.append(xb)
    return jnp.stack(outs)
```

```python
import math

import jax
import jax.numpy as jnp
import numpy as np
from jax import lax
from jax.experimental import pallas as pl
from jax.experimental.pallas import tpu as pltpu

D_MODEL = 2048
SEQ = 8192
HEAD_DIM = 64
N_Q_HEADS = 16
N_KV_HEADS = 2
BLOCK = 128
ROT_DIM = HEAD_DIM // 4
ROPE_THETA = 500000.0
Q_W = N_Q_HEADS * HEAD_DIM
KV_W = N_KV_HEADS * HEAD_DIM
SSM_W = D_MODEL // 2
QKV_W = Q_W + 2 * KV_W
IN_MIX_W = QKV_W + SSM_W
SSM_GC = 16
SSM_G = SSM_W // SSM_GC
SSM_P = 64
D_FF = 4 * D_MODEL
EPS = 1e-6

LANES = 128
CHUNK = 16
N_CHUNKS = SEQ // CHUNK
SG = SSM_W // LANES
GPS = LANES // SSM_GC
SG_IN = CHUNK * LANES
SG_STATE = 2 * GPS * SSM_P
VMEM_LIMIT = 56 * 1024 * 1024
ROW_SUB = 256
OUT_SUB = 128
MASKED = -1e30

bf16 = jnp.bfloat16
f32 = jnp.float32


def _dot(a, b):
    return jnp.dot(a, b, preferred_element_type=f32)


def _sigmoid(x):
    return 1.0 / (1.0 + jnp.exp(-x))


def _gelu_tanh(x):
    c = math.sqrt(2.0 / math.pi)
    return x * (0.5 * (1.0 + jnp.tanh(c * (x + 0.044715 * (x * x * x)))))


def _rms_norm(x, g):
    return x * lax.rsqrt(jnp.mean(x * x, axis=-1, keepdims=True) + EPS) * g


def _params(sem):
    return pltpu.CompilerParams(dimension_semantics=sem, vmem_limit_bytes=VMEM_LIMIT)


def _const_spec(shape):
    nd = len(shape)
    return pl.BlockSpec(shape, lambda *_: (0,) * nd, pipeline_mode=pl.Buffered(1))


def _rope_block(z, cos, sin_a, sin_b):
    return (z * cos + pltpu.roll(z, LANES - ROT_DIM // 2, axis=1) * sin_a
            + pltpu.roll(z, ROT_DIM // 2, axis=1) * sin_b)


def _inproj_kernel(x_ref, g_ref, w_ref, wcast_ref, cos_ref, sa_ref, sb_ref,
                   q_ref, k_ref, v_ref, u2_ref, h_ref, wg_ref, zu_ref):
    wg_ref[...] = wcast_ref[...].astype(bf16)
    for r in range(x_ref.shape[0] // ROW_SUB):
        rows = pl.ds(r * ROW_SUB, ROW_SUB)
        h = _rms_norm(x_ref[rows, :], g_ref[...]).astype(bf16)
        h_ref[rows, :] = h
        cos, sin_a, sin_b = cos_ref[rows, :], sa_ref[rows, :], sb_ref[rows, :]
        for half in range(2):
            zq = _dot(h, w_ref[:, half * 512:(half + 1) * 512])
            for b in range(4):
                blk = zq[:, b * LANES:(b + 1) * LANES]
                col = (half * 4 + b) * LANES
                q_ref[rows, col:col + LANES] = _rope_block(blk, cos, sin_a, sin_b).astype(bf16)
        zkv = _dot(h, w_ref[:, Q_W:QKV_W])
        k_ref[rows, :] = _rope_block(zkv[:, :KV_W], cos, sin_a, sin_b).astype(bf16)
        v_ref[rows, :] = zkv[:, KV_W:].astype(bf16)
        zu = _dot(h, w_ref[:, QKV_W:])
        chunk_rows = pl.ds(r * (ROW_SUB // CHUNK), ROW_SUB // CHUNK)
        for sg in range(SG):
            zu_ref[r, sg] = zu[:, sg * LANES:(sg + 1) * LANES]
            for s in range(CHUNK):
                col = sg * SG_IN + s * LANES
                u2_ref[chunk_rows, col:col + LANES] = (
                    zu_ref[r, sg, pl.ds(s, ROW_SUB // CHUNK, stride=CHUNK), :].astype(bf16))


def _inproj(x, gain, wmix, w_in, cos, sin_a, sin_b, tm=512):
    L = x.shape[0]
    steps = L // tm
    gate_w = w_in.shape[1] - IN_MIX_W
    slab = gate_w // steps
    assert IN_MIX_W % slab == 0 and slab % LANES == 0
    row = lambda w: pl.BlockSpec((tm, w), lambda i: (i, 0))
    return pl.pallas_call(
        _inproj_kernel,
        grid=(steps,),
        in_specs=[row(D_MODEL), _const_spec((1, D_MODEL)), _const_spec(wmix.shape),
                  pl.BlockSpec((D_MODEL, slab), lambda i: (0, IN_MIX_W // slab + i)),
                  row(LANES), row(LANES), row(LANES)],
        out_specs=[row(Q_W), row(KV_W), row(KV_W),
                   pl.BlockSpec((tm // CHUNK, SG * SG_IN), lambda i: (i, 0)), row(D_MODEL),
                   pl.BlockSpec((D_MODEL, slab), lambda i: (0, i))],
        out_shape=[jax.ShapeDtypeStruct((L, Q_W), bf16), jax.ShapeDtypeStruct((L, KV_W), bf16),
                   jax.ShapeDtypeStruct((L, KV_W), bf16),
                   jax.ShapeDtypeStruct((L // CHUNK, SG * SG_IN), bf16),
                   jax.ShapeDtypeStruct((L, D_MODEL), bf16),
                   jax.ShapeDtypeStruct((D_MODEL, gate_w), bf16)],
        scratch_shapes=[pltpu.VMEM((tm // ROW_SUB, SG, ROW_SUB, LANES), f32)],
        compiler_params=_params(("parallel",)),
        name="inproj",
    )(x, gain, wmix, w_in, cos, sin_a, sin_b)


def _gates_kernel(h_ref, w_ref, *rest):
    n_cast = (len(rest) - 1) // 2
    o_ref = rest[n_cast]
    for r in range(h_ref.shape[0] // ROW_SUB):
        rows = pl.ds(r * ROW_SUB, ROW_SUB)
        o_ref[rows, :] = _sigmoid(_dot(h_ref[rows, :], w_ref[...])).astype(bf16)
    for src, dst in zip(rest[:n_cast], rest[n_cast + 1:]):
        dst[...] = src[...].astype(bf16)


def _gates(h, wg, w_up, row_cast_weights, tm=1024, tn=2048):
    L, N = h.shape[0], wg.shape[1]
    ni = L // tm
    steps = ni * (N // tn)
    step = lambda j, i: j * ni + i
    slabs = [pl.BlockSpec((w_up.shape[0], w_up.shape[1] // steps), lambda j, i: (0, step(j, i)))]
    slabs += [pl.BlockSpec((w.shape[0] // steps, w.shape[1]), lambda j, i: (step(j, i), 0))
              for w in row_cast_weights]
    cast_weights = [w_up] + list(row_cast_weights)
    outs = pl.pallas_call(
        _gates_kernel,
        grid=(N // tn, ni),
        in_specs=[pl.BlockSpec((tm, D_MODEL), lambda j, i: (i, 0)),
                  pl.BlockSpec((D_MODEL, tn), lambda j, i: (0, j))] + slabs,
        out_specs=[pl.BlockSpec((tm, tn), lambda j, i: (i, j))] + slabs,
        out_shape=[jax.ShapeDtypeStruct((L, N), bf16)]
                  + [jax.ShapeDtypeStruct(w.shape, bf16) for w in cast_weights],
        compiler_params=_params(("parallel", "arbitrary")),
        name="gates",
    )(h, wg, *cast_weights)
    return outs[0], outs[1], outs[2:]


GC_SHIFT = SSM_GC.bit_length() - 1
P_SHIFT = SSM_P.bit_length() - 1
Y_TILE = 512
NT = SG_STATE // LANES
NSEG = 8
SEG = N_CHUNKS // NSEG
SEG_PITCH = SEG + 8


def _ssm_kernel(u_ref, pre_ref, pim_ref, kt_ref, qc_ref, rep_ref, ar_ref, ai_ref, y_ref,
                p_scr, m_scr, q_scr, s_scr, xl_scr, pw_scr, xs_scr):
    half = SG_STATE // 2
    row = lax.broadcasted_iota(jnp.int32, (SG_IN, LANES), 0)
    lane = lax.broadcasted_iota(jnp.int32, (SG_IN, LANES), 1)
    row_grp = (row >> GC_SHIFT) & (GPS - 1)
    pre, pim = pre_ref[...], pim_ref[...]
    for k in range(half // LANES):
        sel = row_grp == 2 * k + (lane >> P_SHIFT)
        p_scr[:, k * LANES:(k + 1) * LANES] = jnp.where(sel, pre, 0.0).astype(bf16)
        p_scr[:, half + k * LANES:half + (k + 1) * LANES] = jnp.where(sel, pim, 0.0).astype(bf16)

    blk_row = lax.broadcasted_iota(jnp.int32, (LANES, LANES), 0) >> GC_SHIFT
    blk_lane = lax.broadcasted_iota(jnp.int32, (LANES, LANES), 1) >> GC_SHIFT
    zero_blk = jnp.zeros((LANES, LANES), bf16)
    for tau in range(CHUNK):
        blk = jnp.where(blk_row == blk_lane, kt_ref[tau * LANES:(tau + 1) * LANES, :], 0.0).astype(bf16)
        for s in range(CHUNK - tau):
            t = s + tau
            m_scr[s * LANES:(s + 1) * LANES, t * LANES:(t + 1) * LANES] = blk
            if tau > 0:
                m_scr[t * LANES:(t + 1) * LANES, s * LANES:(s + 1) * LANES] = zero_blk

    q_row_grp = (lax.broadcasted_iota(jnp.int32, (SG_STATE, Y_TILE), 0) >> P_SHIFT) & (GPS - 1)
    q_lane_grp = (lax.broadcasted_iota(jnp.int32, (SG_STATE, Y_TILE), 1) >> GC_SHIFT) & (GPS - 1)
    for j in range(SG_IN // Y_TILE):
        full = _dot(qc_ref[...], rep_ref[:, j * Y_TILE:(j + 1) * Y_TILE])
        q_scr[:, j * Y_TILE:(j + 1) * Y_TILE] = jnp.where(q_row_grp == q_lane_grp, full, 0.0).astype(bf16)

    s = _dot(u_ref[...], p_scr[...])
    for j in range(NT):
        for k in range(NSEG):
            s_scr[j, k * SEG_PITCH:k * SEG_PITCH + SEG, :] = s[k * SEG:(k + 1) * SEG, j * LANES:(j + 1) * LANES]
    ht = NT // 2
    cmul = lambda xr, xi, yr, yi: (xr * yr - xi * yi, xr * yi + xi * yr)
    ar = [ar_ref[:, j * LANES:(j + 1) * LANES] for j in range(ht)]
    ai = [ai_ref[:, j * LANES:(j + 1) * LANES] for j in range(ht)]
    one, nil = jnp.ones((1, LANES), f32), jnp.zeros((1, LANES), f32)
    a_seg = []
    for j in range(ht):
        rr, ri = [one], [nil]
        for _ in range(7):
            nr, ni = cmul(rr[-1], ri[-1], ar[j], ai[j])
            rr.append(nr)
            ri.append(ni)
        a8r, a8i = cmul(rr[-1], ri[-1], ar[j], ai[j])
        blk_r, blk_i = jnp.concatenate(rr, axis=0), jnp.concatenate(ri, axis=0)
        for b in range(SEG // 8):
            pw_scr[j, b * 8:(b + 1) * 8, :] = blk_r
            pw_scr[ht + j, b * 8:(b + 1) * 8, :] = blk_i
            blk_r, blk_i = cmul(blk_r, blk_i, a8r, a8i)
        a_seg.append((blk_r[0:1], blk_i[0:1]))

    ar_b = [jnp.broadcast_to(a, (NSEG, LANES)) for a in ar]
    ai_b = [jnp.broadcast_to(a, (NSEG, LANES)) for a in ai]

    def body(i, carry):
        new_r, new_i = [], []
        for j in range(ht):
            xr, xi = carry[j], carry[ht + j]
            seg_rows = pl.ds(i, NSEG, stride=SEG_PITCH)
            xl_scr[j, seg_rows, :] = xr
            xl_scr[ht + j, seg_rows, :] = xi
            pr, pi = cmul(xr, xi, ar_b[j], ai_b[j])
            new_r.append(pr + s_scr[j, seg_rows, :])
            new_i.append(pi + s_scr[ht + j, seg_rows, :])
        return tuple(new_r + new_i)

    ends = tuple(jnp.zeros((NSEG, LANES), f32) for _ in range(NT))
    for i in range(SEG):
        ends = body(i, ends)

    for j in range(ht):
        zr, zi = nil, nil
        for k in range(NSEG):
            rows = slice(k * SEG, (k + 1) * SEG)
            loc = slice(k * SEG_PITCH, k * SEG_PITCH + SEG)
            fr, fi = cmul(pw_scr[j], pw_scr[ht + j], zr, zi)
            xs_scr[rows, j * LANES:(j + 1) * LANES] = (xl_scr[j, loc, :] + fr).astype(bf16)
            xs_scr[rows, half + j * LANES:half + (j + 1) * LANES] = (xl_scr[ht + j, loc, :] + fi).astype(bf16)
            pr, pi = cmul(zr, zi, *a_seg[j])
            zr, zi = pr + ends[j][k:k + 1], pi + ends[ht + j][k:k + 1]

    xs = xs_scr[...]
    for j in range(SG_IN // Y_TILE):
        kk = (j + 1) * Y_TILE
        cols = slice(j * Y_TILE, (j + 1) * Y_TILE)
        y = _dot(u_ref[:, :kk], m_scr[:kk, cols]) + _dot(xs, q_scr[:, cols])
        y_ref[:, cols] = _gelu_tanh(y).astype(bf16)


def _ssm(u2, pre, pim, kt, qc, rep, at_re, at_im):
    per_sg = lambda r, c: pl.BlockSpec((None, r, c), lambda g: (g, 0, 0))
    return pl.pallas_call(
        _ssm_kernel,
        grid=(SG,),
        in_specs=[pl.BlockSpec((N_CHUNKS, SG_IN), lambda g: (0, g)),
                  per_sg(SG_IN, LANES), per_sg(SG_IN, LANES), per_sg(SG_IN, LANES),
                  per_sg(SG_STATE, CHUNK * SSM_GC), _const_spec(rep.shape),
                  per_sg(1, SG_STATE // 2), per_sg(1, SG_STATE // 2)],
        out_specs=pl.BlockSpec((N_CHUNKS, SG_IN), lambda g: (0, g)),
        out_shape=jax.ShapeDtypeStruct((N_CHUNKS, SG * SG_IN), bf16),
        scratch_shapes=[pltpu.VMEM((SG_IN, SG_STATE), bf16), pltpu.VMEM((SG_IN, SG_IN), bf16),
                        pltpu.VMEM((SG_STATE, SG_IN), bf16),
                        pltpu.VMEM((NT, NSEG * SEG_PITCH, LANES), f32),
                        pltpu.VMEM((NT, NSEG * SEG_PITCH, LANES), f32),
                        pltpu.VMEM((NT, SEG, LANES), f32), pltpu.VMEM((N_CHUNKS, SG_STATE), bf16)],
        compiler_params=_params(("arbitrary",)),
        name="ssm",
    )(u2, pre, pim, kt, qc, rep, at_re, at_im)


def _ssm_prep_kernel(lam_re_ref, lam_im_ref, ldt_ref, bt_re_ref, bt_im_ref, yc_ref, d_ref, eye_ref, win_ref,
                     pre_ref, pim_ref, kt_ref, qc_ref, at_re_ref, at_im_ref, wmix_ref):
    wmix_ref[...] = win_ref[...].astype(bf16)
    lr, li = lam_re_ref[...], lam_im_ref[...]
    dt = jnp.exp(ldt_ref[...])
    mag = jnp.exp(lr * dt)
    a_re, a_im = mag * jnp.cos(li * dt), mag * jnp.sin(li * dt)
    den = lr * lr + li * li
    nr, ni = a_re - 1.0, a_im
    coef_re = (nr * lr + ni * li) / den
    coef_im = (ni * lr - nr * li) / den
    pw_re, pw_im = [jnp.ones_like(a_re)], [jnp.zeros_like(a_im)]
    for _ in range(CHUNK):
        pr, pi = pw_re[-1], pw_im[-1]
        pw_re.append(pr * a_re - pi * a_im)
        pw_im.append(pr * a_im + pi * a_re)

    low = lax.broadcasted_iota(jnp.int32, (SSM_GC, LANES), 1) < SSM_P
    diag = ((lax.broadcasted_iota(jnp.int32, (SSM_GC, LANES), 1) & (SSM_GC - 1))
            == lax.broadcasted_iota(jnp.int32, (SSM_GC, LANES), 0))
    contract_lanes = (((1,), (1,)), ((), ()))
    for g in range(GPS):
        grp = slice(g * SSM_GC, (g + 1) * SSM_GC)
        one = lambda x: x[g:g + 1]
        bb_re = one(coef_re) * bt_re_ref[grp, :] - one(coef_im) * bt_im_ref[grp, :]
        bb_im = one(coef_re) * bt_im_ref[grp, :] + one(coef_im) * bt_re_ref[grp, :]
        yc = yc_ref[grp, :]
        yc_sw = pltpu.roll(yc, SSM_P, axis=1)
        y2 = jnp.where(low, yc_sw, -yc_sw)
        x_rows, ca_rows = [], []
        for tau in range(CHUNK):
            pr, pi = one(pw_re[tau]), one(pw_im[tau])
            ab_re = bb_re * pr - bb_im * pi
            ab_im = bb_re * pi + bb_im * pr
            s = CHUNK - 1 - tau
            rows = slice(s * LANES + g * SSM_GC, s * LANES + (g + 1) * SSM_GC)
            pre_ref[rows, :] = ab_re
            pim_ref[rows, :] = ab_im
            x_rows.append(jnp.where(low, ab_re, ab_im))
            ca_rows.append(yc * one(pw_re[tau + 1]) + y2 * one(pw_im[tau + 1]))
        kern = lax.dot_general(jnp.concatenate(x_rows, axis=0), jnp.concatenate([yc] * GPS, axis=0),
                               contract_lanes, precision=lax.Precision.HIGHEST,
                               preferred_element_type=f32)
        for tau in range(CHUNK):
            blk = kern[tau * SSM_GC:(tau + 1) * SSM_GC]
            if tau == 0:
                blk = blk + jnp.where(diag, d_ref[...], 0.0)
            kt_ref[tau * LANES + g * SSM_GC:tau * LANES + (g + 1) * SSM_GC, :] = blk
        qt = lax.dot_general(eye_ref[...], jnp.concatenate(ca_rows, axis=0).astype(bf16),
                             contract_lanes, preferred_element_type=f32)
        qc_ref[g * SSM_P:(g + 1) * SSM_P, :] = qt[:SSM_P].astype(bf16)
        qc_ref[GPS * SSM_P + g * SSM_P:GPS * SSM_P + (g + 1) * SSM_P, :] = qt[SSM_P:].astype(bf16)
    lo_lane = lax.broadcasted_iota(jnp.int32, (1, LANES), 1) < SSM_P
    for k in range(GPS // 2):
        sel = lambda pw: jnp.where(lo_lane, pw[2 * k:2 * k + 1], pw[2 * k + 1:2 * k + 2])
        at_re_ref[:, k * LANES:(k + 1) * LANES] = sel(pw_re[CHUNK])
        at_im_ref[:, k * LANES:(k + 1) * LANES] = sel(pw_im[CHUNK])


def _ssm_operators(lam_re, lam_im, log_dt, b_re, b_im, c_re, c_im, d_skip, w_in):
    dup = lambda x: jnp.concatenate([x, x], axis=-1)
    per_sg = lambda x: x.astype(f32).reshape((SG, x.size // (SG * x.shape[-1]), x.shape[-1]))
    bt = lambda b: per_sg(dup(b.astype(f32).swapaxes(1, 2)))
    yc = per_sg(jnp.concatenate([c_re.astype(f32), -c_im.astype(f32)], axis=-1))
    args = (per_sg(dup(lam_re)), per_sg(dup(lam_im)), log_dt.astype(f32).reshape(SG, GPS, 1),
            bt(b_re), bt(b_im), yc, d_skip.astype(f32).reshape(SG, 1, LANES), jnp.eye(LANES, dtype=bf16),
            w_in)
    blk = lambda r, c: pl.BlockSpec((None, r, c), lambda g: (g, 0, 0))
    slab = pl.BlockSpec((w_in.shape[0] // SG, IN_MIX_W), lambda g: (g, 0))
    return pl.pallas_call(
        _ssm_prep_kernel,
        grid=(SG,),
        in_specs=[blk(GPS, LANES), blk(GPS, LANES), blk(GPS, 1), blk(LANES, LANES), blk(LANES, LANES),
                  blk(LANES, LANES), blk(1, LANES), _const_spec((LANES, LANES)), slab],
        out_specs=[blk(SG_IN, LANES), blk(SG_IN, LANES), blk(SG_IN, LANES),
                   blk(SG_STATE, CHUNK * SSM_GC), blk(1, SG_STATE // 2), blk(1, SG_STATE // 2),
                   slab],
        out_shape=[jax.ShapeDtypeStruct((SG, SG_IN, LANES), f32)] * 3
                  + [jax.ShapeDtypeStruct((SG, SG_STATE, CHUNK * SSM_GC), bf16)]
                  + [jax.ShapeDtypeStruct((SG, 1, SG_STATE // 2), f32)] * 2
                  + [jax.ShapeDtypeStruct((w_in.shape[0], IN_MIX_W), bf16)],
        compiler_params=_params(("parallel",)),
        name="ssm_prep",
    )(*args)


def _replication_matrix():
    src = jnp.arange(CHUNK * SSM_GC)
    dst = jnp.arange(SG_IN)
    same_t = (src[:, None] // SSM_GC) == (dst[None, :] // LANES)
    same_c = (src[:, None] % SSM_GC) == (dst[None, :] % SSM_GC)
    return (same_t & same_c).astype(bf16)


def _kv_operands(kp_ref, kc_ref, vp_ref, vc_ref):
    kcat = jnp.concatenate([kp_ref[...], kc_ref[...]], axis=0).astype(f32)
    vcat = jnp.concatenate([vp_ref[...], vc_ref[...]], axis=0).astype(f32)
    k_sw = pltpu.roll(kcat, HEAD_DIM, axis=1)
    v_sw = pltpu.roll(vcat, HEAD_DIM, axis=1)
    low_kv = lax.broadcasted_iota(jnp.int32, kcat.shape, 1) < HEAD_DIM
    operands = []
    for hk in range(N_KV_HEADS):
        k_src, k_oth = (kcat, k_sw) if hk == 0 else (k_sw, kcat)
        k_lo = jnp.where(low_kv, k_src, 0.0).astype(bf16)
        k_hi = jnp.where(low_kv, 0.0, k_oth).astype(bf16)
        vv = (jnp.where(low_kv, vcat, v_sw) if hk == 0 else jnp.where(low_kv, v_sw, vcat)).astype(bf16)
        operands.append((k_lo, k_hi, vv))
    return operands


def _attention_block(q, kv_operands, sinks_ref, blk, no_prev, oa_ref):
    q_rows = slice(blk * BLOCK, (blk + 1) * BLOCK)
    kv_rows = slice(blk * BLOCK, (blk + 2) * BLOCK)
    from_prev = (lax.broadcasted_iota(jnp.int32, (BLOCK, BLOCK), 1)
                 > lax.broadcasted_iota(jnp.int32, (BLOCK, BLOCK), 0))
    low_o = lax.broadcasted_iota(jnp.int32, (BLOCK, LANES), 1) < HEAD_DIM
    contract_lanes = (((1,), (1,)), ((), ()))
    pairs_per_kv = (N_Q_HEADS // N_KV_HEADS) // 2
    for hk, (k_lo, k_hi, vv) in enumerate(kv_operands):
        qs = jnp.concatenate(
            [q[q_rows, (pairs_per_kv * hk + b) * LANES:(pairs_per_kv * hk + b + 1) * LANES]
             for b in range(pairs_per_kv)], axis=0)
        s_par = (lax.dot_general(qs, k_lo[kv_rows], contract_lanes, preferred_element_type=f32),
                 lax.dot_general(qs, k_hi[kv_rows], contract_lanes, preferred_element_type=f32))
        for b in range(pairs_per_kv):
            outs = []
            for par in range(2):
                head = 2 * (pairs_per_kv * hk + b) + par
                s_all = s_par[par][b * BLOCK:(b + 1) * BLOCK]
                s_prev = s_all[:, :BLOCK] + no_prev if blk == 0 else s_all[:, :BLOCK]
                s = jnp.where(from_prev, s_prev, s_all[:, BLOCK:])
                sink = sinks_ref[head]
                m = jnp.maximum(jnp.max(s, axis=1, keepdims=True), sink)
                p = jnp.exp(s - m)
                denom = jnp.sum(p, axis=1, keepdims=True) + jnp.exp(sink - m)
                p_kv = jnp.concatenate([jnp.where(from_prev, p, 0.0), jnp.where(from_prev, 0.0, p)],
                                       axis=1).astype(bf16)
                outs.append(_dot(p_kv, vv[kv_rows]) * (1.0 / denom))
            col = (pairs_per_kv * hk + b) * LANES
            oa_ref[q_rows, col:col + LANES] = jnp.where(low_o, outs[0], outs[1]).astype(bf16)


def _attn_mix_kernel(sinks_ref, q_ref, kp_ref, kc_ref, vp_ref, vc_ref, y2_ref, ga_ref, gs_ref,
                     wglu_ref, wb_ref, wdn_ref, m_ref, wdn_o_ref, oa_ref, yb_ref):
    wdn_o_ref[...] = wdn_ref[...].astype(bf16)
    no_prev = jnp.where(pl.program_id(0) > 0, 0.0, MASKED)
    q = q_ref[...] * jnp.asarray(1.0 / math.sqrt(HEAD_DIM), bf16)
    kv_operands = _kv_operands(kp_ref, kc_ref, vp_ref, vc_ref)
    for r in range(q_ref.shape[0] // ROW_SUB):
        for blk in range(r * (ROW_SUB // BLOCK), (r + 1) * (ROW_SUB // BLOCK)):
            _attention_block(q, kv_operands, sinks_ref, blk, no_prev, oa_ref)
        rows = pl.ds(r * ROW_SUB, ROW_SUB)
        chunk_rows = pl.ds(r * (ROW_SUB // CHUNK), ROW_SUB // CHUNK)
        for sg in range(SG):
            for t in range(CHUNK):
                col = sg * SG_IN + t * LANES
                yb_ref[r, sg, pl.ds(t, ROW_SUB // CHUNK, stride=CHUNK), :] = (
                    y2_ref[chunk_rows, col:col + LANES].astype(f32))
        y = jnp.concatenate([yb_ref[r, sg] for sg in range(SG)], axis=1).astype(bf16)
        zg = _dot(y, wglu_ref[...])
        o_ssm = zg[:, :SSM_W] * _sigmoid(zg[:, SSM_W:])
        y_s = _dot(o_ssm.astype(bf16), wb_ref[Q_W:, :])
        y_a = _dot(oa_ref[rows, :], wb_ref[:Q_W, :])
        m_ref[rows, :] = (ga_ref[rows, :].astype(f32) * y_a + gs_ref[rows, :].astype(f32) * y_s).astype(bf16)


def _attn_mix(q, k, v, sinks, y2, gates, wglu, wb, w_down, tm=512):
    L = q.shape[0]
    cur = lambda i: (i, 0)
    prev = lambda i: (jnp.maximum(i * (tm // BLOCK) - 1, 0), 0)
    dn_slab = pl.BlockSpec((w_down.shape[0] // (L // tm), w_down.shape[1]), cur)
    return pl.pallas_call(
        _attn_mix_kernel,
        grid=(L // tm,),
        in_specs=[pl.BlockSpec(memory_space=pltpu.SMEM),
                  pl.BlockSpec((tm, Q_W), cur),
                  pl.BlockSpec((BLOCK, KV_W), prev), pl.BlockSpec((tm, KV_W), cur),
                  pl.BlockSpec((BLOCK, KV_W), prev), pl.BlockSpec((tm, KV_W), cur),
                  pl.BlockSpec((tm // CHUNK, SG * SG_IN), cur),
                  pl.BlockSpec((tm, D_MODEL), cur),
                  pl.BlockSpec((tm, D_MODEL), lambda i: (i, 1)),
                  _const_spec(wglu.shape), _const_spec(wb.shape), dn_slab],
        out_specs=[pl.BlockSpec((tm, D_MODEL), cur), dn_slab],
        out_shape=[jax.ShapeDtypeStruct((L, D_MODEL), bf16), jax.ShapeDtypeStruct(w_down.shape, bf16)],
        scratch_shapes=[pltpu.VMEM((tm, Q_W), bf16),
                        pltpu.VMEM((tm // ROW_SUB, SG, ROW_SUB, LANES), f32)],
        compiler_params=_params(("parallel",)),
        name="attn_mix",
    )(sinks, q, k, k, v, v, y2, gates, gates, wglu, wb, w_down)


def _outproj_kernel(m_ref, w_ref, x_ref, gpost_ref, gpre_ref, x1_ref, h2_ref):
    for r in range(m_ref.shape[0] // OUT_SUB):
        rows = pl.ds(r * OUT_SUB, OUT_SUB)
        out = _dot(m_ref[rows, :], w_ref[...])
        x1 = x_ref[rows, :] + _rms_norm(out, gpost_ref[...])
        x1_ref[rows, :] = x1
        h2_ref[rows, :] = _rms_norm(x1, gpre_ref[...]).astype(bf16)


def _outproj(m, w_out, x, g_post, g_pre, tm=512):
    L = x.shape[0]
    row = lambda: pl.BlockSpec((tm, D_MODEL), lambda i: (i, 0))
    return pl.pallas_call(
        _outproj_kernel,
        grid=(L // tm,),
        in_specs=[row(), _const_spec(w_out.shape), row(),
                  _const_spec((1, D_MODEL)), _const_spec((1, D_MODEL))],
        out_specs=[row(), row()],
        out_shape=[jax.ShapeDtypeStruct((L, D_MODEL), f32), jax.ShapeDtypeStruct((L, D_MODEL), bf16)],
        compiler_params=_params(("parallel",)),
        name="outproj",
    )(m, w_out, x, g_post, g_pre)


MLP_SUB = 512


def _mlp_kernel(h_ref, wu_ref, wd_ref, x_ref, g_ref, o_ref, acc_ref):
    i, j = pl.program_id(0), pl.program_id(1)
    n_tiles = pl.num_programs(0) - 1
    tm = h_ref.shape[0]
    slab = o_ref.shape[0]
    cur, prev = acc_ref.at[i % 2], acc_ref.at[(i + 1) % 2]
    slab_rows = pl.ds(pl.multiple_of(j * slab, slab), slab)

    def finish_prev_slab():
        o_ref[...] = x_ref[...] + _rms_norm(prev[slab_rows, :], g_ref[...])
        prev[slab_rows, :] = jnp.zeros((slab, D_MODEL), f32)

    @pl.when((i == 0) & (j == 0))
    def _():
        acc_ref[...] = jnp.zeros_like(acc_ref)

    @pl.when(i < n_tiles)
    def _():
        for r in range(tm // MLP_SUB):
            rows = pl.ds(r * MLP_SUB, MLP_SUB)
            a = jnp.maximum(_dot(h_ref[rows, :], wu_ref[...]), 0.0)
            cur[rows, :] += _dot((a * a).astype(bf16), wd_ref[...])
        finish_prev_slab()

    @pl.when(i == n_tiles)
    def _():
        finish_prev_slab()


def _mlp(h2, w_up, w_down, x1, g_post, tm=1024, tf=1024):
    L = x1.shape[0]
    n_tiles, n_ff = L // tm, D_FF // tf
    slab = tm // n_ff
    ff = lambda i, j: jnp.where(i < n_tiles, j, n_ff - 1)
    prev_slab = lambda i, j: (jnp.maximum(i - 1, 0) * n_ff + jnp.where(i > 0, j, 0), 0)
    return pl.pallas_call(
        _mlp_kernel,
        grid=(n_tiles + 1, n_ff),
        in_specs=[pl.BlockSpec((tm, D_MODEL), lambda i, j: (jnp.minimum(i, n_tiles - 1), 0)),
                  pl.BlockSpec((D_MODEL, tf), lambda i, j: (0, ff(i, j))),
                  pl.BlockSpec((tf, D_MODEL), lambda i, j: (ff(i, j), 0)),
                  pl.BlockSpec((slab, D_MODEL), prev_slab),
                  _const_spec((1, D_MODEL))],
        out_specs=pl.BlockSpec((slab, D_MODEL), prev_slab),
        out_shape=jax.ShapeDtypeStruct((L, D_MODEL), f32),
        scratch_shapes=[pltpu.VMEM((2, tm, D_MODEL), f32)],
        compiler_params=_params(("arbitrary", "arbitrary")),
        name="mlp",
    )(h2, w_up, w_down, x1, g_post)


def _rope_tables(L):
    half = ROT_DIM // 2
    inv = ROPE_THETA ** (-np.arange(half, dtype=np.float64) * 2.0 / ROT_DIM)
    ang = np.arange(L, dtype=np.float64)[:, None] * inv[None, :]
    cos, sin = np.cos(ang).astype(np.float32), np.sin(ang).astype(np.float32)
    ones = np.ones((L, HEAD_DIM - ROT_DIM), np.float32)
    zeros = np.zeros((L, HEAD_DIM - ROT_DIM), np.float32)
    zh = np.zeros((L, half), np.float32)
    per_head = lambda parts: jnp.asarray(np.tile(np.concatenate(parts, axis=1), (1, LANES // HEAD_DIM)))
    return (per_head([cos, cos, ones]), per_head([-sin, zh, zeros]), per_head([zh, sin, zeros]))


def _layer(x, norm_mix_pre, norm_mix_post, norm_mlp_pre, norm_mlp_post, w_in, sinks,
           lam_re, lam_im, log_dt, b_re, b_im, c_re, c_im, d_skip, w_glu, w_branch, w_out,
           w_up, w_down, rope):
    gain = lambda g: g.astype(f32).reshape(1, D_MODEL)
    pre, pim, kt, qc, at_re, at_im, wmix = _ssm_operators(
        lam_re, lam_im, log_dt, b_re, b_im, c_re, c_im, d_skip, w_in.astype(f32))
    q, k, v, u2, h, wg = _inproj(x, gain(norm_mix_pre), wmix, w_in.astype(f32), *rope)
    gates, wup, (wglu, wb, wo) = _gates(h, wg, w_up.astype(f32),
                                        [w_glu.astype(f32), w_branch.astype(f32), w_out.astype(f32)])
    y2 = _ssm(u2, pre, pim, kt, qc, _replication_matrix(), at_re, at_im)
    mixed, wdn = _attn_mix(q, k, v, sinks.astype(f32), y2, gates, wglu, wb, w_down.astype(f32))
    x1, h2 = _outproj(mixed, wo, x, gain(norm_mix_post), gain(norm_mlp_pre))
    return _mlp(h2, wup, wdn, x1, gain(norm_mlp_post))


def kernel(x, norm_mix_pre, norm_mix_post, norm_mlp_pre, norm_mlp_post, w_in, sinks, lam_re, lam_im, log_dt, b_re, b_im, c_re, c_im, d_skip, w_glu, w_branch, w_out, w_up, w_down):
    B, L, _ = x.shape
    depth = w_in.shape[0]
    rope = _rope_tables(L)
    outs = []
    for b in range(B):
        xb = x[b]
        for l in range(depth):
            xb = _layer(xb, norm_mix_pre[l], norm_mix_post[l], norm_mlp_pre[l], norm_mlp_post[l],
                        w_in[l], sinks[l], lam_re[l], lam_im[l], log_dt[l], b_re[l], b_im[l],
                        c_re[l], c_im[l], d_skip[l], w_glu[l], w_branch[l], w_out[l],
                        w_up[l], w_down[l], rope)
        outs.append(xb)
    return jnp.stack(outs)
```

```python
import math

import jax
import jax.numpy as jnp
import numpy as np
from jax import lax
from jax.experimental import pallas as pl
from jax.experimental.pallas import tpu as pltpu

D_MODEL = 2048
SEQ = 8192
HEAD_DIM = 64
N_Q_HEADS = 16
N_KV_HEADS = 2
BLOCK = 128
ROT_DIM = HEAD_DIM // 4
ROPE_THETA = 500000.0
Q_W = N_Q_HEADS * HEAD_DIM
KV_W = N_KV_HEADS * HEAD_DIM
SSM_W = D_MODEL // 2
QKV_W = Q_W + 2 * KV_W
IN_MIX_W = QKV_W + SSM_W
SSM_GC = 16
SSM_G = SSM_W // SSM_GC
SSM_P = 64
D_FF = 4 * D_MODEL
EPS = 1e-6

LANES = 128
CHUNK = 16
N_CHUNKS = SEQ // CHUNK
SG = SSM_W // LANES
GPS = LANES // SSM_GC
SG_IN = CHUNK * LANES
SG_STATE = 2 * GPS * SSM_P
VMEM_LIMIT = 56 * 1024 * 1024
ROW_SUB = 256
OUT_SUB = 128
MASKED = float(jnp.finfo(jnp.float32).min)

bf16 = jnp.bfloat16
f32 = jnp.float32


def _dot(a, b):
    return jnp.dot(a, b, preferred_element_type=f32)


def _sigmoid(x):
    return 1.0 / (1.0 + jnp.exp(-x))


def _gelu_tanh(x):
    c = math.sqrt(2.0 / math.pi)
    return x * (0.5 * (1.0 + jnp.tanh(c * (x + 0.044715 * (x * x * x)))))


def _rms_norm(x, g):
    return x * lax.rsqrt(jnp.mean(x * x, axis=-1, keepdims=True) + EPS) * g


def _params(sem):
    return pltpu.CompilerParams(dimension_semantics=sem, vmem_limit_bytes=VMEM_LIMIT)


def _const_spec(shape):
    nd = len(shape)
    return pl.BlockSpec(shape, lambda *_: (0,) * nd, pipeline_mode=pl.Buffered(1))


def _rope_block(z, cos, sin_a, sin_b):
    return (z * cos + pltpu.roll(z, LANES - ROT_DIM // 2, axis=1) * sin_a
            + pltpu.roll(z, ROT_DIM // 2, axis=1) * sin_b)


def _inproj_kernel(x_ref, g_ref, w_ref, wcast_ref, cos_ref, sa_ref, sb_ref,
                   q_ref, k_ref, v_ref, u2_ref, h_ref, wg_ref, zu_ref):
    wg_ref[...] = wcast_ref[...].astype(bf16)
    for r in range(x_ref.shape[0] // ROW_SUB):
        rows = pl.ds(r * ROW_SUB, ROW_SUB)
        h = _rms_norm(x_ref[rows, :], g_ref[...]).astype(bf16)
        h_ref[rows, :] = h
        cos, sin_a, sin_b = cos_ref[rows, :], sa_ref[rows, :], sb_ref[rows, :]
        for half in range(2):
            zq = _dot(h, w_ref[:, half * 512:(half + 1) * 512])
            for b in range(4):
                blk = zq[:, b * LANES:(b + 1) * LANES]
                col = (half * 4 + b) * LANES
                q_ref[rows, col:col + LANES] = _rope_block(blk, cos, sin_a, sin_b).astype(bf16)
        zkv = _dot(h, w_ref[:, Q_W:QKV_W])
        k_ref[rows, :] = _rope_block(zkv[:, :KV_W], cos, sin_a, sin_b).astype(bf16)
        v_ref[rows, :] = zkv[:, KV_W:].astype(bf16)
        zu = _dot(h, w_ref[:, QKV_W:])
        chunk_rows = pl.ds(r * (ROW_SUB // CHUNK), ROW_SUB // CHUNK)
        for sg in range(SG):
            zu_ref[r, sg] = zu[:, sg * LANES:(sg + 1) * LANES]
            for s in range(CHUNK):
                col = sg * SG_IN + s * LANES
                u2_ref[chunk_rows, col:col + LANES] = (
                    zu_ref[r, sg, pl.ds(s, ROW_SUB // CHUNK, stride=CHUNK), :].astype(bf16))


def _inproj(x, gain, wmix, w_in, cos, sin_a, sin_b, tm=512):
    L = x.shape[0]
    steps = L // tm
    gate_w = w_in.shape[1] - IN_MIX_W
    slab = gate_w // steps
    assert IN_MIX_W % slab == 0 and slab % LANES == 0
    row = lambda w: pl.BlockSpec((tm, w), lambda i: (i, 0))
    return pl.pallas_call(
        _inproj_kernel,
        grid=(steps,),
        in_specs=[row(D_MODEL), _const_spec((1, D_MODEL)), _const_spec(wmix.shape),
                  pl.BlockSpec((D_MODEL, slab), lambda i: (0, IN_MIX_W // slab + i)),
                  row(LANES), row(LANES), row(LANES)],
        out_specs=[row(Q_W), row(KV_W), row(KV_W),
                   pl.BlockSpec((tm // CHUNK, SG * SG_IN), lambda i: (i, 0)), row(D_MODEL),
                   pl.BlockSpec((D_MODEL, slab), lambda i: (0, i))],
        out_shape=[jax.ShapeDtypeStruct((L, Q_W), bf16), jax.ShapeDtypeStruct((L, KV_W), bf16),
                   jax.ShapeDtypeStruct((L, KV_W), bf16),
                   jax.ShapeDtypeStruct((L // CHUNK, SG * SG_IN), bf16),
                   jax.ShapeDtypeStruct((L, D_MODEL), bf16),
                   jax.ShapeDtypeStruct((D_MODEL, gate_w), bf16)],
        scratch_shapes=[pltpu.VMEM((tm // ROW_SUB, SG, ROW_SUB, LANES), f32)],
        compiler_params=_params(("parallel",)),
        name="inproj",
    )(x, gain, wmix, w_in, cos, sin_a, sin_b)


def _gates_kernel(h_ref, w_ref, *rest):
    n_cast = (len(rest) - 1) // 2
    o_ref = rest[n_cast]
    for r in range(h_ref.shape[0] // ROW_SUB):
        rows = pl.ds(r * ROW_SUB, ROW_SUB)
        o_ref[rows, :] = _sigmoid(_dot(h_ref[rows, :], w_ref[...])).astype(bf16)
    for src, dst in zip(rest[:n_cast], rest[n_cast + 1:]):
        dst[...] = src[...].astype(bf16)


def _gates(h, wg, w_up, row_cast_weights, tm=1024, tn=2048):
    L, N = h.shape[0], wg.shape[1]
    ni = L // tm
    steps = ni * (N // tn)
    step = lambda j, i: j * ni + i
    slabs = [pl.BlockSpec((w_up.shape[0], w_up.shape[1] // steps), lambda j, i: (0, step(j, i)))]
    slabs += [pl.BlockSpec((w.shape[0] // steps, w.shape[1]), lambda j, i: (step(j, i), 0))
              for w in row_cast_weights]
    cast_weights = [w_up] + list(row_cast_weights)
    outs = pl.pallas_call(
        _gates_kernel,
        grid=(N // tn, ni),
        in_specs=[pl.BlockSpec((tm, D_MODEL), lambda j, i: (i, 0)),
                  pl.BlockSpec((D_MODEL, tn), lambda j, i: (0, j))] + slabs,
        out_specs=[pl.BlockSpec((tm, tn), lambda j, i: (i, j))] + slabs,
        out_shape=[jax.ShapeDtypeStruct((L, N), bf16)]
                  + [jax.ShapeDtypeStruct(w.shape, bf16) for w in cast_weights],
        compiler_params=_params(("parallel", "arbitrary")),
        name="gates",
    )(h, wg, *cast_weights)
    return outs[0], outs[1], outs[2:]


GC_SHIFT = SSM_GC.bit_length() - 1
P_SHIFT = SSM_P.bit_length() - 1
Y_TILE = 512
NT = SG_STATE // LANES
NSEG = 8
SEG = N_CHUNKS // NSEG
SEG_PITCH = SEG + 8


def _ssm_kernel(u_ref, pre_ref, pim_ref, kt_ref, qc_ref, rep_ref, ar_ref, ai_ref, y_ref,
                p_scr, m_scr, q_scr, s_scr, xl_scr, pw_scr, xs_scr):
    half = SG_STATE // 2
    row = lax.broadcasted_iota(jnp.int32, (SG_IN, LANES), 0)
    lane = lax.broadcasted_iota(jnp.int32, (SG_IN, LANES), 1)
    row_grp = (row >> GC_SHIFT) & (GPS - 1)
    pre, pim = pre_ref[...], pim_ref[...]
    for k in range(half // LANES):
        sel = row_grp == 2 * k + (lane >> P_SHIFT)
        p_scr[:, k * LANES:(k + 1) * LANES] = jnp.where(sel, pre, 0.0).astype(bf16)
        p_scr[:, half + k * LANES:half + (k + 1) * LANES] = jnp.where(sel, pim, 0.0).astype(bf16)

    blk_row = lax.broadcasted_iota(jnp.int32, (LANES, LANES), 0) >> GC_SHIFT
    blk_lane = lax.broadcasted_iota(jnp.int32, (LANES, LANES), 1) >> GC_SHIFT
    zero_blk = jnp.zeros((LANES, LANES), bf16)
    for tau in range(CHUNK):
        blk = jnp.where(blk_row == blk_lane, kt_ref[tau * LANES:(tau + 1) * LANES, :], 0.0).astype(bf16)
        for s in range(CHUNK - tau):
            t = s + tau
            m_scr[s * LANES:(s + 1) * LANES, t * LANES:(t + 1) * LANES] = blk
            if tau > 0:
                m_scr[t * LANES:(t + 1) * LANES, s * LANES:(s + 1) * LANES] = zero_blk

    q_row_grp = (lax.broadcasted_iota(jnp.int32, (SG_STATE, Y_TILE), 0) >> P_SHIFT) & (GPS - 1)
    q_lane_grp = (lax.broadcasted_iota(jnp.int32, (SG_STATE, Y_TILE), 1) >> GC_SHIFT) & (GPS - 1)
    for j in range(SG_IN // Y_TILE):
        full = _dot(qc_ref[...], rep_ref[:, j * Y_TILE:(j + 1) * Y_TILE])
        q_scr[:, j * Y_TILE:(j + 1) * Y_TILE] = jnp.where(q_row_grp == q_lane_grp, full, 0.0).astype(bf16)

    s = _dot(u_ref[...], p_scr[...])
    for j in range(NT):
        for k in range(NSEG):
            s_scr[j, k * SEG_PITCH:k * SEG_PITCH + SEG, :] = s[k * SEG:(k + 1) * SEG, j * LANES:(j + 1) * LANES]
    ht = NT // 2
    cmul = lambda xr, xi, yr, yi: (xr * yr - xi * yi, xr * yi + xi * yr)
    ar = [ar_ref[:, j * LANES:(j + 1) * LANES] for j in range(ht)]
    ai = [ai_ref[:, j * LANES:(j + 1) * LANES] for j in range(ht)]
    one, nil = jnp.ones((1, LANES), f32), jnp.zeros((1, LANES), f32)
    a_seg = []
    for j in range(ht):
        rr, ri = [one], [nil]
        for _ in range(7):
            nr, ni = cmul(rr[-1], ri[-1], ar[j], ai[j])
            rr.append(nr)
            ri.append(ni)
        a8r, a8i = cmul(rr[-1], ri[-1], ar[j], ai[j])
        blk_r, blk_i = jnp.concatenate(rr, axis=0), jnp.concatenate(ri, axis=0)
        for b in range(SEG // 8):
            pw_scr[j, b * 8:(b + 1) * 8, :] = blk_r
            pw_scr[ht + j, b * 8:(b + 1) * 8, :] = blk_i
            blk_r, blk_i = cmul(blk_r, blk_i, a8r, a8i)
        a_seg.append((blk_r[0:1], blk_i[0:1]))

    ar_b = [jnp.broadcast_to(a, (NSEG, LANES)) for a in ar]
    ai_b = [jnp.broadcast_to(a, (NSEG, LANES)) for a in ai]

    def body(i, carry):
        new_r, new_i = [], []
        for j in range(ht):
            xr, xi = carry[j], carry[ht + j]
            seg_rows = pl.ds(i, NSEG, stride=SEG_PITCH)
            xl_scr[j, seg_rows, :] = xr
            xl_scr[ht + j, seg_rows, :] = xi
            pr, pi = cmul(xr, xi, ar_b[j], ai_b[j])
            new_r.append(pr + s_scr[j, seg_rows, :])
            new_i.append(pi + s_scr[ht + j, seg_rows, :])
        return tuple(new_r + new_i)

    ends = tuple(jnp.zeros((NSEG, LANES), f32) for _ in range(NT))
    for i in range(SEG):
        ends = body(i, ends)

    for j in range(ht):
        zr, zi = nil, nil
        for k in range(NSEG):
            rows = slice(k * SEG, (k + 1) * SEG)
            loc = slice(k * SEG_PITCH, k * SEG_PITCH + SEG)
            fr, fi = cmul(pw_scr[j], pw_scr[ht + j], zr, zi)
            xs_scr[rows, j * LANES:(j + 1) * LANES] = (xl_scr[j, loc, :] + fr).astype(bf16)
            xs_scr[rows, half + j * LANES:half + (j + 1) * LANES] = (xl_scr[ht + j, loc, :] + fi).astype(bf16)
            pr, pi = cmul(zr, zi, *a_seg[j])
            zr, zi = pr + ends[j][k:k + 1], pi + ends[ht + j][k:k + 1]

    xs = xs_scr[...]
    for j in range(SG_IN // Y_TILE):
        kk = (j + 1) * Y_TILE
        cols = slice(j * Y_TILE, (j + 1) * Y_TILE)
        y = _dot(u_ref[:, :kk], m_scr[:kk, cols]) + _dot(xs, q_scr[:, cols])
        y_ref[:, cols] = _gelu_tanh(y).astype(bf16)


def _ssm(u2, pre, pim, kt, qc, rep, at_re, at_im):
    per_sg = lambda r, c: pl.BlockSpec((None, r, c), lambda g: (g, 0, 0))
    return pl.pallas_call(
        _ssm_kernel,
        grid=(SG,),
        in_specs=[pl.BlockSpec((N_CHUNKS, SG_IN), lambda g: (0, g)),
                  per_sg(SG_IN, LANES), per_sg(SG_IN, LANES), per_sg(SG_IN, LANES),
                  per_sg(SG_STATE, CHUNK * SSM_GC), _const_spec(rep.shape),
                  per_sg(1, SG_STATE // 2), per_sg(1, SG_STATE // 2)],
        out_specs=pl.BlockSpec((N_CHUNKS, SG_IN), lambda g: (0, g)),
        out_shape=jax.ShapeDtypeStruct((N_CHUNKS, SG * SG_IN), bf16),
        scratch_shapes=[pltpu.VMEM((SG_IN, SG_STATE), bf16), pltpu.VMEM((SG_IN, SG_IN), bf16),
                        pltpu.VMEM((SG_STATE, SG_IN), bf16),
                        pltpu.VMEM((NT, NSEG * SEG_PITCH, LANES), f32),
                        pltpu.VMEM((NT, NSEG * SEG_PITCH, LANES), f32),
                        pltpu.VMEM((NT, SEG, LANES), f32), pltpu.VMEM((N_CHUNKS, SG_STATE), bf16)],
        compiler_params=_params(("arbitrary",)),
        name="ssm",
    )(u2, pre, pim, kt, qc, rep, at_re, at_im)


def _ssm_prep_kernel(lam_re_ref, lam_im_ref, ldt_ref, bt_re_ref, bt_im_ref, yc_ref, d_ref, eye_ref, win_ref,
                     pre_ref, pim_ref, kt_ref, qc_ref, at_re_ref, at_im_ref, wmix_ref):
    wmix_ref[...] = win_ref[...].astype(bf16)
    lr, li = lam_re_ref[...], lam_im_ref[...]
    dt = jnp.exp(ldt_ref[...])
    mag = jnp.exp(lr * dt)
    a_re, a_im = mag * jnp.cos(li * dt), mag * jnp.sin(li * dt)
    den = lr * lr + li * li
    nr, ni = a_re - 1.0, a_im
    coef_re = (nr * lr + ni * li) / den
    coef_im = (ni * lr - nr * li) / den
    pw_re, pw_im = [jnp.ones_like(a_re)], [jnp.zeros_like(a_im)]
    for _ in range(CHUNK):
        pr, pi = pw_re[-1], pw_im[-1]
        pw_re.append(pr * a_re - pi * a_im)
        pw_im.append(pr * a_im + pi * a_re)

    low = lax.broadcasted_iota(jnp.int32, (SSM_GC, LANES), 1) < SSM_P
    diag = ((lax.broadcasted_iota(jnp.int32, (SSM_GC, LANES), 1) & (SSM_GC - 1))
            == lax.broadcasted_iota(jnp.int32, (SSM_GC, LANES), 0))
    contract_lanes = (((1,), (1,)), ((), ()))
    for g in range(GPS):
        grp = slice(g * SSM_GC, (g + 1) * SSM_GC)
        one = lambda x: x[g:g + 1]
        bb_re = one(coef_re) * bt_re_ref[grp, :] - one(coef_im) * bt_im_ref[grp, :]
        bb_im = one(coef_re) * bt_im_ref[grp, :] + one(coef_im) * bt_re_ref[grp, :]
        yc = yc_ref[grp, :]
        yc_sw = pltpu.roll(yc, SSM_P, axis=1)
        y2 = jnp.where(low, yc_sw, -yc_sw)
        x_rows, ca_rows = [], []
        for tau in range(CHUNK):
            pr, pi = one(pw_re[tau]), one(pw_im[tau])
            ab_re = bb_re * pr - bb_im * pi
            ab_im = bb_re * pi + bb_im * pr
            s = CHUNK - 1 - tau
            rows = slice(s * LANES + g * SSM_GC, s * LANES + (g + 1) * SSM_GC)
            pre_ref[rows, :] = ab_re
            pim_ref[rows, :] = ab_im
            x_rows.append(jnp.where(low, ab_re, ab_im))
            ca_rows.append(yc * one(pw_re[tau + 1]) + y2 * one(pw_im[tau + 1]))
        kern = lax.dot_general(jnp.concatenate(x_rows, axis=0), jnp.concatenate([yc] * GPS, axis=0),
                               contract_lanes, precision=lax.Precision.HIGHEST,
                               preferred_element_type=f32)
        for tau in range(CHUNK):
            blk = kern[tau * SSM_GC:(tau + 1) * SSM_GC]
            if tau == 0:
                blk = blk + jnp.where(diag, d_ref[...], 0.0)
            kt_ref[tau * LANES + g * SSM_GC:tau * LANES + (g + 1) * SSM_GC, :] = blk
        qt = lax.dot_general(eye_ref[...], jnp.concatenate(ca_rows, axis=0).astype(bf16),
                             contract_lanes, preferred_element_type=f32)
        qc_ref[g * SSM_P:(g + 1) * SSM_P, :] = qt[:SSM_P].astype(bf16)
        qc_ref[GPS * SSM_P + g * SSM_P:GPS * SSM_P + (g + 1) * SSM_P, :] = qt[SSM_P:].astype(bf16)
    lo_lane = lax.broadcasted_iota(jnp.int32, (1, LANES), 1) < SSM_P
    for k in range(GPS // 2):
        sel = lambda pw: jnp.where(lo_lane, pw[2 * k:2 * k + 1], pw[2 * k + 1:2 * k + 2])
        at_re_ref[:, k * LANES:(k + 1) * LANES] = sel(pw_re[CHUNK])
        at_im_ref[:, k * LANES:(k + 1) * LANES] = sel(pw_im[CHUNK])


def _ssm_operators(lam_re, lam_im, log_dt, b_re, b_im, c_re, c_im, d_skip, w_in):
    dup = lambda x: jnp.concatenate([x, x], axis=-1)
    per_sg = lambda x: x.astype(f32).reshape((SG, x.size // (SG * x.shape[-1]), x.shape[-1]))
    bt = lambda b: per_sg(dup(b.astype(f32).swapaxes(1, 2)))
    yc = per_sg(jnp.concatenate([c_re.astype(f32), -c_im.astype(f32)], axis=-1))
    args = (per_sg(dup(lam_re)), per_sg(dup(lam_im)), log_dt.astype(f32).reshape(SG, GPS, 1),
            bt(b_re), bt(b_im), yc, d_skip.astype(f32).reshape(SG, 1, LANES), jnp.eye(LANES, dtype=bf16),
            w_in)
    blk = lambda r, c: pl.BlockSpec((None, r, c), lambda g: (g, 0, 0))
    slab = pl.BlockSpec((w_in.shape[0] // SG, IN_MIX_W), lambda g: (g, 0))
    return pl.pallas_call(
        _ssm_prep_kernel,
        grid=(SG,),
        in_specs=[blk(GPS, LANES), blk(GPS, LANES), blk(GPS, 1), blk(LANES, LANES), blk(LANES, LANES),
                  blk(LANES, LANES), blk(1, LANES), _const_spec((LANES, LANES)), slab],
        out_specs=[blk(SG_IN, LANES), blk(SG_IN, LANES), blk(SG_IN, LANES),
                   blk(SG_STATE, CHUNK * SSM_GC), blk(1, SG_STATE // 2), blk(1, SG_STATE // 2),
                   slab],
        out_shape=[jax.ShapeDtypeStruct((SG, SG_IN, LANES), f32)] * 3
                  + [jax.ShapeDtypeStruct((SG, SG_STATE, CHUNK * SSM_GC), bf16)]
                  + [jax.ShapeDtypeStruct((SG, 1, SG_STATE // 2), f32)] * 2
                  + [jax.ShapeDtypeStruct((w_in.shape[0], IN_MIX_W), bf16)],
        compiler_params=_params(("parallel",)),
        name="ssm_prep",
    )(*args)


def _replication_matrix():
    src = jnp.arange(CHUNK * SSM_GC)
    dst = jnp.arange(SG_IN)
    same_t = (src[:, None] // SSM_GC) == (dst[None, :] // LANES)
    same_c = (src[:, None] % SSM_GC) == (dst[None, :] % SSM_GC)
    return (same_t & same_c).astype(bf16)


def _kv_operands(kp_ref, kc_ref, vp_ref, vc_ref):
    kcat = jnp.concatenate([kp_ref[...], kc_ref[...]], axis=0).astype(f32)
    vcat = jnp.concatenate([vp_ref[...], vc_ref[...]], axis=0).astype(f32)
    k_sw = pltpu.roll(kcat, HEAD_DIM, axis=1)
    v_sw = pltpu.roll(vcat, HEAD_DIM, axis=1)
    low_kv = lax.broadcasted_iota(jnp.int32, kcat.shape, 1) < HEAD_DIM
    operands = []
    for hk in range(N_KV_HEADS):
        k_src, k_oth = (kcat, k_sw) if hk == 0 else (k_sw, kcat)
        k_lo = jnp.where(low_kv, k_src, 0.0).astype(bf16)
        k_hi = jnp.where(low_kv, 0.0, k_oth).astype(bf16)
        vv = (jnp.where(low_kv, vcat, v_sw) if hk == 0 else jnp.where(low_kv, v_sw, vcat)).astype(bf16)
        operands.append((k_lo, k_hi, vv))
    return operands


def _attention_block(q, kv_operands, sinks_ref, blk, no_prev, oa_ref):
    q_rows = slice(blk * BLOCK, (blk + 1) * BLOCK)
    kv_rows = slice(blk * BLOCK, (blk + 2) * BLOCK)
    from_prev = (lax.broadcasted_iota(jnp.int32, (BLOCK, BLOCK), 1)
                 > lax.broadcasted_iota(jnp.int32, (BLOCK, BLOCK), 0))
    low_o = lax.broadcasted_iota(jnp.int32, (BLOCK, LANES), 1) < HEAD_DIM
    contract_lanes = (((1,), (1,)), ((), ()))
    pairs_per_kv = (N_Q_HEADS // N_KV_HEADS) // 2
    for hk, (k_lo, k_hi, vv) in enumerate(kv_operands):
        qs = jnp.concatenate(
            [q[q_rows, (pairs_per_kv * hk + b) * LANES:(pairs_per_kv * hk + b + 1) * LANES]
             for b in range(pairs_per_kv)], axis=0)
        s_par = (lax.dot_general(qs, k_lo[kv_rows], contract_lanes, preferred_element_type=f32),
                 lax.dot_general(qs, k_hi[kv_rows], contract_lanes, preferred_element_type=f32))
        for b in range(pairs_per_kv):
            outs = []
            for par in range(2):
                head = 2 * (pairs_per_kv * hk + b) + par
                s_all = s_par[par][b * BLOCK:(b + 1) * BLOCK]
                s_prev = s_all[:, :BLOCK] + no_prev if blk == 0 else s_all[:, :BLOCK]
                s = jnp.where(from_prev, s_prev, s_all[:, BLOCK:])
                sink = sinks_ref[head]
                m = jnp.maximum(jnp.max(s, axis=1, keepdims=True), sink)
                p = jnp.exp(s - m)
                denom = jnp.sum(p, axis=1, keepdims=True) + jnp.exp(sink - m)
                p_kv = jnp.concatenate([jnp.where(from_prev, p, 0.0), jnp.where(from_prev, 0.0, p)],
                                       axis=1).astype(bf16)
                outs.append(_dot(p_kv, vv[kv_rows]) * (1.0 / denom))
            col = (pairs_per_kv * hk + b) * LANES
            oa_ref[q_rows, col:col + LANES] = jnp.where(low_o, outs[0], outs[1]).astype(bf16)


def _attn_mix_kernel(sinks_ref, q_ref, kp_ref, kc_ref, vp_ref, vc_ref, y2_ref, ga_ref, gs_ref,
                     wglu_ref, wb_ref, wdn_ref, m_ref, wdn_o_ref, oa_ref, yb_ref):
    wdn_o_ref[...] = wdn_ref[...].astype(bf16)
    no_prev = jnp.where(pl.program_id(0) > 0, 0.0, MASKED)
    q = q_ref[...] * jnp.asarray(1.0 / math.sqrt(HEAD_DIM), bf16)
    kv_operands = _kv_operands(kp_ref, kc_ref, vp_ref, vc_ref)
    for r in range(q_ref.shape[0] // ROW_SUB):
        for blk in range(r * (ROW_SUB // BLOCK), (r + 1) * (ROW_SUB // BLOCK)):
            _attention_block(q, kv_operands, sinks_ref, blk, no_prev, oa_ref)
        rows = pl.ds(r * ROW_SUB, ROW_SUB)
        chunk_rows = pl.ds(r * (ROW_SUB // CHUNK), ROW_SUB // CHUNK)
        for sg in range(SG):
            for t in range(CHUNK):
                col = sg * SG_IN + t * LANES
                yb_ref[r, sg, pl.ds(t, ROW_SUB // CHUNK, stride=CHUNK), :] = (
                    y2_ref[chunk_rows, col:col + LANES].astype(f32))
        y = jnp.concatenate([yb_ref[r, sg] for sg in range(SG)], axis=1).astype(bf16)
        zg = _dot(y, wglu_ref[...])
        o_ssm = zg[:, :SSM_W] * _sigmoid(zg[:, SSM_W:])
        y_s = _dot(o_ssm.astype(bf16), wb_ref[Q_W:, :])
        y_a = _dot(oa_ref[rows, :], wb_ref[:Q_W, :])
        m_ref[rows, :] = (ga_ref[rows, :].astype(f32) * y_a + gs_ref[rows, :].astype(f32) * y_s).astype(bf16)


def _attn_mix(q, k, v, sinks, y2, gates, wglu, wb, w_down, tm=512):
    L = q.shape[0]
    cur = lambda i: (i, 0)
    prev = lambda i: (jnp.maximum(i * (tm // BLOCK) - 1, 0), 0)
    dn_slab = pl.BlockSpec((w_down.shape[0] // (L // tm), w_down.shape[1]), cur)
    return pl.pallas_call(
        _attn_mix_kernel,
        grid=(L // tm,),
        in_specs=[pl.BlockSpec(memory_space=pltpu.SMEM),
                  pl.BlockSpec((tm, Q_W), cur),
                  pl.BlockSpec((BLOCK, KV_W), prev), pl.BlockSpec((tm, KV_W), cur),
                  pl.BlockSpec((BLOCK, KV_W), prev), pl.BlockSpec((tm, KV_W), cur),
                  pl.BlockSpec((tm // CHUNK, SG * SG_IN), cur),
                  pl.BlockSpec((tm, D_MODEL), cur),
                  pl.BlockSpec((tm, D_MODEL), lambda i: (i, 1)),
                  _const_spec(wglu.shape), _const_spec(wb.shape), dn_slab],
        out_specs=[pl.BlockSpec((tm, D_MODEL), cur), dn_slab],
        out_shape=[jax.ShapeDtypeStruct((L, D_MODEL), bf16), jax.ShapeDtypeStruct(w_down.shape, bf16)],
        scratch_shapes=[pltpu.VMEM((tm, Q_W), bf16),
                        pltpu.VMEM((tm // ROW_SUB, SG, ROW_SUB, LANES), f32)],
        compiler_params=_params(("parallel",)),
        name="attn_mix",
    )(sinks, q, k, k, v, v, y2, gates, gates, wglu, wb, w_down)


def _outproj_kernel(m_ref, w_ref, x_ref, gpost_ref, gpre_ref, x1_ref, h2_ref):
    for r in range(m_ref.shape[0] // OUT_SUB):
        rows = pl.ds(r * OUT_SUB, OUT_SUB)
        out = _dot(m_ref[rows, :], w_ref[...])
        x1 = x_ref[rows, :] + _rms_norm(out, gpost_ref[...])
        x1_ref[rows, :] = x1
        h2_ref[rows, :] = _rms_norm(x1, gpre_ref[...]).astype(bf16)


def _outproj(m, w_out, x, g_post, g_pre, tm=512):
    L = x.shape[0]
    row = lambda: pl.BlockSpec((tm, D_MODEL), lambda i: (i, 0))
    return pl.pallas_call(
        _outproj_kernel,
        grid=(L // tm,),
        in_specs=[row(), _const_spec(w_out.shape), row(),
                  _const_spec((1, D_MODEL)), _const_spec((1, D_MODEL))],
        out_specs=[row(), row()],
        out_shape=[jax.ShapeDtypeStruct((L, D_MODEL), f32), jax.ShapeDtypeStruct((L, D_MODEL), bf16)],
        compiler_params=_params(("parallel",)),
        name="outproj",
    )(m, w_out, x, g_post, g_pre)


MLP_SUB = 512


def _mlp_kernel(h_ref, wu_ref, wd_ref, x_ref, g_ref, o_ref, acc_ref):
    i, j = pl.program_id(0), pl.program_id(1)
    n_tiles = pl.num_programs(0) - 1
    tm = h_ref.shape[0]
    slab = o_ref.shape[0]
    cur, prev = acc_ref.at[i % 2], acc_ref.at[(i + 1) % 2]
    slab_rows = pl.ds(pl.multiple_of(j * slab, slab), slab)

    def finish_prev_slab():
        o_ref[...] = x_ref[...] + _rms_norm(prev[slab_rows, :], g_ref[...])
        prev[slab_rows, :] = jnp.zeros((slab, D_MODEL), f32)

    @pl.when((i == 0) & (j == 0))
    def _():
        acc_ref[...] = jnp.zeros_like(acc_ref)

    @pl.when(i < n_tiles)
    def _():
        for r in range(tm // MLP_SUB):
            rows = pl.ds(r * MLP_SUB, MLP_SUB)
            a = jnp.maximum(_dot(h_ref[rows, :], wu_ref[...]), 0.0)
            cur[rows, :] += _dot((a * a).astype(bf16), wd_ref[...])
        finish_prev_slab()

    @pl.when(i == n_tiles)
    def _():
        finish_prev_slab()


def _mlp(h2, w_up, w_down, x1, g_post, tm=1024, tf=1024):
    L = x1.shape[0]
    n_tiles, n_ff = L // tm, D_FF // tf
    slab = tm // n_ff
    ff = lambda i, j: jnp.where(i < n_tiles, j, n_ff - 1)
    prev_slab = lambda i, j: (jnp.maximum(i - 1, 0) * n_ff + jnp.where(i > 0, j, 0), 0)
    return pl.pallas_call(
        _mlp_kernel,
        grid=(n_tiles + 1, n_ff),
        in_specs=[pl.BlockSpec((tm, D_MODEL), lambda i, j: (jnp.minimum(i, n_tiles - 1), 0)),
                  pl.BlockSpec((D_MODEL, tf), lambda i, j: (0, ff(i, j))),
                  pl.BlockSpec((tf, D_MODEL), lambda i, j: (ff(i, j), 0)),
                  pl.BlockSpec((slab, D_MODEL), prev_slab),
                  _const_spec((1, D_MODEL))],
        out_specs=pl.BlockSpec((slab, D_MODEL), prev_slab),
        out_shape=jax.ShapeDtypeStruct((L, D_MODEL), f32),
        scratch_shapes=[pltpu.VMEM((2, tm, D_MODEL), f32)],
        compiler_params=_params(("arbitrary", "arbitrary")),
        name="mlp",
    )(h2, w_up, w_down, x1, g_post)


def _rope_tables(L):
    half = ROT_DIM // 2
    inv = ROPE_THETA ** (-np.arange(half, dtype=np.float64) * 2.0 / ROT_DIM)
    ang = np.arange(L, dtype=np.float64)[:, None] * inv[None, :]
    cos, sin = np.cos(ang).astype(np.float32), np.sin(ang).astype(np.float32)
    ones = np.ones((L, HEAD_DIM - ROT_DIM), np.float32)
    zeros = np.zeros((L, HEAD_DIM - ROT_DIM), np.float32)
    zh = np.zeros((L, half), np.float32)
    per_head = lambda parts: jnp.asarray(np.tile(np.concatenate(parts, axis=1), (1, LANES // HEAD_DIM)))
    return (per_head([cos, cos, ones]), per_head([-sin, zh, zeros]), per_head([zh, sin, zeros]))


def _layer(x, norm_mix_pre, norm_mix_post, norm_mlp_pre, norm_mlp_post, w_in, sinks,
           lam_re, lam_im, log_dt, b_re, b_im, c_re, c_im, d_skip, w_glu, w_branch, w_out,
           w_up, w_down, rope):
    gain = lambda g: g.astype(f32).reshape(1, D_MODEL)
    pre, pim, kt, qc, at_re, at_im, wmix = _ssm_operators(
        lam_re, lam_im, log_dt, b_re, b_im, c_re, c_im, d_skip, w_in.astype(f32))
    q, k, v, u2, h, wg = _inproj(x, gain(norm_mix_pre), wmix, w_in.astype(f32), *rope)
    gates, wup, (wglu, wb, wo) = _gates(h, wg, w_up.astype(f32),
                                        [w_glu.astype(f32), w_branch.astype(f32), w_out.astype(f32)])
    y2 = _ssm(u2, pre, pim, kt, qc, _replication_matrix(), at_re, at_im)
    mixed, wdn = _attn_mix(q, k, v, sinks.astype(f32), y2, gates, wglu, wb, w_down.astype(f32))
    x1, h2 = _outproj(mixed, wo, x, gain(norm_mix_post), gain(norm_mlp_pre))
    return _mlp(h2, wup, wdn, x1, gain(norm_mlp_post))


def kernel(x, norm_mix_pre, norm_mix_post, norm_mlp_pre, norm_mlp_post, w_in, sinks, lam_re, lam_im, log_dt, b_re, b_im, c_re, c_im, d_skip, w_glu, w_branch, w_out, w_up, w_down):
    B, L, _ = x.shape
    depth = w_in.shape[0]
    rope = _rope_tables(L)
    outs = []
    for b in range(B):
        xb = x[b]
        for l in range(depth):
            xb = _layer(xb, norm_mix_pre[l], norm_mix_post[l], norm_mlp_pre[l], norm_mlp_post[l],
                        w_in[l], sinks[l], lam_re[l], lam_im[l], log_dt[l], b_re[l], b_im[l],
                        c_re[l], c_im[l], d_skip[l], w_glu[l], w_branch[l], w_out[l],
                        w_up[l], w_down[l], rope)
        outs.append(xb)
    return jnp.stack(outs)
```

```python
import math

import jax
import jax.numpy as jnp
import numpy as np
from jax import lax
from jax.experimental import pallas as pl
from jax.experimental.pallas import tpu as pltpu

D_MODEL = 2048
SEQ = 8192
HEAD_DIM = 64
N_Q_HEADS = 16
N_KV_HEADS = 2
BLOCK = 128
ROT_DIM = HEAD_DIM // 4
ROPE_THETA = 500000.0
Q_W = N_Q_HEADS * HEAD_DIM
KV_W = N_KV_HEADS * HEAD_DIM
SSM_W = D_MODEL // 2
QKV_W = Q_W + 2 * KV_W
IN_MIX_W = QKV_W + SSM_W
SSM_GC = 16
SSM_G = SSM_W // SSM_GC
SSM_P = 64
D_FF = 4 * D_MODEL
EPS = 1e-6

LANES = 128
CHUNK = 16
N_CHUNKS = SEQ // CHUNK
SG = SSM_W // LANES
GPS = LANES // SSM_GC
SG_IN = CHUNK * LANES
SG_STATE = 2 * GPS * SSM_P
VMEM_LIMIT = 56 * 1024 * 1024
ROW_SUB = 256
CHUNK_PITCH = CHUNK + 8
OUT_SUB = 128
MASKED = float(jnp.finfo(jnp.float32).min)

bf16 = jnp.bfloat16
f32 = jnp.float32


def _dot(a, b):
    return jnp.dot(a, b, preferred_element_type=f32)


def _sigmoid(x):
    return 1.0 / (1.0 + jnp.exp(-x))


def _gelu_tanh(x):
    c = math.sqrt(2.0 / math.pi)
    return x * (0.5 * (1.0 + jnp.tanh(c * (x + 0.044715 * (x * x * x)))))


def _rms_norm(x, g):
    return x * lax.rsqrt(jnp.mean(x * x, axis=-1, keepdims=True) + EPS) * g


def _params(sem):
    return pltpu.CompilerParams(dimension_semantics=sem, vmem_limit_bytes=VMEM_LIMIT)


def _const_spec(shape):
    nd = len(shape)
    return pl.BlockSpec(shape, lambda *_: (0,) * nd, pipeline_mode=pl.Buffered(1))


def _rope_block(z, cos, sin_a, sin_b):
    return (z * cos + pltpu.roll(z, LANES - ROT_DIM // 2, axis=1) * sin_a
            + pltpu.roll(z, ROT_DIM // 2, axis=1) * sin_b)


def _inproj_kernel(x_ref, g_ref, w_ref, wcast_ref, cos_ref, sa_ref, sb_ref,
                   q_ref, k_ref, v_ref, u2_ref, h_ref, wg_ref, zu_ref):
    wg_ref[...] = wcast_ref[...].astype(bf16)
    for r in range(x_ref.shape[0] // ROW_SUB):
        rows = pl.ds(r * ROW_SUB, ROW_SUB)
        h = _rms_norm(x_ref[rows, :], g_ref[...]).astype(bf16)
        h_ref[rows, :] = h
        cos, sin_a, sin_b = cos_ref[rows, :], sa_ref[rows, :], sb_ref[rows, :]
        for half in range(2):
            zq = _dot(h, w_ref[:, half * 512:(half + 1) * 512])
            for b in range(4):
                blk = zq[:, b * LANES:(b + 1) * LANES]
                col = (half * 4 + b) * LANES
                q_ref[rows, col:col + LANES] = _rope_block(blk, cos, sin_a, sin_b).astype(bf16)
        zkv = _dot(h, w_ref[:, Q_W:QKV_W])
        k_ref[rows, :] = _rope_block(zkv[:, :KV_W], cos, sin_a, sin_b).astype(bf16)
        v_ref[rows, :] = zkv[:, KV_W:].astype(bf16)
        zu = _dot(h, w_ref[:, QKV_W:])
        chunk_rows = pl.ds(r * (ROW_SUB // CHUNK), ROW_SUB // CHUNK)
        for sg in range(SG):
            for n in range(ROW_SUB // CHUNK):
                zu_ref[r, sg, n * CHUNK_PITCH:n * CHUNK_PITCH + CHUNK, :] = (
                    zu[n * CHUNK:(n + 1) * CHUNK, sg * LANES:(sg + 1) * LANES])
            for s in range(CHUNK):
                col = sg * SG_IN + s * LANES
                u2_ref[chunk_rows, col:col + LANES] = (
                    zu_ref[r, sg, pl.ds(s, ROW_SUB // CHUNK, stride=CHUNK_PITCH), :].astype(bf16))


def _inproj(x, gain, wmix, w_in, cos, sin_a, sin_b, tm=512):
    L = x.shape[0]
    steps = L // tm
    gate_w = w_in.shape[1] - IN_MIX_W
    slab = gate_w // steps
    assert IN_MIX_W % slab == 0 and slab % LANES == 0
    row = lambda w: pl.BlockSpec((tm, w), lambda i: (i, 0))
    return pl.pallas_call(
        _inproj_kernel,
        grid=(steps,),
        in_specs=[row(D_MODEL), _const_spec((1, D_MODEL)), _const_spec(wmix.shape),
                  pl.BlockSpec((D_MODEL, slab), lambda i: (0, IN_MIX_W // slab + i)),
                  row(LANES), row(LANES), row(LANES)],
        out_specs=[row(Q_W), row(KV_W), row(KV_W),
                   pl.BlockSpec((tm // CHUNK, SG * SG_IN), lambda i: (i, 0)), row(D_MODEL),
                   pl.BlockSpec((D_MODEL, slab), lambda i: (0, i))],
        out_shape=[jax.ShapeDtypeStruct((L, Q_W), bf16), jax.ShapeDtypeStruct((L, KV_W), bf16),
                   jax.ShapeDtypeStruct((L, KV_W), bf16),
                   jax.ShapeDtypeStruct((L // CHUNK, SG * SG_IN), bf16),
                   jax.ShapeDtypeStruct((L, D_MODEL), bf16),
                   jax.ShapeDtypeStruct((D_MODEL, gate_w), bf16)],
        scratch_shapes=[pltpu.VMEM((tm // ROW_SUB, SG, ROW_SUB // CHUNK * CHUNK_PITCH, LANES), f32)],
        compiler_params=_params(("parallel",)),
        name="inproj",
    )(x, gain, wmix, w_in, cos, sin_a, sin_b)


def _gates_kernel(h_ref, w_ref, *rest):
    n_cast = (len(rest) - 1) // 2
    o_ref = rest[n_cast]
    for r in range(h_ref.shape[0] // ROW_SUB):
        rows = pl.ds(r * ROW_SUB, ROW_SUB)
        o_ref[rows, :] = _sigmoid(_dot(h_ref[rows, :], w_ref[...])).astype(bf16)
    for src, dst in zip(rest[:n_cast], rest[n_cast + 1:]):
        dst[...] = src[...].astype(bf16)


def _gates(h, wg, w_up, row_cast_weights, tm=1024, tn=2048):
    L, N = h.shape[0], wg.shape[1]
    ni = L // tm
    steps = ni * (N // tn)
    step = lambda j, i: j * ni + i
    slabs = [pl.BlockSpec((w_up.shape[0], w_up.shape[1] // steps), lambda j, i: (0, step(j, i)))]
    slabs += [pl.BlockSpec((w.shape[0] // steps, w.shape[1]), lambda j, i: (step(j, i), 0))
              for w in row_cast_weights]
    cast_weights = [w_up] + list(row_cast_weights)
    outs = pl.pallas_call(
        _gates_kernel,
        grid=(N // tn, ni),
        in_specs=[pl.BlockSpec((tm, D_MODEL), lambda j, i: (i, 0)),
                  pl.BlockSpec((D_MODEL, tn), lambda j, i: (0, j))] + slabs,
        out_specs=[pl.BlockSpec((tm, tn), lambda j, i: (i, j))] + slabs,
        out_shape=[jax.ShapeDtypeStruct((L, N), bf16)]
                  + [jax.ShapeDtypeStruct(w.shape, bf16) for w in cast_weights],
        compiler_params=_params(("parallel", "arbitrary")),
        name="gates",
    )(h, wg, *cast_weights)
    return outs[0], outs[1], outs[2:]


GC_SHIFT = SSM_GC.bit_length() - 1
P_SHIFT = SSM_P.bit_length() - 1
Y_TILE = 512
NT = SG_STATE // LANES
NSEG = 8
SEG = N_CHUNKS // NSEG
SEG_PITCH = SEG + 8


def _ssm_kernel(u_ref, pre_ref, pim_ref, kt_ref, qc_ref, rep_ref, ar_ref, ai_ref, y_ref,
                p_scr, m_scr, q_scr, s_scr, xl_scr, pw_scr, xs_scr):
    half = SG_STATE // 2
    row = lax.broadcasted_iota(jnp.int32, (SG_IN, LANES), 0)
    lane = lax.broadcasted_iota(jnp.int32, (SG_IN, LANES), 1)
    row_grp = (row >> GC_SHIFT) & (GPS - 1)
    pre, pim = pre_ref[...], pim_ref[...]
    for k in range(half // LANES):
        sel = row_grp == 2 * k + (lane >> P_SHIFT)
        p_scr[:, k * LANES:(k + 1) * LANES] = jnp.where(sel, pre, 0.0).astype(bf16)
        p_scr[:, half + k * LANES:half + (k + 1) * LANES] = jnp.where(sel, pim, 0.0).astype(bf16)

    blk_row = lax.broadcasted_iota(jnp.int32, (LANES, LANES), 0) >> GC_SHIFT
    blk_lane = lax.broadcasted_iota(jnp.int32, (LANES, LANES), 1) >> GC_SHIFT
    zero_blk = jnp.zeros((LANES, LANES), bf16)
    for tau in range(CHUNK):
        blk = jnp.where(blk_row == blk_lane, kt_ref[tau * LANES:(tau + 1) * LANES, :], 0.0).astype(bf16)
        for s in range(CHUNK - tau):
            t = s + tau
            m_scr[s * LANES:(s + 1) * LANES, t * LANES:(t + 1) * LANES] = blk
            if tau > 0:
                m_scr[t * LANES:(t + 1) * LANES, s * LANES:(s + 1) * LANES] = zero_blk

    q_row_grp = (lax.broadcasted_iota(jnp.int32, (SG_STATE, Y_TILE), 0) >> P_SHIFT) & (GPS - 1)
    q_lane_grp = (lax.broadcasted_iota(jnp.int32, (SG_STATE, Y_TILE), 1) >> GC_SHIFT) & (GPS - 1)
    for j in range(SG_IN // Y_TILE):
        full = _dot(qc_ref[...], rep_ref[:, j * Y_TILE:(j + 1) * Y_TILE])
        q_scr[:, j * Y_TILE:(j + 1) * Y_TILE] = jnp.where(q_row_grp == q_lane_grp, full, 0.0).astype(bf16)

    s = _dot(u_ref[...], p_scr[...])
    for j in range(NT):
        for k in range(NSEG):
            s_scr[j, k * SEG_PITCH:k * SEG_PITCH + SEG, :] = s[k * SEG:(k + 1) * SEG, j * LANES:(j + 1) * LANES]
    ht = NT // 2
    cmul = lambda xr, xi, yr, yi: (xr * yr - xi * yi, xr * yi + xi * yr)
    ar = [ar_ref[:, j * LANES:(j + 1) * LANES] for j in range(ht)]
    ai = [ai_ref[:, j * LANES:(j + 1) * LANES] for j in range(ht)]
    one, nil = jnp.ones((1, LANES), f32), jnp.zeros((1, LANES), f32)
    a_seg = []
    for j in range(ht):
        rr, ri = [one], [nil]
        for _ in range(7):
            nr, ni = cmul(rr[-1], ri[-1], ar[j], ai[j])
            rr.append(nr)
            ri.append(ni)
        a8r, a8i = cmul(rr[-1], ri[-1], ar[j], ai[j])
        blk_r, blk_i = jnp.concatenate(rr, axis=0), jnp.concatenate(ri, axis=0)
        for b in range(SEG // 8):
            pw_scr[j, b * 8:(b + 1) * 8, :] = blk_r
            pw_scr[ht + j, b * 8:(b + 1) * 8, :] = blk_i
            blk_r, blk_i = cmul(blk_r, blk_i, a8r, a8i)
        a_seg.append((blk_r[0:1], blk_i[0:1]))

    ar_b = [jnp.broadcast_to(a, (NSEG, LANES)) for a in ar]
    ai_b = [jnp.broadcast_to(a, (NSEG, LANES)) for a in ai]

    def body(i, carry):
        new_r, new_i = [], []
        for j in range(ht):
            xr, xi = carry[j], carry[ht + j]
            seg_rows = pl.ds(i, NSEG, stride=SEG_PITCH)
            xl_scr[j, seg_rows, :] = xr
            xl_scr[ht + j, seg_rows, :] = xi
            pr, pi = cmul(xr, xi, ar_b[j], ai_b[j])
            new_r.append(pr + s_scr[j, seg_rows, :])
            new_i.append(pi + s_scr[ht + j, seg_rows, :])
        return tuple(new_r + new_i)

    ends = tuple(jnp.zeros((NSEG, LANES), f32) for _ in range(NT))
    for i in range(SEG):
        ends = body(i, ends)

    for j in range(ht):
        zr, zi = nil, nil
        for k in range(NSEG):
            rows = slice(k * SEG, (k + 1) * SEG)
            loc = slice(k * SEG_PITCH, k * SEG_PITCH + SEG)
            fr, fi = cmul(pw_scr[j], pw_scr[ht + j], zr, zi)
            xs_scr[rows, j * LANES:(j + 1) * LANES] = (xl_scr[j, loc, :] + fr).astype(bf16)
            xs_scr[rows, half + j * LANES:half + (j + 1) * LANES] = (xl_scr[ht + j, loc, :] + fi).astype(bf16)
            pr, pi = cmul(zr, zi, *a_seg[j])
            zr, zi = pr + ends[j][k:k + 1], pi + ends[ht + j][k:k + 1]

    xs = xs_scr[...]
    for j in range(SG_IN // Y_TILE):
        kk = (j + 1) * Y_TILE
        cols = slice(j * Y_TILE, (j + 1) * Y_TILE)
        y = _dot(u_ref[:, :kk], m_scr[:kk, cols]) + _dot(xs, q_scr[:, cols])
        y_ref[:, cols] = _gelu_tanh(y).astype(bf16)


def _ssm(u2, pre, pim, kt, qc, rep, at_re, at_im):
    per_sg = lambda r, c: pl.BlockSpec((None, r, c), lambda g: (g, 0, 0))
    return pl.pallas_call(
        _ssm_kernel,
        grid=(SG,),
        in_specs=[pl.BlockSpec((N_CHUNKS, SG_IN), lambda g: (0, g)),
                  per_sg(SG_IN, LANES), per_sg(SG_IN, LANES), per_sg(SG_IN, LANES),
                  per_sg(SG_STATE, CHUNK * SSM_GC), _const_spec(rep.shape),
                  per_sg(1, SG_STATE // 2), per_sg(1, SG_STATE // 2)],
        out_specs=pl.BlockSpec((N_CHUNKS, SG_IN), lambda g: (0, g)),
        out_shape=jax.ShapeDtypeStruct((N_CHUNKS, SG * SG_IN), bf16),
        scratch_shapes=[pltpu.VMEM((SG_IN, SG_STATE), bf16), pltpu.VMEM((SG_IN, SG_IN), bf16),
                        pltpu.VMEM((SG_STATE, SG_IN), bf16),
                        pltpu.VMEM((NT, NSEG * SEG_PITCH, LANES), f32),
                        pltpu.VMEM((NT, NSEG * SEG_PITCH, LANES), f32),
                        pltpu.VMEM((NT, SEG, LANES), f32), pltpu.VMEM((N_CHUNKS, SG_STATE), bf16)],
        compiler_params=_params(("arbitrary",)),
        name="ssm",
    )(u2, pre, pim, kt, qc, rep, at_re, at_im)


def _ssm_prep_kernel(lam_re_ref, lam_im_ref, ldt_ref, bt_re_ref, bt_im_ref, yc_ref, d_ref, eye_ref, win_ref,
                     pre_ref, pim_ref, kt_ref, qc_ref, at_re_ref, at_im_ref, wmix_ref):
    wmix_ref[...] = win_ref[...].astype(bf16)
    lr, li = lam_re_ref[...], lam_im_ref[...]
    dt = jnp.exp(ldt_ref[...])
    mag = jnp.exp(lr * dt)
    a_re, a_im = mag * jnp.cos(li * dt), mag * jnp.sin(li * dt)
    den = lr * lr + li * li
    nr, ni = a_re - 1.0, a_im
    coef_re = (nr * lr + ni * li) / den
    coef_im = (ni * lr - nr * li) / den
    pw_re, pw_im = [jnp.ones_like(a_re)], [jnp.zeros_like(a_im)]
    for _ in range(CHUNK):
        pr, pi = pw_re[-1], pw_im[-1]
        pw_re.append(pr * a_re - pi * a_im)
        pw_im.append(pr * a_im + pi * a_re)

    low = lax.broadcasted_iota(jnp.int32, (SSM_GC, LANES), 1) < SSM_P
    diag = ((lax.broadcasted_iota(jnp.int32, (SSM_GC, LANES), 1) & (SSM_GC - 1))
            == lax.broadcasted_iota(jnp.int32, (SSM_GC, LANES), 0))
    contract_lanes = (((1,), (1,)), ((), ()))
    for g in range(GPS):
        grp = slice(g * SSM_GC, (g + 1) * SSM_GC)
        one = lambda x: x[g:g + 1]
        bb_re = one(coef_re) * bt_re_ref[grp, :] - one(coef_im) * bt_im_ref[grp, :]
        bb_im = one(coef_re) * bt_im_ref[grp, :] + one(coef_im) * bt_re_ref[grp, :]
        yc = yc_ref[grp, :]
        yc_sw = pltpu.roll(yc, SSM_P, axis=1)
        y2 = jnp.where(low, yc_sw, -yc_sw)
        x_rows, ca_rows = [], []
        for tau in range(CHUNK):
            pr, pi = one(pw_re[tau]), one(pw_im[tau])
            ab_re = bb_re * pr - bb_im * pi
            ab_im = bb_re * pi + bb_im * pr
            s = CHUNK - 1 - tau
            rows = slice(s * LANES + g * SSM_GC, s * LANES + (g + 1) * SSM_GC)
            pre_ref[rows, :] = ab_re
            pim_ref[rows, :] = ab_im
            x_rows.append(jnp.where(low, ab_re, ab_im))
            ca_rows.append(yc * one(pw_re[tau + 1]) + y2 * one(pw_im[tau + 1]))
        kern = lax.dot_general(jnp.concatenate(x_rows, axis=0), jnp.concatenate([yc] * GPS, axis=0),
                               contract_lanes, precision=lax.Precision.HIGHEST,
                               preferred_element_type=f32)
        for tau in range(CHUNK):
            blk = kern[tau * SSM_GC:(tau + 1) * SSM_GC]
            if tau == 0:
                blk = blk + jnp.where(diag, d_ref[...], 0.0)
            kt_ref[tau * LANES + g * SSM_GC:tau * LANES + (g + 1) * SSM_GC, :] = blk
        qt = lax.dot_general(eye_ref[...], jnp.concatenate(ca_rows, axis=0).astype(bf16),
                             contract_lanes, preferred_element_type=f32)
        qc_ref[g * SSM_P:(g + 1) * SSM_P, :] = qt[:SSM_P].astype(bf16)
        qc_ref[GPS * SSM_P + g * SSM_P:GPS * SSM_P + (g + 1) * SSM_P, :] = qt[SSM_P:].astype(bf16)
    lo_lane = lax.broadcasted_iota(jnp.int32, (1, LANES), 1) < SSM_P
    for k in range(GPS // 2):
        sel = lambda pw: jnp.where(lo_lane, pw[2 * k:2 * k + 1], pw[2 * k + 1:2 * k + 2])
        at_re_ref[:, k * LANES:(k + 1) * LANES] = sel(pw_re[CHUNK])
        at_im_ref[:, k * LANES:(k + 1) * LANES] = sel(pw_im[CHUNK])


def _ssm_operators(lam_re, lam_im, log_dt, b_re, b_im, c_re, c_im, d_skip, w_in):
    dup = lambda x: jnp.concatenate([x, x], axis=-1)
    per_sg = lambda x: x.astype(f32).reshape((SG, x.size // (SG * x.shape[-1]), x.shape[-1]))
    bt = lambda b: per_sg(dup(b.astype(f32).swapaxes(1, 2)))
    yc = per_sg(jnp.concatenate([c_re.astype(f32), -c_im.astype(f32)], axis=-1))
    args = (per_sg(dup(lam_re)), per_sg(dup(lam_im)), log_dt.astype(f32).reshape(SG, GPS, 1),
            bt(b_re), bt(b_im), yc, d_skip.astype(f32).reshape(SG, 1, LANES), jnp.eye(LANES, dtype=bf16),
            w_in)
    blk = lambda r, c: pl.BlockSpec((None, r, c), lambda g: (g, 0, 0))
    slab = pl.BlockSpec((w_in.shape[0] // SG, IN_MIX_W), lambda g: (g, 0))
    return pl.pallas_call(
        _ssm_prep_kernel,
        grid=(SG,),
        in_specs=[blk(GPS, LANES), blk(GPS, LANES), blk(GPS, 1), blk(LANES, LANES), blk(LANES, LANES),
                  blk(LANES, LANES), blk(1, LANES), _const_spec((LANES, LANES)), slab],
        out_specs=[blk(SG_IN, LANES), blk(SG_IN, LANES), blk(SG_IN, LANES),
                   blk(SG_STATE, CHUNK * SSM_GC), blk(1, SG_STATE // 2), blk(1, SG_STATE // 2),
                   slab],
        out_shape=[jax.ShapeDtypeStruct((SG, SG_IN, LANES), f32)] * 3
                  + [jax.ShapeDtypeStruct((SG, SG_STATE, CHUNK * SSM_GC), bf16)]
                  + [jax.ShapeDtypeStruct((SG, 1, SG_STATE // 2), f32)] * 2
                  + [jax.ShapeDtypeStruct((w_in.shape[0], IN_MIX_W), bf16)],
        compiler_params=_params(("parallel",)),
        name="ssm_prep",
    )(*args)


def _replication_matrix():
    src = jnp.arange(CHUNK * SSM_GC)
    dst = jnp.arange(SG_IN)
    same_t = (src[:, None] // SSM_GC) == (dst[None, :] // LANES)
    same_c = (src[:, None] % SSM_GC) == (dst[None, :] % SSM_GC)
    return (same_t & same_c).astype(bf16)


def _kv_operands(kp_ref, kc_ref, vp_ref, vc_ref):
    kcat = jnp.concatenate([kp_ref[...], kc_ref[...]], axis=0).astype(f32)
    vcat = jnp.concatenate([vp_ref[...], vc_ref[...]], axis=0).astype(f32)
    k_sw = pltpu.roll(kcat, HEAD_DIM, axis=1)
    v_sw = pltpu.roll(vcat, HEAD_DIM, axis=1)
    low_kv = lax.broadcasted_iota(jnp.int32, kcat.shape, 1) < HEAD_DIM
    operands = []
    for hk in range(N_KV_HEADS):
        k_src, k_oth = (kcat, k_sw) if hk == 0 else (k_sw, kcat)
        k_lo = jnp.where(low_kv, k_src, 0.0).astype(bf16)
        k_hi = jnp.where(low_kv, 0.0, k_oth).astype(bf16)
        vv = (jnp.where(low_kv, vcat, v_sw) if hk == 0 else jnp.where(low_kv, v_sw, vcat)).astype(bf16)
        operands.append((k_lo, k_hi, vv))
    return operands


def _attention_block(q, kv_operands, sinks_ref, blk, no_prev, oa_ref):
    q_rows = slice(blk * BLOCK, (blk + 1) * BLOCK)
    kv_rows = slice(blk * BLOCK, (blk + 2) * BLOCK)
    from_prev = (lax.broadcasted_iota(jnp.int32, (BLOCK, BLOCK), 1)
                 > lax.broadcasted_iota(jnp.int32, (BLOCK, BLOCK), 0))
    low_o = lax.broadcasted_iota(jnp.int32, (BLOCK, LANES), 1) < HEAD_DIM
    contract_lanes = (((1,), (1,)), ((), ()))
    pairs_per_kv = (N_Q_HEADS // N_KV_HEADS) // 2
    for hk, (k_lo, k_hi, vv) in enumerate(kv_operands):
        qs = jnp.concatenate(
            [q[q_rows, (pairs_per_kv * hk + b) * LANES:(pairs_per_kv * hk + b + 1) * LANES]
             for b in range(pairs_per_kv)], axis=0)
        s_par = (lax.dot_general(qs, k_lo[kv_rows], contract_lanes, preferred_element_type=f32),
                 lax.dot_general(qs, k_hi[kv_rows], contract_lanes, preferred_element_type=f32))
        for b in range(pairs_per_kv):
            outs = []
            for par in range(2):
                head = 2 * (pairs_per_kv * hk + b) + par
                s_all = s_par[par][b * BLOCK:(b + 1) * BLOCK]
                s_prev = s_all[:, :BLOCK] + no_prev if blk == 0 else s_all[:, :BLOCK]
                s = jnp.where(from_prev, s_prev, s_all[:, BLOCK:])
                sink = sinks_ref[head]
                m = jnp.maximum(jnp.max(s, axis=1, keepdims=True), sink)
                p = jnp.exp(s - m)
                denom = jnp.sum(p, axis=1, keepdims=True) + jnp.exp(sink - m)
                p_kv = jnp.concatenate([jnp.where(from_prev, p, 0.0), jnp.where(from_prev, 0.0, p)],
                                       axis=1).astype(bf16)
                outs.append(_dot(p_kv, vv[kv_rows]) * (1.0 / denom))
            col = (pairs_per_kv * hk + b) * LANES
            oa_ref[q_rows, col:col + LANES] = jnp.where(low_o, outs[0], outs[1]).astype(bf16)


def _attn_mix_kernel(sinks_ref, q_ref, kp_ref, kc_ref, vp_ref, vc_ref, y2_ref, ga_ref, gs_ref,
                     wglu_ref, wb_ref, wdn_ref, m_ref, wdn_o_ref, oa_ref, yb_ref):
    wdn_o_ref[...] = wdn_ref[...].astype(bf16)
    no_prev = jnp.where(pl.program_id(0) > 0, 0.0, MASKED)
    q = q_ref[...] * jnp.asarray(1.0 / math.sqrt(HEAD_DIM), bf16)
    kv_operands = _kv_operands(kp_ref, kc_ref, vp_ref, vc_ref)
    for r in range(q_ref.shape[0] // ROW_SUB):
        rows = pl.ds(r * ROW_SUB, ROW_SUB)
        chunk_rows = pl.ds(r * (ROW_SUB // CHUNK), ROW_SUB // CHUNK)
        for sg in range(SG):
            for t in range(CHUNK):
                col = sg * SG_IN + t * LANES
                yb_ref[r, sg, pl.ds(t, ROW_SUB // CHUNK, stride=CHUNK_PITCH), :] = (
                    y2_ref[chunk_rows, col:col + LANES].astype(f32))
        y = jnp.concatenate(
            [jnp.concatenate([yb_ref[r, sg, n * CHUNK_PITCH:n * CHUNK_PITCH + CHUNK, :]
                              for n in range(ROW_SUB // CHUNK)], axis=0) for sg in range(SG)],
            axis=1).astype(bf16)
        zg = _dot(y, wglu_ref[...])
        o_ssm = zg[:, :SSM_W] * _sigmoid(zg[:, SSM_W:])
        y_s = _dot(o_ssm.astype(bf16), wb_ref[Q_W:, :])
        for blk in range(r * (ROW_SUB // BLOCK), (r + 1) * (ROW_SUB // BLOCK)):
            _attention_block(q, kv_operands, sinks_ref, blk, no_prev, oa_ref)
        y_a = _dot(oa_ref[rows, :], wb_ref[:Q_W, :])
        m_ref[rows, :] = (ga_ref[rows, :].astype(f32) * y_a + gs_ref[rows, :].astype(f32) * y_s).astype(bf16)


def _attn_mix(q, k, v, sinks, y2, gates, wglu, wb, w_down, tm=512):
    L = q.shape[0]
    cur = lambda i: (i, 0)
    prev = lambda i: (jnp.maximum(i * (tm // BLOCK) - 1, 0), 0)
    dn_slab = pl.BlockSpec((w_down.shape[0] // (L // tm), w_down.shape[1]), cur)
    return pl.pallas_call(
        _attn_mix_kernel,
        grid=(L // tm,),
        in_specs=[pl.BlockSpec(memory_space=pltpu.SMEM),
                  pl.BlockSpec((tm, Q_W), cur),
                  pl.BlockSpec((BLOCK, KV_W), prev), pl.BlockSpec((tm, KV_W), cur),
                  pl.BlockSpec((BLOCK, KV_W), prev), pl.BlockSpec((tm, KV_W), cur),
                  pl.BlockSpec((tm // CHUNK, SG * SG_IN), cur),
                  pl.BlockSpec((tm, D_MODEL), cur),
                  pl.BlockSpec((tm, D_MODEL), lambda i: (i, 1)),
                  _const_spec(wglu.shape), _const_spec(wb.shape), dn_slab],
        out_specs=[pl.BlockSpec((tm, D_MODEL), cur), dn_slab],
        out_shape=[jax.ShapeDtypeStruct((L, D_MODEL), bf16), jax.ShapeDtypeStruct(w_down.shape, bf16)],
        scratch_shapes=[pltpu.VMEM((tm, Q_W), bf16),
                        pltpu.VMEM((tm // ROW_SUB, SG, ROW_SUB // CHUNK * CHUNK_PITCH, LANES), f32)],
        compiler_params=_params(("parallel",)),
        name="attn_mix",
    )(sinks, q, k, k, v, v, y2, gates, gates, wglu, wb, w_down)


def _outproj_kernel(m_ref, w_ref, x_ref, gpost_ref, gpre_ref, x1_ref, h2_ref):
    for r in range(m_ref.shape[0] // OUT_SUB):
        rows = pl.ds(r * OUT_SUB, OUT_SUB)
        out = _dot(m_ref[rows, :], w_ref[...])
        x1 = x_ref[rows, :] + _rms_norm(out, gpost_ref[...])
        x1_ref[rows, :] = x1
        h2_ref[rows, :] = _rms_norm(x1, gpre_ref[...]).astype(bf16)


def _outproj(m, w_out, x, g_post, g_pre, tm=512):
    L = x.shape[0]
    row = lambda: pl.BlockSpec((tm, D_MODEL), lambda i: (i, 0))
    return pl.pallas_call(
        _outproj_kernel,
        grid=(L // tm,),
        in_specs=[row(), _const_spec(w_out.shape), row(),
                  _const_spec((1, D_MODEL)), _const_spec((1, D_MODEL))],
        out_specs=[row(), row()],
        out_shape=[jax.ShapeDtypeStruct((L, D_MODEL), f32), jax.ShapeDtypeStruct((L, D_MODEL), bf16)],
        compiler_params=_params(("parallel",)),
        name="outproj",
    )(m, w_out, x, g_post, g_pre)


MLP_SUB = 512


def _mlp_kernel(h_ref, wu_ref, wd_ref, x_ref, g_ref, o_ref, acc_ref):
    i, j = pl.program_id(0), pl.program_id(1)
    n_tiles = pl.num_programs(0) - 1
    tm = h_ref.shape[0]
    slab = o_ref.shape[0]
    cur, prev = acc_ref.at[i % 2], acc_ref.at[(i + 1) % 2]
    slab_rows = pl.ds(pl.multiple_of(j * slab, slab), slab)

    def finish_prev_slab():
        o_ref[...] = x_ref[...] + _rms_norm(prev[slab_rows, :], g_ref[...])
        prev[slab_rows, :] = jnp.zeros((slab, D_MODEL), f32)

    @pl.when((i == 0) & (j == 0))
    def _():
        acc_ref[...] = jnp.zeros_like(acc_ref)

    @pl.when(i < n_tiles)
    def _():
        for r in range(tm // MLP_SUB):
            rows = pl.ds(r * MLP_SUB, MLP_SUB)
            a = jnp.maximum(_dot(h_ref[rows, :], wu_ref[...]), 0.0)
            cur[rows, :] += _dot((a * a).astype(bf16), wd_ref[...])
        finish_prev_slab()

    @pl.when(i == n_tiles)
    def _():
        finish_prev_slab()


def _mlp(h2, w_up, w_down, x1, g_post, tm=1024, tf=1024):
    L = x1.shape[0]
    n_tiles, n_ff = L // tm, D_FF // tf
    slab = tm // n_ff
    ff = lambda i, j: jnp.where(i < n_tiles, j, n_ff - 1)
    prev_slab = lambda i, j: (jnp.maximum(i - 1, 0) * n_ff + jnp.where(i > 0, j, 0), 0)
    return pl.pallas_call(
        _mlp_kernel,
        grid=(n_tiles + 1, n_ff),
        in_specs=[pl.BlockSpec((tm, D_MODEL), lambda i, j: (jnp.minimum(i, n_tiles - 1), 0)),
                  pl.BlockSpec((D_MODEL, tf), lambda i, j: (0, ff(i, j))),
                  pl.BlockSpec((tf, D_MODEL), lambda i, j: (ff(i, j), 0)),
                  pl.BlockSpec((slab, D_MODEL), prev_slab),
                  _const_spec((1, D_MODEL))],
        out_specs=pl.BlockSpec((slab, D_MODEL), prev_slab),
        out_shape=jax.ShapeDtypeStruct((L, D_MODEL), f32),
        scratch_shapes=[pltpu.VMEM((2, tm, D_MODEL), f32)],
        compiler_params=_params(("arbitrary", "arbitrary")),
        name="mlp",
    )(h2, w_up, w_down, x1, g_post)


def _rope_tables(L):
    half = ROT_DIM // 2
    inv = ROPE_THETA ** (-np.arange(half, dtype=np.float64) * 2.0 / ROT_DIM)
    ang = np.arange(L, dtype=np.float64)[:, None] * inv[None, :]
    cos, sin = np.cos(ang).astype(np.float32), np.sin(ang).astype(np.float32)
    ones = np.ones((L, HEAD_DIM - ROT_DIM), np.float32)
    zeros = np.zeros((L, HEAD_DIM - ROT_DIM), np.float32)
    zh = np.zeros((L, half), np.float32)
    per_head = lambda parts: jnp.asarray(np.tile(np.concatenate(parts, axis=1), (1, LANES // HEAD_DIM)))
    return (per_head([cos, cos, ones]), per_head([-sin, zh, zeros]), per_head([zh, sin, zeros]))


def _layer(x, norm_mix_pre, norm_mix_post, norm_mlp_pre, norm_mlp_post, w_in, sinks,
           lam_re, lam_im, log_dt, b_re, b_im, c_re, c_im, d_skip, w_glu, w_branch, w_out,
           w_up, w_down, rope):
    gain = lambda g: g.astype(f32).reshape(1, D_MODEL)
    pre, pim, kt, qc, at_re, at_im, wmix = _ssm_operators(
        lam_re, lam_im, log_dt, b_re, b_im, c_re, c_im, d_skip, w_in.astype(f32))
    q, k, v, u2, h, wg = _inproj(x, gain(norm_mix_pre), wmix, w_in.astype(f32), *rope)
    gates, wup, (wglu, wb, wo) = _gates(h, wg, w_up.astype(f32),
                                        [w_glu.astype(f32), w_branch.astype(f32), w_out.astype(f32)])
    y2 = _ssm(u2, pre, pim, kt, qc, _replication_matrix(), at_re, at_im)
    mixed, wdn = _attn_mix(q, k, v, sinks.astype(f32), y2, gates, wglu, wb, w_down.astype(f32))
    x1, h2 = _outproj(mixed, wo, x, gain(norm_mix_post), gain(norm_mlp_pre))
    return _mlp(h2, wup, wdn, x1, gain(norm_mlp_post))


def kernel(x, norm_mix_pre, norm_mix_post, norm_mlp_pre, norm_mlp_post, w_in, sinks, lam_re, lam_im, log_dt, b_re, b_im, c_re, c_im, d_skip, w_glu, w_branch, w_out, w_up, w_down):
    B, L, _ = x.shape
    depth = w_in.shape[0]
    rope = _rope_tables(L)
    outs = []
    for b in range(B):
        xb = x[b]
        for l in range(depth):
            xb = _layer(xb, norm_mix_pre[l], norm_mix_post[l], norm_mlp_pre[l], norm_mlp_post[l],
                        w_in[l], sinks[l], lam_re[l], lam_im[l], log_dt[l], b_re[l], b_im[l],
                        c_re[l], c_im[l], d_skip[l], w_glu[l], w_branch[l], w_out[l],
                        w_up[l], w_down[l], rope)
        outs.append(xb)
    return jnp.stack(outs)
```

```python
import math

import jax
import jax.numpy as jnp
import numpy as np
from jax import lax
from jax.experimental import pallas as pl
from jax.experimental.pallas import tpu as pltpu

D_MODEL = 2048
SEQ = 8192
HEAD_DIM = 64
N_Q_HEADS = 16
N_KV_HEADS = 2
BLOCK = 128
ROT_DIM = HEAD_DIM // 4
ROPE_THETA = 500000.0
Q_W = N_Q_HEADS * HEAD_DIM
KV_W = N_KV_HEADS * HEAD_DIM
SSM_W = D_MODEL // 2
QKV_W = Q_W + 2 * KV_W
IN_MIX_W = QKV_W + SSM_W
SSM_GC = 16
SSM_G = SSM_W // SSM_GC
SSM_P = 64
D_FF = 4 * D_MODEL
EPS = 1e-6

LANES = 128
CHUNK = 8
N_CHUNKS = SEQ // CHUNK
SG = SSM_W // LANES
GPS = LANES // SSM_GC
SG_IN = CHUNK * LANES
SG_STATE = 2 * GPS * SSM_P
VMEM_LIMIT = 56 * 1024 * 1024
ROW_SUB = 256
CHUNK_PITCH = CHUNK if (CHUNK // 8) % 2 else CHUNK + 8
OUT_SUB = 128
MASKED = float(jnp.finfo(jnp.float32).min)

bf16 = jnp.bfloat16
f32 = jnp.float32


def _dot(a, b):
    return jnp.dot(a, b, preferred_element_type=f32)


def _sigmoid(x):
    return 1.0 / (1.0 + jnp.exp(-x))


def _gelu_tanh(x):
    c = math.sqrt(2.0 / math.pi)
    return x * (0.5 * (1.0 + jnp.tanh(c * (x + 0.044715 * (x * x * x)))))


def _rms_norm(x, g):
    return x * lax.rsqrt(jnp.mean(x * x, axis=-1, keepdims=True) + EPS) * g


def _params(sem):
    return pltpu.CompilerParams(dimension_semantics=sem, vmem_limit_bytes=VMEM_LIMIT)


def _const_spec(shape):
    nd = len(shape)
    return pl.BlockSpec(shape, lambda *_: (0,) * nd, pipeline_mode=pl.Buffered(1))


def _rope_block(z, cos, sin_a, sin_b):
    return (z * cos + pltpu.roll(z, LANES - ROT_DIM // 2, axis=1) * sin_a
            + pltpu.roll(z, ROT_DIM // 2, axis=1) * sin_b)


def _inproj_kernel(x_ref, g_ref, w_ref, wcast_ref, rope_ref, q_ref, kv_ref, u2_ref, h_ref, wg_ref, zu_ref):
    wg_ref[...] = wcast_ref[...].astype(bf16)
    for r in range(x_ref.shape[0] // ROW_SUB):
        rows = pl.ds(r * ROW_SUB, ROW_SUB)
        h = _rms_norm(x_ref[rows, :], g_ref[...]).astype(bf16)
        h_ref[rows, :] = h
        cos, sin_a, sin_b = (rope_ref[rows, t * LANES:(t + 1) * LANES] for t in range(3))
        for half in range(2):
            zq = _dot(h, w_ref[:, half * 512:(half + 1) * 512])
            for b in range(4):
                blk = zq[:, b * LANES:(b + 1) * LANES]
                col = (half * 4 + b) * LANES
                q_ref[rows, col:col + LANES] = _rope_block(blk, cos, sin_a, sin_b).astype(bf16)
        zkv = _dot(h, w_ref[:, Q_W:QKV_W])
        kv_ref[rows, :KV_W] = _rope_block(zkv[:, :KV_W], cos, sin_a, sin_b).astype(bf16)
        kv_ref[rows, KV_W:] = zkv[:, KV_W:].astype(bf16)
        zu = _dot(h, w_ref[:, QKV_W:])
        chunk_rows = pl.ds(r * (ROW_SUB // CHUNK), ROW_SUB // CHUNK)
        for sg in range(SG):
            for n in range(ROW_SUB // CHUNK):
                zu_ref[r, sg, n * CHUNK_PITCH:n * CHUNK_PITCH + CHUNK, :] = (
                    zu[n * CHUNK:(n + 1) * CHUNK, sg * LANES:(sg + 1) * LANES])
            for s in range(CHUNK):
                col = sg * SG_IN + s * LANES
                u2_ref[chunk_rows, col:col + LANES] = (
                    zu_ref[r, sg, pl.ds(s, ROW_SUB // CHUNK, stride=CHUNK_PITCH), :].astype(bf16))


def _inproj(x, gain, wmix, w_in, rope, tm=512):
    L = x.shape[0]
    steps = L // tm
    gate_w = w_in.shape[1] - IN_MIX_W
    slab = gate_w // steps
    assert IN_MIX_W % slab == 0 and slab % LANES == 0
    row = lambda w: pl.BlockSpec((tm, w), lambda i: (i, 0))
    return pl.pallas_call(
        _inproj_kernel,
        grid=(steps,),
        in_specs=[row(D_MODEL), _const_spec((1, D_MODEL)), _const_spec(wmix.shape),
                  pl.BlockSpec((D_MODEL, slab), lambda i: (0, IN_MIX_W // slab + i)),
                  row(rope.shape[1])],
        out_specs=[row(Q_W), row(2 * KV_W),
                   pl.BlockSpec((tm // CHUNK, SG * SG_IN), lambda i: (i, 0)), row(D_MODEL),
                   pl.BlockSpec((D_MODEL, slab), lambda i: (0, i))],
        out_shape=[jax.ShapeDtypeStruct((L, Q_W), bf16), jax.ShapeDtypeStruct((L, 2 * KV_W), bf16),
                   jax.ShapeDtypeStruct((L // CHUNK, SG * SG_IN), bf16),
                   jax.ShapeDtypeStruct((L, D_MODEL), bf16),
                   jax.ShapeDtypeStruct((D_MODEL, gate_w), bf16)],
        scratch_shapes=[pltpu.VMEM((tm // ROW_SUB, SG, ROW_SUB // CHUNK * CHUNK_PITCH, LANES), f32)],
        compiler_params=_params(("parallel",)),
        name="inproj",
    )(x, gain, wmix, w_in, rope)


def _gates_kernel(h_ref, w_ref, *rest):
    n_cast = (len(rest) - 1) // 2
    o_ref = rest[n_cast]
    for r in range(h_ref.shape[0] // ROW_SUB):
        rows = pl.ds(r * ROW_SUB, ROW_SUB)
        o_ref[rows, :] = _sigmoid(_dot(h_ref[rows, :], w_ref[...])).astype(bf16)
    for src, dst in zip(rest[:n_cast], rest[n_cast + 1:]):
        dst[...] = src[...].astype(bf16)


def _gates(h, wg, w_up, row_cast_weights, tm=1024, tn=2048):
    L, N = h.shape[0], wg.shape[1]
    ni = L // tm
    steps = ni * (N // tn)
    step = lambda j, i: j * ni + i
    slabs = [pl.BlockSpec((w_up.shape[0], w_up.shape[1] // steps), lambda j, i: (0, step(j, i)))]
    slabs += [pl.BlockSpec((w.shape[0] // steps, w.shape[1]), lambda j, i: (step(j, i), 0))
              for w in row_cast_weights]
    cast_weights = [w_up] + list(row_cast_weights)
    outs = pl.pallas_call(
        _gates_kernel,
        grid=(N // tn, ni),
        in_specs=[pl.BlockSpec((tm, D_MODEL), lambda j, i: (i, 0)),
                  pl.BlockSpec((D_MODEL, tn), lambda j, i: (0, j))] + slabs,
        out_specs=[pl.BlockSpec((tm, tn), lambda j, i: (i, j))] + slabs,
        out_shape=[jax.ShapeDtypeStruct((L, N), bf16)]
                  + [jax.ShapeDtypeStruct(w.shape, bf16) for w in cast_weights],
        compiler_params=_params(("parallel", "arbitrary")),
        name="gates",
    )(h, wg, *cast_weights)
    return outs[0], outs[1], outs[2:]


GC_SHIFT = SSM_GC.bit_length() - 1
P_SHIFT = SSM_P.bit_length() - 1
Y_TILE = 256
NT = SG_STATE // LANES
NSEG = 8
SEG = N_CHUNKS // NSEG
SEG_PITCH = SEG + 8


def _ssm_kernel(u_ref, pre_ref, pim_ref, kt_ref, qc_ref, rep_ref, ar_ref, ai_ref, y_ref,
                p_scr, m_scr, q_scr, s_scr, xl_scr, pw_scr, xs_scr):
    half = SG_STATE // 2
    row = lax.broadcasted_iota(jnp.int32, (SG_IN, LANES), 0)
    lane = lax.broadcasted_iota(jnp.int32, (SG_IN, LANES), 1)
    row_grp = (row >> GC_SHIFT) & (GPS - 1)
    pre, pim = pre_ref[...], pim_ref[...]
    for k in range(half // LANES):
        sel = row_grp == 2 * k + (lane >> P_SHIFT)
        p_scr[:, k * LANES:(k + 1) * LANES] = jnp.where(sel, pre, 0.0).astype(bf16)
        p_scr[:, half + k * LANES:half + (k + 1) * LANES] = jnp.where(sel, pim, 0.0).astype(bf16)

    blk_row = lax.broadcasted_iota(jnp.int32, (LANES, LANES), 0) >> GC_SHIFT
    blk_lane = lax.broadcasted_iota(jnp.int32, (LANES, LANES), 1) >> GC_SHIFT
    zero_blk = jnp.zeros((LANES, LANES), bf16)
    for tau in range(CHUNK):
        blk = jnp.where(blk_row == blk_lane, kt_ref[tau * LANES:(tau + 1) * LANES, :], 0.0).astype(bf16)
        for s in range(CHUNK - tau):
            t = s + tau
            m_scr[s * LANES:(s + 1) * LANES, t * LANES:(t + 1) * LANES] = blk
            if tau > 0:
                m_scr[t * LANES:(t + 1) * LANES, s * LANES:(s + 1) * LANES] = zero_blk

    q_row_grp = (lax.broadcasted_iota(jnp.int32, (SG_STATE, Y_TILE), 0) >> P_SHIFT) & (GPS - 1)
    q_lane_grp = (lax.broadcasted_iota(jnp.int32, (SG_STATE, Y_TILE), 1) >> GC_SHIFT) & (GPS - 1)
    for j in range(SG_IN // Y_TILE):
        full = _dot(qc_ref[...], rep_ref[:, j * Y_TILE:(j + 1) * Y_TILE])
        q_scr[:, j * Y_TILE:(j + 1) * Y_TILE] = jnp.where(q_row_grp == q_lane_grp, full, 0.0).astype(bf16)

    s = _dot(u_ref[...], p_scr[...])
    for j in range(NT):
        for k in range(NSEG):
            s_scr[j, k * SEG_PITCH:k * SEG_PITCH + SEG, :] = s[k * SEG:(k + 1) * SEG, j * LANES:(j + 1) * LANES]
    ht = NT // 2
    cmul = lambda xr, xi, yr, yi: (xr * yr - xi * yi, xr * yi + xi * yr)
    ar = [ar_ref[:, j * LANES:(j + 1) * LANES] for j in range(ht)]
    ai = [ai_ref[:, j * LANES:(j + 1) * LANES] for j in range(ht)]
    one, nil = jnp.ones((1, LANES), f32), jnp.zeros((1, LANES), f32)
    a_seg = []
    for j in range(ht):
        rr, ri = [one], [nil]
        for _ in range(7):
            nr, ni = cmul(rr[-1], ri[-1], ar[j], ai[j])
            rr.append(nr)
            ri.append(ni)
        a8r, a8i = cmul(rr[-1], ri[-1], ar[j], ai[j])
        blk_r, blk_i = jnp.concatenate(rr, axis=0), jnp.concatenate(ri, axis=0)
        for b in range(SEG // 8):
            pw_scr[j, b * 8:(b + 1) * 8, :] = blk_r
            pw_scr[ht + j, b * 8:(b + 1) * 8, :] = blk_i
            blk_r, blk_i = cmul(blk_r, blk_i, a8r, a8i)
        a_seg.append((blk_r[0:1], blk_i[0:1]))

    ar_b = [jnp.broadcast_to(a, (NSEG, LANES)) for a in ar]
    ai_b = [jnp.broadcast_to(a, (NSEG, LANES)) for a in ai]

    def body(i, carry):
        new_r, new_i = [], []
        for j in range(ht):
            xr, xi = carry[j], carry[ht + j]
            seg_rows = pl.ds(i, NSEG, stride=SEG_PITCH)
            xl_scr[j, seg_rows, :] = xr
            xl_scr[ht + j, seg_rows, :] = xi
            pr, pi = cmul(xr, xi, ar_b[j], ai_b[j])
            new_r.append(pr + s_scr[j, seg_rows, :])
            new_i.append(pi + s_scr[ht + j, seg_rows, :])
        return tuple(new_r + new_i)

    ends = tuple(jnp.zeros((NSEG, LANES), f32) for _ in range(NT))
    for i in range(SEG):
        ends = body(i, ends)

    for j in range(ht):
        zr, zi = nil, nil
        for k in range(NSEG):
            rows = slice(k * SEG, (k + 1) * SEG)
            loc = slice(k * SEG_PITCH, k * SEG_PITCH + SEG)
            fr, fi = cmul(pw_scr[j], pw_scr[ht + j], zr, zi)
            xs_scr[rows, j * LANES:(j + 1) * LANES] = (xl_scr[j, loc, :] + fr).astype(bf16)
            xs_scr[rows, half + j * LANES:half + (j + 1) * LANES] = (xl_scr[ht + j, loc, :] + fi).astype(bf16)
            pr, pi = cmul(zr, zi, *a_seg[j])
            zr, zi = pr + ends[j][k:k + 1], pi + ends[ht + j][k:k + 1]

    xs = xs_scr[...]
    for j in range(SG_IN // Y_TILE):
        kk = (j + 1) * Y_TILE
        cols = slice(j * Y_TILE, (j + 1) * Y_TILE)
        y = _dot(u_ref[:, :kk], m_scr[:kk, cols]) + _dot(xs, q_scr[:, cols])
        y_ref[:, cols] = _gelu_tanh(y).astype(bf16)


def _ssm(u2, pre, pim, kt, qc, rep, at_re, at_im):
    per_sg = lambda r, c: pl.BlockSpec((None, r, c), lambda g: (g, 0, 0))
    return pl.pallas_call(
        _ssm_kernel,
        grid=(SG,),
        in_specs=[pl.BlockSpec((N_CHUNKS, SG_IN), lambda g: (0, g)),
                  per_sg(SG_IN, LANES), per_sg(SG_IN, LANES), per_sg(SG_IN, LANES),
                  per_sg(SG_STATE, CHUNK * SSM_GC), _const_spec(rep.shape),
                  per_sg(1, SG_STATE // 2), per_sg(1, SG_STATE // 2)],
        out_specs=pl.BlockSpec((N_CHUNKS, SG_IN), lambda g: (0, g)),
        out_shape=jax.ShapeDtypeStruct((N_CHUNKS, SG * SG_IN), bf16),
        scratch_shapes=[pltpu.VMEM((SG_IN, SG_STATE), bf16), pltpu.VMEM((SG_IN, SG_IN), bf16),
                        pltpu.VMEM((SG_STATE, SG_IN), bf16),
                        pltpu.VMEM((NT, NSEG * SEG_PITCH, LANES), f32),
                        pltpu.VMEM((NT, NSEG * SEG_PITCH, LANES), f32),
                        pltpu.VMEM((NT, SEG, LANES), f32), pltpu.VMEM((N_CHUNKS, SG_STATE), bf16)],
        compiler_params=_params(("arbitrary",)),
        name="ssm",
    )(u2, pre, pim, kt, qc, rep, at_re, at_im)


def _ssm_prep_kernel(lam_re_ref, lam_im_ref, ldt_ref, bt_re_ref, bt_im_ref, c_re_ref, c_im_ref, d_ref,
                     eye_ref, win_ref, pre_ref, pim_ref, kt_ref, qc_ref, at_re_ref, at_im_ref, wmix_ref):
    wmix_ref[...] = win_ref[...].astype(bf16)
    dup = lambda x: jnp.concatenate([x, x], axis=1)
    lr, li = dup(lam_re_ref[...]), dup(lam_im_ref[...])
    dt = jnp.exp(ldt_ref[...])
    mag = jnp.exp(lr * dt)
    a_re, a_im = mag * jnp.cos(li * dt), mag * jnp.sin(li * dt)
    den = lr * lr + li * li
    nr, ni = a_re - 1.0, a_im
    coef_re = (nr * lr + ni * li) / den
    coef_im = (ni * lr - nr * li) / den
    pw_re, pw_im = [jnp.ones_like(a_re)], [jnp.zeros_like(a_im)]
    for _ in range(CHUNK):
        pr, pi = pw_re[-1], pw_im[-1]
        pw_re.append(pr * a_re - pi * a_im)
        pw_im.append(pr * a_im + pi * a_re)

    low = lax.broadcasted_iota(jnp.int32, (SSM_GC, LANES), 1) < SSM_P
    diag = ((lax.broadcasted_iota(jnp.int32, (SSM_GC, LANES), 1) & (SSM_GC - 1))
            == lax.broadcasted_iota(jnp.int32, (SSM_GC, LANES), 0))
    contract_lanes = (((1,), (1,)), ((), ()))
    for g in range(GPS):
        grp = slice(g * SSM_GC, (g + 1) * SSM_GC)
        one = lambda x: x[g:g + 1]
        bt_re, bt_im = dup(bt_re_ref[grp, :]), dup(bt_im_ref[grp, :])
        bb_re = one(coef_re) * bt_re - one(coef_im) * bt_im
        bb_im = one(coef_re) * bt_im + one(coef_im) * bt_re
        yc = jnp.concatenate([c_re_ref[grp, :], -c_im_ref[grp, :]], axis=1)
        yc_sw = pltpu.roll(yc, SSM_P, axis=1)
        y2 = jnp.where(low, yc_sw, -yc_sw)
        x_rows, ca_rows = [], []
        for tau in range(CHUNK):
            pr, pi = one(pw_re[tau]), one(pw_im[tau])
            ab_re = bb_re * pr - bb_im * pi
            ab_im = bb_re * pi + bb_im * pr
            s = CHUNK - 1 - tau
            rows = slice(s * LANES + g * SSM_GC, s * LANES + (g + 1) * SSM_GC)
            pre_ref[rows, :] = ab_re
            pim_ref[rows, :] = ab_im
            x_rows.append(jnp.where(low, ab_re, ab_im))
            ca_rows.append(yc * one(pw_re[tau + 1]) + y2 * one(pw_im[tau + 1]))
        kern = lax.dot_general(jnp.concatenate(x_rows, axis=0), jnp.concatenate([yc] * GPS, axis=0),
                               contract_lanes, precision=lax.Precision.HIGHEST,
                               preferred_element_type=f32)
        for tau in range(CHUNK):
            blk = kern[tau * SSM_GC:(tau + 1) * SSM_GC]
            if tau == 0:
                blk = blk + jnp.where(diag, d_ref[...], 0.0)
            kt_ref[tau * LANES + g * SSM_GC:tau * LANES + (g + 1) * SSM_GC, :] = blk
        qt = lax.dot_general(eye_ref[...], jnp.concatenate(ca_rows, axis=0).astype(bf16),
                             contract_lanes, preferred_element_type=f32)
        qc_ref[g * SSM_P:(g + 1) * SSM_P, :] = qt[:SSM_P].astype(bf16)
        qc_ref[GPS * SSM_P + g * SSM_P:GPS * SSM_P + (g + 1) * SSM_P, :] = qt[SSM_P:].astype(bf16)
    lo_lane = lax.broadcasted_iota(jnp.int32, (1, LANES), 1) < SSM_P
    for k in range(GPS // 2):
        sel = lambda pw: jnp.where(lo_lane, pw[2 * k:2 * k + 1], pw[2 * k + 1:2 * k + 2])
        at_re_ref[:, k * LANES:(k + 1) * LANES] = sel(pw_re[CHUNK])
        at_im_ref[:, k * LANES:(k + 1) * LANES] = sel(pw_im[CHUNK])


def _ssm_operators(lam_re, lam_im, log_dt, b_re, b_im, c_re, c_im, d_skip, w_in):
    per_sg = lambda x: x.astype(f32).reshape((SG, x.size // (SG * x.shape[-1]), x.shape[-1]))
    bt = lambda b: per_sg(b.swapaxes(1, 2))
    args = (per_sg(lam_re), per_sg(lam_im), log_dt.astype(f32).reshape(SG, GPS, 1),
            bt(b_re), bt(b_im), per_sg(c_re), per_sg(c_im), d_skip.astype(f32).reshape(SG, 1, LANES),
            jnp.asarray(np.eye(LANES), bf16), w_in)
    blk = lambda r, c: pl.BlockSpec((None, r, c), lambda g: (g, 0, 0))
    slab = pl.BlockSpec((w_in.shape[0] // SG, IN_MIX_W), lambda g: (g, 0))
    return pl.pallas_call(
        _ssm_prep_kernel,
        grid=(SG,),
        in_specs=[blk(GPS, SSM_P), blk(GPS, SSM_P), blk(GPS, 1), blk(LANES, SSM_P), blk(LANES, SSM_P),
                  blk(LANES, SSM_P), blk(LANES, SSM_P), blk(1, LANES), _const_spec((LANES, LANES)), slab],
        out_specs=[blk(SG_IN, LANES), blk(SG_IN, LANES), blk(SG_IN, LANES),
                   blk(SG_STATE, CHUNK * SSM_GC), blk(1, SG_STATE // 2), blk(1, SG_STATE // 2),
                   slab],
        out_shape=[jax.ShapeDtypeStruct((SG, SG_IN, LANES), f32)] * 3
                  + [jax.ShapeDtypeStruct((SG, SG_STATE, CHUNK * SSM_GC), bf16)]
                  + [jax.ShapeDtypeStruct((SG, 1, SG_STATE // 2), f32)] * 2
                  + [jax.ShapeDtypeStruct((w_in.shape[0], IN_MIX_W), bf16)],
        compiler_params=_params(("parallel",)),
        name="ssm_prep",
    )(*args)


def _replication_matrix():
    src = np.arange(CHUNK * SSM_GC)
    dst = np.arange(SG_IN)
    same_t = (src[:, None] // SSM_GC) == (dst[None, :] // LANES)
    same_c = (src[:, None] % SSM_GC) == (dst[None, :] % SSM_GC)
    return jnp.asarray(same_t & same_c, bf16)


def _kv_operands(kvp_ref, kvc_ref):
    kcat = jnp.concatenate([kvp_ref[:, :KV_W], kvc_ref[:, :KV_W]], axis=0).astype(f32)
    vcat = jnp.concatenate([kvp_ref[:, KV_W:], kvc_ref[:, KV_W:]], axis=0).astype(f32)
    k_sw = pltpu.roll(kcat, HEAD_DIM, axis=1)
    v_sw = pltpu.roll(vcat, HEAD_DIM, axis=1)
    low_kv = lax.broadcasted_iota(jnp.int32, kcat.shape, 1) < HEAD_DIM
    operands = []
    for hk in range(N_KV_HEADS):
        k_src, k_oth = (kcat, k_sw) if hk == 0 else (k_sw, kcat)
        k_lo = jnp.where(low_kv, k_src, 0.0).astype(bf16)
        k_hi = jnp.where(low_kv, 0.0, k_oth).astype(bf16)
        vv = (jnp.where(low_kv, vcat, v_sw) if hk == 0 else jnp.where(low_kv, v_sw, vcat)).astype(bf16)
        operands.append((k_lo, k_hi, vv))
    return operands


def _attention_block(q, kv_operands, sinks_ref, blk, no_prev, oa_ref):
    q_rows = slice(blk * BLOCK, (blk + 1) * BLOCK)
    kv_rows = slice(blk * BLOCK, (blk + 2) * BLOCK)
    from_prev = (lax.broadcasted_iota(jnp.int32, (BLOCK, BLOCK), 1)
                 > lax.broadcasted_iota(jnp.int32, (BLOCK, BLOCK), 0))
    low_o = lax.broadcasted_iota(jnp.int32, (BLOCK, LANES), 1) < HEAD_DIM
    contract_lanes = (((1,), (1,)), ((), ()))
    pairs_per_kv = (N_Q_HEADS // N_KV_HEADS) // 2
    for hk, (k_lo, k_hi, vv) in enumerate(kv_operands):
        qs = jnp.concatenate(
            [q[q_rows, (pairs_per_kv * hk + b) * LANES:(pairs_per_kv * hk + b + 1) * LANES]
             for b in range(pairs_per_kv)], axis=0)
        s_par = (lax.dot_general(qs, k_lo[kv_rows], contract_lanes, preferred_element_type=f32),
                 lax.dot_general(qs, k_hi[kv_rows], contract_lanes, preferred_element_type=f32))
        for b in range(pairs_per_kv):
            outs = []
            for par in range(2):
                head = 2 * (pairs_per_kv * hk + b) + par
                s_all = s_par[par][b * BLOCK:(b + 1) * BLOCK]
                s_prev = s_all[:, :BLOCK] + no_prev if blk == 0 else s_all[:, :BLOCK]
                s = jnp.where(from_prev, s_prev, s_all[:, BLOCK:])
                sink = sinks_ref[head]
                m = jnp.maximum(jnp.max(s, axis=1, keepdims=True), sink)
                p = jnp.exp(s - m)
                denom = jnp.sum(p, axis=1, keepdims=True) + jnp.exp(sink - m)
                p_kv = jnp.concatenate([jnp.where(from_prev, p, 0.0), jnp.where(from_prev, 0.0, p)],
                                       axis=1).astype(bf16)
                outs.append(_dot(p_kv, vv[kv_rows]) * (1.0 / denom))
            col = (pairs_per_kv * hk + b) * LANES
            oa_ref[q_rows, col:col + LANES] = jnp.where(low_o, outs[0], outs[1]).astype(bf16)


def _attn_mix_kernel(sinks_ref, q_ref, kvp_ref, kvc_ref, y2_ref, gates_ref,
                     wglu_ref, wb_ref, wdn_ref, m_ref, wdn_o_ref, oa_ref, yb_ref):
    wdn_o_ref[...] = wdn_ref[...].astype(bf16)
    no_prev = jnp.where(pl.program_id(0) > 0, 0.0, MASKED)
    q = q_ref[...] * jnp.asarray(1.0 / math.sqrt(HEAD_DIM), bf16)
    kv_operands = _kv_operands(kvp_ref, kvc_ref)
    for r in range(q_ref.shape[0] // ROW_SUB):
        rows = pl.ds(r * ROW_SUB, ROW_SUB)
        chunk_rows = pl.ds(r * (ROW_SUB // CHUNK), ROW_SUB // CHUNK)
        for sg in range(SG):
            for t in range(CHUNK):
                col = sg * SG_IN + t * LANES
                yb_ref[r, sg, pl.ds(t, ROW_SUB // CHUNK, stride=CHUNK_PITCH), :] = (
                    y2_ref[chunk_rows, col:col + LANES].astype(f32))
        y = jnp.concatenate(
            [jnp.concatenate([yb_ref[r, sg, n * CHUNK_PITCH:n * CHUNK_PITCH + CHUNK, :]
                              for n in range(ROW_SUB // CHUNK)], axis=0) for sg in range(SG)],
            axis=1).astype(bf16)
        zg = _dot(y, wglu_ref[...])
        o_ssm = zg[:, :SSM_W] * _sigmoid(zg[:, SSM_W:])
        y_s = _dot(o_ssm.astype(bf16), wb_ref[Q_W:, :])
        for blk in range(r * (ROW_SUB // BLOCK), (r + 1) * (ROW_SUB // BLOCK)):
            _attention_block(q, kv_operands, sinks_ref, blk, no_prev, oa_ref)
        y_a = _dot(oa_ref[rows, :], wb_ref[:Q_W, :])
        m_ref[rows, :] = (gates_ref[rows, :D_MODEL].astype(f32) * y_a
                          + gates_ref[rows, D_MODEL:].astype(f32) * y_s).astype(bf16)


def _attn_mix(q, kv, sinks, y2, gates, wglu, wb, w_down, tm=512):
    L = q.shape[0]
    cur = lambda i: (i, 0)
    prev = lambda i: (jnp.maximum(i * (tm // BLOCK) - 1, 0), 0)
    dn_slab = pl.BlockSpec((w_down.shape[0] // (L // tm), w_down.shape[1]), cur)
    return pl.pallas_call(
        _attn_mix_kernel,
        grid=(L // tm,),
        in_specs=[pl.BlockSpec(memory_space=pltpu.SMEM),
                  pl.BlockSpec((tm, Q_W), cur),
                  pl.BlockSpec((BLOCK, 2 * KV_W), prev), pl.BlockSpec((tm, 2 * KV_W), cur),
                  pl.BlockSpec((tm // CHUNK, SG * SG_IN), cur),
                  pl.BlockSpec((tm, 2 * D_MODEL), cur),
                  _const_spec(wglu.shape), _const_spec(wb.shape), dn_slab],
        out_specs=[pl.BlockSpec((tm, D_MODEL), cur), dn_slab],
        out_shape=[jax.ShapeDtypeStruct((L, D_MODEL), bf16), jax.ShapeDtypeStruct(w_down.shape, bf16)],
        scratch_shapes=[pltpu.VMEM((tm, Q_W), bf16),
                        pltpu.VMEM((tm // ROW_SUB, SG, ROW_SUB // CHUNK * CHUNK_PITCH, LANES), f32)],
        compiler_params=_params(("parallel",)),
        name="attn_mix",
    )(sinks, q, kv, kv, y2, gates, wglu, wb, w_down)


def _outproj_kernel(m_ref, w_ref, x_ref, gpost_ref, gpre_ref, x1_ref, h2_ref):
    for r in range(m_ref.shape[0] // OUT_SUB):
        rows = pl.ds(r * OUT_SUB, OUT_SUB)
        out = _dot(m_ref[rows, :], w_ref[...])
        x1 = x_ref[rows, :] + _rms_norm(out, gpost_ref[...])
        x1_ref[rows, :] = x1
        h2_ref[rows, :] = _rms_norm(x1, gpre_ref[...]).astype(bf16)


def _outproj(m, w_out, x, g_post, g_pre, tm=512):
    L = x.shape[0]
    row = lambda: pl.BlockSpec((tm, D_MODEL), lambda i: (i, 0))
    return pl.pallas_call(
        _outproj_kernel,
        grid=(L // tm,),
        in_specs=[row(), _const_spec(w_out.shape), row(),
                  _const_spec((1, D_MODEL)), _const_spec((1, D_MODEL))],
        out_specs=[row(), row()],
        out_shape=[jax.ShapeDtypeStruct((L, D_MODEL), f32), jax.ShapeDtypeStruct((L, D_MODEL), bf16)],
        compiler_params=_params(("parallel",)),
        name="outproj",
    )(m, w_out, x, g_post, g_pre)


MLP_SUB = 512


def _mlp_kernel(h_ref, wu_ref, wd_ref, x_ref, g_ref, o_ref, acc_ref):
    i, j = pl.program_id(0), pl.program_id(1)
    n_tiles = pl.num_programs(0) - 1
    tm = h_ref.shape[0]
    slab = o_ref.shape[0]
    cur, prev = acc_ref.at[i % 2], acc_ref.at[(i + 1) % 2]
    slab_rows = pl.ds(pl.multiple_of(j * slab, slab), slab)

    def finish_prev_slab():
        o_ref[...] = x_ref[...] + _rms_norm(prev[slab_rows, :], g_ref[...])
        prev[slab_rows, :] = jnp.zeros((slab, D_MODEL), f32)

    @pl.when((i == 0) & (j == 0))
    def _():
        acc_ref[...] = jnp.zeros_like(acc_ref)

    @pl.when(i < n_tiles)
    def _():
        for r in range(tm // MLP_SUB):
            rows = pl.ds(r * MLP_SUB, MLP_SUB)
            a = jnp.maximum(_dot(h_ref[rows, :], wu_ref[...]), 0.0)
            cur[rows, :] += _dot((a * a).astype(bf16), wd_ref[...])
        finish_prev_slab()

    @pl.when(i == n_tiles)
    def _():
        finish_prev_slab()


def _mlp(h2, w_up, w_down, x1, g_post, tm=1024, tf=1024):
    L = x1.shape[0]
    n_tiles, n_ff = L // tm, D_FF // tf
    slab = tm // n_ff
    ff = lambda i, j: jnp.where(i < n_tiles, j, n_ff - 1)
    prev_slab = lambda i, j: (jnp.maximum(i - 1, 0) * n_ff + jnp.where(i > 0, j, 0), 0)
    return pl.pallas_call(
        _mlp_kernel,
        grid=(n_tiles + 1, n_ff),
        in_specs=[pl.BlockSpec((tm, D_MODEL), lambda i, j: (jnp.minimum(i, n_tiles - 1), 0)),
                  pl.BlockSpec((D_MODEL, tf), lambda i, j: (0, ff(i, j))),
                  pl.BlockSpec((tf, D_MODEL), lambda i, j: (ff(i, j), 0)),
                  pl.BlockSpec((slab, D_MODEL), prev_slab),
                  _const_spec((1, D_MODEL))],
        out_specs=pl.BlockSpec((slab, D_MODEL), prev_slab),
        out_shape=jax.ShapeDtypeStruct((L, D_MODEL), f32),
        scratch_shapes=[pltpu.VMEM((2, tm, D_MODEL), f32)],
        compiler_params=_params(("arbitrary", "arbitrary")),
        name="mlp",
    )(h2, w_up, w_down, x1, g_post)


def _rope_tables(L):
    half = ROT_DIM // 2
    inv = ROPE_THETA ** (-np.arange(half, dtype=np.float64) * 2.0 / ROT_DIM)
    ang = np.arange(L, dtype=np.float64)[:, None] * inv[None, :]
    cos, sin = np.cos(ang).astype(np.float32), np.sin(ang).astype(np.float32)
    ones = np.ones((L, HEAD_DIM - ROT_DIM), np.float32)
    zeros = np.zeros((L, HEAD_DIM - ROT_DIM), np.float32)
    zh = np.zeros((L, half), np.float32)
    per_head = lambda parts: np.tile(np.concatenate(parts, axis=1), (1, LANES // HEAD_DIM))
    return jnp.asarray(np.concatenate(
        [per_head([cos, cos, ones]), per_head([-sin, zh, zeros]), per_head([zh, sin, zeros])], axis=1))


def _layer(x, norm_mix_pre, norm_mix_post, norm_mlp_pre, norm_mlp_post, w_in, sinks,
           lam_re, lam_im, log_dt, b_re, b_im, c_re, c_im, d_skip, w_glu, w_branch, w_out,
           w_up, w_down, rope):
    gain = lambda g: g.astype(f32).reshape(1, D_MODEL)
    pre, pim, kt, qc, at_re, at_im, wmix = _ssm_operators(
        lam_re, lam_im, log_dt, b_re, b_im, c_re, c_im, d_skip, w_in.astype(f32))
    q, kv, u2, h, wg = _inproj(x, gain(norm_mix_pre), wmix, w_in.astype(f32), rope)
    gates, wup, (wglu, wb, wo) = _gates(h, wg, w_up.astype(f32),
                                        [w_glu.astype(f32), w_branch.astype(f32), w_out.astype(f32)])
    y2 = _ssm(u2, pre, pim, kt, qc, _replication_matrix(), at_re, at_im)
    mixed, wdn = _attn_mix(q, kv, sinks.astype(f32), y2, gates, wglu, wb, w_down.astype(f32))
    x1, h2 = _outproj(mixed, wo, x, gain(norm_mix_post), gain(norm_mlp_pre))
    return _mlp(h2, wup, wdn, x1, gain(norm_mlp_post))


def kernel(x, norm_mix_pre, norm_mix_post, norm_mlp_pre, norm_mlp_post, w_in, sinks, lam_re, lam_im, log_dt, b_re, b_im, c_re, c_im, d_skip, w_glu, w_branch, w_out, w_up, w_down):
    B, L, _ = x.shape
    depth = w_in.shape[0]
    rope = _rope_tables(L)
    outs = []
    for b in range(B):
        xb = x[b]
        for l in range(depth):
            xb = _layer(xb, norm_mix_pre[l], norm_mix_post[l], norm_mlp_pre[l], norm_mlp_post[l],
                        w_in[l], sinks[l], lam_re[l], lam_im[l], log_dt[l], b_re[l], b_im[l],
                        c_re[l], c_im[l], d_skip[l], w_glu[l], w_branch[l], w_out[l],
                        w_up[l], w_down[l], rope)
        outs.append(xb)
    return jnp.stack(outs)
```

```python
import math

import jax
import jax.numpy as jnp
import numpy as np
from jax import lax
from jax.experimental import pallas as pl
from jax.experimental.pallas import tpu as pltpu

D_MODEL = 2048
SEQ = 8192
HEAD_DIM = 64
N_Q_HEADS = 16
N_KV_HEADS = 2
BLOCK = 128
ROT_DIM = HEAD_DIM // 4
ROPE_THETA = 500000.0
Q_W = N_Q_HEADS * HEAD_DIM
KV_W = N_KV_HEADS * HEAD_DIM
SSM_W = D_MODEL // 2
QKV_W = Q_W + 2 * KV_W
IN_MIX_W = QKV_W + SSM_W
SSM_GC = 16
SSM_G = SSM_W // SSM_GC
SSM_P = 64
D_FF = 4 * D_MODEL
EPS = 1e-6

LANES = 128
CHUNK = 8
N_CHUNKS = SEQ // CHUNK
SG = SSM_W // LANES
GPS = LANES // SSM_GC
SG_IN = CHUNK * LANES
SG_STATE = 2 * GPS * SSM_P
VMEM_LIMIT = 56 * 1024 * 1024
ROW_SUB = 256
CHUNK_PITCH = CHUNK if (CHUNK // 8) % 2 else CHUNK + 8
OUT_SUB = 128
MASKED = float(jnp.finfo(jnp.float32).min)

bf16 = jnp.bfloat16
f32 = jnp.float32


def _dot(a, b):
    return jnp.dot(a, b, preferred_element_type=f32)


def _sigmoid(x):
    return 1.0 / (1.0 + jnp.exp(-x))


def _gelu_tanh(x):
    c = math.sqrt(2.0 / math.pi)
    return x * (0.5 * (1.0 + jnp.tanh(c * (x + 0.044715 * (x * x * x)))))


def _rms_norm(x, g):
    return x * lax.rsqrt(jnp.mean(x * x, axis=-1, keepdims=True) + EPS) * g


def _params(sem):
    return pltpu.CompilerParams(dimension_semantics=sem, vmem_limit_bytes=VMEM_LIMIT)


def _const_spec(shape):
    nd = len(shape)
    return pl.BlockSpec(shape, lambda *_: (0,) * nd, pipeline_mode=pl.Buffered(1))


def _rope_block(z, cos, sin_a, sin_b):
    return (z * cos + pltpu.roll(z, LANES - ROT_DIM // 2, axis=1) * sin_a
            + pltpu.roll(z, ROT_DIM // 2, axis=1) * sin_b)


def _inproj_kernel(x_ref, g_ref, w_ref, wcast_ref, rope_ref, q_ref, kv_ref, u2_ref, h_ref, wg_ref, zu_ref):
    wg_ref[...] = wcast_ref[...].astype(bf16)
    for r in range(x_ref.shape[0] // ROW_SUB):
        rows = pl.ds(r * ROW_SUB, ROW_SUB)
        h = _rms_norm(x_ref[rows, :], g_ref[...]).astype(bf16)
        h_ref[rows, :] = h
        cos, sin_a, sin_b = (rope_ref[rows, t * LANES:(t + 1) * LANES] for t in range(3))
        for half in range(2):
            zq = _dot(h, w_ref[:, half * 512:(half + 1) * 512])
            for b in range(4):
                blk = zq[:, b * LANES:(b + 1) * LANES]
                col = (half * 4 + b) * LANES
                q_ref[rows, col:col + LANES] = _rope_block(blk, cos, sin_a, sin_b).astype(bf16)
        zkv = _dot(h, w_ref[:, Q_W:QKV_W])
        kv_ref[rows, :KV_W] = _rope_block(zkv[:, :KV_W], cos, sin_a, sin_b).astype(bf16)
        kv_ref[rows, KV_W:] = zkv[:, KV_W:].astype(bf16)
        zu = _dot(h, w_ref[:, QKV_W:])
        chunk_rows = pl.ds(r * (ROW_SUB // CHUNK), ROW_SUB // CHUNK)
        for sg in range(SG):
            for n in range(ROW_SUB // CHUNK):
                zu_ref[r, sg, n * CHUNK_PITCH:n * CHUNK_PITCH + CHUNK, :] = (
                    zu[n * CHUNK:(n + 1) * CHUNK, sg * LANES:(sg + 1) * LANES])
            for s in range(CHUNK):
                col = sg * SG_IN + s * LANES
                u2_ref[chunk_rows, col:col + LANES] = (
                    zu_ref[r, sg, pl.ds(s, ROW_SUB // CHUNK, stride=CHUNK_PITCH), :].astype(bf16))


def _inproj(x, gain, wmix, w_in, rope, tm=512):
    L = x.shape[0]
    steps = L // tm
    gate_w = w_in.shape[1] - IN_MIX_W
    slab = gate_w // steps
    assert IN_MIX_W % slab == 0 and slab % LANES == 0
    row = lambda w: pl.BlockSpec((tm, w), lambda i: (i, 0))
    return pl.pallas_call(
        _inproj_kernel,
        grid=(steps,),
        in_specs=[row(D_MODEL), _const_spec((1, D_MODEL)), _const_spec(wmix.shape),
                  pl.BlockSpec((D_MODEL, slab), lambda i: (0, IN_MIX_W // slab + i)),
                  row(rope.shape[1])],
        out_specs=[row(Q_W), row(2 * KV_W),
                   pl.BlockSpec((tm // CHUNK, SG * SG_IN), lambda i: (i, 0)), row(D_MODEL),
                   pl.BlockSpec((D_MODEL, slab), lambda i: (0, i))],
        out_shape=[jax.ShapeDtypeStruct((L, Q_W), bf16), jax.ShapeDtypeStruct((L, 2 * KV_W), bf16),
                   jax.ShapeDtypeStruct((L // CHUNK, SG * SG_IN), bf16),
                   jax.ShapeDtypeStruct((L, D_MODEL), bf16),
                   jax.ShapeDtypeStruct((D_MODEL, gate_w), bf16)],
        scratch_shapes=[pltpu.VMEM((tm // ROW_SUB, SG, ROW_SUB // CHUNK * CHUNK_PITCH, LANES), f32)],
        compiler_params=_params(("parallel",)),
        name="inproj",
    )(x, gain, wmix, w_in, rope)


def _gates_kernel(h_ref, w_ref, *rest):
    n_cast = (len(rest) - 1) // 2
    o_ref = rest[n_cast]
    for r in range(h_ref.shape[0] // ROW_SUB):
        rows = pl.ds(r * ROW_SUB, ROW_SUB)
        o_ref[rows, :] = _sigmoid(_dot(h_ref[rows, :], w_ref[...])).astype(bf16)
    for src, dst in zip(rest[:n_cast], rest[n_cast + 1:]):
        dst[...] = src[...].astype(bf16)


def _gates(h, wg, w_up, row_cast_weights, tm=1024, tn=2048):
    L, N = h.shape[0], wg.shape[1]
    ni = L // tm
    steps = ni * (N // tn)
    step = lambda j, i: j * ni + i
    slabs = [pl.BlockSpec((w_up.shape[0], w_up.shape[1] // steps), lambda j, i: (0, step(j, i)))]
    slabs += [pl.BlockSpec((w.shape[0] // steps, w.shape[1]), lambda j, i: (step(j, i), 0))
              for w in row_cast_weights]
    cast_weights = [w_up] + list(row_cast_weights)
    outs = pl.pallas_call(
        _gates_kernel,
        grid=(N // tn, ni),
        in_specs=[pl.BlockSpec((tm, D_MODEL), lambda j, i: (i, 0)),
                  pl.BlockSpec((D_MODEL, tn), lambda j, i: (0, j))] + slabs,
        out_specs=[pl.BlockSpec((tm, tn), lambda j, i: (i, j))] + slabs,
        out_shape=[jax.ShapeDtypeStruct((L, N), bf16)]
                  + [jax.ShapeDtypeStruct(w.shape, bf16) for w in cast_weights],
        compiler_params=_params(("parallel", "arbitrary")),
        name="gates",
    )(h, wg, *cast_weights)
    return outs[0], outs[1], outs[2:]


GC_SHIFT = SSM_GC.bit_length() - 1
P_SHIFT = SSM_P.bit_length() - 1
Y_TILE = 256
NT = SG_STATE // LANES
NSEG = 16
SEG = N_CHUNKS // NSEG
SEG_PITCH = SEG + 8


def _ssm_kernel(u_ref, pre_ref, pim_ref, kt_ref, qc_ref, rep_ref, ar_ref, ai_ref, y_ref,
                p_scr, m_scr, q_scr, s_scr, xl_scr, pw_scr, xs_scr):
    half = SG_STATE // 2
    row = lax.broadcasted_iota(jnp.int32, (SG_IN, LANES), 0)
    lane = lax.broadcasted_iota(jnp.int32, (SG_IN, LANES), 1)
    row_grp = (row >> GC_SHIFT) & (GPS - 1)
    pre, pim = pre_ref[...], pim_ref[...]
    for k in range(half // LANES):
        sel = row_grp == 2 * k + (lane >> P_SHIFT)
        p_scr[:, k * LANES:(k + 1) * LANES] = jnp.where(sel, pre, 0.0).astype(bf16)
        p_scr[:, half + k * LANES:half + (k + 1) * LANES] = jnp.where(sel, pim, 0.0).astype(bf16)

    blk_row = lax.broadcasted_iota(jnp.int32, (LANES, LANES), 0) >> GC_SHIFT
    blk_lane = lax.broadcasted_iota(jnp.int32, (LANES, LANES), 1) >> GC_SHIFT
    zero_blk = jnp.zeros((LANES, LANES), bf16)
    for tau in range(CHUNK):
        blk = jnp.where(blk_row == blk_lane, kt_ref[tau * LANES:(tau + 1) * LANES, :], 0.0).astype(bf16)
        for s in range(CHUNK - tau):
            t = s + tau
            m_scr[s * LANES:(s + 1) * LANES, t * LANES:(t + 1) * LANES] = blk
            if tau > 0:
                m_scr[t * LANES:(t + 1) * LANES, s * LANES:(s + 1) * LANES] = zero_blk

    q_row_grp = (lax.broadcasted_iota(jnp.int32, (SG_STATE, Y_TILE), 0) >> P_SHIFT) & (GPS - 1)
    q_lane_grp = (lax.broadcasted_iota(jnp.int32, (SG_STATE, Y_TILE), 1) >> GC_SHIFT) & (GPS - 1)
    for j in range(SG_IN // Y_TILE):
        full = _dot(qc_ref[...], rep_ref[:, j * Y_TILE:(j + 1) * Y_TILE])
        q_scr[:, j * Y_TILE:(j + 1) * Y_TILE] = jnp.where(q_row_grp == q_lane_grp, full, 0.0).astype(bf16)

    s = _dot(u_ref[...], p_scr[...])
    for j in range(NT):
        for k in range(NSEG):
            s_scr[j, k * SEG_PITCH:k * SEG_PITCH + SEG, :] = s[k * SEG:(k + 1) * SEG, j * LANES:(j + 1) * LANES]
    ht = NT // 2
    cmul = lambda xr, xi, yr, yi: (xr * yr - xi * yi, xr * yi + xi * yr)
    ar = [ar_ref[:, j * LANES:(j + 1) * LANES] for j in range(ht)]
    ai = [ai_ref[:, j * LANES:(j + 1) * LANES] for j in range(ht)]
    one, nil = jnp.ones((1, LANES), f32), jnp.zeros((1, LANES), f32)
    a_seg = []
    for j in range(ht):
        rr, ri = [one], [nil]
        for _ in range(7):
            nr, ni = cmul(rr[-1], ri[-1], ar[j], ai[j])
            rr.append(nr)
            ri.append(ni)
        a8r, a8i = cmul(rr[-1], ri[-1], ar[j], ai[j])
        blk_r, blk_i = jnp.concatenate(rr, axis=0), jnp.concatenate(ri, axis=0)
        for b in range(SEG // 8):
            pw_scr[j, b * 8:(b + 1) * 8, :] = blk_r
            pw_scr[ht + j, b * 8:(b + 1) * 8, :] = blk_i
            blk_r, blk_i = cmul(blk_r, blk_i, a8r, a8i)
        a_seg.append((blk_r[0:1], blk_i[0:1]))

    ar_b = [jnp.broadcast_to(a, (NSEG, LANES)) for a in ar]
    ai_b = [jnp.broadcast_to(a, (NSEG, LANES)) for a in ai]

    def body(i, carry):
        new_r, new_i = [], []
        for j in range(ht):
            xr, xi = carry[j], carry[ht + j]
            seg_rows = pl.ds(i, NSEG, stride=SEG_PITCH)
            xl_scr[j, seg_rows, :] = xr
            xl_scr[ht + j, seg_rows, :] = xi
            pr, pi = cmul(xr, xi, ar_b[j], ai_b[j])
            new_r.append(pr + s_scr[j, seg_rows, :])
            new_i.append(pi + s_scr[ht + j, seg_rows, :])
        return tuple(new_r + new_i)

    ends = tuple(jnp.zeros((NSEG, LANES), f32) for _ in range(NT))
    for i in range(SEG):
        ends = body(i, ends)

    for j in range(ht):
        zr, zi = nil, nil
        for k in range(NSEG):
            rows = slice(k * SEG, (k + 1) * SEG)
            loc = slice(k * SEG_PITCH, k * SEG_PITCH + SEG)
            fr, fi = cmul(pw_scr[j], pw_scr[ht + j], zr, zi)
            xs_scr[rows, j * LANES:(j + 1) * LANES] = (xl_scr[j, loc, :] + fr).astype(bf16)
            xs_scr[rows, half + j * LANES:half + (j + 1) * LANES] = (xl_scr[ht + j, loc, :] + fi).astype(bf16)
            pr, pi = cmul(zr, zi, *a_seg[j])
            zr, zi = pr + ends[j][k:k + 1], pi + ends[ht + j][k:k + 1]

    xs = xs_scr[...]
    for j in range(SG_IN // Y_TILE):
        kk = (j + 1) * Y_TILE
        cols = slice(j * Y_TILE, (j + 1) * Y_TILE)
        y = _dot(u_ref[:, :kk], m_scr[:kk, cols]) + _dot(xs, q_scr[:, cols])
        y_ref[:, cols] = _gelu_tanh(y).astype(bf16)


def _ssm(u2, pre, pim, kt, qc, rep, at_re, at_im):
    per_sg = lambda r, c: pl.BlockSpec((None, r, c), lambda g: (g, 0, 0))
    return pl.pallas_call(
        _ssm_kernel,
        grid=(SG,),
        in_specs=[pl.BlockSpec((N_CHUNKS, SG_IN), lambda g: (0, g)),
                  per_sg(SG_IN, LANES), per_sg(SG_IN, LANES), per_sg(SG_IN, LANES),
                  per_sg(SG_STATE, CHUNK * SSM_GC), _const_spec(rep.shape),
                  per_sg(1, SG_STATE // 2), per_sg(1, SG_STATE // 2)],
        out_specs=pl.BlockSpec((N_CHUNKS, SG_IN), lambda g: (0, g)),
        out_shape=jax.ShapeDtypeStruct((N_CHUNKS, SG * SG_IN), bf16),
        scratch_shapes=[pltpu.VMEM((SG_IN, SG_STATE), bf16), pltpu.VMEM((SG_IN, SG_IN), bf16),
                        pltpu.VMEM((SG_STATE, SG_IN), bf16),
                        pltpu.VMEM((NT, NSEG * SEG_PITCH, LANES), f32),
                        pltpu.VMEM((NT, NSEG * SEG_PITCH, LANES), f32),
                        pltpu.VMEM((NT, SEG, LANES), f32), pltpu.VMEM((N_CHUNKS, SG_STATE), bf16)],
        compiler_params=_params(("arbitrary",)),
        name="ssm",
    )(u2, pre, pim, kt, qc, rep, at_re, at_im)


def _ssm_prep_kernel(lam_re_ref, lam_im_ref, ldt_ref, bt_re_ref, bt_im_ref, c_re_ref, c_im_ref, d_ref,
                     eye_ref, win_ref, pre_ref, pim_ref, kt_ref, qc_ref, at_re_ref, at_im_ref, wmix_ref):
    wmix_ref[...] = win_ref[...].astype(bf16)
    dup = lambda x: jnp.concatenate([x, x], axis=1)
    lr, li = dup(lam_re_ref[...]), dup(lam_im_ref[...])
    dt = jnp.exp(ldt_ref[...])
    mag = jnp.exp(lr * dt)
    a_re, a_im = mag * jnp.cos(li * dt), mag * jnp.sin(li * dt)
    den = lr * lr + li * li
    nr, ni = a_re - 1.0, a_im
    coef_re = (nr * lr + ni * li) / den
    coef_im = (ni * lr - nr * li) / den
    pw_re, pw_im = [jnp.ones_like(a_re)], [jnp.zeros_like(a_im)]
    for _ in range(CHUNK):
        pr, pi = pw_re[-1], pw_im[-1]
        pw_re.append(pr * a_re - pi * a_im)
        pw_im.append(pr * a_im + pi * a_re)

    low = lax.broadcasted_iota(jnp.int32, (SSM_GC, LANES), 1) < SSM_P
    diag = ((lax.broadcasted_iota(jnp.int32, (SSM_GC, LANES), 1) & (SSM_GC - 1))
            == lax.broadcasted_iota(jnp.int32, (SSM_GC, LANES), 0))
    contract_lanes = (((1,), (1,)), ((), ()))
    for g in range(GPS):
        grp = slice(g * SSM_GC, (g + 1) * SSM_GC)
        one = lambda x: x[g:g + 1]
        bt_re, bt_im = dup(bt_re_ref[grp, :]), dup(bt_im_ref[grp, :])
        bb_re = one(coef_re) * bt_re - one(coef_im) * bt_im
        bb_im = one(coef_re) * bt_im + one(coef_im) * bt_re
        yc = jnp.concatenate([c_re_ref[grp, :], -c_im_ref[grp, :]], axis=1)
        yc_sw = pltpu.roll(yc, SSM_P, axis=1)
        y2 = jnp.where(low, yc_sw, -yc_sw)
        x_rows, ca_rows = [], []
        for tau in range(CHUNK):
            pr, pi = one(pw_re[tau]), one(pw_im[tau])
            ab_re = bb_re * pr - bb_im * pi
            ab_im = bb_re * pi + bb_im * pr
            s = CHUNK - 1 - tau
            rows = slice(s * LANES + g * SSM_GC, s * LANES + (g + 1) * SSM_GC)
            pre_ref[rows, :] = ab_re
            pim_ref[rows, :] = ab_im
            x_rows.append(jnp.where(low, ab_re, ab_im))
            ca_rows.append(yc * one(pw_re[tau + 1]) + y2 * one(pw_im[tau + 1]))
        kern = lax.dot_general(jnp.concatenate(x_rows, axis=0), jnp.concatenate([yc] * GPS, axis=0),
                               contract_lanes, precision=lax.Precision.HIGHEST,
                               preferred_element_type=f32)
        for tau in range(CHUNK):
            blk = kern[tau * SSM_GC:(tau + 1) * SSM_GC]
            if tau == 0:
                blk = blk + jnp.where(diag, d_ref[...], 0.0)
            kt_ref[tau * LANES + g * SSM_GC:tau * LANES + (g + 1) * SSM_GC, :] = blk
        qt = lax.dot_general(eye_ref[...], jnp.concatenate(ca_rows, axis=0).astype(bf16),
                             contract_lanes, preferred_element_type=f32)
        qc_ref[g * SSM_P:(g + 1) * SSM_P, :] = qt[:SSM_P].astype(bf16)
        qc_ref[GPS * SSM_P + g * SSM_P:GPS * SSM_P + (g + 1) * SSM_P, :] = qt[SSM_P:].astype(bf16)
    lo_lane = lax.broadcasted_iota(jnp.int32, (1, LANES), 1) < SSM_P
    for k in range(GPS // 2):
        sel = lambda pw: jnp.where(lo_lane, pw[2 * k:2 * k + 1], pw[2 * k + 1:2 * k + 2])
        at_re_ref[:, k * LANES:(k + 1) * LANES] = sel(pw_re[CHUNK])
        at_im_ref[:, k * LANES:(k + 1) * LANES] = sel(pw_im[CHUNK])


def _ssm_operators(lam_re, lam_im, log_dt, b_re, b_im, c_re, c_im, d_skip, w_in):
    per_sg = lambda x: x.astype(f32).reshape((SG, x.size // (SG * x.shape[-1]), x.shape[-1]))
    bt = lambda b: per_sg(b.swapaxes(1, 2))
    args = (per_sg(lam_re), per_sg(lam_im), log_dt.astype(f32).reshape(SG, GPS, 1),
            bt(b_re), bt(b_im), per_sg(c_re), per_sg(c_im), d_skip.astype(f32).reshape(SG, 1, LANES),
            jnp.asarray(np.eye(LANES), bf16), w_in)
    blk = lambda r, c: pl.BlockSpec((None, r, c), lambda g: (g, 0, 0))
    slab = pl.BlockSpec((w_in.shape[0] // SG, IN_MIX_W), lambda g: (g, 0))
    return pl.pallas_call(
        _ssm_prep_kernel,
        grid=(SG,),
        in_specs=[blk(GPS, SSM_P), blk(GPS, SSM_P), blk(GPS, 1), blk(LANES, SSM_P), blk(LANES, SSM_P),
                  blk(LANES, SSM_P), blk(LANES, SSM_P), blk(1, LANES), _const_spec((LANES, LANES)), slab],
        out_specs=[blk(SG_IN, LANES), blk(SG_IN, LANES), blk(SG_IN, LANES),
                   blk(SG_STATE, CHUNK * SSM_GC), blk(1, SG_STATE // 2), blk(1, SG_STATE // 2),
                   slab],
        out_shape=[jax.ShapeDtypeStruct((SG, SG_IN, LANES), f32)] * 3
                  + [jax.ShapeDtypeStruct((SG, SG_STATE, CHUNK * SSM_GC), bf16)]
                  + [jax.ShapeDtypeStruct((SG, 1, SG_STATE // 2), f32)] * 2
                  + [jax.ShapeDtypeStruct((w_in.shape[0], IN_MIX_W), bf16)],
        compiler_params=_params(("parallel",)),
        name="ssm_prep",
    )(*args)


def _replication_matrix():
    src = np.arange(CHUNK * SSM_GC)
    dst = np.arange(SG_IN)
    same_t = (src[:, None] // SSM_GC) == (dst[None, :] // LANES)
    same_c = (src[:, None] % SSM_GC) == (dst[None, :] % SSM_GC)
    return jnp.asarray(same_t & same_c, bf16)


def _kv_operands(kvp_ref, kvc_ref):
    kcat = jnp.concatenate([kvp_ref[:, :KV_W], kvc_ref[:, :KV_W]], axis=0).astype(f32)
    vcat = jnp.concatenate([kvp_ref[:, KV_W:], kvc_ref[:, KV_W:]], axis=0).astype(f32)
    k_sw = pltpu.roll(kcat, HEAD_DIM, axis=1)
    v_sw = pltpu.roll(vcat, HEAD_DIM, axis=1)
    low_kv = lax.broadcasted_iota(jnp.int32, kcat.shape, 1) < HEAD_DIM
    operands = []
    for hk in range(N_KV_HEADS):
        k_src, k_oth = (kcat, k_sw) if hk == 0 else (k_sw, kcat)
        k_lo = jnp.where(low_kv, k_src, 0.0).astype(bf16)
        k_hi = jnp.where(low_kv, 0.0, k_oth).astype(bf16)
        vv = (jnp.where(low_kv, vcat, v_sw) if hk == 0 else jnp.where(low_kv, v_sw, vcat)).astype(bf16)
        operands.append((k_lo, k_hi, vv))
    return operands


def _attention_block(q, kv_operands, sinks_ref, blk, no_prev, oa_ref):
    q_rows = slice(blk * BLOCK, (blk + 1) * BLOCK)
    kv_rows = slice(blk * BLOCK, (blk + 2) * BLOCK)
    from_prev = (lax.broadcasted_iota(jnp.int32, (BLOCK, BLOCK), 1)
                 > lax.broadcasted_iota(jnp.int32, (BLOCK, BLOCK), 0))
    low_o = lax.broadcasted_iota(jnp.int32, (BLOCK, LANES), 1) < HEAD_DIM
    contract_lanes = (((1,), (1,)), ((), ()))
    pairs_per_kv = (N_Q_HEADS // N_KV_HEADS) // 2
    for hk, (k_lo, k_hi, vv) in enumerate(kv_operands):
        qs = jnp.concatenate(
            [q[q_rows, (pairs_per_kv * hk + b) * LANES:(pairs_per_kv * hk + b + 1) * LANES]
             for b in range(pairs_per_kv)], axis=0)
        s_par = (lax.dot_general(qs, k_lo[kv_rows], contract_lanes, preferred_element_type=f32),
                 lax.dot_general(qs, k_hi[kv_rows], contract_lanes, preferred_element_type=f32))
        for b in range(pairs_per_kv):
            outs = []
            for par in range(2):
                head = 2 * (pairs_per_kv * hk + b) + par
                s_all = s_par[par][b * BLOCK:(b + 1) * BLOCK]
                s_prev = s_all[:, :BLOCK] + no_prev if blk == 0 else s_all[:, :BLOCK]
                s = jnp.where(from_prev, s_prev, s_all[:, BLOCK:])
                sink = sinks_ref[head]
                m = jnp.maximum(jnp.max(s, axis=1, keepdims=True), sink)
                p = jnp.exp(s - m)
                denom = jnp.sum(p, axis=1, keepdims=True) + jnp.exp(sink - m)
                p_kv = jnp.concatenate([jnp.where(from_prev, p, 0.0), jnp.where(from_prev, 0.0, p)],
                                       axis=1).astype(bf16)
                outs.append(_dot(p_kv, vv[kv_rows]) * (1.0 / denom))
            col = (pairs_per_kv * hk + b) * LANES
            oa_ref[q_rows, col:col + LANES] = jnp.where(low_o, outs[0], outs[1]).astype(bf16)


def _attn_mix_kernel(sinks_ref, q_ref, kvp_ref, kvc_ref, y2_ref, gates_ref,
                     wglu_ref, wb_ref, wdn_ref, m_ref, wdn_o_ref, oa_ref, yb_ref):
    wdn_o_ref[...] = wdn_ref[...].astype(bf16)
    no_prev = jnp.where(pl.program_id(0) > 0, 0.0, MASKED)
    q = q_ref[...] * jnp.asarray(1.0 / math.sqrt(HEAD_DIM), bf16)
    kv_operands = _kv_operands(kvp_ref, kvc_ref)
    for r in range(q_ref.shape[0] // ROW_SUB):
        rows = pl.ds(r * ROW_SUB, ROW_SUB)
        chunk_rows = pl.ds(r * (ROW_SUB // CHUNK), ROW_SUB // CHUNK)
        for sg in range(SG):
            for t in range(CHUNK):
                col = sg * SG_IN + t * LANES
                yb_ref[r, sg, pl.ds(t, ROW_SUB // CHUNK, stride=CHUNK_PITCH), :] = (
                    y2_ref[chunk_rows, col:col + LANES].astype(f32))
        y = jnp.concatenate(
            [jnp.concatenate([yb_ref[r, sg, n * CHUNK_PITCH:n * CHUNK_PITCH + CHUNK, :]
                              for n in range(ROW_SUB // CHUNK)], axis=0) for sg in range(SG)],
            axis=1).astype(bf16)
        zg = _dot(y, wglu_ref[...])
        o_ssm = zg[:, :SSM_W] * _sigmoid(zg[:, SSM_W:])
        y_s = _dot(o_ssm.astype(bf16), wb_ref[Q_W:, :])
        for blk in range(r * (ROW_SUB // BLOCK), (r + 1) * (ROW_SUB // BLOCK)):
            _attention_block(q, kv_operands, sinks_ref, blk, no_prev, oa_ref)
        y_a = _dot(oa_ref[rows, :], wb_ref[:Q_W, :])
        m_ref[rows, :] = (gates_ref[rows, :D_MODEL].astype(f32) * y_a
                          + gates_ref[rows, D_MODEL:].astype(f32) * y_s).astype(bf16)


def _attn_mix(q, kv, sinks, y2, gates, wglu, wb, w_down, tm=512):
    L = q.shape[0]
    cur = lambda i: (i, 0)
    prev = lambda i: (jnp.maximum(i * (tm // BLOCK) - 1, 0), 0)
    dn_slab = pl.BlockSpec((w_down.shape[0] // (L // tm), w_down.shape[1]), cur)
    return pl.pallas_call(
        _attn_mix_kernel,
        grid=(L // tm,),
        in_specs=[pl.BlockSpec(memory_space=pltpu.SMEM),
                  pl.BlockSpec((tm, Q_W), cur),
                  pl.BlockSpec((BLOCK, 2 * KV_W), prev), pl.BlockSpec((tm, 2 * KV_W), cur),
                  pl.BlockSpec((tm // CHUNK, SG * SG_IN), cur),
                  pl.BlockSpec((tm, 2 * D_MODEL), cur),
                  _const_spec(wglu.shape), _const_spec(wb.shape), dn_slab],
        out_specs=[pl.BlockSpec((tm, D_MODEL), cur), dn_slab],
        out_shape=[jax.ShapeDtypeStruct((L, D_MODEL), bf16), jax.ShapeDtypeStruct(w_down.shape, bf16)],
        scratch_shapes=[pltpu.VMEM((tm, Q_W), bf16),
                        pltpu.VMEM((tm // ROW_SUB, SG, ROW_SUB // CHUNK * CHUNK_PITCH, LANES), f32)],
        compiler_params=_params(("parallel",)),
        name="attn_mix",
    )(sinks, q, kv, kv, y2, gates, wglu, wb, w_down)


def _outproj_kernel(m_ref, w_ref, x_ref, gpost_ref, gpre_ref, x1_ref, h2_ref):
    for r in range(m_ref.shape[0] // OUT_SUB):
        rows = pl.ds(r * OUT_SUB, OUT_SUB)
        out = _dot(m_ref[rows, :], w_ref[...])
        x1 = x_ref[rows, :] + _rms_norm(out, gpost_ref[...])
        x1_ref[rows, :] = x1
        h2_ref[rows, :] = _rms_norm(x1, gpre_ref[...]).astype(bf16)


def _outproj(m, w_out, x, g_post, g_pre, tm=512):
    L = x.shape[0]
    row = lambda: pl.BlockSpec((tm, D_MODEL), lambda i: (i, 0))
    return pl.pallas_call(
        _outproj_kernel,
        grid=(L // tm,),
        in_specs=[row(), _const_spec(w_out.shape), row(),
                  _const_spec((1, D_MODEL)), _const_spec((1, D_MODEL))],
        out_specs=[row(), row()],
        out_shape=[jax.ShapeDtypeStruct((L, D_MODEL), f32), jax.ShapeDtypeStruct((L, D_MODEL), bf16)],
        compiler_params=_params(("parallel",)),
        name="outproj",
    )(m, w_out, x, g_post, g_pre)


MLP_SUB = 512


def _mlp_kernel(h_ref, wu_ref, wd_ref, x_ref, g_ref, o_ref, acc_ref):
    i, j = pl.program_id(0), pl.program_id(1)
    n_tiles = pl.num_programs(0) - 1
    tm = h_ref.shape[0]
    slab = o_ref.shape[0]
    cur, prev = acc_ref.at[i % 2], acc_ref.at[(i + 1) % 2]
    slab_rows = pl.ds(pl.multiple_of(j * slab, slab), slab)

    def finish_prev_slab():
        o_ref[...] = x_ref[...] + _rms_norm(prev[slab_rows, :], g_ref[...])
        prev[slab_rows, :] = jnp.zeros((slab, D_MODEL), f32)

    @pl.when((i == 0) & (j == 0))
    def _():
        acc_ref[...] = jnp.zeros_like(acc_ref)

    @pl.when(i < n_tiles)
    def _():
        for r in range(tm // MLP_SUB):
            rows = pl.ds(r * MLP_SUB, MLP_SUB)
            a = jnp.maximum(_dot(h_ref[rows, :], wu_ref[...]), 0.0)
            cur[rows, :] += _dot((a * a).astype(bf16), wd_ref[...])
        finish_prev_slab()

    @pl.when(i == n_tiles)
    def _():
        finish_prev_slab()


def _mlp(h2, w_up, w_down, x1, g_post, tm=1024, tf=1024):
    L = x1.shape[0]
    n_tiles, n_ff = L // tm, D_FF // tf
    slab = tm // n_ff
    ff = lambda i, j: jnp.where(i < n_tiles, j, n_ff - 1)
    prev_slab = lambda i, j: (jnp.maximum(i - 1, 0) * n_ff + jnp.where(i > 0, j, 0), 0)
    return pl.pallas_call(
        _mlp_kernel,
        grid=(n_tiles + 1, n_ff),
        in_specs=[pl.BlockSpec((tm, D_MODEL), lambda i, j: (jnp.minimum(i, n_tiles - 1), 0)),
                  pl.BlockSpec((D_MODEL, tf), lambda i, j: (0, ff(i, j))),
                  pl.BlockSpec((tf, D_MODEL), lambda i, j: (ff(i, j), 0)),
                  pl.BlockSpec((slab, D_MODEL), prev_slab),
                  _const_spec((1, D_MODEL))],
        out_specs=pl.BlockSpec((slab, D_MODEL), prev_slab),
        out_shape=jax.ShapeDtypeStruct((L, D_MODEL), f32),
        scratch_shapes=[pltpu.VMEM((2, tm, D_MODEL), f32)],
        compiler_params=_params(("arbitrary", "arbitrary")),
        name="mlp",
    )(h2, w_up, w_down, x1, g_post)


def _rope_tables(L):
    half = ROT_DIM // 2
    inv = ROPE_THETA ** (-np.arange(half, dtype=np.float64) * 2.0 / ROT_DIM)
    ang = np.arange(L, dtype=np.float64)[:, None] * inv[None, :]
    cos, sin = np.cos(ang).astype(np.float32), np.sin(ang).astype(np.float32)
    ones = np.ones((L, HEAD_DIM - ROT_DIM), np.float32)
    zeros = np.zeros((L, HEAD_DIM - ROT_DIM), np.float32)
    zh = np.zeros((L, half), np.float32)
    per_head = lambda parts: np.tile(np.concatenate(parts, axis=1), (1, LANES // HEAD_DIM))
    return jnp.asarray(np.concatenate(
        [per_head([cos, cos, ones]), per_head([-sin, zh, zeros]), per_head([zh, sin, zeros])], axis=1))


def _layer(x, norm_mix_pre, norm_mix_post, norm_mlp_pre, norm_mlp_post, w_in, sinks,
           lam_re, lam_im, log_dt, b_re, b_im, c_re, c_im, d_skip, w_glu, w_branch, w_out,
           w_up, w_down, rope):
    gain = lambda g: g.astype(f32).reshape(1, D_MODEL)
    pre, pim, kt, qc, at_re, at_im, wmix = _ssm_operators(
        lam_re, lam_im, log_dt, b_re, b_im, c_re, c_im, d_skip, w_in.astype(f32))
    q, kv, u2, h, wg = _inproj(x, gain(norm_mix_pre), wmix, w_in.astype(f32), rope)
    gates, wup, (wglu, wb, wo) = _gates(h, wg, w_up.astype(f32),
                                        [w_glu.astype(f32), w_branch.astype(f32), w_out.astype(f32)])
    y2 = _ssm(u2, pre, pim, kt, qc, _replication_matrix(), at_re, at_im)
    mixed, wdn = _attn_mix(q, kv, sinks.astype(f32), y2, gates, wglu, wb, w_down.astype(f32))
    x1, h2 = _outproj(mixed, wo, x, gain(norm_mix_post), gain(norm_mlp_pre))
    return _mlp(h2, wup, wdn, x1, gain(norm_mlp_post))


def kernel(x, norm_mix_pre, norm_mix_post, norm_mlp_pre, norm_mlp_post, w_in, sinks, lam_re, lam_im, log_dt, b_re, b_im, c_re, c_im, d_skip, w_glu, w_branch, w_out, w_up, w_down):
    B, L, _ = x.shape
    depth = w_in.shape[0]
    rope = _rope_tables(L)
    outs = []
    for b in range(B):
        xb = x[b]
        for l in range(depth):
            xb = _layer(xb, norm_mix_pre[l], norm_mix_post[l], norm_mlp_pre[l], norm_mlp_post[l],
                        w_in[l], sinks[l], lam_re[l], lam_im[l], log_dt[l], b_re[l], b_im[l],
                        c_re[l], c_im[l], d_skip[l], w_glu[l], w_branch[l], w_out[l],
                        w_up[l], w_down[l], rope)
        outs.append(xb)
    return jnp.stack(outs)
```

```python
import math

import jax
import jax.numpy as jnp
import numpy as np
from jax import lax
from jax.experimental import pallas as pl
from jax.experimental.pallas import tpu as pltpu

D_MODEL = 2048
SEQ = 8192
HEAD_DIM = 64
N_Q_HEADS = 16
N_KV_HEADS = 2
BLOCK = 128
ROT_DIM = HEAD_DIM // 4
ROPE_THETA = 500000.0
Q_W = N_Q_HEADS * HEAD_DIM
KV_W = N_KV_HEADS * HEAD_DIM
SSM_W = D_MODEL // 2
QKV_W = Q_W + 2 * KV_W
IN_MIX_W = QKV_W + SSM_W
SSM_GC = 16
SSM_G = SSM_W // SSM_GC
SSM_P = 64
D_FF = 4 * D_MODEL
EPS = 1e-6

LANES = 128
CHUNK = 8
N_CHUNKS = SEQ // CHUNK
SG = SSM_W // LANES
GPS = LANES // SSM_GC
SG_IN = CHUNK * LANES
SG_STATE = 2 * GPS * SSM_P
VMEM_LIMIT = 56 * 1024 * 1024
ROW_SUB = 256
CHUNK_PITCH = CHUNK if (CHUNK // 8) % 2 else CHUNK + 8
OUT_SUB = 128
MASKED = float(jnp.finfo(jnp.float32).min)

bf16 = jnp.bfloat16
f32 = jnp.float32


def _dot(a, b):
    return jnp.dot(a, b, preferred_element_type=f32)


def _sigmoid(x):
    return 1.0 / (1.0 + jnp.exp(-x))


def _gelu_tanh(x):
    c = math.sqrt(2.0 / math.pi)
    return x * (0.5 * (1.0 + jnp.tanh(c * (x + 0.044715 * (x * x * x)))))


def _rms_norm(x, g):
    return x * lax.rsqrt(jnp.mean(x * x, axis=-1, keepdims=True) + EPS) * g


def _params(sem):
    return pltpu.CompilerParams(dimension_semantics=sem, vmem_limit_bytes=VMEM_LIMIT)


def _const_spec(shape):
    nd = len(shape)
    return pl.BlockSpec(shape, lambda *_: (0,) * nd, pipeline_mode=pl.Buffered(1))


def _rope_block(z, cos, sin_a, sin_b):
    return (z * cos + pltpu.roll(z, LANES - ROT_DIM // 2, axis=1) * sin_a
            + pltpu.roll(z, ROT_DIM // 2, axis=1) * sin_b)


def _inproj_kernel(x_ref, g_ref, w_ref, wcast_ref, rope_ref, q_ref, kv_ref, u2_ref, h_ref, wg_ref, zu_ref):
    wg_ref[...] = wcast_ref[...].astype(bf16)
    for r in range(x_ref.shape[0] // ROW_SUB):
        rows = pl.ds(r * ROW_SUB, ROW_SUB)
        h = _rms_norm(x_ref[rows, :], g_ref[...]).astype(bf16)
        h_ref[rows, :] = h
        cos, sin_a, sin_b = (rope_ref[rows, t * LANES:(t + 1) * LANES] for t in range(3))
        for half in range(2):
            zq = _dot(h, w_ref[:, half * 512:(half + 1) * 512])
            for b in range(4):
                blk = zq[:, b * LANES:(b + 1) * LANES]
                col = (half * 4 + b) * LANES
                q_ref[rows, col:col + LANES] = _rope_block(blk, cos, sin_a, sin_b).astype(bf16)
        zkv = _dot(h, w_ref[:, Q_W:QKV_W])
        kv_ref[rows, :KV_W] = _rope_block(zkv[:, :KV_W], cos, sin_a, sin_b).astype(bf16)
        kv_ref[rows, KV_W:] = zkv[:, KV_W:].astype(bf16)
        zu = _dot(h, w_ref[:, QKV_W:])
        chunk_rows = pl.ds(r * (ROW_SUB // CHUNK), ROW_SUB // CHUNK)
        for sg in range(SG):
            for n in range(ROW_SUB // CHUNK):
                zu_ref[r, sg, n * CHUNK_PITCH:n * CHUNK_PITCH + CHUNK, :] = (
                    zu[n * CHUNK:(n + 1) * CHUNK, sg * LANES:(sg + 1) * LANES])
            for s in range(CHUNK):
                col = sg * SG_IN + s * LANES
                u2_ref[chunk_rows, col:col + LANES] = (
                    zu_ref[r, sg, pl.ds(s, ROW_SUB // CHUNK, stride=CHUNK_PITCH), :].astype(bf16))


def _inproj(x, gain, wmix, w_in, rope, tm=512):
    L = x.shape[0]
    steps = L // tm
    gate_w = w_in.shape[1] - IN_MIX_W
    slab = gate_w // steps
    assert IN_MIX_W % slab == 0 and slab % LANES == 0
    row = lambda w: pl.BlockSpec((tm, w), lambda i: (i, 0))
    return pl.pallas_call(
        _inproj_kernel,
        grid=(steps,),
        in_specs=[row(D_MODEL), _const_spec((1, D_MODEL)), _const_spec(wmix.shape),
                  pl.BlockSpec((D_MODEL, slab), lambda i: (0, IN_MIX_W // slab + i)),
                  row(rope.shape[1])],
        out_specs=[row(Q_W), row(2 * KV_W),
                   pl.BlockSpec((tm // CHUNK, SG * SG_IN), lambda i: (i, 0)), row(D_MODEL),
                   pl.BlockSpec((D_MODEL, slab), lambda i: (0, i))],
        out_shape=[jax.ShapeDtypeStruct((L, Q_W), bf16), jax.ShapeDtypeStruct((L, 2 * KV_W), bf16),
                   jax.ShapeDtypeStruct((L // CHUNK, SG * SG_IN), bf16),
                   jax.ShapeDtypeStruct((L, D_MODEL), bf16),
                   jax.ShapeDtypeStruct((D_MODEL, gate_w), bf16)],
        scratch_shapes=[pltpu.VMEM((tm // ROW_SUB, SG, ROW_SUB // CHUNK * CHUNK_PITCH, LANES), f32)],
        compiler_params=_params(("parallel",)),
        name="inproj",
    )(x, gain, wmix, w_in, rope)


def _gates_kernel(h_ref, w_ref, *rest):
    n_cast = (len(rest) - 1) // 2
    o_ref = rest[n_cast]
    for r in range(h_ref.shape[0] // ROW_SUB):
        rows = pl.ds(r * ROW_SUB, ROW_SUB)
        o_ref[rows, :] = _sigmoid(_dot(h_ref[rows, :], w_ref[...])).astype(bf16)
    for src, dst in zip(rest[:n_cast], rest[n_cast + 1:]):
        dst[...] = src[...].astype(bf16)


def _gates(h, wg, w_up, row_cast_weights, tm=1024, tn=2048):
    L, N = h.shape[0], wg.shape[1]
    ni = L // tm
    steps = ni * (N // tn)
    step = lambda j, i: j * ni + i
    slabs = [pl.BlockSpec((w_up.shape[0], w_up.shape[1] // steps), lambda j, i: (0, step(j, i)))]
    slabs += [pl.BlockSpec((w.shape[0] // steps, w.shape[1]), lambda j, i: (step(j, i), 0))
              for w in row_cast_weights]
    cast_weights = [w_up] + list(row_cast_weights)
    outs = pl.pallas_call(
        _gates_kernel,
        grid=(N // tn, ni),
        in_specs=[pl.BlockSpec((tm, D_MODEL), lambda j, i: (i, 0)),
                  pl.BlockSpec((D_MODEL, tn), lambda j, i: (0, j))] + slabs,
        out_specs=[pl.BlockSpec((tm, tn), lambda j, i: (i, j))] + slabs,
        out_shape=[jax.ShapeDtypeStruct((L, N), bf16)]
                  + [jax.ShapeDtypeStruct(w.shape, bf16) for w in cast_weights],
        compiler_params=_params(("parallel", "arbitrary")),
        name="gates",
    )(h, wg, *cast_weights)
    return outs[0], outs[1], outs[2:]


GC_SHIFT = SSM_GC.bit_length() - 1
P_SHIFT = SSM_P.bit_length() - 1
Y_TILE = 256
NT = SG_STATE // LANES
NSEG = 16
SEG = N_CHUNKS // NSEG
SEG_PITCH = SEG + 8
PREP_TILES = 2


def _ssm_kernel(u_ref, pre_ref, pim_ref, kt_ref, qc_ref, rep_ref, ar_ref, ai_ref, y_ref,
                p_scr, m_scr, q_scr, s_scr, xl_scr, pw_scr, xs_scr):
    half = SG_STATE // 2
    row = lax.broadcasted_iota(jnp.int32, (SG_IN, LANES), 0)
    lane = lax.broadcasted_iota(jnp.int32, (SG_IN, LANES), 1)
    row_grp = (row >> GC_SHIFT) & (GPS - 1)
    pre, pim = pre_ref[...], pim_ref[...]
    for k in range(half // LANES):
        sel = row_grp == 2 * k + (lane >> P_SHIFT)
        p_scr[:, k * LANES:(k + 1) * LANES] = jnp.where(sel, pre, 0.0).astype(bf16)
        p_scr[:, half + k * LANES:half + (k + 1) * LANES] = jnp.where(sel, pim, 0.0).astype(bf16)

    blk_row = lax.broadcasted_iota(jnp.int32, (LANES, LANES), 0) >> GC_SHIFT
    blk_lane = lax.broadcasted_iota(jnp.int32, (LANES, LANES), 1) >> GC_SHIFT
    zero_blk = jnp.zeros((LANES, LANES), bf16)
    for tau in range(CHUNK):
        blk = jnp.where(blk_row == blk_lane, kt_ref[tau * LANES:(tau + 1) * LANES, :], 0.0).astype(bf16)
        for s in range(CHUNK - tau):
            t = s + tau
            m_scr[s * LANES:(s + 1) * LANES, t * LANES:(t + 1) * LANES] = blk
            if tau > 0:
                m_scr[t * LANES:(t + 1) * LANES, s * LANES:(s + 1) * LANES] = zero_blk

    q_row_grp = (lax.broadcasted_iota(jnp.int32, (SG_STATE, Y_TILE), 0) >> P_SHIFT) & (GPS - 1)
    q_lane_grp = (lax.broadcasted_iota(jnp.int32, (SG_STATE, Y_TILE), 1) >> GC_SHIFT) & (GPS - 1)
    for j in range(SG_IN // Y_TILE):
        full = _dot(qc_ref[...], rep_ref[:, j * Y_TILE:(j + 1) * Y_TILE])
        q_scr[:, j * Y_TILE:(j + 1) * Y_TILE] = jnp.where(q_row_grp == q_lane_grp, full, 0.0).astype(bf16)

    s = _dot(u_ref[...], p_scr[...])
    for j in range(NT):
        for k in range(NSEG):
            s_scr[j, k * SEG_PITCH:k * SEG_PITCH + SEG, :] = s[k * SEG:(k + 1) * SEG, j * LANES:(j + 1) * LANES]
    ht = NT // 2
    cmul = lambda xr, xi, yr, yi: (xr * yr - xi * yi, xr * yi + xi * yr)
    ar = [ar_ref[:, j * LANES:(j + 1) * LANES] for j in range(ht)]
    ai = [ai_ref[:, j * LANES:(j + 1) * LANES] for j in range(ht)]
    one, nil = jnp.ones((1, LANES), f32), jnp.zeros((1, LANES), f32)
    a_seg = []
    for j in range(ht):
        rr, ri = [one], [nil]
        for _ in range(7):
            nr, ni = cmul(rr[-1], ri[-1], ar[j], ai[j])
            rr.append(nr)
            ri.append(ni)
        a8r, a8i = cmul(rr[-1], ri[-1], ar[j], ai[j])
        blk_r, blk_i = jnp.concatenate(rr, axis=0), jnp.concatenate(ri, axis=0)
        for b in range(SEG // 8):
            pw_scr[j, b * 8:(b + 1) * 8, :] = blk_r
            pw_scr[ht + j, b * 8:(b + 1) * 8, :] = blk_i
            blk_r, blk_i = cmul(blk_r, blk_i, a8r, a8i)
        a_seg.append((blk_r[0:1], blk_i[0:1]))

    ar_b = [jnp.broadcast_to(a, (NSEG, LANES)) for a in ar]
    ai_b = [jnp.broadcast_to(a, (NSEG, LANES)) for a in ai]

    def body(i, carry):
        new_r, new_i = [], []
        for j in range(ht):
            xr, xi = carry[j], carry[ht + j]
            seg_rows = pl.ds(i, NSEG, stride=SEG_PITCH)
            xl_scr[j, seg_rows, :] = xr
            xl_scr[ht + j, seg_rows, :] = xi
            pr, pi = cmul(xr, xi, ar_b[j], ai_b[j])
            new_r.append(pr + s_scr[j, seg_rows, :])
            new_i.append(pi + s_scr[ht + j, seg_rows, :])
        return tuple(new_r + new_i)

    ends = tuple(jnp.zeros((NSEG, LANES), f32) for _ in range(NT))
    for i in range(SEG):
        ends = body(i, ends)

    for j in range(ht):
        zr, zi = nil, nil
        for k in range(NSEG):
            rows = slice(k * SEG, (k + 1) * SEG)
            loc = slice(k * SEG_PITCH, k * SEG_PITCH + SEG)
            fr, fi = cmul(pw_scr[j], pw_scr[ht + j], zr, zi)
            xs_scr[rows, j * LANES:(j + 1) * LANES] = (xl_scr[j, loc, :] + fr).astype(bf16)
            xs_scr[rows, half + j * LANES:half + (j + 1) * LANES] = (xl_scr[ht + j, loc, :] + fi).astype(bf16)
            pr, pi = cmul(zr, zi, *a_seg[j])
            zr, zi = pr + ends[j][k:k + 1], pi + ends[ht + j][k:k + 1]

    xs = xs_scr[...]
    for j in range(SG_IN // Y_TILE):
        kk = (j + 1) * Y_TILE
        cols = slice(j * Y_TILE, (j + 1) * Y_TILE)
        y = _dot(u_ref[:, :kk], m_scr[:kk, cols]) + _dot(xs, q_scr[:, cols])
        y_ref[:, cols] = _gelu_tanh(y).astype(bf16)


def _ssm(u2, pre, pim, kt, qc, rep, at_re, at_im):
    per_sg = lambda r, c: pl.BlockSpec((None, r, c), lambda g: (g, 0, 0))
    return pl.pallas_call(
        _ssm_kernel,
        grid=(SG,),
        in_specs=[pl.BlockSpec((N_CHUNKS, SG_IN), lambda g: (0, g)),
                  per_sg(SG_IN, LANES), per_sg(SG_IN, LANES), per_sg(SG_IN, LANES),
                  per_sg(SG_STATE, CHUNK * SSM_GC), _const_spec(rep.shape),
                  per_sg(1, SG_STATE // 2), per_sg(1, SG_STATE // 2)],
        out_specs=pl.BlockSpec((N_CHUNKS, SG_IN), lambda g: (0, g)),
        out_shape=jax.ShapeDtypeStruct((N_CHUNKS, SG * SG_IN), bf16),
        scratch_shapes=[pltpu.VMEM((SG_IN, SG_STATE), bf16), pltpu.VMEM((SG_IN, SG_IN), bf16),
                        pltpu.VMEM((SG_STATE, SG_IN), bf16),
                        pltpu.VMEM((NT, NSEG * SEG_PITCH, LANES), f32),
                        pltpu.VMEM((NT, NSEG * SEG_PITCH, LANES), f32),
                        pltpu.VMEM((NT, SEG, LANES), f32), pltpu.VMEM((N_CHUNKS, SG_STATE), bf16)],
        compiler_params=_params(("arbitrary",)),
        name="ssm",
    )(u2, pre, pim, kt, qc, rep, at_re, at_im)


def _ssm_prep_kernel(*refs):
    ins, (eye_ref, win_ref), outs, wmix_ref = refs[:8], refs[8:10], refs[10:16], refs[16]
    wmix_ref[...] = win_ref[...].astype(bf16)
    for t in range(PREP_TILES):
        _ssm_prep_tile(*(r.at[t] for r in ins), eye_ref, *(r.at[t] for r in outs))


def _ssm_prep_tile(lam_re_ref, lam_im_ref, ldt_ref, bt_re_ref, bt_im_ref, c_re_ref, c_im_ref, d_ref,
                   eye_ref, pre_ref, pim_ref, kt_ref, qc_ref, at_re_ref, at_im_ref):
    dup = lambda x: jnp.concatenate([x, x], axis=1)
    lr, li = dup(lam_re_ref[...]), dup(lam_im_ref[...])
    dt = jnp.exp(ldt_ref[...])
    mag = jnp.exp(lr * dt)
    a_re, a_im = mag * jnp.cos(li * dt), mag * jnp.sin(li * dt)
    den = lr * lr + li * li
    nr, ni = a_re - 1.0, a_im
    coef_re = (nr * lr + ni * li) / den
    coef_im = (ni * lr - nr * li) / den
    pw_re, pw_im = [jnp.ones_like(a_re)], [jnp.zeros_like(a_im)]
    for _ in range(CHUNK):
        pr, pi = pw_re[-1], pw_im[-1]
        pw_re.append(pr * a_re - pi * a_im)
        pw_im.append(pr * a_im + pi * a_re)

    low = lax.broadcasted_iota(jnp.int32, (SSM_GC, LANES), 1) < SSM_P
    diag = ((lax.broadcasted_iota(jnp.int32, (SSM_GC, LANES), 1) & (SSM_GC - 1))
            == lax.broadcasted_iota(jnp.int32, (SSM_GC, LANES), 0))
    contract_lanes = (((1,), (1,)), ((), ()))
    for g in range(GPS):
        grp = slice(g * SSM_GC, (g + 1) * SSM_GC)
        one = lambda x: x[g:g + 1]
        bt_re, bt_im = dup(bt_re_ref[grp, :]), dup(bt_im_ref[grp, :])
        bb_re = one(coef_re) * bt_re - one(coef_im) * bt_im
        bb_im = one(coef_re) * bt_im + one(coef_im) * bt_re
        yc = jnp.concatenate([c_re_ref[grp, :], -c_im_ref[grp, :]], axis=1)
        yc_sw = pltpu.roll(yc, SSM_P, axis=1)
        y2 = jnp.where(low, yc_sw, -yc_sw)
        x_rows, ca_rows = [], []
        for tau in range(CHUNK):
            pr, pi = one(pw_re[tau]), one(pw_im[tau])
            ab_re = bb_re * pr - bb_im * pi
            ab_im = bb_re * pi + bb_im * pr
            s = CHUNK - 1 - tau
            rows = slice(s * LANES + g * SSM_GC, s * LANES + (g + 1) * SSM_GC)
            pre_ref[rows, :] = ab_re
            pim_ref[rows, :] = ab_im
            x_rows.append(jnp.where(low, ab_re, ab_im))
            ca_rows.append(yc * one(pw_re[tau + 1]) + y2 * one(pw_im[tau + 1]))
        kern = lax.dot_general(jnp.concatenate(x_rows, axis=0), jnp.concatenate([yc] * GPS, axis=0),
                               contract_lanes, precision=lax.Precision.HIGHEST,
                               preferred_element_type=f32)
        for tau in range(CHUNK):
            blk = kern[tau * SSM_GC:(tau + 1) * SSM_GC]
            if tau == 0:
                blk = blk + jnp.where(diag, d_ref[...], 0.0)
            kt_ref[tau * LANES + g * SSM_GC:tau * LANES + (g + 1) * SSM_GC, :] = blk
        qt = lax.dot_general(eye_ref[...], jnp.concatenate(ca_rows, axis=0).astype(bf16),
                             contract_lanes, preferred_element_type=f32)
        qc_ref[g * SSM_P:(g + 1) * SSM_P, :] = qt[:SSM_P].astype(bf16)
        qc_ref[GPS * SSM_P + g * SSM_P:GPS * SSM_P + (g + 1) * SSM_P, :] = qt[SSM_P:].astype(bf16)
    lo_lane = lax.broadcasted_iota(jnp.int32, (1, LANES), 1) < SSM_P
    for k in range(GPS // 2):
        sel = lambda pw: jnp.where(lo_lane, pw[2 * k:2 * k + 1], pw[2 * k + 1:2 * k + 2])
        at_re_ref[:, k * LANES:(k + 1) * LANES] = sel(pw_re[CHUNK])
        at_im_ref[:, k * LANES:(k + 1) * LANES] = sel(pw_im[CHUNK])


def _ssm_operators(lam_re, lam_im, log_dt, b_re, b_im, c_re, c_im, d_skip, w_in):
    per_sg = lambda x: x.astype(f32).reshape((SG, x.size // (SG * x.shape[-1]), x.shape[-1]))
    bt = lambda b: per_sg(b.swapaxes(1, 2))
    args = (per_sg(lam_re), per_sg(lam_im), log_dt.astype(f32).reshape(SG, GPS, 1),
            bt(b_re), bt(b_im), per_sg(c_re), per_sg(c_im), d_skip.astype(f32).reshape(SG, 1, LANES),
            jnp.asarray(np.eye(LANES), bf16), w_in)
    steps = SG // PREP_TILES
    blk = lambda r, c: pl.BlockSpec((PREP_TILES, r, c), lambda g: (g, 0, 0))
    slab = pl.BlockSpec((w_in.shape[0] // steps, IN_MIX_W), lambda g: (g, 0))
    return pl.pallas_call(
        _ssm_prep_kernel,
        grid=(steps,),
        in_specs=[blk(GPS, SSM_P), blk(GPS, SSM_P), blk(GPS, 1), blk(LANES, SSM_P), blk(LANES, SSM_P),
                  blk(LANES, SSM_P), blk(LANES, SSM_P), blk(1, LANES), _const_spec((LANES, LANES)), slab],
        out_specs=[blk(SG_IN, LANES), blk(SG_IN, LANES), blk(SG_IN, LANES),
                   blk(SG_STATE, CHUNK * SSM_GC), blk(1, SG_STATE // 2), blk(1, SG_STATE // 2),
                   slab],
        out_shape=[jax.ShapeDtypeStruct((SG, SG_IN, LANES), f32)] * 3
                  + [jax.ShapeDtypeStruct((SG, SG_STATE, CHUNK * SSM_GC), bf16)]
                  + [jax.ShapeDtypeStruct((SG, 1, SG_STATE // 2), f32)] * 2
                  + [jax.ShapeDtypeStruct((w_in.shape[0], IN_MIX_W), bf16)],
        compiler_params=_params(("parallel",)),
        name="ssm_prep",
    )(*args)


def _replication_matrix():
    src = np.arange(CHUNK * SSM_GC)
    dst = np.arange(SG_IN)
    same_t = (src[:, None] // SSM_GC) == (dst[None, :] // LANES)
    same_c = (src[:, None] % SSM_GC) == (dst[None, :] % SSM_GC)
    return jnp.asarray(same_t & same_c, bf16)


def _kv_operands(kvp_ref, kvc_ref):
    kcat = jnp.concatenate([kvp_ref[:, :KV_W], kvc_ref[:, :KV_W]], axis=0).astype(f32)
    vcat = jnp.concatenate([kvp_ref[:, KV_W:], kvc_ref[:, KV_W:]], axis=0).astype(f32)
    k_sw = pltpu.roll(kcat, HEAD_DIM, axis=1)
    v_sw = pltpu.roll(vcat, HEAD_DIM, axis=1)
    low_kv = lax.broadcasted_iota(jnp.int32, kcat.shape, 1) < HEAD_DIM
    operands = []
    for hk in range(N_KV_HEADS):
        k_src, k_oth = (kcat, k_sw) if hk == 0 else (k_sw, kcat)
        k_lo = jnp.where(low_kv, k_src, 0.0).astype(bf16)
        k_hi = jnp.where(low_kv, 0.0, k_oth).astype(bf16)
        vv = (jnp.where(low_kv, vcat, v_sw) if hk == 0 else jnp.where(low_kv, v_sw, vcat)).astype(bf16)
        operands.append((k_lo, k_hi, vv))
    return operands


def _attention_block(q, kv_operands, sinks_ref, blk, no_prev, oa_ref):
    q_rows = slice(blk * BLOCK, (blk + 1) * BLOCK)
    kv_rows = slice(blk * BLOCK, (blk + 2) * BLOCK)
    from_prev = (lax.broadcasted_iota(jnp.int32, (BLOCK, BLOCK), 1)
                 > lax.broadcasted_iota(jnp.int32, (BLOCK, BLOCK), 0))
    low_o = lax.broadcasted_iota(jnp.int32, (BLOCK, LANES), 1) < HEAD_DIM
    contract_lanes = (((1,), (1,)), ((), ()))
    pairs_per_kv = (N_Q_HEADS // N_KV_HEADS) // 2
    for hk, (k_lo, k_hi, vv) in enumerate(kv_operands):
        qs = jnp.concatenate(
            [q[q_rows, (pairs_per_kv * hk + b) * LANES:(pairs_per_kv * hk + b + 1) * LANES]
             for b in range(pairs_per_kv)], axis=0)
        s_par = (lax.dot_general(qs, k_lo[kv_rows], contract_lanes, preferred_element_type=f32),
                 lax.dot_general(qs, k_hi[kv_rows], contract_lanes, preferred_element_type=f32))
        for b in range(pairs_per_kv):
            outs = []
            for par in range(2):
                head = 2 * (pairs_per_kv * hk + b) + par
                s_all = s_par[par][b * BLOCK:(b + 1) * BLOCK]
                s_prev = s_all[:, :BLOCK] + no_prev if blk == 0 else s_all[:, :BLOCK]
                s = jnp.where(from_prev, s_prev, s_all[:, BLOCK:])
                sink = sinks_ref[head]
                m = jnp.maximum(jnp.max(s, axis=1, keepdims=True), sink)
                p = jnp.exp(s - m)
                denom = jnp.sum(p, axis=1, keepdims=True) + jnp.exp(sink - m)
                p_kv = jnp.concatenate([jnp.where(from_prev, p, 0.0), jnp.where(from_prev, 0.0, p)],
                                       axis=1).astype(bf16)
                outs.append(_dot(p_kv, vv[kv_rows]) * (1.0 / denom))
            col = (pairs_per_kv * hk + b) * LANES
            oa_ref[q_rows, col:col + LANES] = jnp.where(low_o, outs[0], outs[1]).astype(bf16)


def _attn_mix_kernel(sinks_ref, q_ref, kvp_ref, kvc_ref, y2_ref, gates_ref,
                     wglu_ref, wb_ref, wdn_ref, m_ref, wdn_o_ref, oa_ref, yb_ref):
    wdn_o_ref[...] = wdn_ref[...].astype(bf16)
    no_prev = jnp.where(pl.program_id(0) > 0, 0.0, MASKED)
    q = q_ref[...] * jnp.asarray(1.0 / math.sqrt(HEAD_DIM), bf16)
    kv_operands = _kv_operands(kvp_ref, kvc_ref)
    for r in range(q_ref.shape[0] // ROW_SUB):
        rows = pl.ds(r * ROW_SUB, ROW_SUB)
        chunk_rows = pl.ds(r * (ROW_SUB // CHUNK), ROW_SUB // CHUNK)
        for sg in range(SG):
            for t in range(CHUNK):
                col = sg * SG_IN + t * LANES
                yb_ref[r, sg, pl.ds(t, ROW_SUB // CHUNK, stride=CHUNK_PITCH), :] = (
                    y2_ref[chunk_rows, col:col + LANES].astype(f32))
        y = jnp.concatenate(
            [jnp.concatenate([yb_ref[r, sg, n * CHUNK_PITCH:n * CHUNK_PITCH + CHUNK, :]
                              for n in range(ROW_SUB // CHUNK)], axis=0) for sg in range(SG)],
            axis=1).astype(bf16)
        zg = _dot(y, wglu_ref[...])
        o_ssm = zg[:, :SSM_W] * _sigmoid(zg[:, SSM_W:])
        y_s = _dot(o_ssm.astype(bf16), wb_ref[Q_W:, :])
        for blk in range(r * (ROW_SUB // BLOCK), (r + 1) * (ROW_SUB // BLOCK)):
            _attention_block(q, kv_operands, sinks_ref, blk, no_prev, oa_ref)
        y_a = _dot(oa_ref[rows, :], wb_ref[:Q_W, :])
        m_ref[rows, :] = (gates_ref[rows, :D_MODEL].astype(f32) * y_a
                          + gates_ref[rows, D_MODEL:].astype(f32) * y_s).astype(bf16)


def _attn_mix(q, kv, sinks, y2, gates, wglu, wb, w_down, tm=512):
    L = q.shape[0]
    cur = lambda i: (i, 0)
    prev = lambda i: (jnp.maximum(i * (tm // BLOCK) - 1, 0), 0)
    dn_slab = pl.BlockSpec((w_down.shape[0] // (L // tm), w_down.shape[1]), cur)
    return pl.pallas_call(
        _attn_mix_kernel,
        grid=(L // tm,),
        in_specs=[pl.BlockSpec(memory_space=pltpu.SMEM),
                  pl.BlockSpec((tm, Q_W), cur),
                  pl.BlockSpec((BLOCK, 2 * KV_W), prev), pl.BlockSpec((tm, 2 * KV_W), cur),
                  pl.BlockSpec((tm // CHUNK, SG * SG_IN), cur),
                  pl.BlockSpec((tm, 2 * D_MODEL), cur),
                  _const_spec(wglu.shape), _const_spec(wb.shape), dn_slab],
        out_specs=[pl.BlockSpec((tm, D_MODEL), cur), dn_slab],
        out_shape=[jax.ShapeDtypeStruct((L, D_MODEL), bf16), jax.ShapeDtypeStruct(w_down.shape, bf16)],
        scratch_shapes=[pltpu.VMEM((tm, Q_W), bf16),
                        pltpu.VMEM((tm // ROW_SUB, SG, ROW_SUB // CHUNK * CHUNK_PITCH, LANES), f32)],
        compiler_params=_params(("parallel",)),
        name="attn_mix",
    )(sinks, q, kv, kv, y2, gates, wglu, wb, w_down)


def _outproj_kernel(m_ref, w_ref, x_ref, gpost_ref, gpre_ref, x1_ref, h2_ref):
    for r in range(m_ref.shape[0] // OUT_SUB):
        rows = pl.ds(r * OUT_SUB, OUT_SUB)
        out = _dot(m_ref[rows, :], w_ref[...])
        x1 = x_ref[rows, :] + _rms_norm(out, gpost_ref[...])
        x1_ref[rows, :] = x1
        h2_ref[rows, :] = _rms_norm(x1, gpre_ref[...]).astype(bf16)


def _outproj(m, w_out, x, g_post, g_pre, tm=512):
    L = x.shape[0]
    row = lambda: pl.BlockSpec((tm, D_MODEL), lambda i: (i, 0))
    return pl.pallas_call(
        _outproj_kernel,
        grid=(L // tm,),
        in_specs=[row(), _const_spec(w_out.shape), row(),
                  _const_spec((1, D_MODEL)), _const_spec((1, D_MODEL))],
        out_specs=[row(), row()],
        out_shape=[jax.ShapeDtypeStruct((L, D_MODEL), f32), jax.ShapeDtypeStruct((L, D_MODEL), bf16)],
        compiler_params=_params(("parallel",)),
        name="outproj",
    )(m, w_out, x, g_post, g_pre)


MLP_SUB = 512


def _mlp_kernel(h_ref, wu_ref, wd_ref, x_ref, g_ref, o_ref, acc_ref):
    i, j = pl.program_id(0), pl.program_id(1)
    n_tiles = pl.num_programs(0) - 1
    tm = h_ref.shape[0]
    slab = o_ref.shape[0]
    cur, prev = acc_ref.at[i % 2], acc_ref.at[(i + 1) % 2]
    slab_rows = pl.ds(pl.multiple_of(j * slab, slab), slab)

    def finish_prev_slab():
        o_ref[...] = x_ref[...] + _rms_norm(prev[slab_rows, :], g_ref[...])
        prev[slab_rows, :] = jnp.zeros((slab, D_MODEL), f32)

    @pl.when((i == 0) & (j == 0))
    def _():
        acc_ref[...] = jnp.zeros_like(acc_ref)

    @pl.when(i < n_tiles)
    def _():
        for r in range(tm // MLP_SUB):
            rows = pl.ds(r * MLP_SUB, MLP_SUB)
            a = jnp.maximum(_dot(h_ref[rows, :], wu_ref[...]), 0.0)
            cur[rows, :] += _dot((a * a).astype(bf16), wd_ref[...])
        finish_prev_slab()

    @pl.when(i == n_tiles)
    def _():
        finish_prev_slab()


def _mlp(h2, w_up, w_down, x1, g_post, tm=1024, tf=1024):
    L = x1.shape[0]
    n_tiles, n_ff = L // tm, D_FF // tf
    slab = tm // n_ff
    ff = lambda i, j: jnp.where(i < n_tiles, j, n_ff - 1)
    prev_slab = lambda i, j: (jnp.maximum(i - 1, 0) * n_ff + jnp.where(i > 0, j, 0), 0)
    return pl.pallas_call(
        _mlp_kernel,
        grid=(n_tiles + 1, n_ff),
        in_specs=[pl.BlockSpec((tm, D_MODEL), lambda i, j: (jnp.minimum(i, n_tiles - 1), 0)),
                  pl.BlockSpec((D_MODEL, tf), lambda i, j: (0, ff(i, j))),
                  pl.BlockSpec((tf, D_MODEL), lambda i, j: (ff(i, j), 0)),
                  pl.BlockSpec((slab, D_MODEL), prev_slab),
                  _const_spec((1, D_MODEL))],
        out_specs=pl.BlockSpec((slab, D_MODEL), prev_slab),
        out_shape=jax.ShapeDtypeStruct((L, D_MODEL), f32),
        scratch_shapes=[pltpu.VMEM((2, tm, D_MODEL), f32)],
        compiler_params=_params(("arbitrary", "arbitrary")),
        name="mlp",
    )(h2, w_up, w_down, x1, g_post)


def _rope_tables(L):
    half = ROT_DIM // 2
    inv = ROPE_THETA ** (-np.arange(half, dtype=np.float64) * 2.0 / ROT_DIM)
    ang = np.arange(L, dtype=np.float64)[:, None] * inv[None, :]
    cos, sin = np.cos(ang).astype(np.float32), np.sin(ang).astype(np.float32)
    ones = np.ones((L, HEAD_DIM - ROT_DIM), np.float32)
    zeros = np.zeros((L, HEAD_DIM - ROT_DIM), np.float32)
    zh = np.zeros((L, half), np.float32)
    per_head = lambda parts: np.tile(np.concatenate(parts, axis=1), (1, LANES // HEAD_DIM))
    return jnp.asarray(np.concatenate(
        [per_head([cos, cos, ones]), per_head([-sin, zh, zeros]), per_head([zh, sin, zeros])], axis=1))


def _layer(x, norm_mix_pre, norm_mix_post, norm_mlp_pre, norm_mlp_post, w_in, sinks,
           lam_re, lam_im, log_dt, b_re, b_im, c_re, c_im, d_skip, w_glu, w_branch, w_out,
           w_up, w_down, rope):
    gain = lambda g: g.astype(f32).reshape(1, D_MODEL)
    pre, pim, kt, qc, at_re, at_im, wmix = _ssm_operators(
        lam_re, lam_im, log_dt, b_re, b_im, c_re, c_im, d_skip, w_in.astype(f32))
    q, kv, u2, h, wg = _inproj(x, gain(norm_mix_pre), wmix, w_in.astype(f32), rope)
    gates, wup, (wglu, wb, wo) = _gates(h, wg, w_up.astype(f32),
                                        [w_glu.astype(f32), w_branch.astype(f32), w_out.astype(f32)])
    y2 = _ssm(u2, pre, pim, kt, qc, _replication_matrix(), at_re, at_im)
    mixed, wdn = _attn_mix(q, kv, sinks.astype(f32), y2, gates, wglu, wb, w_down.astype(f32))
    x1, h2 = _outproj(mixed, wo, x, gain(norm_mix_post), gain(norm_mlp_pre))
    return _mlp(h2, wup, wdn, x1, gain(norm_mlp_post))


def kernel(x, norm_mix_pre, norm_mix_post, norm_mlp_pre, norm_mlp_post, w_in, sinks, lam_re, lam_im, log_dt, b_re, b_im, c_re, c_im, d_skip, w_glu, w_branch, w_out, w_up, w_down):
    B, L, _ = x.shape
    depth = w_in.shape[0]
    rope = _rope_tables(L)
    outs = []
    for b in range(B):
        xb = x[b]
        for l in range(depth):
            xb = _layer(xb, norm_mix_pre[l], norm_mix_post[l], norm_mlp_pre[l], norm_mlp_post[l],
                        w_in[l], sinks[l], lam_re[l], lam_im[l], log_dt[l], b_re[l], b_im[l],
                        c_re[l], c_im[l], d_skip[l], w_glu[l], w_branch[l], w_out[l],
                        w_up[l], w_down[l], rope)
        outs.append(xb)
    return jnp.stack(outs)
```

```python
import math

import jax
import jax.numpy as jnp
import numpy as np
from jax import lax
from jax.experimental import pallas as pl
from jax.experimental.pallas import tpu as pltpu

D_MODEL = 2048
SEQ = 8192
HEAD_DIM = 64
N_Q_HEADS = 16
N_KV_HEADS = 2
BLOCK = 128
ROT_DIM = HEAD_DIM // 4
ROPE_THETA = 500000.0
Q_W = N_Q_HEADS * HEAD_DIM
KV_W = N_KV_HEADS * HEAD_DIM
SSM_W = D_MODEL // 2
QKV_W = Q_W + 2 * KV_W
IN_MIX_W = QKV_W + SSM_W
SSM_GC = 16
SSM_G = SSM_W // SSM_GC
SSM_P = 64
D_FF = 4 * D_MODEL
EPS = 1e-6

LANES = 128
CHUNK = 8
N_CHUNKS = SEQ // CHUNK
SG = SSM_W // LANES
GPS = LANES // SSM_GC
SG_IN = CHUNK * LANES
SG_STATE = 2 * GPS * SSM_P
VMEM_LIMIT = 56 * 1024 * 1024
ROW_SUB = 256
CHUNK_PITCH = CHUNK if (CHUNK // 8) % 2 else CHUNK + 8
OUT_SUB = 128
MASKED = float(jnp.finfo(jnp.float32).min)

bf16 = jnp.bfloat16
f32 = jnp.float32


def _dot(a, b):
    return jnp.dot(a, b, preferred_element_type=f32)


def _sigmoid(x):
    return 1.0 / (1.0 + jnp.exp(-x))


def _gelu_tanh(x):
    c = math.sqrt(2.0 / math.pi)
    return x * (0.5 * (1.0 + jnp.tanh(c * (x + 0.044715 * (x * x * x)))))


def _rms_norm(x, g):
    return x * lax.rsqrt(jnp.mean(x * x, axis=-1, keepdims=True) + EPS) * g


def _params(sem):
    return pltpu.CompilerParams(dimension_semantics=sem, vmem_limit_bytes=VMEM_LIMIT)


def _const_spec(shape):
    nd = len(shape)
    return pl.BlockSpec(shape, lambda *_: (0,) * nd, pipeline_mode=pl.Buffered(1))


def _rope_block(z, cos, sin_a, sin_b):
    return (z * cos + pltpu.roll(z, LANES - ROT_DIM // 2, axis=1) * sin_a
            + pltpu.roll(z, ROT_DIM // 2, axis=1) * sin_b)


def _inproj_kernel(x_ref, g_ref, w_ref, wcast_ref, rope_ref, q_ref, kv_ref, u2_ref, h_ref, wg_ref, zu_ref):
    wg_ref[...] = wcast_ref[...].astype(bf16)
    for r in range(x_ref.shape[0] // ROW_SUB):
        rows = pl.ds(r * ROW_SUB, ROW_SUB)
        h = _rms_norm(x_ref[rows, :], g_ref[...]).astype(bf16)
        h_ref[rows, :] = h
        cos, sin_a, sin_b = (rope_ref[rows, t * LANES:(t + 1) * LANES] for t in range(3))
        for half in range(2):
            zq = _dot(h, w_ref[:, half * 512:(half + 1) * 512])
            for b in range(4):
                blk = zq[:, b * LANES:(b + 1) * LANES]
                col = (half * 4 + b) * LANES
                q_ref[rows, col:col + LANES] = _rope_block(blk, cos, sin_a, sin_b).astype(bf16)
        zkv = _dot(h, w_ref[:, Q_W:QKV_W])
        kv_ref[rows, :KV_W] = _rope_block(zkv[:, :KV_W], cos, sin_a, sin_b).astype(bf16)
        kv_ref[rows, KV_W:] = zkv[:, KV_W:].astype(bf16)
        zu = _dot(h, w_ref[:, QKV_W:])
        chunk_rows = pl.ds(r * (ROW_SUB // CHUNK), ROW_SUB // CHUNK)
        for sg in range(SG):
            for n in range(ROW_SUB // CHUNK):
                zu_ref[r, sg, n * CHUNK_PITCH:n * CHUNK_PITCH + CHUNK, :] = (
                    zu[n * CHUNK:(n + 1) * CHUNK, sg * LANES:(sg + 1) * LANES])
            for s in range(CHUNK):
                col = sg * SG_IN + s * LANES
                u2_ref[chunk_rows, col:col + LANES] = (
                    zu_ref[r, sg, pl.ds(s, ROW_SUB // CHUNK, stride=CHUNK_PITCH), :].astype(bf16))


def _inproj(x, gain, wmix, w_in, rope, tm=512):
    L = x.shape[0]
    steps = L // tm
    gate_w = w_in.shape[1] - IN_MIX_W
    slab = gate_w // steps
    assert IN_MIX_W % slab == 0 and slab % LANES == 0
    row = lambda w: pl.BlockSpec((tm, w), lambda i: (i, 0))
    return pl.pallas_call(
        _inproj_kernel,
        grid=(steps,),
        in_specs=[row(D_MODEL), _const_spec((1, D_MODEL)), _const_spec(wmix.shape),
                  pl.BlockSpec((D_MODEL, slab), lambda i: (0, IN_MIX_W // slab + i)),
                  row(rope.shape[1])],
        out_specs=[row(Q_W), row(2 * KV_W),
                   pl.BlockSpec((tm // CHUNK, SG * SG_IN), lambda i: (i, 0)), row(D_MODEL),
                   pl.BlockSpec((D_MODEL, slab), lambda i: (0, i))],
        out_shape=[jax.ShapeDtypeStruct((L, Q_W), bf16), jax.ShapeDtypeStruct((L, 2 * KV_W), bf16),
                   jax.ShapeDtypeStruct((L // CHUNK, SG * SG_IN), bf16),
                   jax.ShapeDtypeStruct((L, D_MODEL), bf16),
                   jax.ShapeDtypeStruct((D_MODEL, gate_w), bf16)],
        scratch_shapes=[pltpu.VMEM((tm // ROW_SUB, SG, ROW_SUB // CHUNK * CHUNK_PITCH, LANES), f32)],
        compiler_params=_params(("parallel",)),
        name="inproj",
    )(x, gain, wmix, w_in, rope)


def _gates_kernel(h_ref, w_ref, *rest):
    n_cast = (len(rest) - 1) // 2
    o_ref = rest[n_cast]
    for r in range(h_ref.shape[0] // ROW_SUB):
        rows = pl.ds(r * ROW_SUB, ROW_SUB)
        o_ref[rows, :] = _sigmoid(_dot(h_ref[rows, :], w_ref[...])).astype(bf16)
    for src, dst in zip(rest[:n_cast], rest[n_cast + 1:]):
        dst[...] = src[...].astype(bf16)


def _gates(h, wg, w_up, row_cast_weights, tm=1024, tn=2048):
    L, N = h.shape[0], wg.shape[1]
    ni = L // tm
    steps = ni * (N // tn)
    step = lambda j, i: j * ni + i
    slabs = [pl.BlockSpec((w_up.shape[0], w_up.shape[1] // steps), lambda j, i: (0, step(j, i)))]
    slabs += [pl.BlockSpec((w.shape[0] // steps, w.shape[1]), lambda j, i: (step(j, i), 0))
              for w in row_cast_weights]
    cast_weights = [w_up] + list(row_cast_weights)
    outs = pl.pallas_call(
        _gates_kernel,
        grid=(N // tn, ni),
        in_specs=[pl.BlockSpec((tm, D_MODEL), lambda j, i: (i, 0)),
                  pl.BlockSpec((D_MODEL, tn), lambda j, i: (0, j))] + slabs,
        out_specs=[pl.BlockSpec((tm, tn), lambda j, i: (i, j))] + slabs,
        out_shape=[jax.ShapeDtypeStruct((L, N), bf16)]
                  + [jax.ShapeDtypeStruct(w.shape, bf16) for w in cast_weights],
        compiler_params=_params(("parallel", "arbitrary")),
        name="gates",
    )(h, wg, *cast_weights)
    return outs[0], outs[1], outs[2:]


GC_SHIFT = SSM_GC.bit_length() - 1
P_SHIFT = SSM_P.bit_length() - 1
Y_TILE = 256
NT = SG_STATE // LANES
NSEG = 16
SEG = N_CHUNKS // NSEG
SEG_PITCH = SEG + 8


def _ssm_kernel(u_ref, pre_ref, pim_ref, kt_ref, qc_ref, rep_ref, ar_ref, ai_ref, y_ref,
                p_scr, m_scr, q_scr, s_scr, xl_scr, pw_scr, xs_scr):
    half = SG_STATE // 2
    row = lax.broadcasted_iota(jnp.int32, (SG_IN, LANES), 0)
    lane = lax.broadcasted_iota(jnp.int32, (SG_IN, LANES), 1)
    row_grp = (row >> GC_SHIFT) & (GPS - 1)
    pre, pim = pre_ref[...], pim_ref[...]
    for k in range(half // LANES):
        sel = row_grp == 2 * k + (lane >> P_SHIFT)
        p_scr[:, k * LANES:(k + 1) * LANES] = jnp.where(sel, pre, 0.0).astype(bf16)
        p_scr[:, half + k * LANES:half + (k + 1) * LANES] = jnp.where(sel, pim, 0.0).astype(bf16)

    blk_row = lax.broadcasted_iota(jnp.int32, (LANES, LANES), 0) >> GC_SHIFT
    blk_lane = lax.broadcasted_iota(jnp.int32, (LANES, LANES), 1) >> GC_SHIFT
    zero_blk = jnp.zeros((LANES, LANES), bf16)
    for tau in range(CHUNK):
        blk = jnp.where(blk_row == blk_lane, kt_ref[tau * LANES:(tau + 1) * LANES, :], 0.0).astype(bf16)
        for s in range(CHUNK - tau):
            t = s + tau
            m_scr[s * LANES:(s + 1) * LANES, t * LANES:(t + 1) * LANES] = blk
            if tau > 0:
                m_scr[t * LANES:(t + 1) * LANES, s * LANES:(s + 1) * LANES] = zero_blk

    q_row_grp = (lax.broadcasted_iota(jnp.int32, (SG_STATE, Y_TILE), 0) >> P_SHIFT) & (GPS - 1)
    q_lane_grp = (lax.broadcasted_iota(jnp.int32, (SG_STATE, Y_TILE), 1) >> GC_SHIFT) & (GPS - 1)
    for j in range(SG_IN // Y_TILE):
        full = _dot(qc_ref[...], rep_ref[:, j * Y_TILE:(j + 1) * Y_TILE])
        q_scr[:, j * Y_TILE:(j + 1) * Y_TILE] = jnp.where(q_row_grp == q_lane_grp, full, 0.0).astype(bf16)

    s = _dot(u_ref[...], p_scr[...])
    for j in range(NT):
        for k in range(NSEG):
            s_scr[j, k * SEG_PITCH:k * SEG_PITCH + SEG, :] = s[k * SEG:(k + 1) * SEG, j * LANES:(j + 1) * LANES]
    ht = NT // 2
    cmul = lambda xr, xi, yr, yi: (xr * yr - xi * yi, xr * yi + xi * yr)
    ar = [ar_ref[:, j * LANES:(j + 1) * LANES] for j in range(ht)]
    ai = [ai_ref[:, j * LANES:(j + 1) * LANES] for j in range(ht)]
    one, nil = jnp.ones((1, LANES), f32), jnp.zeros((1, LANES), f32)
    a_seg = []
    for j in range(ht):
        rr, ri = [one], [nil]
        for _ in range(7):
            nr, ni = cmul(rr[-1], ri[-1], ar[j], ai[j])
            rr.append(nr)
            ri.append(ni)
        a8r, a8i = cmul(rr[-1], ri[-1], ar[j], ai[j])
        blk_r, blk_i = jnp.concatenate(rr, axis=0), jnp.concatenate(ri, axis=0)
        for b in range(SEG // 8):
            pw_scr[j, b * 8:(b + 1) * 8, :] = blk_r
            pw_scr[ht + j, b * 8:(b + 1) * 8, :] = blk_i
            blk_r, blk_i = cmul(blk_r, blk_i, a8r, a8i)
        a_seg.append((blk_r[0:1], blk_i[0:1]))

    ar_b = [jnp.broadcast_to(a, (NSEG, LANES)) for a in ar]
    ai_b = [jnp.broadcast_to(a, (NSEG, LANES)) for a in ai]

    def body(i, carry):
        new_r, new_i = [], []
        for j in range(ht):
            xr, xi = carry[j], carry[ht + j]
            seg_rows = pl.ds(i, NSEG, stride=SEG_PITCH)
            xl_scr[j, seg_rows, :] = xr
            xl_scr[ht + j, seg_rows, :] = xi
            pr, pi = cmul(xr, xi, ar_b[j], ai_b[j])
            new_r.append(pr + s_scr[j, seg_rows, :])
            new_i.append(pi + s_scr[ht + j, seg_rows, :])
        return tuple(new_r + new_i)

    ends = tuple(jnp.zeros((NSEG, LANES), f32) for _ in range(NT))
    for i in range(SEG):
        ends = body(i, ends)

    for j in range(ht):
        zr, zi = nil, nil
        for k in range(NSEG):
            rows = slice(k * SEG, (k + 1) * SEG)
            loc = slice(k * SEG_PITCH, k * SEG_PITCH + SEG)
            fr, fi = cmul(pw_scr[j], pw_scr[ht + j], zr, zi)
            xs_scr[rows, j * LANES:(j + 1) * LANES] = (xl_scr[j, loc, :] + fr).astype(bf16)
            xs_scr[rows, half + j * LANES:half + (j + 1) * LANES] = (xl_scr[ht + j, loc, :] + fi).astype(bf16)
            pr, pi = cmul(zr, zi, *a_seg[j])
            zr, zi = pr + ends[j][k:k + 1], pi + ends[ht + j][k:k + 1]

    xs = xs_scr[...]
    for j in range(SG_IN // Y_TILE):
        kk = (j + 1) * Y_TILE
        cols = slice(j * Y_TILE, (j + 1) * Y_TILE)
        y = _dot(u_ref[:, :kk], m_scr[:kk, cols]) + _dot(xs, q_scr[:, cols])
        y_ref[:, cols] = _gelu_tanh(y).astype(bf16)


def _ssm(u2, pre, pim, kt, qc, rep, at_re, at_im):
    per_sg = lambda r, c: pl.BlockSpec((None, r, c), lambda g: (g, 0, 0))
    return pl.pallas_call(
        _ssm_kernel,
        grid=(SG,),
        in_specs=[pl.BlockSpec((N_CHUNKS, SG_IN), lambda g: (0, g)),
                  per_sg(SG_IN, LANES), per_sg(SG_IN, LANES), per_sg(SG_IN, LANES),
                  per_sg(SG_STATE, CHUNK * SSM_GC), _const_spec(rep.shape),
                  per_sg(1, SG_STATE // 2), per_sg(1, SG_STATE // 2)],
        out_specs=pl.BlockSpec((N_CHUNKS, SG_IN), lambda g: (0, g)),
        out_shape=jax.ShapeDtypeStruct((N_CHUNKS, SG * SG_IN), bf16),
        scratch_shapes=[pltpu.VMEM((SG_IN, SG_STATE), bf16), pltpu.VMEM((SG_IN, SG_IN), bf16),
                        pltpu.VMEM((SG_STATE, SG_IN), bf16),
                        pltpu.VMEM((NT, NSEG * SEG_PITCH, LANES), f32),
                        pltpu.VMEM((NT, NSEG * SEG_PITCH, LANES), f32),
                        pltpu.VMEM((NT, SEG, LANES), f32), pltpu.VMEM((N_CHUNKS, SG_STATE), bf16)],
        compiler_params=_params(("arbitrary",)),
        name="ssm",
    )(u2, pre, pim, kt, qc, rep, at_re, at_im)


def _ssm_prep_kernel(lam_re_ref, lam_im_ref, ldt_ref, bt_re_ref, bt_im_ref, c_re_ref, c_im_ref, d_ref,
                     eye_ref, win_ref, pre_ref, pim_ref, kt_ref, qc_ref, at_re_ref, at_im_ref, wmix_ref):
    wmix_ref[...] = win_ref[...].astype(bf16)
    dup = lambda x: jnp.concatenate([x, x], axis=1)
    lr, li = dup(lam_re_ref[...]), dup(lam_im_ref[...])
    dt = jnp.exp(ldt_ref[...])
    mag = jnp.exp(lr * dt)
    a_re, a_im = mag * jnp.cos(li * dt), mag * jnp.sin(li * dt)
    den = lr * lr + li * li
    nr, ni = a_re - 1.0, a_im
    coef_re = (nr * lr + ni * li) / den
    coef_im = (ni * lr - nr * li) / den
    pw_re, pw_im = [jnp.ones_like(a_re)], [jnp.zeros_like(a_im)]
    for _ in range(CHUNK):
        pr, pi = pw_re[-1], pw_im[-1]
        pw_re.append(pr * a_re - pi * a_im)
        pw_im.append(pr * a_im + pi * a_re)

    low = lax.broadcasted_iota(jnp.int32, (SSM_GC, LANES), 1) < SSM_P
    diag = ((lax.broadcasted_iota(jnp.int32, (SSM_GC, LANES), 1) & (SSM_GC - 1))
            == lax.broadcasted_iota(jnp.int32, (SSM_GC, LANES), 0))
    contract_lanes = (((1,), (1,)), ((), ()))
    for g in range(GPS):
        grp = slice(g * SSM_GC, (g + 1) * SSM_GC)
        one = lambda x: x[g:g + 1]
        bt_re, bt_im = dup(bt_re_ref[grp, :]), dup(bt_im_ref[grp, :])
        bb_re = one(coef_re) * bt_re - one(coef_im) * bt_im
        bb_im = one(coef_re) * bt_im + one(coef_im) * bt_re
        yc = jnp.concatenate([c_re_ref[grp, :], -c_im_ref[grp, :]], axis=1)
        yc_sw = pltpu.roll(yc, SSM_P, axis=1)
        y2 = jnp.where(low, yc_sw, -yc_sw)
        x_rows, ca_rows = [], []
        for tau in range(CHUNK):
            pr, pi = one(pw_re[tau]), one(pw_im[tau])
            ab_re = bb_re * pr - bb_im * pi
            ab_im = bb_re * pi + bb_im * pr
            s = CHUNK - 1 - tau
            rows = slice(s * LANES + g * SSM_GC, s * LANES + (g + 1) * SSM_GC)
            pre_ref[rows, :] = ab_re
            pim_ref[rows, :] = ab_im
            x_rows.append(jnp.where(low, ab_re, ab_im))
            ca_rows.append(yc * one(pw_re[tau + 1]) + y2 * one(pw_im[tau + 1]))
        kern = lax.dot_general(jnp.concatenate(x_rows, axis=0), jnp.concatenate([yc] * GPS, axis=0),
                               contract_lanes, precision=lax.Precision.HIGHEST,
                               preferred_element_type=f32)
        for tau in range(CHUNK):
            blk = kern[tau * SSM_GC:(tau + 1) * SSM_GC]
            if tau == 0:
                blk = blk + jnp.where(diag, d_ref[...], 0.0)
            kt_ref[tau * LANES + g * SSM_GC:tau * LANES + (g + 1) * SSM_GC, :] = blk
        qt = lax.dot_general(eye_ref[...], jnp.concatenate(ca_rows, axis=0).astype(bf16),
                             contract_lanes, preferred_element_type=f32)
        qc_ref[g * SSM_P:(g + 1) * SSM_P, :] = qt[:SSM_P].astype(bf16)
        qc_ref[GPS * SSM_P + g * SSM_P:GPS * SSM_P + (g + 1) * SSM_P, :] = qt[SSM_P:].astype(bf16)
    lo_lane = lax.broadcasted_iota(jnp.int32, (1, LANES), 1) < SSM_P
    for k in range(GPS // 2):
        sel = lambda pw: jnp.where(lo_lane, pw[2 * k:2 * k + 1], pw[2 * k + 1:2 * k + 2])
        at_re_ref[:, k * LANES:(k + 1) * LANES] = sel(pw_re[CHUNK])
        at_im_ref[:, k * LANES:(k + 1) * LANES] = sel(pw_im[CHUNK])


def _ssm_operators(lam_re, lam_im, log_dt, b_re, b_im, c_re, c_im, d_skip, w_in):
    per_sg = lambda x: x.astype(f32).reshape((SG, x.size // (SG * x.shape[-1]), x.shape[-1]))
    bt = lambda b: per_sg(b.swapaxes(1, 2))
    args = (per_sg(lam_re), per_sg(lam_im), log_dt.astype(f32).reshape(SG, GPS, 1),
            bt(b_re), bt(b_im), per_sg(c_re), per_sg(c_im), d_skip.astype(f32).reshape(SG, 1, LANES),
            jnp.asarray(np.eye(LANES), bf16), w_in)
    blk = lambda r, c: pl.BlockSpec((None, r, c), lambda g: (g, 0, 0))
    slab = pl.BlockSpec((w_in.shape[0] // SG, IN_MIX_W), lambda g: (g, 0))
    return pl.pallas_call(
        _ssm_prep_kernel,
        grid=(SG,),
        in_specs=[blk(GPS, SSM_P), blk(GPS, SSM_P), blk(GPS, 1), blk(LANES, SSM_P), blk(LANES, SSM_P),
                  blk(LANES, SSM_P), blk(LANES, SSM_P), blk(1, LANES), _const_spec((LANES, LANES)), slab],
        out_specs=[blk(SG_IN, LANES), blk(SG_IN, LANES), blk(SG_IN, LANES),
                   blk(SG_STATE, CHUNK * SSM_GC), blk(1, SG_STATE // 2), blk(1, SG_STATE // 2),
                   slab],
        out_shape=[jax.ShapeDtypeStruct((SG, SG_IN, LANES), f32)] * 3
                  + [jax.ShapeDtypeStruct((SG, SG_STATE, CHUNK * SSM_GC), bf16)]
                  + [jax.ShapeDtypeStruct((SG, 1, SG_STATE // 2), f32)] * 2
                  + [jax.ShapeDtypeStruct((w_in.shape[0], IN_MIX_W), bf16)],
        compiler_params=_params(("parallel",)),
        name="ssm_prep",
    )(*args)


def _replication_matrix():
    src = np.arange(CHUNK * SSM_GC)
    dst = np.arange(SG_IN)
    same_t = (src[:, None] // SSM_GC) == (dst[None, :] // LANES)
    same_c = (src[:, None] % SSM_GC) == (dst[None, :] % SSM_GC)
    return jnp.asarray(same_t & same_c, bf16)


def _kv_operands(kvp_ref, kvc_ref):
    kcat = jnp.concatenate([kvp_ref[:, :KV_W], kvc_ref[:, :KV_W]], axis=0).astype(f32)
    vcat = jnp.concatenate([kvp_ref[:, KV_W:], kvc_ref[:, KV_W:]], axis=0).astype(f32)
    k_sw = pltpu.roll(kcat, HEAD_DIM, axis=1)
    v_sw = pltpu.roll(vcat, HEAD_DIM, axis=1)
    low_kv = lax.broadcasted_iota(jnp.int32, kcat.shape, 1) < HEAD_DIM
    operands = []
    for hk in range(N_KV_HEADS):
        k_src, k_oth = (kcat, k_sw) if hk == 0 else (k_sw, kcat)
        k_lo = jnp.where(low_kv, k_src, 0.0).astype(bf16)
        k_hi = jnp.where(low_kv, 0.0, k_oth).astype(bf16)
        vv = (jnp.where(low_kv, vcat, v_sw) if hk == 0 else jnp.where(low_kv, v_sw, vcat)).astype(bf16)
        operands.append((k_lo, k_hi, vv))
    return operands


def _attention_block(q, kv_operands, sinks_ref, blk, no_prev, oa_ref):
    q_rows = slice(blk * BLOCK, (blk + 1) * BLOCK)
    kv_rows = slice(blk * BLOCK, (blk + 2) * BLOCK)
    from_prev = (lax.broadcasted_iota(jnp.int32, (BLOCK, BLOCK), 1)
                 > lax.broadcasted_iota(jnp.int32, (BLOCK, BLOCK), 0))
    low_o = lax.broadcasted_iota(jnp.int32, (BLOCK, LANES), 1) < HEAD_DIM
    contract_lanes = (((1,), (1,)), ((), ()))
    pairs_per_kv = (N_Q_HEADS // N_KV_HEADS) // 2
    for hk, (k_lo, k_hi, vv) in enumerate(kv_operands):
        qs = jnp.concatenate(
            [q[q_rows, (pairs_per_kv * hk + b) * LANES:(pairs_per_kv * hk + b + 1) * LANES]
             for b in range(pairs_per_kv)], axis=0)
        s_par = (lax.dot_general(qs, k_lo[kv_rows], contract_lanes, preferred_element_type=f32),
                 lax.dot_general(qs, k_hi[kv_rows], contract_lanes, preferred_element_type=f32))
        for b in range(pairs_per_kv):
            outs = []
            for par in range(2):
                head = 2 * (pairs_per_kv * hk + b) + par
                s_all = s_par[par][b * BLOCK:(b + 1) * BLOCK]
                s_prev = s_all[:, :BLOCK] + no_prev if blk == 0 else s_all[:, :BLOCK]
                s = jnp.where(from_prev, s_prev, s_all[:, BLOCK:])
                sink = sinks_ref[head]
                m = jnp.maximum(jnp.max(s, axis=1, keepdims=True), sink)
                p = jnp.exp(s - m)
                denom = jnp.sum(p, axis=1, keepdims=True) + jnp.exp(sink - m)
                p_kv = jnp.concatenate([jnp.where(from_prev, p, 0.0), jnp.where(from_prev, 0.0, p)],
                                       axis=1).astype(bf16)
                outs.append(_dot(p_kv, vv[kv_rows]) * (1.0 / denom))
            col = (pairs_per_kv * hk + b) * LANES
            oa_ref[q_rows, col:col + LANES] = jnp.where(low_o, outs[0], outs[1]).astype(bf16)


def _attn_mix_kernel(sinks_ref, q_ref, kvp_ref, kvc_ref, y2_ref, gates_ref,
                     wglu_ref, wb_ref, wdn_ref, m_ref, wdn_o_ref, oa_ref, yb_ref):
    wdn_o_ref[...] = wdn_ref[...].astype(bf16)
    no_prev = jnp.where(pl.program_id(0) > 0, 0.0, MASKED)
    q = q_ref[...] * jnp.asarray(1.0 / math.sqrt(HEAD_DIM), bf16)
    kv_operands = _kv_operands(kvp_ref, kvc_ref)
    for r in range(q_ref.shape[0] // ROW_SUB):
        rows = pl.ds(r * ROW_SUB, ROW_SUB)
        chunk_rows = pl.ds(r * (ROW_SUB // CHUNK), ROW_SUB // CHUNK)
        for sg in range(SG):
            for t in range(CHUNK):
                col = sg * SG_IN + t * LANES
                yb_ref[r, sg, pl.ds(t, ROW_SUB // CHUNK, stride=CHUNK_PITCH), :] = (
                    y2_ref[chunk_rows, col:col + LANES].astype(f32))
        y = jnp.concatenate(
            [jnp.concatenate([yb_ref[r, sg, n * CHUNK_PITCH:n * CHUNK_PITCH + CHUNK, :]
                              for n in range(ROW_SUB // CHUNK)], axis=0) for sg in range(SG)],
            axis=1).astype(bf16)
        zg = _dot(y, wglu_ref[...])
        o_ssm = zg[:, :SSM_W] * _sigmoid(zg[:, SSM_W:])
        y_s = _dot(o_ssm.astype(bf16), wb_ref[Q_W:, :])
        for blk in range(r * (ROW_SUB // BLOCK), (r + 1) * (ROW_SUB // BLOCK)):
            _attention_block(q, kv_operands, sinks_ref, blk, no_prev, oa_ref)
        y_a = _dot(oa_ref[rows, :], wb_ref[:Q_W, :])
        m_ref[rows, :] = (gates_ref[rows, :D_MODEL].astype(f32) * y_a
                          + gates_ref[rows, D_MODEL:].astype(f32) * y_s).astype(bf16)


def _attn_mix(q, kv, sinks, y2, gates, wglu, wb, w_down, tm=512):
    L = q.shape[0]
    cur = lambda i: (i, 0)
    prev = lambda i: (jnp.maximum(i * (tm // BLOCK) - 1, 0), 0)
    dn_slab = pl.BlockSpec((w_down.shape[0] // (L // tm), w_down.shape[1]), cur)
    return pl.pallas_call(
        _attn_mix_kernel,
        grid=(L // tm,),
        in_specs=[pl.BlockSpec(memory_space=pltpu.SMEM),
                  pl.BlockSpec((tm, Q_W), cur),
                  pl.BlockSpec((BLOCK, 2 * KV_W), prev), pl.BlockSpec((tm, 2 * KV_W), cur),
                  pl.BlockSpec((tm // CHUNK, SG * SG_IN), cur),
                  pl.BlockSpec((tm, 2 * D_MODEL), cur),
                  _const_spec(wglu.shape), _const_spec(wb.shape), dn_slab],
        out_specs=[pl.BlockSpec((tm, D_MODEL), cur), dn_slab],
        out_shape=[jax.ShapeDtypeStruct((L, D_MODEL), bf16), jax.ShapeDtypeStruct(w_down.shape, bf16)],
        scratch_shapes=[pltpu.VMEM((tm, Q_W), bf16),
                        pltpu.VMEM((tm // ROW_SUB, SG, ROW_SUB // CHUNK * CHUNK_PITCH, LANES), f32)],
        compiler_params=_params(("parallel",)),
        name="attn_mix",
    )(sinks, q, kv, kv, y2, gates, wglu, wb, w_down)


X_RING = 3


def _outproj_kernel(m_ref, w_ref, x_hbm, gpost_ref, gpre_ref, x1_ref, h2_ref, xbuf, sem):
    s, n, tm = pl.program_id(0), pl.num_programs(0), m_ref.shape[0]

    def x_copy(step):
        slot = step % X_RING
        return pltpu.make_async_copy(x_hbm.at[pl.ds(pl.multiple_of(step * tm, tm), tm)], xbuf.at[slot], sem.at[slot])

    @pl.when(s == 0)
    def _():
        for step in range(X_RING - 1):
            x_copy(step).start()

    @pl.when(s + (X_RING - 1) < n)
    def _():
        x_copy(s + (X_RING - 1)).start()

    x_copy(s).wait()
    x_ref = xbuf.at[s % X_RING]
    for r in range(m_ref.shape[0] // OUT_SUB):
        rows = pl.ds(r * OUT_SUB, OUT_SUB)
        out = _dot(m_ref[rows, :], w_ref[...])
        x1 = x_ref[rows, :] + _rms_norm(out, gpost_ref[...])
        x1_ref[rows, :] = x1
        h2_ref[rows, :] = _rms_norm(x1, gpre_ref[...]).astype(bf16)


def _outproj(m, w_out, x, g_post, g_pre, tm=512):
    L = x.shape[0]
    assert L // tm >= X_RING - 1
    row = lambda: pl.BlockSpec((tm, D_MODEL), lambda i: (i, 0))
    return pl.pallas_call(
        _outproj_kernel,
        grid=(L // tm,),
        in_specs=[row(), _const_spec(w_out.shape), pl.BlockSpec(memory_space=pl.ANY),
                  _const_spec((1, D_MODEL)), _const_spec((1, D_MODEL))],
        out_specs=[row(), row()],
        out_shape=[jax.ShapeDtypeStruct((L, D_MODEL), f32), jax.ShapeDtypeStruct((L, D_MODEL), bf16)],
        scratch_shapes=[pltpu.VMEM((X_RING, tm, D_MODEL), f32), pltpu.SemaphoreType.DMA((X_RING,))],
        compiler_params=_params(("arbitrary",)),
        name="outproj",
    )(m, w_out, x, g_post, g_pre)


MLP_SUB = 512


def _mlp_kernel(h_ref, wu_ref, wd_ref, x_ref, g_ref, o_ref, acc_ref):
    i, j = pl.program_id(0), pl.program_id(1)
    n_tiles = pl.num_programs(0) - 1
    tm = h_ref.shape[0]
    slab = o_ref.shape[0]
    cur, prev = acc_ref.at[i % 2], acc_ref.at[(i + 1) % 2]
    slab_rows = pl.ds(pl.multiple_of(j * slab, slab), slab)

    def finish_prev_slab():
        o_ref[...] = x_ref[...] + _rms_norm(prev[slab_rows, :], g_ref[...])
        prev[slab_rows, :] = jnp.zeros((slab, D_MODEL), f32)

    @pl.when((i == 0) & (j == 0))
    def _():
        acc_ref[...] = jnp.zeros_like(acc_ref)

    @pl.when(i < n_tiles)
    def _():
        for r in range(tm // MLP_SUB):
            rows = pl.ds(r * MLP_SUB, MLP_SUB)
            a = jnp.maximum(_dot(h_ref[rows, :], wu_ref[...]), 0.0)
            cur[rows, :] += _dot((a * a).astype(bf16), wd_ref[...])
        finish_prev_slab()

    @pl.when(i == n_tiles)
    def _():
        finish_prev_slab()


def _mlp(h2, w_up, w_down, x1, g_post, tm=1024, tf=1024):
    L = x1.shape[0]
    n_tiles, n_ff = L // tm, D_FF // tf
    slab = tm // n_ff
    ff = lambda i, j: jnp.where(i < n_tiles, j, n_ff - 1)
    prev_slab = lambda i, j: (jnp.maximum(i - 1, 0) * n_ff + jnp.where(i > 0, j, 0), 0)
    return pl.pallas_call(
        _mlp_kernel,
        grid=(n_tiles + 1, n_ff),
        in_specs=[pl.BlockSpec((tm, D_MODEL), lambda i, j: (jnp.minimum(i, n_tiles - 1), 0)),
                  pl.BlockSpec((D_MODEL, tf), lambda i, j: (0, ff(i, j))),
                  pl.BlockSpec((tf, D_MODEL), lambda i, j: (ff(i, j), 0)),
                  pl.BlockSpec((slab, D_MODEL), prev_slab),
                  _const_spec((1, D_MODEL))],
        out_specs=pl.BlockSpec((slab, D_MODEL), prev_slab),
        out_shape=jax.ShapeDtypeStruct((L, D_MODEL), f32),
        scratch_shapes=[pltpu.VMEM((2, tm, D_MODEL), f32)],
        compiler_params=_params(("arbitrary", "arbitrary")),
        name="mlp",
    )(h2, w_up, w_down, x1, g_post)


def _rope_tables(L):
    half = ROT_DIM // 2
    inv = ROPE_THETA ** (-np.arange(half, dtype=np.float64) * 2.0 / ROT_DIM)
    ang = np.arange(L, dtype=np.float64)[:, None] * inv[None, :]
    cos, sin = np.cos(ang).astype(np.float32), np.sin(ang).astype(np.float32)
    ones = np.ones((L, HEAD_DIM - ROT_DIM), np.float32)
    zeros = np.zeros((L, HEAD_DIM - ROT_DIM), np.float32)
    zh = np.zeros((L, half), np.float32)
    per_head = lambda parts: np.tile(np.concatenate(parts, axis=1), (1, LANES // HEAD_DIM))
    return jnp.asarray(np.concatenate(
        [per_head([cos, cos, ones]), per_head([-sin, zh, zeros]), per_head([zh, sin, zeros])], axis=1))


def _layer(x, norm_mix_pre, norm_mix_post, norm_mlp_pre, norm_mlp_post, w_in, sinks,
           lam_re, lam_im, log_dt, b_re, b_im, c_re, c_im, d_skip, w_glu, w_branch, w_out,
           w_up, w_down, rope):
    gain = lambda g: g.astype(f32).reshape(1, D_MODEL)
    pre, pim, kt, qc, at_re, at_im, wmix = _ssm_operators(
        lam_re, lam_im, log_dt, b_re, b_im, c_re, c_im, d_skip, w_in.astype(f32))
    q, kv, u2, h, wg = _inproj(x, gain(norm_mix_pre), wmix, w_in.astype(f32), rope)
    gates, wup, (wglu, wb, wo) = _gates(h, wg, w_up.astype(f32),
                                        [w_glu.astype(f32), w_branch.astype(f32), w_out.astype(f32)])
    y2 = _ssm(u2, pre, pim, kt, qc, _replication_matrix(), at_re, at_im)
    mixed, wdn = _attn_mix(q, kv, sinks.astype(f32), y2, gates, wglu, wb, w_down.astype(f32))
    x1, h2 = _outproj(mixed, wo, x, gain(norm_mix_post), gain(norm_mlp_pre))
    return _mlp(h2, wup, wdn, x1, gain(norm_mlp_post))


def kernel(x, norm_mix_pre, norm_mix_post, norm_mlp_pre, norm_mlp_post, w_in, sinks, lam_re, lam_im, log_dt, b_re, b_im, c_re, c_im, d_skip, w_glu, w_branch, w_out, w_up, w_down):
    B, L, _ = x.shape
    depth = w_in.shape[0]
    rope = _rope_tables(L)
    outs = []
    for b in range(B):
        xb = x[b]
        for l in range(depth):
            xb = _layer(xb, norm_mix_pre[l], norm_mix_post[l], norm_mlp_pre[l], norm_mlp_post[l],
                        w_in[l], sinks[l], lam_re[l], lam_im[l], log_dt[l], b_re[l], b_im[l],
                        c_re[l], c_im[l], d_skip[l], w_glu[l], w_branch[l], w_out[l],
                        w_up[l], w_down[l], rope)
        outs.append(xb)
    return jnp.stack(outs)
```

```python
import math

import jax
import jax.numpy as jnp
import numpy as np
from jax import lax
from jax.experimental import pallas as pl
from jax.experimental.pallas import tpu as pltpu

D_MODEL = 2048
SEQ = 8192
HEAD_DIM = 64
N_Q_HEADS = 16
N_KV_HEADS = 2
BLOCK = 128
ROT_DIM = HEAD_DIM // 4
ROPE_THETA = 500000.0
Q_W = N_Q_HEADS * HEAD_DIM
KV_W = N_KV_HEADS * HEAD_DIM
SSM_W = D_MODEL // 2
QKV_W = Q_W + 2 * KV_W
IN_MIX_W = QKV_W + SSM_W
SSM_GC = 16
SSM_G = SSM_W // SSM_GC
SSM_P = 64
D_FF = 4 * D_MODEL
EPS = 1e-6

LANES = 128
CHUNK = 8
N_CHUNKS = SEQ // CHUNK
SG = SSM_W // LANES
GPS = LANES // SSM_GC
SG_IN = CHUNK * LANES
SG_STATE = 2 * GPS * SSM_P
VMEM_LIMIT = 56 * 1024 * 1024
ROW_SUB = 256
CHUNK_PITCH = CHUNK if (CHUNK // 8) % 2 else CHUNK + 8
OUT_SUB = 128
MASKED = float(jnp.finfo(jnp.float32).min)

bf16 = jnp.bfloat16
f32 = jnp.float32


def _dot(a, b):
    return jnp.dot(a, b, preferred_element_type=f32)


def _sigmoid(x):
    return 1.0 / (1.0 + jnp.exp(-x))


def _gelu_tanh(x):
    c = math.sqrt(2.0 / math.pi)
    return x * (0.5 * (1.0 + jnp.tanh(c * (x + 0.044715 * (x * x * x)))))


def _rms_norm(x, g):
    return x * lax.rsqrt(jnp.mean(x * x, axis=-1, keepdims=True) + EPS) * g


def _params(sem):
    return pltpu.CompilerParams(dimension_semantics=sem, vmem_limit_bytes=VMEM_LIMIT)


def _const_spec(shape):
    nd = len(shape)
    return pl.BlockSpec(shape, lambda *_: (0,) * nd, pipeline_mode=pl.Buffered(1))


def _rope_block(z, cos, sin_a, sin_b):
    return (z * cos + pltpu.roll(z, LANES - ROT_DIM // 2, axis=1) * sin_a
            + pltpu.roll(z, ROT_DIM // 2, axis=1) * sin_b)


def _inproj_kernel(x_ref, g_ref, w_ref, wcast_ref, rope_ref, q_ref, kv_ref, u2_ref, h_ref, wg_ref, zu_ref):
    wg_ref[...] = wcast_ref[...].astype(bf16)
    for r in range(x_ref.shape[0] // ROW_SUB):
        rows = pl.ds(r * ROW_SUB, ROW_SUB)
        h = _rms_norm(x_ref[rows, :], g_ref[...]).astype(bf16)
        h_ref[rows, :] = h
        cos, sin_a, sin_b = (rope_ref[rows, t * LANES:(t + 1) * LANES] for t in range(3))
        for half in range(2):
            zq = _dot(h, w_ref[:, half * 512:(half + 1) * 512])
            for b in range(4):
                blk = zq[:, b * LANES:(b + 1) * LANES]
                col = (half * 4 + b) * LANES
                q_ref[rows, col:col + LANES] = _rope_block(blk, cos, sin_a, sin_b).astype(bf16)
        zkv = _dot(h, w_ref[:, Q_W:QKV_W])
        kv_ref[rows, :KV_W] = _rope_block(zkv[:, :KV_W], cos, sin_a, sin_b).astype(bf16)
        kv_ref[rows, KV_W:] = zkv[:, KV_W:].astype(bf16)
        zu = _dot(h, w_ref[:, QKV_W:])
        chunk_rows = pl.ds(r * (ROW_SUB // CHUNK), ROW_SUB // CHUNK)
        for sg in range(SG):
            for n in range(ROW_SUB // CHUNK):
                zu_ref[r, sg, n * CHUNK_PITCH:n * CHUNK_PITCH + CHUNK, :] = (
                    zu[n * CHUNK:(n + 1) * CHUNK, sg * LANES:(sg + 1) * LANES])
            for s in range(CHUNK):
                col = sg * SG_IN + s * LANES
                u2_ref[chunk_rows, col:col + LANES] = (
                    zu_ref[r, sg, pl.ds(s, ROW_SUB // CHUNK, stride=CHUNK_PITCH), :].astype(bf16))


def _inproj(x, gain, wmix, w_in, rope, tm=512):
    L = x.shape[0]
    steps = L // tm
    gate_w = w_in.shape[1] - IN_MIX_W
    slab = gate_w // steps
    assert IN_MIX_W % slab == 0 and slab % LANES == 0
    row = lambda w: pl.BlockSpec((tm, w), lambda i: (i, 0))
    return pl.pallas_call(
        _inproj_kernel,
        grid=(steps,),
        in_specs=[row(D_MODEL), _const_spec((1, D_MODEL)), _const_spec(wmix.shape),
                  pl.BlockSpec((D_MODEL, slab), lambda i: (0, IN_MIX_W // slab + i)),
                  row(rope.shape[1])],
        out_specs=[row(Q_W), row(2 * KV_W),
                   pl.BlockSpec((tm // CHUNK, SG * SG_IN), lambda i: (i, 0)), row(D_MODEL),
                   pl.BlockSpec((D_MODEL, slab), lambda i: (0, i))],
        out_shape=[jax.ShapeDtypeStruct((L, Q_W), bf16), jax.ShapeDtypeStruct((L, 2 * KV_W), bf16),
                   jax.ShapeDtypeStruct((L // CHUNK, SG * SG_IN), bf16),
                   jax.ShapeDtypeStruct((L, D_MODEL), bf16),
                   jax.ShapeDtypeStruct((D_MODEL, gate_w), bf16)],
        scratch_shapes=[pltpu.VMEM((tm // ROW_SUB, SG, ROW_SUB // CHUNK * CHUNK_PITCH, LANES), f32)],
        compiler_params=_params(("parallel",)),
        name="inproj",
    )(x, gain, wmix, w_in, rope)


def _gates_kernel(h_ref, w_ref, *rest):
    n_cast = (len(rest) - 1) // 2
    o_ref = rest[n_cast]
    for r in range(h_ref.shape[0] // ROW_SUB):
        rows = pl.ds(r * ROW_SUB, ROW_SUB)
        o_ref[rows, :] = _sigmoid(_dot(h_ref[rows, :], w_ref[...])).astype(bf16)
    for src, dst in zip(rest[:n_cast], rest[n_cast + 1:]):
        dst[...] = src[...].astype(bf16)


def _gates(h, wg, w_up, row_cast_weights, tm=1024, tn=2048):
    L, N = h.shape[0], wg.shape[1]
    ni = L // tm
    steps = ni * (N // tn)
    step = lambda j, i: j * ni + i
    slabs = [pl.BlockSpec((w_up.shape[0], w_up.shape[1] // steps), lambda j, i: (0, step(j, i)))]
    slabs += [pl.BlockSpec((w.shape[0] // steps, w.shape[1]), lambda j, i: (step(j, i), 0))
              for w in row_cast_weights]
    cast_weights = [w_up] + list(row_cast_weights)
    outs = pl.pallas_call(
        _gates_kernel,
        grid=(N // tn, ni),
        in_specs=[pl.BlockSpec((tm, D_MODEL), lambda j, i: (i, 0)),
                  pl.BlockSpec((D_MODEL, tn), lambda j, i: (0, j))] + slabs,
        out_specs=[pl.BlockSpec((tm, tn), lambda j, i: (i, j))] + slabs,
        out_shape=[jax.ShapeDtypeStruct((L, N), bf16)]
                  + [jax.ShapeDtypeStruct(w.shape, bf16) for w in cast_weights],
        compiler_params=_params(("parallel", "arbitrary")),
        name="gates",
    )(h, wg, *cast_weights)
    return outs[0], outs[1], outs[2:]


GC_SHIFT = SSM_GC.bit_length() - 1
P_SHIFT = SSM_P.bit_length() - 1
Y_TILE = 256
NT = SG_STATE // LANES
NSEG = 16
SEG = N_CHUNKS // NSEG
SEG_PITCH = SEG + 8


def _ssm_kernel(u_ref, pre_ref, pim_ref, kt_ref, qc_ref, rep_ref, ar_ref, ai_ref, y_ref,
                p_scr, m_scr, q_scr, s_scr, xl_scr, pw_scr, xs_scr):
    half = SG_STATE // 2
    row = lax.broadcasted_iota(jnp.int32, (SG_IN, LANES), 0)
    lane = lax.broadcasted_iota(jnp.int32, (SG_IN, LANES), 1)
    row_grp = (row >> GC_SHIFT) & (GPS - 1)
    pre, pim = pre_ref[...], pim_ref[...]
    for k in range(half // LANES):
        sel = row_grp == 2 * k + (lane >> P_SHIFT)
        p_scr[:, k * LANES:(k + 1) * LANES] = jnp.where(sel, pre, 0.0).astype(bf16)
        p_scr[:, half + k * LANES:half + (k + 1) * LANES] = jnp.where(sel, pim, 0.0).astype(bf16)

    blk_row = lax.broadcasted_iota(jnp.int32, (LANES, LANES), 0) >> GC_SHIFT
    blk_lane = lax.broadcasted_iota(jnp.int32, (LANES, LANES), 1) >> GC_SHIFT
    zero_blk = jnp.zeros((LANES, LANES), bf16)
    for tau in range(CHUNK):
        blk = jnp.where(blk_row == blk_lane, kt_ref[tau * LANES:(tau + 1) * LANES, :], 0.0).astype(bf16)
        for s in range(CHUNK - tau):
            t = s + tau
            m_scr[s * LANES:(s + 1) * LANES, t * LANES:(t + 1) * LANES] = blk
            if tau > 0:
                m_scr[t * LANES:(t + 1) * LANES, s * LANES:(s + 1) * LANES] = zero_blk

    q_row_grp = (lax.broadcasted_iota(jnp.int32, (SG_STATE, Y_TILE), 0) >> P_SHIFT) & (GPS - 1)
    q_lane_grp = (lax.broadcasted_iota(jnp.int32, (SG_STATE, Y_TILE), 1) >> GC_SHIFT) & (GPS - 1)
    for j in range(SG_IN // Y_TILE):
        full = _dot(qc_ref[...], rep_ref[:, j * Y_TILE:(j + 1) * Y_TILE])
        q_scr[:, j * Y_TILE:(j + 1) * Y_TILE] = jnp.where(q_row_grp == q_lane_grp, full, 0.0).astype(bf16)

    s = _dot(u_ref[...], p_scr[...])
    for j in range(NT):
        for k in range(NSEG):
            s_scr[j, k * SEG_PITCH:k * SEG_PITCH + SEG, :] = s[k * SEG:(k + 1) * SEG, j * LANES:(j + 1) * LANES]
    ht = NT // 2
    cmul = lambda xr, xi, yr, yi: (xr * yr - xi * yi, xr * yi + xi * yr)
    ar = [ar_ref[:, j * LANES:(j + 1) * LANES] for j in range(ht)]
    ai = [ai_ref[:, j * LANES:(j + 1) * LANES] for j in range(ht)]
    one, nil = jnp.ones((1, LANES), f32), jnp.zeros((1, LANES), f32)
    a_seg = []
    for j in range(ht):
        rr, ri = [one], [nil]
        for _ in range(7):
            nr, ni = cmul(rr[-1], ri[-1], ar[j], ai[j])
            rr.append(nr)
            ri.append(ni)
        a8r, a8i = cmul(rr[-1], ri[-1], ar[j], ai[j])
        blk_r, blk_i = jnp.concatenate(rr, axis=0), jnp.concatenate(ri, axis=0)
        for b in range(SEG // 8):
            pw_scr[j, b * 8:(b + 1) * 8, :] = blk_r
            pw_scr[ht + j, b * 8:(b + 1) * 8, :] = blk_i
            blk_r, blk_i = cmul(blk_r, blk_i, a8r, a8i)
        a_seg.append((blk_r[0:1], blk_i[0:1]))

    ar_b = [jnp.broadcast_to(a, (NSEG, LANES)) for a in ar]
    ai_b = [jnp.broadcast_to(a, (NSEG, LANES)) for a in ai]

    def body(i, carry):
        new_r, new_i = [], []
        for j in range(ht):
            xr, xi = carry[j], carry[ht + j]
            seg_rows = pl.ds(i, NSEG, stride=SEG_PITCH)
            xl_scr[j, seg_rows, :] = xr
            xl_scr[ht + j, seg_rows, :] = xi
            pr, pi = cmul(xr, xi, ar_b[j], ai_b[j])
            new_r.append(pr + s_scr[j, seg_rows, :])
            new_i.append(pi + s_scr[ht + j, seg_rows, :])
        return tuple(new_r + new_i)

    ends = tuple(jnp.zeros((NSEG, LANES), f32) for _ in range(NT))
    for i in range(SEG):
        ends = body(i, ends)

    for j in range(ht):
        zr, zi = nil, nil
        for k in range(NSEG):
            rows = slice(k * SEG, (k + 1) * SEG)
            loc = slice(k * SEG_PITCH, k * SEG_PITCH + SEG)
            fr, fi = cmul(pw_scr[j], pw_scr[ht + j], zr, zi)
            xs_scr[rows, j * LANES:(j + 1) * LANES] = (xl_scr[j, loc, :] + fr).astype(bf16)
            xs_scr[rows, half + j * LANES:half + (j + 1) * LANES] = (xl_scr[ht + j, loc, :] + fi).astype(bf16)
            pr, pi = cmul(zr, zi, *a_seg[j])
            zr, zi = pr + ends[j][k:k + 1], pi + ends[ht + j][k:k + 1]

    xs = xs_scr[...]
    for j in range(SG_IN // Y_TILE):
        kk = (j + 1) * Y_TILE
        cols = slice(j * Y_TILE, (j + 1) * Y_TILE)
        y = _dot(u_ref[:, :kk], m_scr[:kk, cols]) + _dot(xs, q_scr[:, cols])
        y_ref[:, cols] = _gelu_tanh(y).astype(bf16)


def _ssm(u2, pre, pim, kt, qc, rep, at_re, at_im):
    per_sg = lambda r, c: pl.BlockSpec((None, r, c), lambda g: (g, 0, 0))
    return pl.pallas_call(
        _ssm_kernel,
        grid=(SG,),
        in_specs=[pl.BlockSpec((N_CHUNKS, SG_IN), lambda g: (0, g)),
                  per_sg(SG_IN, LANES), per_sg(SG_IN, LANES), per_sg(SG_IN, LANES),
                  per_sg(SG_STATE, CHUNK * SSM_GC), _const_spec(rep.shape),
                  per_sg(1, SG_STATE // 2), per_sg(1, SG_STATE // 2)],
        out_specs=pl.BlockSpec((N_CHUNKS, SG_IN), lambda g: (0, g)),
        out_shape=jax.ShapeDtypeStruct((N_CHUNKS, SG * SG_IN), bf16),
        scratch_shapes=[pltpu.VMEM((SG_IN, SG_STATE), bf16), pltpu.VMEM((SG_IN, SG_IN), bf16),
                        pltpu.VMEM((SG_STATE, SG_IN), bf16),
                        pltpu.VMEM((NT, NSEG * SEG_PITCH, LANES), f32),
                        pltpu.VMEM((NT, NSEG * SEG_PITCH, LANES), f32),
                        pltpu.VMEM((NT, SEG, LANES), f32), pltpu.VMEM((N_CHUNKS, SG_STATE), bf16)],
        compiler_params=_params(("arbitrary",)),
        name="ssm",
    )(u2, pre, pim, kt, qc, rep, at_re, at_im)


def _ssm_prep_kernel(lam_re_ref, lam_im_ref, ldt_ref, bt_re_ref, bt_im_ref, c_re_ref, c_im_ref, d_ref,
                     eye_ref, win_ref, pre_ref, pim_ref, kt_ref, qc_ref, at_re_ref, at_im_ref, wmix_ref):
    wmix_ref[...] = win_ref[...].astype(bf16)
    dup = lambda x: jnp.concatenate([x, x], axis=1)
    lr, li = dup(lam_re_ref[...]), dup(lam_im_ref[...])
    dt = jnp.exp(ldt_ref[...])
    mag = jnp.exp(lr * dt)
    a_re, a_im = mag * jnp.cos(li * dt), mag * jnp.sin(li * dt)
    den = lr * lr + li * li
    nr, ni = a_re - 1.0, a_im
    coef_re = (nr * lr + ni * li) / den
    coef_im = (ni * lr - nr * li) / den
    pw_re, pw_im = [jnp.ones_like(a_re)], [jnp.zeros_like(a_im)]
    for _ in range(CHUNK):
        pr, pi = pw_re[-1], pw_im[-1]
        pw_re.append(pr * a_re - pi * a_im)
        pw_im.append(pr * a_im + pi * a_re)

    low = lax.broadcasted_iota(jnp.int32, (SSM_GC, LANES), 1) < SSM_P
    diag = ((lax.broadcasted_iota(jnp.int32, (SSM_GC, LANES), 1) & (SSM_GC - 1))
            == lax.broadcasted_iota(jnp.int32, (SSM_GC, LANES), 0))
    contract_lanes = (((1,), (1,)), ((), ()))
    for g in range(GPS):
        grp = slice(g * SSM_GC, (g + 1) * SSM_GC)
        one = lambda x: x[g:g + 1]
        bt_re, bt_im = dup(bt_re_ref[grp, :]), dup(bt_im_ref[grp, :])
        bb_re = one(coef_re) * bt_re - one(coef_im) * bt_im
        bb_im = one(coef_re) * bt_im + one(coef_im) * bt_re
        yc = jnp.concatenate([c_re_ref[grp, :], -c_im_ref[grp, :]], axis=1)
        yc_sw = pltpu.roll(yc, SSM_P, axis=1)
        y2 = jnp.where(low, yc_sw, -yc_sw)
        x_rows, ca_rows = [], []
        for tau in range(CHUNK):
            pr, pi = one(pw_re[tau]), one(pw_im[tau])
            ab_re = bb_re * pr - bb_im * pi
            ab_im = bb_re * pi + bb_im * pr
            s = CHUNK - 1 - tau
            rows = slice(s * LANES + g * SSM_GC, s * LANES + (g + 1) * SSM_GC)
            pre_ref[rows, :] = ab_re
            pim_ref[rows, :] = ab_im
            x_rows.append(jnp.where(low, ab_re, ab_im))
            ca_rows.append(yc * one(pw_re[tau + 1]) + y2 * one(pw_im[tau + 1]))
        kern = lax.dot_general(jnp.concatenate(x_rows, axis=0), jnp.concatenate([yc] * GPS, axis=0),
                               contract_lanes, precision=lax.Precision.HIGHEST,
                               preferred_element_type=f32)
        for tau in range(CHUNK):
            blk = kern[tau * SSM_GC:(tau + 1) * SSM_GC]
            if tau == 0:
                blk = blk + jnp.where(diag, d_ref[...], 0.0)
            kt_ref[tau * LANES + g * SSM_GC:tau * LANES + (g + 1) * SSM_GC, :] = blk
        qt = lax.dot_general(eye_ref[...], jnp.concatenate(ca_rows, axis=0).astype(bf16),
                             contract_lanes, preferred_element_type=f32)
        qc_ref[g * SSM_P:(g + 1) * SSM_P, :] = qt[:SSM_P].astype(bf16)
        qc_ref[GPS * SSM_P + g * SSM_P:GPS * SSM_P + (g + 1) * SSM_P, :] = qt[SSM_P:].astype(bf16)
    lo_lane = lax.broadcasted_iota(jnp.int32, (1, LANES), 1) < SSM_P
    for k in range(GPS // 2):
        sel = lambda pw: jnp.where(lo_lane, pw[2 * k:2 * k + 1], pw[2 * k + 1:2 * k + 2])
        at_re_ref[:, k * LANES:(k + 1) * LANES] = sel(pw_re[CHUNK])
        at_im_ref[:, k * LANES:(k + 1) * LANES] = sel(pw_im[CHUNK])


def _ssm_operators(lam_re, lam_im, log_dt, b_re, b_im, c_re, c_im, d_skip, w_in):
    per_sg = lambda x: x.astype(f32).reshape((SG, x.size // (SG * x.shape[-1]), x.shape[-1]))
    bt = lambda b: per_sg(b.swapaxes(1, 2))
    args = (per_sg(lam_re), per_sg(lam_im), log_dt.astype(f32).reshape(SG, GPS, 1),
            bt(b_re), bt(b_im), per_sg(c_re), per_sg(c_im), d_skip.astype(f32).reshape(SG, 1, LANES),
            jnp.asarray(np.eye(LANES), bf16), w_in)
    blk = lambda r, c: pl.BlockSpec((None, r, c), lambda g: (g, 0, 0))
    slab = pl.BlockSpec((w_in.shape[0] // SG, IN_MIX_W), lambda g: (g, 0))
    return pl.pallas_call(
        _ssm_prep_kernel,
        grid=(SG,),
        in_specs=[blk(GPS, SSM_P), blk(GPS, SSM_P), blk(GPS, 1), blk(LANES, SSM_P), blk(LANES, SSM_P),
                  blk(LANES, SSM_P), blk(LANES, SSM_P), blk(1, LANES), _const_spec((LANES, LANES)), slab],
        out_specs=[blk(SG_IN, LANES), blk(SG_IN, LANES), blk(SG_IN, LANES),
                   blk(SG_STATE, CHUNK * SSM_GC), blk(1, SG_STATE // 2), blk(1, SG_STATE // 2),
                   slab],
        out_shape=[jax.ShapeDtypeStruct((SG, SG_IN, LANES), f32)] * 3
                  + [jax.ShapeDtypeStruct((SG, SG_STATE, CHUNK * SSM_GC), bf16)]
                  + [jax.ShapeDtypeStruct((SG, 1, SG_STATE // 2), f32)] * 2
                  + [jax.ShapeDtypeStruct((w_in.shape[0], IN_MIX_W), bf16)],
        compiler_params=_params(("parallel",)),
        name="ssm_prep",
    )(*args)


def _replication_matrix():
    src = np.arange(CHUNK * SSM_GC)
    dst = np.arange(SG_IN)
    same_t = (src[:, None] // SSM_GC) == (dst[None, :] // LANES)
    same_c = (src[:, None] % SSM_GC) == (dst[None, :] % SSM_GC)
    return jnp.asarray(same_t & same_c, bf16)


def _kv_operands(kvp_ref, kvc_ref):
    kcat = jnp.concatenate([kvp_ref[:, :KV_W], kvc_ref[:, :KV_W]], axis=0).astype(f32)
    vcat = jnp.concatenate([kvp_ref[:, KV_W:], kvc_ref[:, KV_W:]], axis=0).astype(f32)
    k_sw = pltpu.roll(kcat, HEAD_DIM, axis=1)
    v_sw = pltpu.roll(vcat, HEAD_DIM, axis=1)
    low_kv = lax.broadcasted_iota(jnp.int32, kcat.shape, 1) < HEAD_DIM
    operands = []
    for hk in range(N_KV_HEADS):
        k_src, k_oth = (kcat, k_sw) if hk == 0 else (k_sw, kcat)
        k_lo = jnp.where(low_kv, k_src, 0.0).astype(bf16)
        k_hi = jnp.where(low_kv, 0.0, k_oth).astype(bf16)
        vv = (jnp.where(low_kv, vcat, v_sw) if hk == 0 else jnp.where(low_kv, v_sw, vcat)).astype(bf16)
        operands.append((k_lo, k_hi, vv))
    return operands


def _attention_block(q, kv_operands, sinks_ref, blk, no_prev, oa_ref):
    q_rows = slice(blk * BLOCK, (blk + 1) * BLOCK)
    kv_rows = slice(blk * BLOCK, (blk + 2) * BLOCK)
    from_prev = (lax.broadcasted_iota(jnp.int32, (BLOCK, BLOCK), 1)
                 > lax.broadcasted_iota(jnp.int32, (BLOCK, BLOCK), 0))
    low_o = lax.broadcasted_iota(jnp.int32, (BLOCK, LANES), 1) < HEAD_DIM
    contract_lanes = (((1,), (1,)), ((), ()))
    ones_kv = jnp.ones((2 * BLOCK, LANES), bf16)
    pairs_per_kv = (N_Q_HEADS // N_KV_HEADS) // 2
    for hk, (k_lo, k_hi, vv) in enumerate(kv_operands):
        qs = jnp.concatenate(
            [q[q_rows, (pairs_per_kv * hk + b) * LANES:(pairs_per_kv * hk + b + 1) * LANES]
             for b in range(pairs_per_kv)], axis=0)
        s_par = (lax.dot_general(qs, k_lo[kv_rows], contract_lanes, preferred_element_type=f32),
                 lax.dot_general(qs, k_hi[kv_rows], contract_lanes, preferred_element_type=f32))
        for b in range(pairs_per_kv):
            outs = []
            for par in range(2):
                head = 2 * (pairs_per_kv * hk + b) + par
                s_all = s_par[par][b * BLOCK:(b + 1) * BLOCK]
                s_prev = s_all[:, :BLOCK] + no_prev if blk == 0 else s_all[:, :BLOCK]
                s = jnp.where(from_prev, s_prev, s_all[:, BLOCK:])
                sink = sinks_ref[head]
                m = jnp.maximum(jnp.max(s, axis=1, keepdims=True), sink)
                p = jnp.exp(s - m)
                p_kv = jnp.concatenate([jnp.where(from_prev, p, 0.0), jnp.where(from_prev, 0.0, p)],
                                       axis=1).astype(bf16)
                denom = _dot(p_kv, ones_kv) + jnp.exp(sink - m)
                outs.append(_dot(p_kv, vv[kv_rows]) * (1.0 / denom))
            col = (pairs_per_kv * hk + b) * LANES
            oa_ref[q_rows, col:col + LANES] = jnp.where(low_o, outs[0], outs[1]).astype(bf16)


def _attn_mix_kernel(sinks_ref, q_ref, kvp_ref, kvc_ref, y2_ref, gates_ref,
                     wglu_ref, wb_ref, wdn_ref, m_ref, wdn_o_ref, oa_ref, yb_ref):
    wdn_o_ref[...] = wdn_ref[...].astype(bf16)
    no_prev = jnp.where(pl.program_id(0) > 0, 0.0, MASKED)
    q = q_ref[...] * jnp.asarray(1.0 / math.sqrt(HEAD_DIM), bf16)
    kv_operands = _kv_operands(kvp_ref, kvc_ref)
    for r in range(q_ref.shape[0] // ROW_SUB):
        rows = pl.ds(r * ROW_SUB, ROW_SUB)
        chunk_rows = pl.ds(r * (ROW_SUB // CHUNK), ROW_SUB // CHUNK)
        for sg in range(SG):
            for t in range(CHUNK):
                col = sg * SG_IN + t * LANES
                yb_ref[r, sg, pl.ds(t, ROW_SUB // CHUNK, stride=CHUNK_PITCH), :] = (
                    y2_ref[chunk_rows, col:col + LANES].astype(f32))
        y = jnp.concatenate(
            [jnp.concatenate([yb_ref[r, sg, n * CHUNK_PITCH:n * CHUNK_PITCH + CHUNK, :]
                              for n in range(ROW_SUB // CHUNK)], axis=0) for sg in range(SG)],
            axis=1).astype(bf16)
        zg = _dot(y, wglu_ref[...])
        o_ssm = zg[:, :SSM_W] * _sigmoid(zg[:, SSM_W:])
        y_s = _dot(o_ssm.astype(bf16), wb_ref[Q_W:, :])
        for blk in range(r * (ROW_SUB // BLOCK), (r + 1) * (ROW_SUB // BLOCK)):
            _attention_block(q, kv_operands, sinks_ref, blk, no_prev, oa_ref)
        y_a = _dot(oa_ref[rows, :], wb_ref[:Q_W, :])
        m_ref[rows, :] = (gates_ref[rows, :D_MODEL].astype(f32) * y_a
                          + gates_ref[rows, D_MODEL:].astype(f32) * y_s).astype(bf16)


def _attn_mix(q, kv, sinks, y2, gates, wglu, wb, w_down, tm=512):
    L = q.shape[0]
    cur = lambda i: (i, 0)
    prev = lambda i: (jnp.maximum(i * (tm // BLOCK) - 1, 0), 0)
    dn_slab = pl.BlockSpec((w_down.shape[0] // (L // tm), w_down.shape[1]), cur)
    return pl.pallas_call(
        _attn_mix_kernel,
        grid=(L // tm,),
        in_specs=[pl.BlockSpec(memory_space=pltpu.SMEM),
                  pl.BlockSpec((tm, Q_W), cur),
                  pl.BlockSpec((BLOCK, 2 * KV_W), prev), pl.BlockSpec((tm, 2 * KV_W), cur),
                  pl.BlockSpec((tm // CHUNK, SG * SG_IN), cur),
                  pl.BlockSpec((tm, 2 * D_MODEL), cur),
                  _const_spec(wglu.shape), _const_spec(wb.shape), dn_slab],
        out_specs=[pl.BlockSpec((tm, D_MODEL), cur), dn_slab],
        out_shape=[jax.ShapeDtypeStruct((L, D_MODEL), bf16), jax.ShapeDtypeStruct(w_down.shape, bf16)],
        scratch_shapes=[pltpu.VMEM((tm, Q_W), bf16),
                        pltpu.VMEM((tm // ROW_SUB, SG, ROW_SUB // CHUNK * CHUNK_PITCH, LANES), f32)],
        compiler_params=_params(("parallel",)),
        name="attn_mix",
    )(sinks, q, kv, kv, y2, gates, wglu, wb, w_down)


def _outproj_kernel(m_ref, w_ref, x_ref, gpost_ref, gpre_ref, x1_ref, h2_ref):
    for r in range(m_ref.shape[0] // OUT_SUB):
        rows = pl.ds(r * OUT_SUB, OUT_SUB)
        out = _dot(m_ref[rows, :], w_ref[...])
        x1 = x_ref[rows, :] + _rms_norm(out, gpost_ref[...])
        x1_ref[rows, :] = x1
        h2_ref[rows, :] = _rms_norm(x1, gpre_ref[...]).astype(bf16)


def _outproj(m, w_out, x, g_post, g_pre, tm=512):
    L = x.shape[0]
    row = lambda: pl.BlockSpec((tm, D_MODEL), lambda i: (i, 0))
    return pl.pallas_call(
        _outproj_kernel,
        grid=(L // tm,),
        in_specs=[row(), _const_spec(w_out.shape), row(),
                  _const_spec((1, D_MODEL)), _const_spec((1, D_MODEL))],
        out_specs=[row(), row()],
        out_shape=[jax.ShapeDtypeStruct((L, D_MODEL), f32), jax.ShapeDtypeStruct((L, D_MODEL), bf16)],
        compiler_params=_params(("parallel",)),
        name="outproj",
    )(m, w_out, x, g_post, g_pre)


MLP_SUB = 512


def _mlp_kernel(h_ref, wu_ref, wd_ref, x_ref, g_ref, o_ref, acc_ref):
    i, j = pl.program_id(0), pl.program_id(1)
    n_tiles = pl.num_programs(0) - 1
    tm = h_ref.shape[0]
    slab = o_ref.shape[0]
    cur, prev = acc_ref.at[i % 2], acc_ref.at[(i + 1) % 2]
    slab_rows = pl.ds(pl.multiple_of(j * slab, slab), slab)

    def finish_prev_slab():
        o_ref[...] = x_ref[...] + _rms_norm(prev[slab_rows, :], g_ref[...])
        prev[slab_rows, :] = jnp.zeros((slab, D_MODEL), f32)

    @pl.when((i == 0) & (j == 0))
    def _():
        acc_ref[...] = jnp.zeros_like(acc_ref)

    @pl.when(i < n_tiles)
    def _():
        for r in range(tm // MLP_SUB):
            rows = pl.ds(r * MLP_SUB, MLP_SUB)
            a = jnp.maximum(_dot(h_ref[rows, :], wu_ref[...]), 0.0)
            cur[rows, :] += _dot((a * a).astype(bf16), wd_ref[...])
        finish_prev_slab()

    @pl.when(i == n_tiles)
    def _():
        finish_prev_slab()


def _mlp(h2, w_up, w_down, x1, g_post, tm=1024, tf=1024):
    L = x1.shape[0]
    n_tiles, n_ff = L // tm, D_FF // tf
    slab = tm // n_ff
    ff = lambda i, j: jnp.where(i < n_tiles, j, n_ff - 1)
    prev_slab = lambda i, j: (jnp.maximum(i - 1, 0) * n_ff + jnp.where(i > 0, j, 0), 0)
    return pl.pallas_call(
        _mlp_kernel,
        grid=(n_tiles + 1, n_ff),
        in_specs=[pl.BlockSpec((tm, D_MODEL), lambda i, j: (jnp.minimum(i, n_tiles - 1), 0)),
                  pl.BlockSpec((D_MODEL, tf), lambda i, j: (0, ff(i, j))),
                  pl.BlockSpec((tf, D_MODEL), lambda i, j: (ff(i, j), 0)),
                  pl.BlockSpec((slab, D_MODEL), prev_slab),
                  _const_spec((1, D_MODEL))],
        out_specs=pl.BlockSpec((slab, D_MODEL), prev_slab),
        out_shape=jax.ShapeDtypeStruct((L, D_MODEL), f32),
        scratch_shapes=[pltpu.VMEM((2, tm, D_MODEL), f32)],
        compiler_params=_params(("arbitrary", "arbitrary")),
        name="mlp",
    )(h2, w_up, w_down, x1, g_post)


def _rope_tables(L):
    half = ROT_DIM // 2
    inv = ROPE_THETA ** (-np.arange(half, dtype=np.float64) * 2.0 / ROT_DIM)
    ang = np.arange(L, dtype=np.float64)[:, None] * inv[None, :]
    cos, sin = np.cos(ang).astype(np.float32), np.sin(ang).astype(np.float32)
    ones = np.ones((L, HEAD_DIM - ROT_DIM), np.float32)
    zeros = np.zeros((L, HEAD_DIM - ROT_DIM), np.float32)
    zh = np.zeros((L, half), np.float32)
    per_head = lambda parts: np.tile(np.concatenate(parts, axis=1), (1, LANES // HEAD_DIM))
    return jnp.asarray(np.concatenate(
        [per_head([cos, cos, ones]), per_head([-sin, zh, zeros]), per_head([zh, sin, zeros])], axis=1))


def _layer(x, norm_mix_pre, norm_mix_post, norm_mlp_pre, norm_mlp_post, w_in, sinks,
           lam_re, lam_im, log_dt, b_re, b_im, c_re, c_im, d_skip, w_glu, w_branch, w_out,
           w_up, w_down, rope):
    gain = lambda g: g.astype(f32).reshape(1, D_MODEL)
    pre, pim, kt, qc, at_re, at_im, wmix = _ssm_operators(
        lam_re, lam_im, log_dt, b_re, b_im, c_re, c_im, d_skip, w_in.astype(f32))
    q, kv, u2, h, wg = _inproj(x, gain(norm_mix_pre), wmix, w_in.astype(f32), rope)
    gates, wup, (wglu, wb, wo) = _gates(h, wg, w_up.astype(f32),
                                        [w_glu.astype(f32), w_branch.astype(f32), w_out.astype(f32)])
    y2 = _ssm(u2, pre, pim, kt, qc, _replication_matrix(), at_re, at_im)
    mixed, wdn = _attn_mix(q, kv, sinks.astype(f32), y2, gates, wglu, wb, w_down.astype(f32))
    x1, h2 = _outproj(mixed, wo, x, gain(norm_mix_post), gain(norm_mlp_pre))
    return _mlp(h2, wup, wdn, x1, gain(norm_mlp_post))


def kernel(x, norm_mix_pre, norm_mix_post, norm_mlp_pre, norm_mlp_post, w_in, sinks, lam_re, lam_im, log_dt, b_re, b_im, c_re, c_im, d_skip, w_glu, w_branch, w_out, w_up, w_down):
    B, L, _ = x.shape
    depth = w_in.shape[0]
    rope = _rope_tables(L)
    outs = []
    for b in range(B):
        xb = x[b]
        for l in range(depth):
            xb = _layer(xb, norm_mix_pre[l], norm_mix_post[l], norm_mlp_pre[l], norm_mlp_post[l],
                        w_in[l], sinks[l], lam_re[l], lam_im[l], log_dt[l], b_re[l], b_im[l],
                        c_re[l], c_im[l], d_skip[l], w_glu[l], w_branch[l], w_out[l],
                        w_up[l], w_down[l], rope)
        outs.append(xb)
    return jnp.stack(outs)
```
